```python
import jax, jax.numpy as jnp
from jax import lax
import numpy as np

D_MODEL = 1024
BATCH = 8
SEQ = 4096
DEPTH = 1

HEAD_DIM = 64
W_CONV = D_MODEL // 2
W_ATTN = D_MODEL - W_CONV
N_CONV_GROUPS = W_CONV // HEAD_DIM
N_ATTN_HEADS = W_ATTN // HEAD_DIM
CONV_K = 3
D_FF = 4 * D_MODEL
PLE_DIM = 256
Q_BLOCK = 128
EPS = 1e-6
IN_COLS = 3 * W_CONV + 3 * W_ATTN

kernel_name = "hymba_shortconv_stickbreaking_hybrid"


def rmsnorm(x, g):
    xf = x.astype(jnp.float32)
    y = xf * lax.rsqrt(jnp.mean(xf * xf, axis=-1, keepdims=True) + EPS)
    return (y * g.astype(jnp.float32)).astype(x.dtype)


def head_rmsnorm(y, g):
    b, s, w = y.shape
    yf = y.astype(jnp.float32).reshape(b, s, w // HEAD_DIM, HEAD_DIM)
    yf = yf * lax.rsqrt(jnp.mean(yf * yf, axis=-1, keepdims=True) + EPS)
    return (yf.reshape(b, s, w) * g.astype(jnp.float32)).astype(y.dtype)


def short_gated_conv(b_gate, c_gate, u, w_conv):
    v = c_gate * u
    y = lax.conv_general_dilated(
        v, w_conv[:, None, :].astype(v.dtype),
        window_strides=(1,), padding=[(CONV_K - 1, 0)],
        dimension_numbers=('NWC', 'WIO', 'NWC'),
        feature_group_count=v.shape[-1])
    return b_gate * y


def stick_breaking_attention(q, k, v):
    b, h, s, dh = q.shape
    n_blk = s // Q_BLOCK
    scale = dh ** -0.5
    qb = q.reshape(b, h, n_blk, Q_BLOCK, dh).transpose(2, 0, 1, 3, 4)
    kf = k.astype(jnp.float32)
    vf = v.astype(jnp.float32)
    key_pos = jnp.arange(s)

    def one_block(args):
        qi, blk = args
        z = jnp.einsum('bhqd,bhkd->bhqk', qi.astype(jnp.float32), kf) * scale
        q_pos = blk * Q_BLOCK + jnp.arange(Q_BLOCK)
        mask = key_pos[None, :] < q_pos[:, None]
        log_keep = jnp.where(mask, jax.nn.log_sigmoid(-z), 0.0)
        suffix = lax.cumsum(log_keep, axis=3, reverse=True) - log_keep
        a = jnp.where(mask, jnp.exp(jax.nn.log_sigmoid(z) + suffix), 0.0)
        return jnp.einsum('bhqk,bhkd->bhqd', a, vf)

    out = lax.map(one_block, (qb, jnp.arange(n_blk)))
    return out.transpose(1, 2, 0, 3, 4).reshape(b, h, s, dh).astype(q.dtype)


def _fwd_setup_inputs(seed: int = 0) -> dict:
    key = jax.random.key(seed)
    ks = jax.random.split(key, 20)
    f32 = jnp.float32

    def nrm(k, shape, fan_in):
        return jax.random.normal(k, shape, f32) * (fan_in ** -0.5)

    def gain(k, shape):
        return 1.0 + 0.02 * jax.random.normal(k, shape, f32)

    return {
        "x": jax.random.normal(ks[0], (BATCH, SEQ, D_MODEL), f32),
        "p": jax.random.normal(ks[1], (DEPTH, BATCH, SEQ, PLE_DIM), f32),
        "g_mix": gain(ks[2], (DEPTH, D_MODEL)),
        "w_in": nrm(ks[3], (DEPTH, D_MODEL, IN_COLS), D_MODEL),
        "conv_w": nrm(ks[4], (DEPTH, CONV_K, W_CONV), CONV_K),
        "g_conv_out": gain(ks[5], (DEPTH, W_CONV)),
        "g_attn_out": gain(ks[6], (DEPTH, W_ATTN)),
        "w_out": nrm(ks[7], (DEPTH, W_CONV + W_ATTN, D_MODEL), W_CONV + W_ATTN),
        "g_mlp": gain(ks[8], (DEPTH, D_MODEL)),
        "w_up": nrm(ks[9], (DEPTH, D_MODEL, D_FF), D_MODEL),
        "w_down": nrm(ks[10], (DEPTH, D_FF, D_MODEL), D_FF),
        "g_ple": gain(ks[11], (DEPTH, D_MODEL)),
        "w_ple_gate": nrm(ks[12], (DEPTH, D_MODEL, D_MODEL), D_MODEL),
        "w_ple_proj": nrm(ks[13], (DEPTH, PLE_DIM, D_MODEL), PLE_DIM),
        "g_final": gain(ks[14], (D_MODEL,)),
    }


def _fwd_reference(x, p, g_mix, w_in, conv_w, g_conv_out, g_attn_out, w_out, g_mlp, w_up, w_down,
              g_ple, w_ple_gate, w_ple_proj, g_final):
    b, s, _ = x.shape
    splits = [W_CONV, 2 * W_CONV, 3 * W_CONV, 3 * W_CONV + W_ATTN, 3 * W_CONV + 2 * W_ATTN]
    h = x
    for i in range(DEPTH):
        a = rmsnorm(h, g_mix[i])
        proj = a @ w_in[i]
        cb, cc, cu, q, k, v = jnp.split(proj, splits, axis=-1)
        conv_out = head_rmsnorm(short_gated_conv(cb, cc, cu, conv_w[i]), g_conv_out[i])
        to_heads = lambda t: t.reshape(b, s, N_ATTN_HEADS, HEAD_DIM).transpose(0, 2, 1, 3)
        attn = stick_breaking_attention(to_heads(q), to_heads(k), to_heads(v))
        attn = head_rmsnorm(attn.transpose(0, 2, 1, 3).reshape(b, s, W_ATTN), g_attn_out[i])
        h = h + jnp.concatenate([conv_out, attn], axis=-1) @ w_out[i]
        m = rmsnorm(h, g_mlp[i])
        h = h + jnp.square(jax.nn.relu(m @ w_up[i])) @ w_down[i]
        gate = jax.nn.sigmoid(rmsnorm(h, g_ple[i]) @ w_ple_gate[i])
        h = h + gate * (p[i] @ w_ple_proj[i])
    return rmsnorm(h, g_final)


import jax as _jax
import jax.numpy as _jnp

TWIN_FORMAT = 'train_step'
FWD_PARAMS = ['x', 'p', 'g_mix', 'w_in', 'conv_w', 'g_conv_out', 'g_attn_out', 'w_out', 'g_mlp', 'w_up', 'w_down', 'g_ple', 'w_ple_gate', 'w_ple_proj', 'g_final']
TWIN_WEIGHTS = ['g_mix', 'w_in', 'conv_w', 'g_conv_out', 'g_attn_out', 'w_out', 'g_mlp', 'w_up', 'w_down', 'g_ple', 'w_ple_gate', 'w_ple_proj', 'g_final']
TWIN_DIFF_INPUT = 'x'
TWIN_INPUTS = ['x', 'p', 'g_mix', 'w_in', 'conv_w', 'g_conv_out', 'g_attn_out', 'w_out', 'g_mlp', 'w_up', 'w_down', 'g_ple', 'w_ple_gate', 'w_ple_proj', 'g_final', 'loss_target', 'm_g_mix', 'm_w_in', 'm_conv_w', 'm_g_conv_out', 'm_g_attn_out', 'm_w_out', 'm_g_mlp', 'm_w_up', 'm_w_down', 'm_g_ple', 'm_w_ple_gate', 'm_w_ple_proj', 'm_g_final', 'v_g_mix', 'v_w_in', 'v_conv_w', 'v_g_conv_out', 'v_g_attn_out', 'v_w_out', 'v_g_mlp', 'v_w_up', 'v_w_down', 'v_g_ple', 'v_w_ple_gate', 'v_w_ple_proj', 'v_g_final']
TWIN_OUTPUTS = ['loss', 'grad_x', 'grad_g_mix', 'grad_w_in', 'grad_conv_w', 'grad_g_conv_out', 'grad_g_attn_out', 'grad_w_out', 'grad_g_mlp', 'grad_w_up', 'grad_w_down', 'grad_g_ple', 'grad_w_ple_gate', 'grad_w_ple_proj', 'grad_g_final', 'delta_g_mix', 'delta_w_in', 'delta_conv_w', 'delta_g_conv_out', 'delta_g_attn_out', 'delta_w_out', 'delta_g_mlp', 'delta_w_up', 'delta_w_down', 'delta_g_ple', 'delta_w_ple_gate', 'delta_w_ple_proj', 'delta_g_final', 'new_m_g_mix', 'new_m_w_in', 'new_m_conv_w', 'new_m_g_conv_out', 'new_m_g_attn_out', 'new_m_w_out', 'new_m_g_mlp', 'new_m_w_up', 'new_m_w_down', 'new_m_g_ple', 'new_m_w_ple_gate', 'new_m_w_ple_proj', 'new_m_g_final', 'new_v_g_mix', 'new_v_w_in', 'new_v_conv_w', 'new_v_g_conv_out', 'new_v_g_attn_out', 'new_v_w_out', 'new_v_g_mlp', 'new_v_w_up', 'new_v_w_down', 'new_v_g_ple', 'new_v_w_ple_gate', 'new_v_w_ple_proj', 'new_v_g_final']
TWIN_LEAF_KINDS = {'loss': 'loss', 'grad_x': 'grad_x', 'grad_g_mix': 'grad_w', 'grad_w_in': 'grad_w', 'grad_conv_w': 'grad_w', 'grad_g_conv_out': 'grad_w', 'grad_g_attn_out': 'grad_w', 'grad_w_out': 'grad_w', 'grad_g_mlp': 'grad_w', 'grad_w_up': 'grad_w', 'grad_w_down': 'grad_w', 'grad_g_ple': 'grad_w', 'grad_w_ple_gate': 'grad_w', 'grad_w_ple_proj': 'grad_w', 'grad_g_final': 'grad_w', 'delta_g_mix': 'delta_w', 'delta_w_in': 'delta_w', 'delta_conv_w': 'delta_w', 'delta_g_conv_out': 'delta_w', 'delta_g_attn_out': 'delta_w', 'delta_w_out': 'delta_w', 'delta_g_mlp': 'delta_w', 'delta_w_up': 'delta_w', 'delta_w_down': 'delta_w', 'delta_g_ple': 'delta_w', 'delta_w_ple_gate': 'delta_w', 'delta_w_ple_proj': 'delta_w', 'delta_g_final': 'delta_w', 'new_m_g_mix': 'new_m', 'new_m_w_in': 'new_m', 'new_m_conv_w': 'new_m', 'new_m_g_conv_out': 'new_m', 'new_m_g_attn_out': 'new_m', 'new_m_w_out': 'new_m', 'new_m_g_mlp': 'new_m', 'new_m_w_up': 'new_m', 'new_m_w_down': 'new_m', 'new_m_g_ple': 'new_m', 'new_m_w_ple_gate': 'new_m', 'new_m_w_ple_proj': 'new_m', 'new_m_g_final': 'new_m', 'new_v_g_mix': 'new_v', 'new_v_w_in': 'new_v', 'new_v_conv_w': 'new_v', 'new_v_g_conv_out': 'new_v', 'new_v_g_attn_out': 'new_v', 'new_v_w_out': 'new_v', 'new_v_g_mlp': 'new_v', 'new_v_w_up': 'new_v', 'new_v_w_down': 'new_v', 'new_v_g_ple': 'new_v', 'new_v_w_ple_gate': 'new_v', 'new_v_w_ple_proj': 'new_v', 'new_v_g_final': 'new_v'}


def _forward(args):
    return _fwd_reference(*[args[k] for k in FWD_PARAMS])


def _output_shape():
    def fwd():
        inp = _fwd_setup_inputs(0)
        return _fwd_reference(*[inp[k] for k in FWD_PARAMS])
    out = _jax.eval_shape(fwd)
    return out.shape, out.dtype

N_MICROBATCH = 1
ADAM_LR = 0.001
ADAM_B1 = 0.9
ADAM_B2 = 0.999
ADAM_EPS = 1e-08
ADAM_WD = 0.01
ADAM_STEP = 10
PER_EXAMPLE_BATCH_AXIS = {'x': 0, 'p': 1, 'loss_target': 0}
SHARED_INPUTS = []
_WEIGHT_DTYPES = {'g_mix': _jnp.float32, 'w_in': _jnp.float32, 'conv_w': _jnp.float32, 'g_conv_out': _jnp.float32, 'g_attn_out': _jnp.float32, 'w_out': _jnp.float32, 'g_mlp': _jnp.float32, 'w_up': _jnp.float32, 'w_down': _jnp.float32, 'g_ple': _jnp.float32, 'w_ple_gate': _jnp.float32, 'w_ple_proj': _jnp.float32, 'g_final': _jnp.float32}
MOMENT_SCALE = {'g_mix': 2.037038e-01, 'w_in': 1.158339e-01, 'conv_w': 1.470581e-01, 'g_conv_out': 1.289252e-01, 'g_attn_out': 1.352076e-01, 'w_out': 1.304558e-01, 'g_mlp': 1.384749e-01, 'w_up': 6.520255e-02, 'w_down': 1.271835e-01, 'g_ple': 1.981805e-02, 'w_ple_gate': 2.032432e-02, 'w_ple_proj': 4.948113e-02, 'g_final': 3.216711e+01}


def _to_microbatches(a, axis):
    t = _jnp.moveaxis(a, axis, 0)
    t = t.reshape((N_MICROBATCH, t.shape[0] // N_MICROBATCH) + t.shape[1:])
    return _jnp.moveaxis(t, 1, axis + 1)


def setup_inputs(seed: int = 0) -> dict:
    inp = _fwd_setup_inputs(seed)
    key = _jax.random.fold_in(_jax.random.key(seed), 7919)
    shape, _ = _output_shape()
    out = dict(inp)
    out["loss_target"] = _jax.random.normal(_jax.random.fold_in(key, 0), shape, _jnp.float32)
    for i, name in enumerate(TWIN_WEIGHTS):
        w = inp[name].astype(_jnp.float32)
        if MOMENT_SCALE is None:
            s = _jnp.sqrt(_jnp.mean(_jnp.square(w)) + 1e-30)
        else:
            s = MOMENT_SCALE[name]
        km, kv = _jax.random.split(_jax.random.fold_in(key, i + 1))
        out[name] = w
        out["m_" + name] = s * _jax.random.normal(km, w.shape, _jnp.float32)
        out["v_" + name] = (s * s) * _jax.random.uniform(kv, w.shape, _jnp.float32, 0.5, 1.5)
    if N_MICROBATCH > 1:
        for name, axis in PER_EXAMPLE_BATCH_AXIS.items():
            out[name] = _to_microbatches(out[name], axis)
    return {'x': out['x'], 'p': out['p'], 'g_mix': out['g_mix'], 'w_in': out['w_in'], 'conv_w': out['conv_w'], 'g_conv_out': out['g_conv_out'], 'g_attn_out': out['g_attn_out'], 'w_out': out['w_out'], 'g_mlp': out['g_mlp'], 'w_up': out['w_up'], 'w_down': out['w_down'], 'g_ple': out['g_ple'], 'w_ple_gate': out['w_ple_gate'], 'w_ple_proj': out['w_ple_proj'], 'g_final': out['g_final'], 'loss_target': out['loss_target'], 'm_g_mix': out['m_g_mix'], 'm_w_in': out['m_w_in'], 'm_conv_w': out['m_conv_w'], 'm_g_conv_out': out['m_g_conv_out'], 'm_g_attn_out': out['m_g_attn_out'], 'm_w_out': out['m_w_out'], 'm_g_mlp': out['m_g_mlp'], 'm_w_up': out['m_w_up'], 'm_w_down': out['m_w_down'], 'm_g_ple': out['m_g_ple'], 'm_w_ple_gate': out['m_w_ple_gate'], 'm_w_ple_proj': out['m_w_ple_proj'], 'm_g_final': out['m_g_final'], 'v_g_mix': out['v_g_mix'], 'v_w_in': out['v_w_in'], 'v_conv_w': out['v_conv_w'], 'v_g_conv_out': out['v_g_conv_out'], 'v_g_attn_out': out['v_g_attn_out'], 'v_w_out': out['v_w_out'], 'v_g_mlp': out['v_g_mlp'], 'v_w_up': out['v_w_up'], 'v_w_down': out['v_w_down'], 'v_g_ple': out['v_g_ple'], 'v_w_ple_gate': out['v_w_ple_gate'], 'v_w_ple_proj': out['v_w_ple_proj'], 'v_g_final': out['v_g_final']}


def _loss(weights, diff, rest, loss_target):
    with _jax.named_scope("forward"):
        args = {**rest, TWIN_DIFF_INPUT: diff, **{k: w.astype(_WEIGHT_DTYPES[k]) for k, w in weights.items()}}
        y = _forward(args)
    with _jax.named_scope("loss_head"):
        err = _jnp.square(y.astype(_jnp.float32) - loss_target)
        return 0.5 * _jnp.sum(_jnp.mean(err, axis=-1)) if err.ndim else 0.5 * err


def _adamw(w, g, m, v):
    m = ADAM_B1 * m + (1.0 - ADAM_B1) * g
    v = ADAM_B2 * v + (1.0 - ADAM_B2) * _jnp.square(g)
    m_hat = m / (1.0 - ADAM_B1 ** ADAM_STEP)
    v_hat = v / (1.0 - ADAM_B2 ** ADAM_STEP)
    delta = -ADAM_LR * (m_hat / (_jnp.sqrt(v_hat) + ADAM_EPS) + ADAM_WD * w)
    return delta, m, v


def reference(x, p, g_mix, w_in, conv_w, g_conv_out, g_attn_out, w_out, g_mlp, w_up, w_down, g_ple, w_ple_gate, w_ple_proj, g_final, loss_target, m_g_mix, m_w_in, m_conv_w, m_g_conv_out, m_g_attn_out, m_w_out, m_g_mlp, m_w_up, m_w_down, m_g_ple, m_w_ple_gate, m_w_ple_proj, m_g_final, v_g_mix, v_w_in, v_conv_w, v_g_conv_out, v_g_attn_out, v_w_out, v_g_mlp, v_w_up, v_w_down, v_g_ple, v_w_ple_gate, v_w_ple_proj, v_g_final):
    given = dict(x=x, p=p, g_mix=g_mix, w_in=w_in, conv_w=conv_w, g_conv_out=g_conv_out, g_attn_out=g_attn_out, w_out=w_out, g_mlp=g_mlp, w_up=w_up, w_down=w_down, g_ple=g_ple, w_ple_gate=w_ple_gate, w_ple_proj=w_ple_proj, g_final=g_final, loss_target=loss_target, m_g_mix=m_g_mix, m_w_in=m_w_in, m_conv_w=m_conv_w, m_g_conv_out=m_g_conv_out, m_g_attn_out=m_g_attn_out, m_w_out=m_w_out, m_g_mlp=m_g_mlp, m_w_up=m_w_up, m_w_down=m_w_down, m_g_ple=m_g_ple, m_w_ple_gate=m_w_ple_gate, m_w_ple_proj=m_w_ple_proj, m_g_final=m_g_final, v_g_mix=v_g_mix, v_w_in=v_w_in, v_conv_w=v_conv_w, v_g_conv_out=v_g_conv_out, v_g_attn_out=v_g_attn_out, v_w_out=v_w_out, v_g_mlp=v_g_mlp, v_w_up=v_w_up, v_w_down=v_w_down, v_g_ple=v_g_ple, v_w_ple_gate=v_w_ple_gate, v_w_ple_proj=v_w_ple_proj, v_g_final=v_g_final)
    weights = {n: given[n] for n in TWIN_WEIGHTS}
    shared = {n: given[n] for n in SHARED_INPUTS}
    per_example = {n: given[n] for n in ['x', 'p']}
    grad_fn = _jax.value_and_grad(_loss, argnums=(0, 1))

    def one_microbatch(ex, loss_target):
        ex = dict(ex)
        diff = ex.pop(TWIN_DIFF_INPUT)
        return grad_fn(weights, diff, {**shared, **ex}, loss_target)

    if N_MICROBATCH == 1:
        loss, (grad_w, grad_x) = one_microbatch(per_example, given["loss_target"])
    else:
        def body(carry, xs):
            loss_sum, grad_sum = carry
            l_k, (gw_k, gx_k) = one_microbatch(xs[0], xs[1])
            with _jax.named_scope("update"):
                return (loss_sum + l_k, _jax.tree.map(_jnp.add, grad_sum, gw_k)), gx_k

        init = (_jnp.zeros((), _jnp.float32), _jax.tree.map(_jnp.zeros_like, weights))
        (loss, grad_w), grad_x = _jax.lax.scan(body, init, (per_example, given["loss_target"]))
    with _jax.named_scope("update"):
        delta_w, new_m, new_v = {}, {}, {}
        for n in TWIN_WEIGHTS:
            delta_w[n], new_m[n], new_v[n] = _adamw(weights[n], grad_w[n], given["m_" + n], given["v_" + n])
    return (loss, grad_x, *[grad_w[n] for n in TWIN_WEIGHTS], *[delta_w[n] for n in TWIN_WEIGHTS],
            *[new_m[n] for n in TWIN_WEIGHTS], *[new_v[n] for n in TWIN_WEIGHTS])
```

```python
import jax
import jax.numpy as jnp
from jax import lax
from jax.experimental import pallas as pl
from jax.experimental.pallas import tpu as pltpu

f32 = jnp.float32
bf16 = jnp.bfloat16

D_MODEL = 1024
HEAD_DIM = 64
W_CONV = 512
W_ATTN = 512
D_FF = 4096
PLE_DIM = 256
IN_COLS = 3 * W_CONV + 3 * W_ATTN
N_CHIPS = 4
EPS = 1e-6
ADAM_LR = 0.001
ADAM_B1 = 0.9
ADAM_B2 = 0.999
ADAM_EPS = 1e-08
ADAM_WD = 0.01
ADAM_STEP = 10

LANES = 128
TOKEN_TILE = 512
ATTN_TILE = 256
CONV_CHUNK = 512
CONV_W_ROWS = 16

MESH = pl.DeviceIdType.MESH
ANY = pl.BlockSpec(memory_space=pl.ANY)
NT = (((1,), (1,)), ((), ()))
TN = (((0,), (0,)), ((), ()))


def _arb(n):
    return pltpu.CompilerParams(dimension_semantics=("arbitrary",) * n)


def _sds(shape, dtype):
    return jax.ShapeDtypeStruct(shape, dtype)


def _dot(a, b, dims=None):
    if dims is None:
        return jnp.dot(a, b, preferred_element_type=f32)
    return lax.dot_general(a, b, dims, preferred_element_type=f32)


def _split_dot(x, ones):
    hi = x.astype(bf16)
    lo = (x - hi.astype(f32)).astype(bf16)
    return _dot(hi, ones) + _dot(lo, ones)


def _rms_fwd(h, g):
    rstd = lax.rsqrt(jnp.mean(h * h, axis=-1, keepdims=True) + EPS)
    return h * rstd * g, rstd


def _rms_bwd(dy, h, g):
    rstd = lax.rsqrt(jnp.mean(h * h, axis=-1, keepdims=True) + EPS)
    hn = h * rstd
    dyg = dy * g
    dh = rstd * (dyg - hn * jnp.mean(dyg * hn, axis=-1, keepdims=True))
    return dh, jnp.sum(dy * hn, axis=0, keepdims=True)


def _group_ones(n):
    r = lax.broadcasted_iota(jnp.int32, (n, n), 0) // HEAD_DIM
    c = lax.broadcasted_iota(jnp.int32, (n, n), 1) // HEAD_DIM
    return (r == c).astype(bf16)


def _head_rms_fwd(y, g, ones):
    rstd = lax.rsqrt(_split_dot(y * y, ones) * (1.0 / HEAD_DIM) + EPS)
    return y * rstd * g


def _head_rms_bwd(dy, y, g, ones):
    rstd = lax.rsqrt(_split_dot(y * y, ones) * (1.0 / HEAD_DIM) + EPS)
    yn = y * rstd
    dyg = dy * g
    dyy = rstd * (dyg - yn * (_split_dot(dyg * yn, ones) * (1.0 / HEAD_DIM)))
    return dyy, jnp.sum(dy * yn, axis=0, keepdims=True)


def _place():
    return lax.axis_index("x"), lax.axis_index("y"), lax.axis_index("c")


def _other_chips(x, y):
    return [(1 - x, y), (x, 1 - y), (1 - x, 1 - y)]


def _gather_weights(shards):
    n = len(shards)
    halves = [s.shape[0] // 2 for s in shards]

    def body(*refs):
        ins, outs = refs[:n], refs[n:2 * n]
        send_sems, recv_sems, local_sems = refs[2 * n:]
        x, y, c = _place()
        me = 2 * x + y
        sibling = (x, y, 1 - c)
        chips = _other_chips(x, y)

        def half(ref, i, which):
            return ref.at[pl.ds(which * halves[i], halves[i]), :]

        local = []
        for i in range(n):
            cp = pltpu.make_async_copy(ins[i], outs[i].at[me], local_sems.at[i])
            cp.start()
            local.append(cp)

        def over_ici(i, j, src, slot, to):
            return pltpu.make_async_remote_copy(
                src_ref=src, dst_ref=half(outs[i].at[slot], i, c),
                send_sem=send_sems.at[3 * i + j], recv_sem=recv_sems.at[3 * i + j],
                device_id=to, device_id_type=MESH)

        def to_sibling(i, j, slot, which):
            blk = half(outs[i].at[slot], i, which)
            return pltpu.make_async_remote_copy(
                src_ref=blk, dst_ref=blk,
                send_sem=send_sems.at[3 * n + 3 * i + j], recv_sem=recv_sems.at[3 * n + 3 * i + j],
                device_id=sibling, device_id_type=MESH)

        sent = []
        for i in range(n):
            for j, (px, py) in enumerate(chips):
                cp = over_ici(i, j, half(ins[i], i, c), me, (px, py, c))
                cp.start()
                sent.append(cp)
        for i in range(n):
            for j, (px, py) in enumerate(chips):
                slot = 2 * px + py
                over_ici(i, j, half(outs[i].at[slot], i, c), slot, (px, py, c)).wait_recv()
                cp = to_sibling(i, j, slot, c)
                cp.start()
                sent.append(cp)
        for i in range(n):
            for j, (px, py) in enumerate(chips):
                to_sibling(i, j, 2 * px + py, 1 - c).wait_recv()
        for cp in sent:
            cp.wait_send()
        for cp in local:
            cp.wait()

    return pl.pallas_call(
        body, name="gather_weights",
        out_shape=[_sds((N_CHIPS,) + s.shape, s.dtype) for s in shards],
        in_specs=[ANY] * n, out_specs=[ANY] * n,
        scratch_shapes=[pltpu.SemaphoreType.DMA((6 * n,)), pltpu.SemaphoreType.DMA((6 * n,)),
                        pltpu.SemaphoreType.DMA((n,))],
    )(*shards)


def _pair_exchange(grads):
    n = len(grads)

    def body(*refs):
        ins, outs = refs[:n], refs[n:2 * n]
        send_sems, recv_sems = refs[2 * n:]
        x, y, c = _place()
        sent = []
        for i in range(n):
            cp = pltpu.make_async_remote_copy(
                src_ref=ins[i].at[:, 1 - c], dst_ref=outs[i],
                send_sem=send_sems.at[i], recv_sem=recv_sems.at[i],
                device_id=(x, y, 1 - c), device_id_type=MESH)
            cp.start()
            sent.append(cp)
        for cp in sent:
            cp.wait()

    views = [g.reshape(N_CHIPS, 2, g.shape[1] // 2, g.shape[2]) for g in grads]
    return pl.pallas_call(
        body, name="grad_pair_exchange",
        out_shape=[_sds((N_CHIPS, v.shape[2], v.shape[3]), f32) for v in views],
        in_specs=[ANY] * n, out_specs=[ANY] * n,
        scratch_shapes=[pltpu.SemaphoreType.DMA((n,)), pltpu.SemaphoreType.DMA((n,))],
    )(*views)


def _chip_exchange(parts):
    n = len(parts)

    def body(*refs):
        ins, outs = refs[:n], refs[n:2 * n]
        send_sems, recv_sems, local_sems = refs[2 * n:]
        x, y, c = _place()
        me = 2 * x + y
        chips = _other_chips(x, y)
        local, sent = [], []
        for i in range(n):
            cp = pltpu.make_async_copy(ins[i].at[me], outs[i].at[me], local_sems.at[i])
            cp.start()
            local.append(cp)
            for j, (px, py) in enumerate(chips):
                cp = pltpu.make_async_remote_copy(
                    src_ref=ins[i].at[2 * px + py], dst_ref=outs[i].at[me],
                    send_sem=send_sems.at[3 * i + j], recv_sem=recv_sems.at[3 * i + j],
                    device_id=(px, py, c), device_id_type=MESH)
                cp.start()
                sent.append(cp)
        for i in range(n):
            for j, (px, py) in enumerate(chips):
                slot = outs[i].at[2 * px + py]
                pltpu.make_async_remote_copy(
                    src_ref=slot, dst_ref=slot,
                    send_sem=send_sems.at[3 * i + j], recv_sem=recv_sems.at[3 * i + j],
                    device_id=(px, py, c), device_id_type=MESH).wait_recv()
        for cp in sent:
            cp.wait_send()
        for cp in local:
            cp.wait()

    return pl.pallas_call(
        body, name="grad_chip_exchange",
        out_shape=[_sds(p.shape, f32) for p in parts],
        in_specs=[ANY] * n, out_specs=[ANY] * n,
        scratch_shapes=[pltpu.SemaphoreType.DMA((3 * n,)), pltpu.SemaphoreType.DMA((3 * n,)),
                        pltpu.SemaphoreType.DMA((n,))],
    )(*parts)


def _sibling_exchange(halves):
    n = len(halves)

    def body(*refs):
        ins, outs = refs[:n], refs[n:2 * n]
        send_sems, recv_sems, local_sems = refs[2 * n:]
        x, y, c = _place()
        started = []
        for i in range(n):
            cp = pltpu.make_async_copy(ins[i], outs[i].at[c], local_sems.at[i])
            cp.start()
            started.append(cp)
            cp = pltpu.make_async_remote_copy(
                src_ref=ins[i], dst_ref=outs[i].at[c],
                send_sem=send_sems.at[i], recv_sem=recv_sems.at[i],
                device_id=(x, y, 1 - c), device_id_type=MESH)
            cp.start()
            started.append(cp)
        for cp in started:
            cp.wait()

    return pl.pallas_call(
        body, name="grad_sibling_exchange",
        out_shape=[_sds((2,) + h.shape, f32) for h in halves],
        in_specs=[ANY] * n, out_specs=[ANY] * n,
        scratch_shapes=[pltpu.SemaphoreType.DMA((n,)), pltpu.SemaphoreType.DMA((n,)),
                        pltpu.SemaphoreType.DMA((n,))],
    )(*halves)


def _row_tile(rows, cols):
    t = rows
    while t * cols * 4 > (1 << 20) and t % 16 == 0:
        t //= 2
    return t


def _pair_sum(grad, recv, core, name):
    _, r, c = grad.shape
    hr = r // 2
    tr = _row_tile(hr, c)
    view = grad.reshape(N_CHIPS, 2, hr, c)

    def body(core_ref, mine_ref, recv_ref, out_ref):
        out_ref[...] = mine_ref[...] + recv_ref[...]

    return pl.pallas_call(
        body, name=name,
        grid_spec=pltpu.PrefetchScalarGridSpec(
            num_scalar_prefetch=1, grid=(N_CHIPS, hr // tr),
            in_specs=[pl.BlockSpec((None, None, tr, c), lambda k, t, core_ref: (k, core_ref[0], t, 0)),
                      pl.BlockSpec((None, tr, c), lambda k, t, core_ref: (k, t, 0))],
            out_specs=pl.BlockSpec((None, tr, c), lambda k, t, core_ref: (k, t, 0))),
        out_shape=_sds((N_CHIPS, hr, c), f32),
        compiler_params=_arb(2),
    )(core, view, recv)


def _chip_sum(parts, name):
    _, hr, c = parts.shape
    tr = _row_tile(hr, c)

    def body(p_ref, out_ref):
        out_ref[...] = ((p_ref[0] + p_ref[1]) + p_ref[2]) + p_ref[3]

    return pl.pallas_call(
        body, name=name, grid=(hr // tr,),
        in_specs=[pl.BlockSpec((N_CHIPS, tr, c), lambda t: (0, t, 0))],
        out_specs=pl.BlockSpec((tr, c), lambda t: (t, 0)),
        out_shape=_sds((hr, c), f32), compiler_params=_arb(1),
    )(parts)


def _adamw(g, w, m, v, name):
    r, c = g.shape
    tr = _row_tile(r, c)

    def body(g_ref, w_ref, m_ref, v_ref, d_ref, nm_ref, nv_ref):
        gv = g_ref[...]
        mv = ADAM_B1 * m_ref[...] + (1.0 - ADAM_B1) * gv
        vv = ADAM_B2 * v_ref[...] + (1.0 - ADAM_B2) * jnp.square(gv)
        m_hat = mv / (1.0 - ADAM_B1 ** ADAM_STEP)
        v_hat = vv / (1.0 - ADAM_B2 ** ADAM_STEP)
        d_ref[...] = -ADAM_LR * (m_hat / (jnp.sqrt(v_hat) + ADAM_EPS) + ADAM_WD * w_ref[...])
        nm_ref[...] = mv
        nv_ref[...] = vv

    spec = pl.BlockSpec((tr, c), lambda t: (t, 0))
    return pl.pallas_call(
        body, name=name, grid=(r // tr,), in_specs=[spec] * 4, out_specs=[spec] * 3,
        out_shape=[_sds((r, c), f32)] * 3, compiler_params=_arb(1),
    )(g, w, m, v)


def _allreduce_small(packed):
    shape = packed.shape

    def body(x_ref, out_ref, buf, send_sems, recv_sems):
        x, y, c = _place()
        me = 4 * x + 2 * y + c
        buf[me] = x_ref[...]
        sent = []
        for r in range(1, 8):
            dx, dy, dc = (r >> 2) & 1, (r >> 1) & 1, r & 1
            peer = ((1 - x) if dx else x, (1 - y) if dy else y, (1 - c) if dc else c)
            cp = pltpu.make_async_remote_copy(
                src_ref=x_ref, dst_ref=buf.at[me],
                send_sem=send_sems.at[r], recv_sem=recv_sems.at[r],
                device_id=peer, device_id_type=MESH)
            cp.start()
            sent.append((cp, peer))
        for r, (cp, peer) in enumerate(sent, start=1):
            src = 4 * peer[0] + 2 * peer[1] + peer[2]
            pltpu.make_async_remote_copy(
                src_ref=x_ref, dst_ref=buf.at[src],
                send_sem=send_sems.at[r], recv_sem=recv_sems.at[r],
                device_id=peer, device_id_type=MESH).wait_recv()
        for cp, _ in sent:
            cp.wait_send()
        total = buf[0]
        for k in range(1, 8):
            total = total + buf[k]
        out_ref[...] = total

    vmem = pl.BlockSpec(memory_space=pltpu.VMEM)
    return pl.pallas_call(
        body, name="allreduce_small", out_shape=_sds(shape, f32),
        in_specs=[vmem], out_specs=vmem,
        scratch_shapes=[pltpu.VMEM((8,) + shape, f32), pltpu.SemaphoreType.DMA((8,)),
                        pltpu.SemaphoreType.DMA((8,))],
    )(packed)


def _in_proj(x, g_mix, w_in, tm):
    s = x.shape[0]
    ncol = IN_COLS // N_CHIPS

    def body(x_ref, g_ref, w_ref, proj_ref, a_ref):
        a, _ = _rms_fwd(x_ref[...], g_ref[...])
        ab = a.astype(bf16)
        a_ref[...] = ab
        for k in range(N_CHIPS):
            proj_ref[:, k * ncol:(k + 1) * ncol] = _dot(ab, w_ref[k])

    return pl.pallas_call(
        body, name="in_proj", grid=(s // tm,),
        in_specs=[pl.BlockSpec((tm, D_MODEL), lambda i: (i, 0)),
                  pl.BlockSpec((1, D_MODEL), lambda i: (0, 0)),
                  pl.BlockSpec((N_CHIPS, D_MODEL, ncol), lambda i: (0, 0, 0))],
        out_specs=[pl.BlockSpec((tm, IN_COLS), lambda i: (i, 0)),
                   pl.BlockSpec((tm, D_MODEL), lambda i: (i, 0))],
        out_shape=[_sds((s, IN_COLS), f32), _sds((s, D_MODEL), bf16)],
        compiler_params=_arb(1),
    )(x, g_mix, w_in)


def _shifted(prev8, cur, shift):
    ext = jnp.concatenate([prev8, cur], axis=0)
    return pltpu.roll(ext, shift, axis=0)[8:]


def _conv_fwd(proj, conv_w, g_conv):
    s = proj.shape[0]
    nblk = W_CONV // LANES
    rc = min(CONV_CHUNK, s)

    def body(cb_ref, cc_ref, cu_ref, w_ref, g_ref, out_ref):
        ones = _group_ones(LANES)
        w0, w1, w2 = w_ref[0:1, :], w_ref[1:2, :], w_ref[2:3, :]
        g = g_ref[...]

        def chunk(i, carry):
            r0 = pl.multiple_of(i * rc, rc)
            rows = pl.ds(r0, rc)
            prev = pl.ds(pl.multiple_of(jnp.maximum(r0 - 8, 0), 8), 8)
            v = cc_ref[rows, :] * cu_ref[rows, :]
            vp = jnp.where(i > 0, cc_ref[prev, :] * cu_ref[prev, :], 0.0)
            y = w2 * v + w1 * _shifted(vp, v, 1) + w0 * _shifted(vp, v, 2)
            out_ref[rows, :] = _head_rms_fwd(cb_ref[rows, :] * y, g, ones).astype(bf16)
            return carry

        lax.fori_loop(0, s // rc, chunk, 0)

    def col(off):
        return pl.BlockSpec((s, LANES), lambda j: (0, off + j))

    return pl.pallas_call(
        body, name="conv_fwd", grid=(nblk,),
        in_specs=[col(0), col(nblk), col(2 * nblk),
                  pl.BlockSpec((None, CONV_W_ROWS, LANES), lambda j: (j, 0, 0)),
                  pl.BlockSpec((1, LANES), lambda j: (0, j))],
        out_specs=pl.BlockSpec((s, LANES), lambda j: (0, j)),
        out_shape=_sds((s, W_CONV), bf16), compiler_params=_arb(1),
    )(proj, proj, proj, conv_w, g_conv)


def _log_keep(z):
    return -(jnp.maximum(z, 0.0) + jnp.log1p(jnp.exp(-jnp.abs(z))))


def _head_pair_masks(rows):
    lane = lax.broadcasted_iota(jnp.int32, (rows, LANES), 1)
    return lane < HEAD_DIM


def _attn_fwd(proj):
    s = proj.shape[0]
    t = ATTN_TILE
    npair = W_ATTN // LANES
    qoff, koff, voff = 3 * W_CONV // LANES, 3 * W_CONV // LANES + npair, 3 * W_CONV // LANES + 2 * npair
    suffix = (lax.broadcasted_iota(jnp.int32, (t, t), 0) >= lax.broadcasted_iota(jnp.int32, (t, t), 1)).astype(bf16)

    def body(q_ref, k_ref, v_ref, u_ref, o_ref, tot_ref, acc):
        qb = pl.program_id(1)
        first = _head_pair_masks(t)
        q = q_ref[...] * (HEAD_DIM ** -0.5)
        qh = (jnp.where(first, q, 0.0).astype(bf16), jnp.where(first, 0.0, q).astype(bf16))
        ones = u_ref[...]
        below = lax.broadcasted_iota(jnp.int32, (t, t), 1) < lax.broadcasted_iota(jnp.int32, (t, t), 0)
        acc[...] = jnp.zeros_like(acc)

        def block(kb, diagonal, later):
            rows = pl.ds(pl.multiple_of(kb * t, t), t)
            k = k_ref[rows, :].astype(bf16)
            v = v_ref[rows, :].astype(bf16)
            out = []
            for h in range(2):
                z = _dot(qh[h], k, NT)
                lk = _log_keep(z)
                if diagonal:
                    lk = jnp.where(below, lk, 0.0)
                incl = _split_dot(lk, ones)
                a = jnp.exp(z + later[h] + incl)
                if diagonal:
                    a = jnp.where(below, a, 0.0)
                acc[h] += _dot(a.astype(bf16), v)
                out.append(later[h] + incl[:, 0:1])
            return tuple(out)

        zero = jnp.zeros((t, 1), f32)
        later = block(qb, True, (zero, zero))
        later = lax.fori_loop(0, qb, lambda i, c: block(qb - 1 - i, False, c), later)
        o_ref[...] = jnp.where(first, acc[0], acc[1])
        tot_ref[...] = jnp.where(first, later[0], later[1])

    return pl.pallas_call(
        body, name="attn_fwd", grid=(npair, s // t),
        in_specs=[pl.BlockSpec((t, LANES), lambda p, i: (i, qoff + p)),
                  pl.BlockSpec((s, LANES), lambda p, i: (0, koff + p)),
                  pl.BlockSpec((s, LANES), lambda p, i: (0, voff + p)),
                  pl.BlockSpec((t, t), lambda p, i: (0, 0))],
        out_specs=[pl.BlockSpec((t, LANES), lambda p, i: (i, p)),
                   pl.BlockSpec((None, t, LANES), lambda p, i: (p, i, 0))],
        out_shape=[_sds((s, W_ATTN), f32), _sds((npair, s, LANES), f32)],
        scratch_shapes=[pltpu.VMEM((2, t, LANES), f32)],
        compiler_params=_arb(2),
    )(proj, proj, proj, suffix)


def _out_proj(o, conv_n, x, g_attn, w_out, tm):
    s = x.shape[0]

    def body(o_ref, c_ref, x_ref, g_ref, w_ref, h_ref, cat_ref):
        ones = _group_ones(LANES)
        cat_ref[:, :W_CONV] = c_ref[...]
        for j in range(W_ATTN // LANES):
            cols = slice(j * LANES, (j + 1) * LANES)
            cat_ref[:, W_CONV + j * LANES:W_CONV + (j + 1) * LANES] = _head_rms_fwd(
                o_ref[:, cols], g_ref[:, cols], ones).astype(bf16)
        h_ref[...] = x_ref[...] + _dot(cat_ref[...], w_ref[...])

    return pl.pallas_call(
        body, name="out_proj", grid=(s // tm,),
        in_specs=[pl.BlockSpec((tm, W_ATTN), lambda i: (i, 0)),
                  pl.BlockSpec((tm, W_CONV), lambda i: (i, 0)),
                  pl.BlockSpec((tm, D_MODEL), lambda i: (i, 0)),
                  pl.BlockSpec((1, W_ATTN), lambda i: (0, 0)),
                  pl.BlockSpec((D_MODEL, D_MODEL), lambda i: (0, 0))],
        out_specs=[pl.BlockSpec((tm, D_MODEL), lambda i: (i, 0)),
                   pl.BlockSpec((tm, D_MODEL), lambda i: (i, 0))],
        out_shape=[_sds((s, D_MODEL), f32), _sds((s, D_MODEL), bf16)],
        compiler_params=_arb(1),
    )(o, conv_n, x, g_attn, w_out)


def _mlp_fwd(h1, g_mlp, w_up, w_down, tm):
    s = h1.shape[0]
    fc = D_FF // N_CHIPS

    def body(h_ref, g_ref, wu_ref, wd_ref, h2_ref, u_ref, m_ref):
        j = pl.program_id(1)

        @pl.when(j == 0)
        def _():
            m, _ = _rms_fwd(h_ref[...], g_ref[...])
            m_ref[...] = m.astype(bf16)
            h2_ref[...] = h_ref[...]

        u = _dot(m_ref[...], wu_ref[...])
        u_ref[...] = u.astype(bf16)
        h2_ref[...] += _dot(jnp.square(jnp.maximum(u, 0.0)).astype(bf16), wd_ref[...])

    return pl.pallas_call(
        body, name="mlp_fwd", grid=(s // tm, N_CHIPS),
        in_specs=[pl.BlockSpec((tm, D_MODEL), lambda i, j: (i, 0)),
                  pl.BlockSpec((1, D_MODEL), lambda i, j: (0, 0)),
                  pl.BlockSpec((None, D_MODEL, fc), lambda i, j: (j, 0, 0)),
                  pl.BlockSpec((None, fc, D_MODEL), lambda i, j: (j, 0, 0))],
        out_specs=[pl.BlockSpec((tm, D_MODEL), lambda i, j: (i, 0)),
                   pl.BlockSpec((tm, fc), lambda i, j: (i, j)),
                   pl.BlockSpec((tm, D_MODEL), lambda i, j: (i, 0))],
        out_shape=[_sds((s, D_MODEL), f32), _sds((s, D_FF), bf16), _sds((s, D_MODEL), bf16)],
        compiler_params=_arb(2),
    )(h1, g_mlp, w_up, w_down)


def _tail(h2, p, target, g_ple, g_final, w_gate, w_proj, tm):
    s = h2.shape[0]
    pc = D_MODEL // N_CHIPS

    def body(h_ref, p_ref, t_ref, gp_ref, gf_ref, wg_ref, wp_ref,
             dh_ref, dhb_ref, n3_ref, dgl_ref, dpp_ref, pb_ref, ggp_ref, ggf_ref, loss_ref, pp_ref):
        i = pl.program_id(0)
        h2v = h_ref[...]
        n3, _ = _rms_fwd(h2v, gp_ref[...])
        n3b = n3.astype(bf16)
        n3_ref[...] = n3b
        gate = jax.nn.sigmoid(_dot(n3b, wg_ref[...]))
        pb = p_ref[...].astype(bf16)
        pb_ref[...] = pb
        for k in range(N_CHIPS):
            pp_ref[:, k * pc:(k + 1) * pc] = _dot(pb, wp_ref[k])
        pp = pp_ref[...]
        h3 = h2v + gate * pp
        yv, _ = _rms_fwd(h3, gf_ref[...])
        err = yv - t_ref[...]
        loss = 0.5 * jnp.sum(err * err) * (1.0 / D_MODEL)
        dh3, ggf = _rms_bwd(err * (1.0 / D_MODEL), h3, gf_ref[...])
        dpp_ref[...] = (dh3 * gate).astype(bf16)
        dgl = (dh3 * pp * gate * (1.0 - gate)).astype(bf16)
        dgl_ref[...] = dgl
        dn3 = _dot(dgl, wg_ref[...], NT)
        dh2n, ggp = _rms_bwd(dn3, h2v, gp_ref[...])
        dh2 = dh3 + dh2n
        dh_ref[...] = dh2
        dhb_ref[...] = dh2.astype(bf16)

        @pl.when(i == 0)
        def _():
            ggp_ref[...] = jnp.zeros_like(ggp_ref)
            ggf_ref[...] = jnp.zeros_like(ggf_ref)
            loss_ref[...] = jnp.zeros_like(loss_ref)

        ggp_ref[...] += ggp
        ggf_ref[...] += ggf
        loss_ref[...] += jnp.full(loss_ref.shape, loss, f32)

    tok = lambda w: pl.BlockSpec((tm, w), lambda i: (i, 0))
    vec = lambda w: pl.BlockSpec((1, w), lambda i: (0, 0))
    return pl.pallas_call(
        body, name="tail", grid=(s // tm,),
        in_specs=[tok(D_MODEL), tok(PLE_DIM), tok(D_MODEL), vec(D_MODEL), vec(D_MODEL),
                  pl.BlockSpec((D_MODEL, D_MODEL), lambda i: (0, 0)),
                  pl.BlockSpec((N_CHIPS, PLE_DIM, pc), lambda i: (0, 0, 0))],
        out_specs=[tok(D_MODEL), tok(D_MODEL), tok(D_MODEL), tok(D_MODEL), tok(D_MODEL), tok(PLE_DIM),
                   vec(D_MODEL), vec(D_MODEL), vec(LANES)],
        out_shape=[_sds((s, D_MODEL), f32), _sds((s, D_MODEL), bf16), _sds((s, D_MODEL), bf16),
                   _sds((s, D_MODEL), bf16), _sds((s, D_MODEL), bf16), _sds((s, PLE_DIM), bf16),
                   _sds((1, D_MODEL), f32), _sds((1, D_MODEL), f32), _sds((1, LANES), f32)],
        scratch_shapes=[pltpu.VMEM((tm, D_MODEL), f32)],
        compiler_params=_arb(1),
    )(h2, p, target, g_ple, g_final, w_gate, w_proj)


def _mlp_bwd(dh2, dh2b, h1, u, g_mlp, w_up, w_down, tm):
    s = h1.shape[0]
    fc = D_FF // N_CHIPS

    def body(dh_ref, dhb_ref, h_ref, u_ref, g_ref, wu_ref, wd_ref, dh1_ref, dh1b_ref, du_ref, gg_ref, dm):
        i, j = pl.program_id(0), pl.program_id(1)

        @pl.when(j == 0)
        def _():
            dm[...] = jnp.zeros_like(dm)

        dr = _dot(dhb_ref[...], wd_ref[...], NT)
        du = (dr * (2.0 * jnp.maximum(u_ref[...].astype(f32), 0.0))).astype(bf16)
        du_ref[...] = du
        dm[...] += _dot(du, wu_ref[...], NT)

        @pl.when((i == 0) & (j == 0))
        def _():
            gg_ref[...] = jnp.zeros_like(gg_ref)

        @pl.when(j == N_CHIPS - 1)
        def _():
            dh1n, gg = _rms_bwd(dm[...], h_ref[...], g_ref[...])
            dh1 = dh_ref[...] + dh1n
            dh1_ref[...] = dh1
            dh1b_ref[...] = dh1.astype(bf16)
            gg_ref[...] += gg

    tok = pl.BlockSpec((tm, D_MODEL), lambda i, j: (i, 0))
    ffb = pl.BlockSpec((tm, fc), lambda i, j: (i, j))
    vec = pl.BlockSpec((1, D_MODEL), lambda i, j: (0, 0))
    return pl.pallas_call(
        body, name="mlp_bwd", grid=(s // tm, N_CHIPS),
        in_specs=[tok, tok, tok, ffb, vec,
                  pl.BlockSpec((None, D_MODEL, fc), lambda i, j: (j, 0, 0)),
                  pl.BlockSpec((None, fc, D_MODEL), lambda i, j: (j, 0, 0))],
        out_specs=[tok, tok, ffb, vec],
        out_shape=[_sds((s, D_MODEL), f32), _sds((s, D_MODEL), bf16), _sds((s, D_FF), bf16),
                   _sds((1, D_MODEL), f32)],
        scratch_shapes=[pltpu.VMEM((tm, D_MODEL), f32)],
        compiler_params=_arb(2),
    )(dh2, dh2b, h1, u, g_mlp, w_up, w_down)


def _out_proj_bwd(dh1b, o, g_attn, w_out, tm):
    s = o.shape[0]

    def body(dh_ref, o_ref, g_ref, w_ref, dc_ref, do_ref, gg_ref, dcat):
        i = pl.program_id(0)
        ones = _group_ones(LANES)
        dcat[...] = _dot(dh_ref[...], w_ref[...], NT)
        dc_ref[...] = dcat[:, :W_CONV]

        @pl.when(i == 0)
        def _():
            gg_ref[...] = jnp.zeros_like(gg_ref)

        for j in range(W_ATTN // LANES):
            cols = slice(j * LANES, (j + 1) * LANES)
            d, gg = _head_rms_bwd(dcat[:, W_CONV + j * LANES:W_CONV + (j + 1) * LANES],
                                  o_ref[:, cols], g_ref[:, cols], ones)
            do_ref[:, cols] = d
            gg_ref[:, cols] += gg

    return pl.pallas_call(
        body, name="out_proj_bwd", grid=(s // tm,),
        in_specs=[pl.BlockSpec((tm, D_MODEL), lambda i: (i, 0)),
                  pl.BlockSpec((tm, W_ATTN), lambda i: (i, 0)),
                  pl.BlockSpec((1, W_ATTN), lambda i: (0, 0)),
                  pl.BlockSpec((D_MODEL, D_MODEL), lambda i: (0, 0))],
        out_specs=[pl.BlockSpec((tm, W_CONV), lambda i: (i, 0)),
                   pl.BlockSpec((tm, W_ATTN), lambda i: (i, 0)),
                   pl.BlockSpec((1, W_ATTN), lambda i: (0, 0))],
        out_shape=[_sds((s, W_CONV), f32), _sds((s, W_ATTN), f32), _sds((1, W_ATTN), f32)],
        scratch_shapes=[pltpu.VMEM((tm, D_MODEL), f32)],
        compiler_params=_arb(1),
    )(dh1b, o, g_attn, w_out)


def _attn_bwd(proj, do, tot):
    s = proj.shape[0]
    t = ATTN_TILE
    npair = W_ATTN // LANES
    nq = s // t
    qoff, koff, voff = 3 * W_CONV // LANES, 3 * W_CONV // LANES + npair, 3 * W_CONV // LANES + 2 * npair
    prefix = (lax.broadcasted_iota(jnp.int32, (t, t), 0) <= lax.broadcasted_iota(jnp.int32, (t, t), 1)).astype(bf16)

    def body(q_ref, k_ref, v_ref, do_ref, tot_ref, l_ref, dq_ref, dk_ref, dv_ref, dq_acc, dk_acc, dv_acc):
        qb = pl.program_id(1)
        first = _head_pair_masks(t)
        q = q_ref[...] * (HEAD_DIM ** -0.5)
        qh = (jnp.where(first, q, 0.0).astype(bf16), jnp.where(first, 0.0, q).astype(bf16))
        dov = do_ref[...]
        doh = (jnp.where(first, dov, 0.0).astype(bf16), jnp.where(first, 0.0, dov).astype(bf16))
        totv = tot_ref[...]
        total = (totv[:, 0:1], totv[:, HEAD_DIM:HEAD_DIM + 1])
        ones = l_ref[...]
        below = lax.broadcasted_iota(jnp.int32, (t, t), 1) < lax.broadcasted_iota(jnp.int32, (t, t), 0)

        @pl.when(qb == 0)
        def _():
            dk_acc[...] = jnp.zeros_like(dk_acc)
            dv_acc[...] = jnp.zeros_like(dv_acc)

        dq_acc[...] = jnp.zeros_like(dq_acc)

        def block(kb, diagonal, carry):
            rows = pl.ds(pl.multiple_of(kb * t, t), t)
            k = k_ref[rows, :].astype(bf16)
            v = v_ref[rows, :].astype(bf16)
            dk_blk = jnp.zeros((t, LANES), f32)
            dv_blk = jnp.zeros((t, LANES), f32)
            out = []
            for h in range(2):
                lk_before, g_before = carry[2 * h], carry[2 * h + 1]
                z = _dot(qh[h], k, NT)
                lk = _log_keep(z)
                if diagonal:
                    lk = jnp.where(below, lk, 0.0)
                incl = _split_dot(lk, ones)
                log_beta = z + lk
                a = jnp.exp(log_beta + (total[h] - lk_before - incl))
                beta = jnp.exp(log_beta)
                if diagonal:
                    a = jnp.where(below, a, 0.0)
                    beta = jnp.where(below, beta, 0.0)
                ab = a.astype(bf16)
                g = a * _dot(doh[h], v, NT)
                g_incl = _split_dot(g, ones)
                dz = g - beta * (g_before + g_incl)
                if diagonal:
                    dz = jnp.where(below, dz, 0.0)
                dzb = dz.astype(bf16)
                dv_blk += _dot(ab, doh[h], TN)
                dk_blk += _dot(dzb, qh[h], TN)
                dq_acc[h] += _dot(dzb, k)
                out += [lk_before + incl[:, t - 1:t], g_before + g_incl[:, t - 1:t]]
            dk_acc[rows, :] += dk_blk
            dv_acc[rows, :] += dv_blk
            return tuple(out)

        zero = jnp.zeros((t, 1), f32)
        carry = lax.fori_loop(0, qb, lambda kb, c: block(kb, False, c), (zero, zero, zero, zero))
        block(qb, True, carry)
        dq_ref[...] = (jnp.where(first, dq_acc[0], dq_acc[1]) * (HEAD_DIM ** -0.5)).astype(bf16)

        @pl.when(qb == nq - 1)
        def _():
            dk_ref[...] = dk_acc[...].astype(bf16)
            dv_ref[...] = dv_acc[...].astype(bf16)

    return pl.pallas_call(
        body, name="attn_bwd", grid=(npair, nq),
        in_specs=[pl.BlockSpec((t, LANES), lambda p, i: (i, qoff + p)),
                  pl.BlockSpec((s, LANES), lambda p, i: (0, koff + p)),
                  pl.BlockSpec((s, LANES), lambda p, i: (0, voff + p)),
                  pl.BlockSpec((t, LANES), lambda p, i: (i, p)),
                  pl.BlockSpec((None, t, LANES), lambda p, i: (p, i, 0)),
                  pl.BlockSpec((t, t), lambda p, i: (0, 0))],
        out_specs=[pl.BlockSpec((t, LANES), lambda p, i: (i, p)),
                   pl.BlockSpec((s, LANES), lambda p, i: (0, p)),
                   pl.BlockSpec((s, LANES), lambda p, i: (0, p))],
        out_shape=[_sds((s, W_ATTN), bf16)] * 3,
        scratch_shapes=[pltpu.VMEM((2, t, LANES), f32), pltpu.VMEM((s, LANES), f32), pltpu.VMEM((s, LANES), f32)],
        compiler_params=_arb(2),
    )(proj, proj, proj, do, tot, prefix)


def _conv_bwd(proj, dcn, conv_w, g_conv):
    s = proj.shape[0]
    nblk = W_CONV // LANES
    rc = min(CONV_CHUNK, s)
    nchunk = s // rc

    def body(cb_ref, cc_ref, cu_ref, d_ref, w_ref, g_ref, dcb_ref, dcc_ref, dcu_ref, gw_ref, gg_ref, dy_buf):
        ones = _group_ones(LANES)
        w0, w1, w2 = w_ref[0:1, :], w_ref[1:2, :], w_ref[2:3, :]
        g = g_ref[...]

        def first_pass(i, carry):
            gw0, gw1, gw2, gg = carry
            r0 = pl.multiple_of(i * rc, rc)
            rows = pl.ds(r0, rc)
            prev = pl.ds(pl.multiple_of(jnp.maximum(r0 - 8, 0), 8), 8)
            v = cc_ref[rows, :] * cu_ref[rows, :]
            vp = jnp.where(i > 0, cc_ref[prev, :] * cu_ref[prev, :], 0.0)
            v1, v2 = _shifted(vp, v, 1), _shifted(vp, v, 2)
            y = w2 * v + w1 * v1 + w0 * v2
            cb = cb_ref[rows, :]
            dcy, ggi = _head_rms_bwd(d_ref[rows, :], cb * y, g, ones)
            dcb_ref[rows, :] = (dcy * y).astype(bf16)
            dy = dcy * cb
            dy_buf[rows, :] = dy
            return (gw0 + jnp.sum(dy * v2, axis=0, keepdims=True), gw1 + jnp.sum(dy * v1, axis=0, keepdims=True),
                    gw2 + jnp.sum(dy * v, axis=0, keepdims=True), gg + ggi)

        zero = jnp.zeros((1, LANES), f32)
        gw0, gw1, gw2, gg = lax.fori_loop(0, nchunk, first_pass, (zero, zero, zero, zero))
        gw_ref[...] = jnp.zeros_like(gw_ref)
        gw_ref[0:1, :] = gw0
        gw_ref[1:2, :] = gw1
        gw_ref[2:3, :] = gw2
        gg_ref[...] = gg

        def second_pass(i, carry):
            r0 = pl.multiple_of(i * rc, rc)
            rows = pl.ds(r0, rc)
            nxt = pl.ds(pl.multiple_of(jnp.minimum(r0 + rc, s - 8), 8), 8)
            dy = dy_buf[rows, :]
            dyn = jnp.where(i < nchunk - 1, dy_buf[nxt, :], 0.0)
            ext = jnp.concatenate([dy, dyn], axis=0)
            up1 = pltpu.roll(ext, rc + 8 - 1, axis=0)[:rc]
            up2 = pltpu.roll(ext, rc + 8 - 2, axis=0)[:rc]
            dv = w2 * dy + w1 * up1 + w0 * up2
            dcc_ref[rows, :] = (dv * cu_ref[rows, :]).astype(bf16)
            dcu_ref[rows, :] = (dv * cc_ref[rows, :]).astype(bf16)
            return carry

        lax.fori_loop(0, nchunk, second_pass, 0)

    def col(off):
        return pl.BlockSpec((s, LANES), lambda j: (0, off + j))

    return pl.pallas_call(
        body, name="conv_bwd", grid=(nblk,),
        in_specs=[col(0), col(nblk), col(2 * nblk), col(0),
                  pl.BlockSpec((None, CONV_W_ROWS, LANES), lambda j: (j, 0, 0)),
                  pl.BlockSpec((1, LANES), lambda j: (0, j))],
        out_specs=[col(0), col(0), col(0),
                   pl.BlockSpec((None, CONV_W_ROWS, LANES), lambda j: (j, 0, 0)),
                   pl.BlockSpec((1, LANES), lambda j: (0, j))],
        out_shape=[_sds((s, W_CONV), bf16)] * 3 + [_sds((nblk, CONV_W_ROWS, LANES), f32), _sds((1, W_CONV), f32)],
        scratch_shapes=[pltpu.VMEM((s, LANES), f32)],
        compiler_params=_arb(1),
    )(proj, proj, proj, dcn, conv_w, g_conv)


def _in_proj_bwd(dproj, dh1, x, g_mix, w_in, tm):
    s = x.shape[0]
    ncol = IN_COLS // N_CHIPS

    def body(dp_ref, dh_ref, x_ref, g_ref, w_ref, dx_ref, gg_ref):
        i = pl.program_id(0)
        da = _dot(dp_ref[:, 0:ncol], w_ref[0], NT)
        for k in range(1, N_CHIPS):
            da += _dot(dp_ref[:, k * ncol:(k + 1) * ncol], w_ref[k], NT)
        dxn, gg = _rms_bwd(da, x_ref[...], g_ref[...])
        dx_ref[...] = dh_ref[...] + dxn

        @pl.when(i == 0)
        def _():
            gg_ref[...] = jnp.zeros_like(gg_ref)

        gg_ref[...] += gg

    return pl.pallas_call(
        body, name="in_proj_bwd", grid=(s // tm,),
        in_specs=[pl.BlockSpec((tm, IN_COLS), lambda i: (i, 0)),
                  pl.BlockSpec((tm, D_MODEL), lambda i: (i, 0)),
                  pl.BlockSpec((tm, D_MODEL), lambda i: (i, 0)),
                  pl.BlockSpec((1, D_MODEL), lambda i: (0, 0)),
                  pl.BlockSpec((N_CHIPS, D_MODEL, ncol), lambda i: (0, 0, 0))],
        out_specs=[pl.BlockSpec((tm, D_MODEL), lambda i: (i, 0)),
                   pl.BlockSpec((1, D_MODEL), lambda i: (0, 0))],
        out_shape=[_sds((s, D_MODEL), f32), _sds((1, D_MODEL), f32)],
        compiler_params=_arb(1),
    )(dproj, dh1, x, g_mix, w_in)


def _weight_grad(a, b, bm, bn, ts, name, relu_sq=False):
    s, m = a.shape
    n = b.shape[1]
    nn = n // bn

    def body(a_ref, b_ref, o_ref):
        @pl.when(pl.program_id(2) == 0)
        def _():
            o_ref[...] = jnp.zeros_like(o_ref)

        av = a_ref[...]
        if relu_sq:
            av = jnp.square(jnp.maximum(av.astype(f32), 0.0)).astype(bf16)
        o_ref[...] += _dot(av, b_ref[...], TN)

    return pl.pallas_call(
        body, name=name, grid=(m // bm, nn, s // ts),
        in_specs=[pl.BlockSpec((ts, bm), lambda i, j, k: (k, i)),
                  pl.BlockSpec((ts, bn), lambda i, j, k: (k, j))],
        out_specs=pl.BlockSpec((None, bm, bn), lambda i, j, k: (i * nn + j, 0, 0)),
        out_shape=_sds(((m // bm) * nn, bm, bn), f32),
        compiler_params=_arb(3),
    )(a, b)


def kernel(x, p, g_mix, w_in, conv_w, g_conv_out, g_attn_out, w_out, g_mlp, w_up, w_down, g_ple, w_ple_gate, w_ple_proj, g_final, loss_target, m_g_mix, m_w_in, m_conv_w, m_g_conv_out, m_g_attn_out, m_w_out, m_g_mlp, m_w_up, m_w_down, m_g_ple, m_w_ple_gate, m_w_ple_proj, m_g_final, v_g_mix, v_w_in, v_conv_w, v_g_conv_out, v_g_attn_out, v_w_out, v_g_mlp, v_w_up, v_w_down, v_g_ple, v_w_ple_gate, v_w_ple_proj, v_g_final):
    s = x.shape[1]
    tm = min(TOKEN_TILE, s)
    xs = x.reshape(s, D_MODEL)
    ps = p.reshape(s, PLE_DIM)
    target = loss_target.reshape(s, D_MODEL)
    core = lax.axis_index("c").astype(jnp.int32).reshape(1)
    chip = 2 * lax.axis_index("x") + lax.axis_index("y")

    big = {"w_in": w_in[0], "w_out": w_out[0], "w_up": w_up[0], "w_down": w_down[0],
           "w_ple_gate": w_ple_gate[0], "w_ple_proj": w_ple_proj[0]}
    names = list(big)
    conv_shard = jnp.pad(conv_w[0], ((0, CONV_W_ROWS - conv_w.shape[1]), (0, 0)))
    full = _gather_weights([big[k].astype(bf16) for k in names] + [conv_shard])
    w_in_f, w_out_f, w_up_f, w_down_f, w_gate_f, w_proj_f, conv_f = full
    w_out_f = w_out_f.reshape(D_MODEL, D_MODEL)
    w_gate_f = w_gate_f.reshape(D_MODEL, D_MODEL)

    proj, a_b = _in_proj(xs, g_mix, w_in_f, tm)
    conv_n = _conv_fwd(proj, conv_f, g_conv_out)
    o, tot = _attn_fwd(proj)
    h1, cat_b = _out_proj(o, conv_n, xs, g_attn_out, w_out_f, tm)
    h2, u_b, m_b = _mlp_fwd(h1, g_mlp, w_up_f, w_down_f, tm)

    dh2, dh2_b, n3_b, dgl_b, dpp_b, p_b, gg_ple, gg_final, loss_row = _tail(
        h2, ps, target, g_ple, g_final.reshape(1, D_MODEL), w_gate_f, w_proj_f, tm)
    dh1, dh1_b, du_b, gg_mlp = _mlp_bwd(dh2, dh2_b, h1, u_b, g_mlp, w_up_f, w_down_f, tm)
    dcn, do, gg_attn = _out_proj_bwd(dh1_b, o, g_attn_out, w_out_f, tm)
    dq, dk, dv = _attn_bwd(proj, do, tot)
    dcb, dcc, dcu, g_conv_w, gg_conv = _conv_bwd(proj, dcn, conv_f, g_conv_out)
    dproj = jnp.concatenate([dcb, dcc, dcu, dq, dk, dv], axis=1)
    grad_x, gg_mix = _in_proj_bwd(dproj, dh1, xs, g_mix, w_in_f, tm)

    part = {
        "w_in": _weight_grad(a_b, dproj, D_MODEL, IN_COLS // N_CHIPS, tm, "grad_w_in"),
        "w_out": _weight_grad(cat_b, dh1_b, D_MODEL, D_MODEL, tm, "grad_w_out").reshape(N_CHIPS, D_MODEL // N_CHIPS, D_MODEL),
        "w_up": _weight_grad(m_b, du_b, D_MODEL, D_FF // N_CHIPS, tm, "grad_w_up"),
        "w_down": _weight_grad(u_b, dh2_b, D_FF // N_CHIPS, D_MODEL, tm, "grad_w_down", relu_sq=True),
        "w_ple_gate": _weight_grad(n3_b, dgl_b, D_MODEL, D_MODEL, tm, "grad_w_ple_gate").reshape(N_CHIPS, D_MODEL // N_CHIPS, D_MODEL),
        "w_ple_proj": _weight_grad(p_b, dpp_b, PLE_DIM, D_MODEL // N_CHIPS, tm, "grad_w_ple_proj"),
    }

    from_sibling = _pair_exchange([part[k] for k in names])
    pair = [_pair_sum(part[k], r, core, "pair_sum_" + k) for k, r in zip(names, from_sibling)]
    from_chips = _chip_exchange(pair)
    half = [_chip_sum(c, "chip_sum_" + k) for k, c in zip(names, from_chips)]
    both = _sibling_exchange(half)
    grad = {k: b.reshape(big[k].shape) for k, b in zip(names, both)}

    gcw = g_conv_w[:, :3, :].transpose(1, 0, 2).reshape(3, W_CONV)
    row = lambda *parts: jnp.concatenate(parts, axis=1)
    packed = jnp.concatenate([
        gg_mix, gg_mlp, gg_ple, gg_final, row(gg_conv, gg_attn), row(gcw[0:1], gcw[1:2]),
        row(gcw[2:3], loss_row, jnp.zeros((1, W_CONV - LANES), f32)), jnp.zeros((1, D_MODEL), f32)], axis=0)
    summed = _allreduce_small(packed)
    loss = summed[6, W_CONV]
    gcw_full = jnp.stack([summed[5, :W_CONV], summed[5, W_CONV:], summed[6, :W_CONV]])
    grad["conv_w"] = lax.dynamic_slice(gcw_full, (0, chip * LANES), (3, LANES))
    vec_names = ["g_mix", "g_mlp", "g_ple", "g_final"]
    vec_w = {"g_mix": g_mix, "g_mlp": g_mlp, "g_ple": g_ple, "g_final": g_final.reshape(1, D_MODEL)}
    vec_m = {"g_mix": m_g_mix, "g_mlp": m_g_mlp, "g_ple": m_g_ple, "g_final": m_g_final.reshape(1, D_MODEL)}
    vec_v = {"g_mix": v_g_mix, "g_mlp": v_g_mlp, "g_ple": v_g_ple, "g_final": v_g_final.reshape(1, D_MODEL)}

    def pack_vec(d, conv, attn):
        return jnp.concatenate([d[k] for k in vec_names] + [row(conv, attn)], axis=0)

    vec_g = summed[0:5]
    vec_d, vec_nm, vec_nv = _adamw(vec_g, pack_vec(vec_w, g_conv_out, g_attn_out),
                                   pack_vec(vec_m, m_g_conv_out, m_g_attn_out),
                                   pack_vec(vec_v, v_g_conv_out, v_g_attn_out), "adamw_vectors")

    given_w = dict(big, conv_w=conv_w[0])
    given_m = {"w_in": m_w_in[0], "w_out": m_w_out[0], "w_up": m_w_up[0], "w_down": m_w_down[0],
               "w_ple_gate": m_w_ple_gate[0], "w_ple_proj": m_w_ple_proj[0], "conv_w": m_conv_w[0]}
    given_v = {"w_in": v_w_in[0], "w_out": v_w_out[0], "w_up": v_w_up[0], "w_down": v_w_down[0],
               "w_ple_gate": v_w_ple_gate[0], "w_ple_proj": v_w_ple_proj[0], "conv_w": v_conv_w[0]}
    delta, new_m, new_v = {}, {}, {}
    for k in names + ["conv_w"]:
        delta[k], new_m[k], new_v[k] = _adamw(grad[k], given_w[k], given_m[k], given_v[k], "adamw_" + k)

    def unpack(vals, kind):
        out = {k: vals[i:i + 1] for i, k in enumerate(vec_names)}
        out["g_final"] = out["g_final"].reshape(D_MODEL)
        out["g_conv_out"] = vals[4:5, :W_CONV]
        out["g_attn_out"] = vals[4:5, W_CONV:]
        out.update({k: v[None] for k, v in kind.items()})
        return out

    order = ["g_mix", "w_in", "conv_w", "g_conv_out", "g_attn_out", "w_out", "g_mlp", "w_up", "w_down",
             "g_ple", "w_ple_gate", "w_ple_proj", "g_final"]
    groups = [unpack(vec_g, grad), unpack(vec_d, delta), unpack(vec_nm, new_m), unpack(vec_nv, new_v)]
    return (loss, grad_x[None]) + tuple(g[k] for g in groups for k in order)
```

```python
import jax
import jax.numpy as jnp
from jax import lax
from jax.experimental import pallas as pl
from jax.experimental.pallas import tpu as pltpu

f32 = jnp.float32
bf16 = jnp.bfloat16

D_MODEL = 1024
HEAD_DIM = 64
W_CONV = 512
W_ATTN = 512
D_FF = 4096
PLE_DIM = 256
IN_COLS = 3 * W_CONV + 3 * W_ATTN
N_CHIPS = 4
EPS = 1e-6
ADAM_LR = 0.001
ADAM_B1 = 0.9
ADAM_B2 = 0.999
ADAM_EPS = 1e-08
ADAM_WD = 0.01
ADAM_STEP = 10

LANES = 128
TOKEN_TILE = 512
ATTN_TILE = 256
CONV_CHUNK = 512
CONV_W_ROWS = 16

MESH = pl.DeviceIdType.MESH
ANY = pl.BlockSpec(memory_space=pl.ANY)
NT = (((1,), (1,)), ((), ()))
TN = (((0,), (0,)), ((), ()))


def _arb(n):
    return pltpu.CompilerParams(dimension_semantics=("arbitrary",) * n)


def _sds(shape, dtype):
    return jax.ShapeDtypeStruct(shape, dtype)


def _dot(a, b, dims=None):
    if dims is None:
        return jnp.dot(a, b, preferred_element_type=f32)
    return lax.dot_general(a, b, dims, preferred_element_type=f32)


def _split_dot(x, ones):
    hi = x.astype(bf16)
    lo = (x - hi.astype(f32)).astype(bf16)
    return _dot(hi, ones) + _dot(lo, ones)


def _rms_fwd(h, g):
    rstd = lax.rsqrt(jnp.mean(h * h, axis=-1, keepdims=True) + EPS)
    return h * rstd * g, rstd


def _rms_bwd(dy, h, g):
    rstd = lax.rsqrt(jnp.mean(h * h, axis=-1, keepdims=True) + EPS)
    hn = h * rstd
    dyg = dy * g
    dh = rstd * (dyg - hn * jnp.mean(dyg * hn, axis=-1, keepdims=True))
    return dh, jnp.sum(dy * hn, axis=0, keepdims=True)


def _group_ones(n):
    r = lax.broadcasted_iota(jnp.int32, (n, n), 0) // HEAD_DIM
    c = lax.broadcasted_iota(jnp.int32, (n, n), 1) // HEAD_DIM
    return (r == c).astype(bf16)


def _head_rms_fwd(y, g, ones):
    rstd = lax.rsqrt(_split_dot(y * y, ones) * (1.0 / HEAD_DIM) + EPS)
    return y * rstd * g


def _head_rms_bwd(dy, y, g, ones):
    rstd = lax.rsqrt(_split_dot(y * y, ones) * (1.0 / HEAD_DIM) + EPS)
    yn = y * rstd
    dyg = dy * g
    dyy = rstd * (dyg - yn * (_split_dot(dyg * yn, ones) * (1.0 / HEAD_DIM)))
    return dyy, jnp.sum(dy * yn, axis=0, keepdims=True)


def _place():
    return lax.axis_index("x"), lax.axis_index("y"), lax.axis_index("c")


def _other_chips(x, y):
    return [(1 - x, y), (x, 1 - y), (1 - x, 1 - y)]


def _gather_weights(shards):
    n = len(shards)
    halves = [s.shape[0] // 2 for s in shards]

    def body(*refs):
        ins, outs = refs[:n], refs[n:2 * n]
        send_sems, recv_sems, local_sems = refs[2 * n:]
        x, y, c = _place()
        me = 2 * x + y
        sibling = (x, y, 1 - c)
        chips = _other_chips(x, y)

        def half(ref, i, which):
            return ref.at[pl.ds(which * halves[i], halves[i]), :]

        local = []
        for i in range(n):
            cp = pltpu.make_async_copy(ins[i], outs[i].at[me], local_sems.at[i])
            cp.start()
            local.append(cp)

        def over_ici(i, j, src, slot, to):
            return pltpu.make_async_remote_copy(
                src_ref=src, dst_ref=half(outs[i].at[slot], i, c),
                send_sem=send_sems.at[3 * i + j], recv_sem=recv_sems.at[3 * i + j],
                device_id=to, device_id_type=MESH)

        def to_sibling(i, j, slot, which):
            blk = half(outs[i].at[slot], i, which)
            return pltpu.make_async_remote_copy(
                src_ref=blk, dst_ref=blk,
                send_sem=send_sems.at[3 * n + 3 * i + j], recv_sem=recv_sems.at[3 * n + 3 * i + j],
                device_id=sibling, device_id_type=MESH)

        sent = []
        for i in range(n):
            for j, (px, py) in enumerate(chips):
                cp = over_ici(i, j, half(ins[i], i, c), me, (px, py, c))
                cp.start()
                sent.append(cp)
        for i in range(n):
            for j, (px, py) in enumerate(chips):
                slot = 2 * px + py
                over_ici(i, j, half(outs[i].at[slot], i, c), slot, (px, py, c)).wait_recv()
                cp = to_sibling(i, j, slot, c)
                cp.start()
                sent.append(cp)
        for i in range(n):
            for j, (px, py) in enumerate(chips):
                to_sibling(i, j, 2 * px + py, 1 - c).wait_recv()
        for cp in sent:
            cp.wait_send()
        for cp in local:
            cp.wait()

    return pl.pallas_call(
        body, name="gather_weights",
        out_shape=[_sds((N_CHIPS,) + s.shape, s.dtype) for s in shards],
        in_specs=[ANY] * n, out_specs=[ANY] * n,
        scratch_shapes=[pltpu.SemaphoreType.DMA((6 * n,)), pltpu.SemaphoreType.DMA((6 * n,)),
                        pltpu.SemaphoreType.DMA((n,))],
    )(*shards)


def _pair_exchange(grads):
    n = len(grads)

    def body(*refs):
        ins, outs = refs[:n], refs[n:2 * n]
        send_sems, recv_sems = refs[2 * n:]
        x, y, c = _place()
        sent = []
        for i in range(n):
            cp = pltpu.make_async_remote_copy(
                src_ref=ins[i].at[:, 1 - c], dst_ref=outs[i],
                send_sem=send_sems.at[i], recv_sem=recv_sems.at[i],
                device_id=(x, y, 1 - c), device_id_type=MESH)
            cp.start()
            sent.append(cp)
        for cp in sent:
            cp.wait()

    views = [g.reshape(N_CHIPS, 2, g.shape[1] // 2, g.shape[2]) for g in grads]
    return pl.pallas_call(
        body, name="grad_pair_exchange",
        out_shape=[_sds((N_CHIPS, v.shape[2], v.shape[3]), f32) for v in views],
        in_specs=[ANY] * n, out_specs=[ANY] * n,
        scratch_shapes=[pltpu.SemaphoreType.DMA((n,)), pltpu.SemaphoreType.DMA((n,))],
    )(*views)


def _chip_exchange(parts):
    n = len(parts)

    def body(*refs):
        ins, outs = refs[:n], refs[n:2 * n]
        send_sems, recv_sems, local_sems = refs[2 * n:]
        x, y, c = _place()
        me = 2 * x + y
        chips = _other_chips(x, y)
        local, sent = [], []
        for i in range(n):
            cp = pltpu.make_async_copy(ins[i].at[me], outs[i].at[me], local_sems.at[i])
            cp.start()
            local.append(cp)
            for j, (px, py) in enumerate(chips):
                cp = pltpu.make_async_remote_copy(
                    src_ref=ins[i].at[2 * px + py], dst_ref=outs[i].at[me],
                    send_sem=send_sems.at[3 * i + j], recv_sem=recv_sems.at[3 * i + j],
                    device_id=(px, py, c), device_id_type=MESH)
                cp.start()
                sent.append(cp)
        for i in range(n):
            for j, (px, py) in enumerate(chips):
                slot = outs[i].at[2 * px + py]
                pltpu.make_async_remote_copy(
                    src_ref=slot, dst_ref=slot,
                    send_sem=send_sems.at[3 * i + j], recv_sem=recv_sems.at[3 * i + j],
                    device_id=(px, py, c), device_id_type=MESH).wait_recv()
        for cp in sent:
            cp.wait_send()
        for cp in local:
            cp.wait()

    return pl.pallas_call(
        body, name="grad_chip_exchange",
        out_shape=[_sds(p.shape, f32) for p in parts],
        in_specs=[ANY] * n, out_specs=[ANY] * n,
        scratch_shapes=[pltpu.SemaphoreType.DMA((3 * n,)), pltpu.SemaphoreType.DMA((3 * n,)),
                        pltpu.SemaphoreType.DMA((n,))],
    )(*parts)


def _sibling_exchange(halves):
    n = len(halves)

    def body(*refs):
        ins, outs = refs[:n], refs[n:2 * n]
        send_sems, recv_sems, local_sems = refs[2 * n:]
        x, y, c = _place()
        started = []
        for i in range(n):
            cp = pltpu.make_async_copy(ins[i], outs[i].at[c], local_sems.at[i])
            cp.start()
            started.append(cp)
            cp = pltpu.make_async_remote_copy(
                src_ref=ins[i], dst_ref=outs[i].at[c],
                send_sem=send_sems.at[i], recv_sem=recv_sems.at[i],
                device_id=(x, y, 1 - c), device_id_type=MESH)
            cp.start()
            started.append(cp)
        for cp in started:
            cp.wait()

    return pl.pallas_call(
        body, name="grad_sibling_exchange",
        out_shape=[_sds((2,) + h.shape, f32) for h in halves],
        in_specs=[ANY] * n, out_specs=[ANY] * n,
        scratch_shapes=[pltpu.SemaphoreType.DMA((n,)), pltpu.SemaphoreType.DMA((n,)),
                        pltpu.SemaphoreType.DMA((n,))],
    )(*halves)


def _row_tile(rows, cols):
    t = rows
    while t * cols * 4 > (1 << 20) and t % 16 == 0:
        t //= 2
    return t


def _pair_sum(grad, recv, core, name):
    _, r, c = grad.shape
    hr = r // 2
    tr = _row_tile(hr, c)
    view = grad.reshape(N_CHIPS, 2, hr, c)

    def body(core_ref, mine_ref, recv_ref, out_ref):
        out_ref[...] = mine_ref[...] + recv_ref[...]

    return pl.pallas_call(
        body, name=name,
        grid_spec=pltpu.PrefetchScalarGridSpec(
            num_scalar_prefetch=1, grid=(N_CHIPS, hr // tr),
            in_specs=[pl.BlockSpec((None, None, tr, c), lambda k, t, core_ref: (k, core_ref[0], t, 0)),
                      pl.BlockSpec((None, tr, c), lambda k, t, core_ref: (k, t, 0))],
            out_specs=pl.BlockSpec((None, tr, c), lambda k, t, core_ref: (k, t, 0))),
        out_shape=_sds((N_CHIPS, hr, c), f32),
        compiler_params=_arb(2),
    )(core, view, recv)


def _chip_sum(parts, name):
    _, hr, c = parts.shape
    tr = _row_tile(hr, c)

    def body(p_ref, out_ref):
        out_ref[...] = ((p_ref[0] + p_ref[1]) + p_ref[2]) + p_ref[3]

    return pl.pallas_call(
        body, name=name, grid=(hr // tr,),
        in_specs=[pl.BlockSpec((N_CHIPS, tr, c), lambda t: (0, t, 0))],
        out_specs=pl.BlockSpec((tr, c), lambda t: (t, 0)),
        out_shape=_sds((hr, c), f32), compiler_params=_arb(1),
    )(parts)


def _adamw(g, w, m, v, name):
    r, c = g.shape
    tr = _row_tile(r, c)

    def body(g_ref, w_ref, m_ref, v_ref, d_ref, nm_ref, nv_ref):
        gv = g_ref[...]
        mv = ADAM_B1 * m_ref[...] + (1.0 - ADAM_B1) * gv
        vv = ADAM_B2 * v_ref[...] + (1.0 - ADAM_B2) * jnp.square(gv)
        m_hat = mv / (1.0 - ADAM_B1 ** ADAM_STEP)
        v_hat = vv / (1.0 - ADAM_B2 ** ADAM_STEP)
        d_ref[...] = -ADAM_LR * (m_hat / (jnp.sqrt(v_hat) + ADAM_EPS) + ADAM_WD * w_ref[...])
        nm_ref[...] = mv
        nv_ref[...] = vv

    spec = pl.BlockSpec((tr, c), lambda t: (t, 0))
    return pl.pallas_call(
        body, name=name, grid=(r // tr,), in_specs=[spec] * 4, out_specs=[spec] * 3,
        out_shape=[_sds((r, c), f32)] * 3, compiler_params=_arb(1),
    )(g, w, m, v)


def _allreduce_small(packed):
    shape = packed.shape

    def body(x_ref, out_ref, buf, send_sems, recv_sems):
        x, y, c = _place()
        me = 4 * x + 2 * y + c
        buf[me] = x_ref[...]
        sent = []
        for r in range(1, 8):
            dx, dy, dc = (r >> 2) & 1, (r >> 1) & 1, r & 1
            peer = ((1 - x) if dx else x, (1 - y) if dy else y, (1 - c) if dc else c)
            cp = pltpu.make_async_remote_copy(
                src_ref=x_ref, dst_ref=buf.at[me],
                send_sem=send_sems.at[r], recv_sem=recv_sems.at[r],
                device_id=peer, device_id_type=MESH)
            cp.start()
            sent.append((cp, peer))
        for r, (cp, peer) in enumerate(sent, start=1):
            src = 4 * peer[0] + 2 * peer[1] + peer[2]
            pltpu.make_async_remote_copy(
                src_ref=x_ref, dst_ref=buf.at[src],
                send_sem=send_sems.at[r], recv_sem=recv_sems.at[r],
                device_id=peer, device_id_type=MESH).wait_recv()
        for cp, _ in sent:
            cp.wait_send()
        total = buf[0]
        for k in range(1, 8):
            total = total + buf[k]
        out_ref[...] = total

    vmem = pl.BlockSpec(memory_space=pltpu.VMEM)
    return pl.pallas_call(
        body, name="allreduce_small", out_shape=_sds(shape, f32),
        in_specs=[vmem], out_specs=vmem,
        scratch_shapes=[pltpu.VMEM((8,) + shape, f32), pltpu.SemaphoreType.DMA((8,)),
                        pltpu.SemaphoreType.DMA((8,))],
    )(packed)


def _in_proj(x, g_mix, w_in, tm):
    s = x.shape[0]
    ncol = IN_COLS // N_CHIPS

    def body(x_ref, g_ref, w_ref, proj_ref, a_ref):
        a, _ = _rms_fwd(x_ref[...], g_ref[...])
        ab = a.astype(bf16)
        a_ref[...] = ab
        for k in range(N_CHIPS):
            proj_ref[:, k * ncol:(k + 1) * ncol] = _dot(ab, w_ref[k])

    return pl.pallas_call(
        body, name="in_proj", grid=(s // tm,),
        in_specs=[pl.BlockSpec((tm, D_MODEL), lambda i: (i, 0)),
                  pl.BlockSpec((1, D_MODEL), lambda i: (0, 0)),
                  pl.BlockSpec((N_CHIPS, D_MODEL, ncol), lambda i: (0, 0, 0))],
        out_specs=[pl.BlockSpec((tm, IN_COLS), lambda i: (i, 0)),
                   pl.BlockSpec((tm, D_MODEL), lambda i: (i, 0))],
        out_shape=[_sds((s, IN_COLS), f32), _sds((s, D_MODEL), bf16)],
        compiler_params=_arb(1),
    )(x, g_mix, w_in)


def _shifted(prev8, cur, shift):
    ext = jnp.concatenate([prev8, cur], axis=0)
    return pltpu.roll(ext, shift, axis=0)[8:]


def _conv_fwd(proj, conv_w, g_conv):
    s = proj.shape[0]
    nblk = W_CONV // LANES
    rc = min(CONV_CHUNK, s)

    def body(cb_ref, cc_ref, cu_ref, w_ref, g_ref, out_ref):
        ones = _group_ones(LANES)
        w0, w1, w2 = w_ref[0:1, :], w_ref[1:2, :], w_ref[2:3, :]
        g = g_ref[...]

        def chunk(i, carry):
            r0 = pl.multiple_of(i * rc, rc)
            rows = pl.ds(r0, rc)
            prev = pl.ds(pl.multiple_of(jnp.maximum(r0 - 8, 0), 8), 8)
            v = cc_ref[rows, :] * cu_ref[rows, :]
            vp = jnp.where(i > 0, cc_ref[prev, :] * cu_ref[prev, :], 0.0)
            y = w2 * v + w1 * _shifted(vp, v, 1) + w0 * _shifted(vp, v, 2)
            out_ref[rows, :] = _head_rms_fwd(cb_ref[rows, :] * y, g, ones).astype(bf16)
            return carry

        lax.fori_loop(0, s // rc, chunk, 0)

    def col(off):
        return pl.BlockSpec((s, LANES), lambda j: (0, off + j))

    return pl.pallas_call(
        body, name="conv_fwd", grid=(nblk,),
        in_specs=[col(0), col(nblk), col(2 * nblk),
                  pl.BlockSpec((None, CONV_W_ROWS, LANES), lambda j: (j, 0, 0)),
                  pl.BlockSpec((1, LANES), lambda j: (0, j))],
        out_specs=pl.BlockSpec((s, LANES), lambda j: (0, j)),
        out_shape=_sds((s, W_CONV), bf16), compiler_params=_arb(1),
    )(proj, proj, proj, conv_w, g_conv)


LOG2_E = 1.4426950408889634


def _log2_keep(z2):
    nz2 = -z2
    return jnp.minimum(nz2, 0.0) - jnp.log2(1.0 + jnp.exp2(jnp.minimum(z2, nz2)))


def _head_pair_masks(rows):
    lane = lax.broadcasted_iota(jnp.int32, (rows, LANES), 1)
    return lane < HEAD_DIM


SUBLANES = 8
KEY_RUN = ATTN_TILE // SUBLANES


def _permute_keys(a):
    s, w = a.shape
    return a.reshape(s // ATTN_TILE, SUBLANES, KEY_RUN, w).transpose(0, 2, 1, 3).reshape(s, w)


def _unpermute_keys(a):
    s, w = a.shape
    return a.reshape(s // ATTN_TILE, KEY_RUN, SUBLANES, w).transpose(0, 2, 1, 3).reshape(s, w)


def _causal_tiles():
    r = lax.broadcasted_iota(jnp.int32, (ATTN_TILE, ATTN_TILE), 0)
    key = (r % SUBLANES) * KEY_RUN + r // SUBLANES
    below = key < lax.broadcasted_iota(jnp.int32, (ATTN_TILE, ATTN_TILE), 1)
    return below.astype(f32), jnp.where(below, 0.0, -1e30).astype(f32)


def _sublane_scan(x, reverse):
    row = lax.broadcasted_iota(jnp.int32, x.shape, 0)
    inc = x
    for sh in (1, 2, 4):
        if reverse:
            inc = inc + jnp.where(row < SUBLANES - sh, pltpu.roll(inc, SUBLANES - sh, axis=0), 0.0)
        else:
            inc = inc + jnp.where(row >= sh, pltpu.roll(inc, sh, axis=0), 0.0)
    return inc - x


def _attn_fwd(proj, kp, vt):
    s = proj.shape[0]
    t = ATTN_TILE
    nblk = s // t
    npair = W_ATTN // LANES
    qoff = 3 * W_CONV // LANES
    keep01, keepneg = _causal_tiles()

    def body(q_ref, k_ref, vt_ref, m01_ref, neg_ref, o_ref, tot_ref, w_s, a_s, acc):
        qb = pl.program_id(1)
        first = _head_pair_masks(t)
        q = q_ref[...] * (HEAD_DIM ** -0.5)
        qh = (jnp.where(first, q, 0.0).astype(bf16), jnp.where(first, 0.0, q).astype(bf16))
        acc[...] = jnp.zeros_like(acc)
        a_s[1] = jnp.zeros((t, t), bf16)

        def scores(kb, h):
            w_s[h] = _dot(k_ref[kb], qh[h], NT)

        def weigh(kb, h):
            acc[h] += _dot(vt_ref[kb], a_s[h])

        def weights(h, diagonal, later):
            run = jnp.zeros((SUBLANES, t), f32)
            for a in reversed(range(KEY_RUN)):
                rows = slice(SUBLANES * a, SUBLANES * (a + 1))
                z2 = w_s[h, rows, :] * LOG2_E
                lk = _log2_keep(z2)
                if diagonal:
                    lk = lk * m01_ref[rows, :]
                run = run + lk
                w_s[h, rows, :] = z2 + run
            off = _sublane_scan(run, reverse=True) + later
            off2 = jnp.concatenate([off, off], axis=0)
            for a in range(t // (2 * SUBLANES)):
                rows = slice(2 * SUBLANES * a, 2 * SUBLANES * (a + 1))
                w = w_s[h, rows, :] + off2
                if diagonal:
                    w = w + neg_ref[rows, :]
                a_s[h, rows, :] = jnp.exp2(w).astype(bf16)
            return later + jnp.sum(run, axis=0, keepdims=True)

        def block(kb, before, after, diagonal, later):
            scores(kb, 1)
            weigh(before, 1)
            l0 = weights(0, diagonal, later[0])
            scores(after, 0)
            weigh(kb, 0)
            l1 = weights(1, diagonal, later[1])
            return l0, l1

        zero = jnp.zeros((1, t), f32)
        scores(qb, 0)
        later = block(qb, qb, jnp.maximum(qb - 1, 0), True, (zero, zero))

        def earlier(i, c):
            kb = qb - 1 - i
            return block(kb, kb + 1, jnp.maximum(kb - 1, 0), False, c)

        later = lax.fori_loop(0, qb, earlier, later)
        weigh(0, 1)
        top = lax.broadcasted_iota(jnp.int32, (LANES, t), 0) < HEAD_DIM
        o_ref[...] = jnp.where(top, acc[0], acc[1]).T
        tot_ref[...] = jnp.concatenate([later[0], later[1], jnp.zeros((SUBLANES - 2, t), f32)], axis=0)

    return pl.pallas_call(
        body, name="attn_fwd", grid=(npair, nblk),
        in_specs=[pl.BlockSpec((t, LANES), lambda p, i: (i, qoff + p)),
                  pl.BlockSpec((nblk, t, LANES), lambda p, i: (0, 0, p)),
                  pl.BlockSpec((nblk, LANES, t), lambda p, i: (0, p, 0)),
                  pl.BlockSpec((t, t), lambda p, i: (0, 0)),
                  pl.BlockSpec((t, t), lambda p, i: (0, 0))],
        out_specs=[pl.BlockSpec((t, LANES), lambda p, i: (i, p)),
                   pl.BlockSpec((None, SUBLANES, t), lambda p, i: (p, 0, i))],
        out_shape=[_sds((s, W_ATTN), f32), _sds((npair, SUBLANES, s), f32)],
        scratch_shapes=[pltpu.VMEM((2, t, t), f32), pltpu.VMEM((2, t, t), bf16), pltpu.VMEM((2, LANES, t), f32)],
        compiler_params=_arb(2),
    )(proj, kp, vt, keep01, keepneg)


def _out_proj(o, conv_n, x, g_attn, w_out, tm):
    s = x.shape[0]

    def body(o_ref, c_ref, x_ref, g_ref, w_ref, h_ref, cat_ref):
        ones = _group_ones(LANES)
        cat_ref[:, :W_CONV] = c_ref[...]
        for j in range(W_ATTN // LANES):
            cols = slice(j * LANES, (j + 1) * LANES)
            cat_ref[:, W_CONV + j * LANES:W_CONV + (j + 1) * LANES] = _head_rms_fwd(
                o_ref[:, cols], g_ref[:, cols], ones).astype(bf16)
        h_ref[...] = x_ref[...] + _dot(cat_ref[...], w_ref[...])

    return pl.pallas_call(
        body, name="out_proj", grid=(s // tm,),
        in_specs=[pl.BlockSpec((tm, W_ATTN), lambda i: (i, 0)),
                  pl.BlockSpec((tm, W_CONV), lambda i: (i, 0)),
                  pl.BlockSpec((tm, D_MODEL), lambda i: (i, 0)),
                  pl.BlockSpec((1, W_ATTN), lambda i: (0, 0)),
                  pl.BlockSpec((D_MODEL, D_MODEL), lambda i: (0, 0))],
        out_specs=[pl.BlockSpec((tm, D_MODEL), lambda i: (i, 0)),
                   pl.BlockSpec((tm, D_MODEL), lambda i: (i, 0))],
        out_shape=[_sds((s, D_MODEL), f32), _sds((s, D_MODEL), bf16)],
        compiler_params=_arb(1),
    )(o, conv_n, x, g_attn, w_out)


def _mlp_fwd(h1, g_mlp, w_up, w_down, tm):
    s = h1.shape[0]
    fc = D_FF // N_CHIPS

    def body(h_ref, g_ref, wu_ref, wd_ref, h2_ref, u_ref, m_ref):
        j = pl.program_id(1)

        @pl.when(j == 0)
        def _():
            m, _ = _rms_fwd(h_ref[...], g_ref[...])
            m_ref[...] = m.astype(bf16)
            h2_ref[...] = h_ref[...]

        u = _dot(m_ref[...], wu_ref[...])
        u_ref[...] = u.astype(bf16)
        h2_ref[...] += _dot(jnp.square(jnp.maximum(u, 0.0)).astype(bf16), wd_ref[...])

    return pl.pallas_call(
        body, name="mlp_fwd", grid=(s // tm, N_CHIPS),
        in_specs=[pl.BlockSpec((tm, D_MODEL), lambda i, j: (i, 0)),
                  pl.BlockSpec((1, D_MODEL), lambda i, j: (0, 0)),
                  pl.BlockSpec((None, D_MODEL, fc), lambda i, j: (j, 0, 0)),
                  pl.BlockSpec((None, fc, D_MODEL), lambda i, j: (j, 0, 0))],
        out_specs=[pl.BlockSpec((tm, D_MODEL), lambda i, j: (i, 0)),
                   pl.BlockSpec((tm, fc), lambda i, j: (i, j)),
                   pl.BlockSpec((tm, D_MODEL), lambda i, j: (i, 0))],
        out_shape=[_sds((s, D_MODEL), f32), _sds((s, D_FF), bf16), _sds((s, D_MODEL), bf16)],
        compiler_params=_arb(2),
    )(h1, g_mlp, w_up, w_down)


def _tail(h2, p, target, g_ple, g_final, w_gate, w_proj, tm):
    s = h2.shape[0]
    pc = D_MODEL // N_CHIPS

    def body(h_ref, p_ref, t_ref, gp_ref, gf_ref, wg_ref, wp_ref,
             dh_ref, dhb_ref, n3_ref, dgl_ref, dpp_ref, pb_ref, ggp_ref, ggf_ref, loss_ref, pp_ref):
        i = pl.program_id(0)
        h2v = h_ref[...]
        n3, _ = _rms_fwd(h2v, gp_ref[...])
        n3b = n3.astype(bf16)
        n3_ref[...] = n3b
        gate = jax.nn.sigmoid(_dot(n3b, wg_ref[...]))
        pb = p_ref[...].astype(bf16)
        pb_ref[...] = pb
        for k in range(N_CHIPS):
            pp_ref[:, k * pc:(k + 1) * pc] = _dot(pb, wp_ref[k])
        pp = pp_ref[...]
        h3 = h2v + gate * pp
        yv, _ = _rms_fwd(h3, gf_ref[...])
        err = yv - t_ref[...]
        loss = 0.5 * jnp.sum(err * err) * (1.0 / D_MODEL)
        dh3, ggf = _rms_bwd(err * (1.0 / D_MODEL), h3, gf_ref[...])
        dpp_ref[...] = (dh3 * gate).astype(bf16)
        dgl = (dh3 * pp * gate * (1.0 - gate)).astype(bf16)
        dgl_ref[...] = dgl
        dn3 = _dot(dgl, wg_ref[...], NT)
        dh2n, ggp = _rms_bwd(dn3, h2v, gp_ref[...])
        dh2 = dh3 + dh2n
        dh_ref[...] = dh2
        dhb_ref[...] = dh2.astype(bf16)

        @pl.when(i == 0)
        def _():
            ggp_ref[...] = jnp.zeros_like(ggp_ref)
            ggf_ref[...] = jnp.zeros_like(ggf_ref)
            loss_ref[...] = jnp.zeros_like(loss_ref)

        ggp_ref[...] += ggp
        ggf_ref[...] += ggf
        loss_ref[...] += jnp.full(loss_ref.shape, loss, f32)

    tok = lambda w: pl.BlockSpec((tm, w), lambda i: (i, 0))
    vec = lambda w: pl.BlockSpec((1, w), lambda i: (0, 0))
    return pl.pallas_call(
        body, name="tail", grid=(s // tm,),
        in_specs=[tok(D_MODEL), tok(PLE_DIM), tok(D_MODEL), vec(D_MODEL), vec(D_MODEL),
                  pl.BlockSpec((D_MODEL, D_MODEL), lambda i: (0, 0)),
                  pl.BlockSpec((N_CHIPS, PLE_DIM, pc), lambda i: (0, 0, 0))],
        out_specs=[tok(D_MODEL), tok(D_MODEL), tok(D_MODEL), tok(D_MODEL), tok(D_MODEL), tok(PLE_DIM),
                   vec(D_MODEL), vec(D_MODEL), vec(LANES)],
        out_shape=[_sds((s, D_MODEL), f32), _sds((s, D_MODEL), bf16), _sds((s, D_MODEL), bf16),
                   _sds((s, D_MODEL), bf16), _sds((s, D_MODEL), bf16), _sds((s, PLE_DIM), bf16),
                   _sds((1, D_MODEL), f32), _sds((1, D_MODEL), f32), _sds((1, LANES), f32)],
        scratch_shapes=[pltpu.VMEM((tm, D_MODEL), f32)],
        compiler_params=_arb(1),
    )(h2, p, target, g_ple, g_final, w_gate, w_proj)


def _mlp_bwd(dh2, dh2b, h1, u, g_mlp, w_up, w_down, tm):
    s = h1.shape[0]
    fc = D_FF // N_CHIPS

    def body(dh_ref, dhb_ref, h_ref, u_ref, g_ref, wu_ref, wd_ref, dh1_ref, dh1b_ref, du_ref, gg_ref, dm):
        i, j = pl.program_id(0), pl.program_id(1)

        @pl.when(j == 0)
        def _():
            dm[...] = jnp.zeros_like(dm)

        dr = _dot(dhb_ref[...], wd_ref[...], NT)
        du = (dr * (2.0 * jnp.maximum(u_ref[...].astype(f32), 0.0))).astype(bf16)
        du_ref[...] = du
        dm[...] += _dot(du, wu_ref[...], NT)

        @pl.when((i == 0) & (j == 0))
        def _():
            gg_ref[...] = jnp.zeros_like(gg_ref)

        @pl.when(j == N_CHIPS - 1)
        def _():
            dh1n, gg = _rms_bwd(dm[...], h_ref[...], g_ref[...])
            dh1 = dh_ref[...] + dh1n
            dh1_ref[...] = dh1
            dh1b_ref[...] = dh1.astype(bf16)
            gg_ref[...] += gg

    tok = pl.BlockSpec((tm, D_MODEL), lambda i, j: (i, 0))
    ffb = pl.BlockSpec((tm, fc), lambda i, j: (i, j))
    vec = pl.BlockSpec((1, D_MODEL), lambda i, j: (0, 0))
    return pl.pallas_call(
        body, name="mlp_bwd", grid=(s // tm, N_CHIPS),
        in_specs=[tok, tok, tok, ffb, vec,
                  pl.BlockSpec((None, D_MODEL, fc), lambda i, j: (j, 0, 0)),
                  pl.BlockSpec((None, fc, D_MODEL), lambda i, j: (j, 0, 0))],
        out_specs=[tok, tok, ffb, vec],
        out_shape=[_sds((s, D_MODEL), f32), _sds((s, D_MODEL), bf16), _sds((s, D_FF), bf16),
                   _sds((1, D_MODEL), f32)],
        scratch_shapes=[pltpu.VMEM((tm, D_MODEL), f32)],
        compiler_params=_arb(2),
    )(dh2, dh2b, h1, u, g_mlp, w_up, w_down)


def _out_proj_bwd(dh1b, o, g_attn, w_out, tm):
    s = o.shape[0]

    def body(dh_ref, o_ref, g_ref, w_ref, dc_ref, do_ref, gg_ref, dcat):
        i = pl.program_id(0)
        ones = _group_ones(LANES)
        dcat[...] = _dot(dh_ref[...], w_ref[...], NT)
        dc_ref[...] = dcat[:, :W_CONV]

        @pl.when(i == 0)
        def _():
            gg_ref[...] = jnp.zeros_like(gg_ref)

        for j in range(W_ATTN // LANES):
            cols = slice(j * LANES, (j + 1) * LANES)
            d, gg = _head_rms_bwd(dcat[:, W_CONV + j * LANES:W_CONV + (j + 1) * LANES],
                                  o_ref[:, cols], g_ref[:, cols], ones)
            do_ref[:, cols] = d
            gg_ref[:, cols] += gg

    return pl.pallas_call(
        body, name="out_proj_bwd", grid=(s // tm,),
        in_specs=[pl.BlockSpec((tm, D_MODEL), lambda i: (i, 0)),
                  pl.BlockSpec((tm, W_ATTN), lambda i: (i, 0)),
                  pl.BlockSpec((1, W_ATTN), lambda i: (0, 0)),
                  pl.BlockSpec((D_MODEL, D_MODEL), lambda i: (0, 0))],
        out_specs=[pl.BlockSpec((tm, W_CONV), lambda i: (i, 0)),
                   pl.BlockSpec((tm, W_ATTN), lambda i: (i, 0)),
                   pl.BlockSpec((1, W_ATTN), lambda i: (0, 0))],
        out_shape=[_sds((s, W_CONV), f32), _sds((s, W_ATTN), f32), _sds((1, W_ATTN), f32)],
        scratch_shapes=[pltpu.VMEM((tm, D_MODEL), f32)],
        compiler_params=_arb(1),
    )(dh1b, o, g_attn, w_out)


def _attn_bwd(proj, kp, vp, kt, do, tot):
    s = proj.shape[0]
    t = ATTN_TILE
    nblk = s // t
    npair = W_ATTN // LANES
    qoff = 3 * W_CONV // LANES
    keep01, keepneg = _causal_tiles()

    def body(q_ref, k_ref, v_ref, kt_ref, do_ref, tot_ref, m01_ref, neg_ref, dq_ref, dk_ref, dv_ref,
             w_s, da_s, b_s, g_s, a_s, dz_s, dq_acc, dk_acc, dv_acc):
        qb = pl.program_id(1)
        first = _head_pair_masks(t)
        q = q_ref[...] * (HEAD_DIM ** -0.5)
        qh = (jnp.where(first, q, 0.0).astype(bf16), jnp.where(first, 0.0, q).astype(bf16))
        dov = do_ref[...]
        doh = (jnp.where(first, dov, 0.0).astype(bf16), jnp.where(first, 0.0, dov).astype(bf16))
        total = (tot_ref[0:1, :], tot_ref[1:2, :])

        @pl.when(qb == 0)
        def _():
            dk_acc[...] = jnp.zeros_like(dk_acc)
            dv_acc[...] = jnp.zeros_like(dv_acc)

        dq_acc[...] = jnp.zeros_like(dq_acc)
        a_s[1] = jnp.zeros((t, t), bf16)
        dz_s[1] = jnp.zeros((t, t), bf16)

        def scores(kb, h):
            w_s[h] = _dot(k_ref[kb], qh[h], NT)
            da_s[h] = _dot(v_ref[kb], doh[h], NT)

        def spread(kb, h):
            dq_acc[h] += _dot(kt_ref[kb], dz_s[h])
            dk_acc[kb] += _dot(dz_s[h], qh[h])
            dv_acc[kb] += _dot(a_s[h], doh[h])

        def grads(h, diagonal, lk_before, g_before):
            run = jnp.zeros((SUBLANES, t), f32)
            for a in range(KEY_RUN):
                rows = slice(SUBLANES * a, SUBLANES * (a + 1))
                z2 = w_s[h, rows, :] * LOG2_E
                lk = _log2_keep(z2)
                if diagonal:
                    lk = lk * m01_ref[rows, :]
                log_beta = jnp.minimum(z2 + lk, 0.0)
                run = run + lk
                b_s[h, rows, :] = jnp.exp2(log_beta)
                w_s[h, rows, :] = log_beta - run
            off = total[h] - lk_before - _sublane_scan(run, reverse=False)
            lk_sum = jnp.sum(run, axis=0, keepdims=True)
            run = jnp.zeros((SUBLANES, t), f32)
            for a in range(KEY_RUN // 2):
                parts = []
                for r in (slice(2 * SUBLANES * a, 2 * SUBLANES * a + SUBLANES),
                          slice(2 * SUBLANES * a + SUBLANES, 2 * SUBLANES * (a + 1))):
                    w = w_s[h, r, :] + off
                    if diagonal:
                        w = w + neg_ref[r, :]
                    av = jnp.exp2(w)
                    g = av * da_s[h, r, :]
                    run = run + g
                    da_s[h, r, :] = g
                    g_s[h, r, :] = run
                    parts.append(av)
                a_s[h, 2 * SUBLANES * a:2 * SUBLANES * (a + 1), :] = jnp.concatenate(parts, axis=0).astype(bf16)
            goff = g_before + _sublane_scan(run, reverse=False)
            goff2 = jnp.concatenate([goff, goff], axis=0)
            for a in range(KEY_RUN // 2):
                rows = slice(2 * SUBLANES * a, 2 * SUBLANES * (a + 1))
                dz = da_s[h, rows, :] - b_s[h, rows, :] * (g_s[h, rows, :] + goff2)
                if diagonal:
                    dz = dz * m01_ref[rows, :]
                dz_s[h, rows, :] = dz.astype(bf16)
            return lk_before + lk_sum, g_before + jnp.sum(run, axis=0, keepdims=True)

        def block(kb, before, after, diagonal, carry):
            scores(kb, 1)
            spread(before, 1)
            c0 = grads(0, diagonal, carry[0], carry[1])
            if after is not None:
                scores(after, 0)
            spread(kb, 0)
            c1 = grads(1, diagonal, carry[2], carry[3])
            return c0 + c1

        zero = jnp.zeros((1, t), f32)
        scores(0, 0)
        carry = lax.fori_loop(0, qb, lambda kb, c: block(kb, jnp.maximum(kb - 1, 0), kb + 1, False, c),
                              (zero, zero, zero, zero))
        block(qb, jnp.maximum(qb - 1, 0), None, True, carry)
        spread(qb, 1)
        top = lax.broadcasted_iota(jnp.int32, (LANES, t), 0) < HEAD_DIM
        dq_ref[...] = (jnp.where(top, dq_acc[0], dq_acc[1]).T * (HEAD_DIM ** -0.5)).astype(bf16)

        @pl.when(qb == nblk - 1)
        def _():
            dk_ref[...] = dk_acc[...].astype(bf16)
            dv_ref[...] = dv_acc[...].astype(bf16)

    keys = pl.BlockSpec((nblk, t, LANES), lambda p, i: (0, 0, p))
    tile = pl.BlockSpec((t, t), lambda p, i: (0, 0))
    return pl.pallas_call(
        body, name="attn_bwd", grid=(npair, nblk),
        in_specs=[pl.BlockSpec((t, LANES), lambda p, i: (i, qoff + p)),
                  keys, keys,
                  pl.BlockSpec((nblk, LANES, t), lambda p, i: (0, p, 0)),
                  pl.BlockSpec((t, LANES), lambda p, i: (i, p)),
                  pl.BlockSpec((None, SUBLANES, t), lambda p, i: (p, 0, i)),
                  tile, tile],
        out_specs=[pl.BlockSpec((t, LANES), lambda p, i: (i, p)), keys, keys],
        out_shape=[_sds((s, W_ATTN), bf16), _sds((nblk, t, W_ATTN), bf16), _sds((nblk, t, W_ATTN), bf16)],
        scratch_shapes=[pltpu.VMEM((2, t, t), f32), pltpu.VMEM((2, t, t), f32), pltpu.VMEM((2, t, t), f32),
                        pltpu.VMEM((2, t, t), f32), pltpu.VMEM((2, t, t), bf16), pltpu.VMEM((2, t, t), bf16),
                        pltpu.VMEM((2, LANES, t), f32), pltpu.VMEM((nblk, t, LANES), f32),
                        pltpu.VMEM((nblk, t, LANES), f32)],
        compiler_params=_arb(2),
    )(proj, kp, vp, kt, do, tot, keep01, keepneg)


def _conv_bwd(proj, dcn, conv_w, g_conv):
    s = proj.shape[0]
    nblk = W_CONV // LANES
    rc = min(CONV_CHUNK, s)
    nchunk = s // rc

    def body(cb_ref, cc_ref, cu_ref, d_ref, w_ref, g_ref, dcb_ref, dcc_ref, dcu_ref, gw_ref, gg_ref, dy_buf):
        ones = _group_ones(LANES)
        w0, w1, w2 = w_ref[0:1, :], w_ref[1:2, :], w_ref[2:3, :]
        g = g_ref[...]

        def first_pass(i, carry):
            gw0, gw1, gw2, gg = carry
            r0 = pl.multiple_of(i * rc, rc)
            rows = pl.ds(r0, rc)
            prev = pl.ds(pl.multiple_of(jnp.maximum(r0 - 8, 0), 8), 8)
            v = cc_ref[rows, :] * cu_ref[rows, :]
            vp = jnp.where(i > 0, cc_ref[prev, :] * cu_ref[prev, :], 0.0)
            v1, v2 = _shifted(vp, v, 1), _shifted(vp, v, 2)
            y = w2 * v + w1 * v1 + w0 * v2
            cb = cb_ref[rows, :]
            dcy, ggi = _head_rms_bwd(d_ref[rows, :], cb * y, g, ones)
            dcb_ref[rows, :] = (dcy * y).astype(bf16)
            dy = dcy * cb
            dy_buf[rows, :] = dy
            return (gw0 + jnp.sum(dy * v2, axis=0, keepdims=True), gw1 + jnp.sum(dy * v1, axis=0, keepdims=True),
                    gw2 + jnp.sum(dy * v, axis=0, keepdims=True), gg + ggi)

        zero = jnp.zeros((1, LANES), f32)
        gw0, gw1, gw2, gg = lax.fori_loop(0, nchunk, first_pass, (zero, zero, zero, zero))
        gw_ref[...] = jnp.zeros_like(gw_ref)
        gw_ref[0:1, :] = gw0
        gw_ref[1:2, :] = gw1
        gw_ref[2:3, :] = gw2
        gg_ref[...] = gg

        def second_pass(i, carry):
            r0 = pl.multiple_of(i * rc, rc)
            rows = pl.ds(r0, rc)
            nxt = pl.ds(pl.multiple_of(jnp.minimum(r0 + rc, s - 8), 8), 8)
            dy = dy_buf[rows, :]
            dyn = jnp.where(i < nchunk - 1, dy_buf[nxt, :], 0.0)
            ext = jnp.concatenate([dy, dyn], axis=0)
            up1 = pltpu.roll(ext, rc + 8 - 1, axis=0)[:rc]
            up2 = pltpu.roll(ext, rc + 8 - 2, axis=0)[:rc]
            dv = w2 * dy + w1 * up1 + w0 * up2
            dcc_ref[rows, :] = (dv * cu_ref[rows, :]).astype(bf16)
            dcu_ref[rows, :] = (dv * cc_ref[rows, :]).astype(bf16)
            return carry

        lax.fori_loop(0, nchunk, second_pass, 0)

    def col(off):
        return pl.BlockSpec((s, LANES), lambda j: (0, off + j))

    return pl.pallas_call(
        body, name="conv_bwd", grid=(nblk,),
        in_specs=[col(0), col(nblk), col(2 * nblk), col(0),
                  pl.BlockSpec((None, CONV_W_ROWS, LANES), lambda j: (j, 0, 0)),
                  pl.BlockSpec((1, LANES), lambda j: (0, j))],
        out_specs=[col(0), col(0), col(0),
                   pl.BlockSpec((None, CONV_W_ROWS, LANES), lambda j: (j, 0, 0)),
                   pl.BlockSpec((1, LANES), lambda j: (0, j))],
        out_shape=[_sds((s, W_CONV), bf16)] * 3 + [_sds((nblk, CONV_W_ROWS, LANES), f32), _sds((1, W_CONV), f32)],
        scratch_shapes=[pltpu.VMEM((s, LANES), f32)],
        compiler_params=_arb(1),
    )(proj, proj, proj, dcn, conv_w, g_conv)


def _in_proj_bwd(dproj, dh1, x, g_mix, w_in, tm):
    s = x.shape[0]
    ncol = IN_COLS // N_CHIPS

    def body(dp_ref, dh_ref, x_ref, g_ref, w_ref, dx_ref, gg_ref):
        i = pl.program_id(0)
        da = _dot(dp_ref[:, 0:ncol], w_ref[0], NT)
        for k in range(1, N_CHIPS):
            da += _dot(dp_ref[:, k * ncol:(k + 1) * ncol], w_ref[k], NT)
        dxn, gg = _rms_bwd(da, x_ref[...], g_ref[...])
        dx_ref[...] = dh_ref[...] + dxn

        @pl.when(i == 0)
        def _():
            gg_ref[...] = jnp.zeros_like(gg_ref)

        gg_ref[...] += gg

    return pl.pallas_call(
        body, name="in_proj_bwd", grid=(s // tm,),
        in_specs=[pl.BlockSpec((tm, IN_COLS), lambda i: (i, 0)),
                  pl.BlockSpec((tm, D_MODEL), lambda i: (i, 0)),
                  pl.BlockSpec((tm, D_MODEL), lambda i: (i, 0)),
                  pl.BlockSpec((1, D_MODEL), lambda i: (0, 0)),
                  pl.BlockSpec((N_CHIPS, D_MODEL, ncol), lambda i: (0, 0, 0))],
        out_specs=[pl.BlockSpec((tm, D_MODEL), lambda i: (i, 0)),
                   pl.BlockSpec((1, D_MODEL), lambda i: (0, 0))],
        out_shape=[_sds((s, D_MODEL), f32), _sds((1, D_MODEL), f32)],
        compiler_params=_arb(1),
    )(dproj, dh1, x, g_mix, w_in)


def _weight_grad(a, b, bm, bn, ts, name, relu_sq=False):
    s, m = a.shape
    n = b.shape[1]
    nn = n // bn

    def body(a_ref, b_ref, o_ref):
        @pl.when(pl.program_id(2) == 0)
        def _():
            o_ref[...] = jnp.zeros_like(o_ref)

        av = a_ref[...]
        if relu_sq:
            av = jnp.square(jnp.maximum(av.astype(f32), 0.0)).astype(bf16)
        o_ref[...] += _dot(av, b_ref[...], TN)

    return pl.pallas_call(
        body, name=name, grid=(m // bm, nn, s // ts),
        in_specs=[pl.BlockSpec((ts, bm), lambda i, j, k: (k, i)),
                  pl.BlockSpec((ts, bn), lambda i, j, k: (k, j))],
        out_specs=pl.BlockSpec((None, bm, bn), lambda i, j, k: (i * nn + j, 0, 0)),
        out_shape=_sds(((m // bm) * nn, bm, bn), f32),
        compiler_params=_arb(3),
    )(a, b)


def kernel(x, p, g_mix, w_in, conv_w, g_conv_out, g_attn_out, w_out, g_mlp, w_up, w_down, g_ple, w_ple_gate, w_ple_proj, g_final, loss_target, m_g_mix, m_w_in, m_conv_w, m_g_conv_out, m_g_attn_out, m_w_out, m_g_mlp, m_w_up, m_w_down, m_g_ple, m_w_ple_gate, m_w_ple_proj, m_g_final, v_g_mix, v_w_in, v_conv_w, v_g_conv_out, v_g_attn_out, v_w_out, v_g_mlp, v_w_up, v_w_down, v_g_ple, v_w_ple_gate, v_w_ple_proj, v_g_final):
    s = x.shape[1]
    tm = min(TOKEN_TILE, s)
    xs = x.reshape(s, D_MODEL)
    ps = p.reshape(s, PLE_DIM)
    target = loss_target.reshape(s, D_MODEL)
    core = lax.axis_index("c").astype(jnp.int32).reshape(1)
    chip = 2 * lax.axis_index("x") + lax.axis_index("y")

    big = {"w_in": w_in[0], "w_out": w_out[0], "w_up": w_up[0], "w_down": w_down[0],
           "w_ple_gate": w_ple_gate[0], "w_ple_proj": w_ple_proj[0]}
    names = list(big)
    conv_shard = jnp.pad(conv_w[0], ((0, CONV_W_ROWS - conv_w.shape[1]), (0, 0)))
    full = _gather_weights([big[k].astype(bf16) for k in names] + [conv_shard])
    w_in_f, w_out_f, w_up_f, w_down_f, w_gate_f, w_proj_f, conv_f = full
    w_out_f = w_out_f.reshape(D_MODEL, D_MODEL)
    w_gate_f = w_gate_f.reshape(D_MODEL, D_MODEL)

    proj, a_b = _in_proj(xs, g_mix, w_in_f, tm)
    conv_n = _conv_fwd(proj, conv_f, g_conv_out)
    nblk = s // ATTN_TILE
    koff = 3 * W_CONV + W_ATTN
    kp = _permute_keys(proj[:, koff:koff + W_ATTN].astype(bf16)).reshape(nblk, ATTN_TILE, W_ATTN)
    vp = _permute_keys(proj[:, koff + W_ATTN:].astype(bf16)).reshape(nblk, ATTN_TILE, W_ATTN)
    kt, vt = kp.transpose(0, 2, 1), vp.transpose(0, 2, 1)
    o, tot = _attn_fwd(proj, kp, vt)
    h1, cat_b = _out_proj(o, conv_n, xs, g_attn_out, w_out_f, tm)
    h2, u_b, m_b = _mlp_fwd(h1, g_mlp, w_up_f, w_down_f, tm)

    dh2, dh2_b, n3_b, dgl_b, dpp_b, p_b, gg_ple, gg_final, loss_row = _tail(
        h2, ps, target, g_ple, g_final.reshape(1, D_MODEL), w_gate_f, w_proj_f, tm)
    dh1, dh1_b, du_b, gg_mlp = _mlp_bwd(dh2, dh2_b, h1, u_b, g_mlp, w_up_f, w_down_f, tm)
    dcn, do, gg_attn = _out_proj_bwd(dh1_b, o, g_attn_out, w_out_f, tm)
    dq, dk, dv = _attn_bwd(proj, kp, vp, kt, do, tot)
    dk = _unpermute_keys(dk.reshape(s, W_ATTN))
    dv = _unpermute_keys(dv.reshape(s, W_ATTN))
    dcb, dcc, dcu, g_conv_w, gg_conv = _conv_bwd(proj, dcn, conv_f, g_conv_out)
    dproj = jnp.concatenate([dcb, dcc, dcu, dq, dk, dv], axis=1)
    grad_x, gg_mix = _in_proj_bwd(dproj, dh1, xs, g_mix, w_in_f, tm)

    part = {
        "w_in": _weight_grad(a_b, dproj, D_MODEL, IN_COLS // N_CHIPS, tm, "grad_w_in"),
        "w_out": _weight_grad(cat_b, dh1_b, D_MODEL, D_MODEL, tm, "grad_w_out").reshape(N_CHIPS, D_MODEL // N_CHIPS, D_MODEL),
        "w_up": _weight_grad(m_b, du_b, D_MODEL, D_FF // N_CHIPS, tm, "grad_w_up"),
        "w_down": _weight_grad(u_b, dh2_b, D_FF // N_CHIPS, D_MODEL, tm, "grad_w_down", relu_sq=True),
        "w_ple_gate": _weight_grad(n3_b, dgl_b, D_MODEL, D_MODEL, tm, "grad_w_ple_gate").reshape(N_CHIPS, D_MODEL // N_CHIPS, D_MODEL),
        "w_ple_proj": _weight_grad(p_b, dpp_b, PLE_DIM, D_MODEL // N_CHIPS, tm, "grad_w_ple_proj"),
    }

    from_sibling = _pair_exchange([part[k] for k in names])
    pair = [_pair_sum(part[k], r, core, "pair_sum_" + k) for k, r in zip(names, from_sibling)]
    from_chips = _chip_exchange(pair)
    half = [_chip_sum(c, "chip_sum_" + k) for k, c in zip(names, from_chips)]
    both = _sibling_exchange(half)
    grad = {k: b.reshape(big[k].shape) for k, b in zip(names, both)}

    gcw = g_conv_w[:, :3, :].transpose(1, 0, 2).reshape(3, W_CONV)
    row = lambda *parts: jnp.concatenate(parts, axis=1)
    packed = jnp.concatenate([
        gg_mix, gg_mlp, gg_ple, gg_final, row(gg_conv, gg_attn), row(gcw[0:1], gcw[1:2]),
        row(gcw[2:3], loss_row, jnp.zeros((1, W_CONV - LANES), f32)), jnp.zeros((1, D_MODEL), f32)], axis=0)
    summed = _allreduce_small(packed)
    loss = summed[6, W_CONV]
    gcw_full = jnp.stack([summed[5, :W_CONV], summed[5, W_CONV:], summed[6, :W_CONV]])
    grad["conv_w"] = lax.dynamic_slice(gcw_full, (0, chip * LANES), (3, LANES))
    vec_names = ["g_mix", "g_mlp", "g_ple", "g_final"]
    vec_w = {"g_mix": g_mix, "g_mlp": g_mlp, "g_ple": g_ple, "g_final": g_final.reshape(1, D_MODEL)}
    vec_m = {"g_mix": m_g_mix, "g_mlp": m_g_mlp, "g_ple": m_g_ple, "g_final": m_g_final.reshape(1, D_MODEL)}
    vec_v = {"g_mix": v_g_mix, "g_mlp": v_g_mlp, "g_ple": v_g_ple, "g_final": v_g_final.reshape(1, D_MODEL)}

    def pack_vec(d, conv, attn):
        return jnp.concatenate([d[k] for k in vec_names] + [row(conv, attn)], axis=0)

    vec_g = summed[0:5]
    vec_d, vec_nm, vec_nv = _adamw(vec_g, pack_vec(vec_w, g_conv_out, g_attn_out),
                                   pack_vec(vec_m, m_g_conv_out, m_g_attn_out),
                                   pack_vec(vec_v, v_g_conv_out, v_g_attn_out), "adamw_vectors")

    given_w = dict(big, conv_w=conv_w[0])
    given_m = {"w_in": m_w_in[0], "w_out": m_w_out[0], "w_up": m_w_up[0], "w_down": m_w_down[0],
               "w_ple_gate": m_w_ple_gate[0], "w_ple_proj": m_w_ple_proj[0], "conv_w": m_conv_w[0]}
    given_v = {"w_in": v_w_in[0], "w_out": v_w_out[0], "w_up": v_w_up[0], "w_down": v_w_down[0],
               "w_ple_gate": v_w_ple_gate[0], "w_ple_proj": v_w_ple_proj[0], "conv_w": v_conv_w[0]}
    delta, new_m, new_v = {}, {}, {}
    for k in names + ["conv_w"]:
        delta[k], new_m[k], new_v[k] = _adamw(grad[k], given_w[k], given_m[k], given_v[k], "adamw_" + k)

    def unpack(vals, kind):
        out = {k: vals[i:i + 1] for i, k in enumerate(vec_names)}
        out["g_final"] = out["g_final"].reshape(D_MODEL)
        out["g_conv_out"] = vals[4:5, :W_CONV]
        out["g_attn_out"] = vals[4:5, W_CONV:]
        out.update({k: v[None] for k, v in kind.items()})
        return out

    order = ["g_mix", "w_in", "conv_w", "g_conv_out", "g_attn_out", "w_out", "g_mlp", "w_up", "w_down",
             "g_ple", "w_ple_gate", "w_ple_proj", "g_final"]
    groups = [unpack(vec_g, grad), unpack(vec_d, delta), unpack(vec_nm, new_m), unpack(vec_nv, new_v)]
    return (loss, grad_x[None]) + tuple(g[k] for g in groups for k in order)
```

```python
import functools

import jax
import jax.numpy as jnp
from jax import lax
from jax.experimental import pallas as pl
from jax.experimental.pallas import tpu as pltpu

f32 = jnp.float32
bf16 = jnp.bfloat16

D_MODEL = 1024
HEAD_DIM = 64
W_CONV = 512
W_ATTN = 512
D_FF = 4096
PLE_DIM = 256
IN_COLS = 3 * W_CONV + 3 * W_ATTN
N_CHIPS = 4
EPS = 1e-6
ADAM_LR = 0.001
ADAM_B1 = 0.9
ADAM_B2 = 0.999
ADAM_EPS = 1e-08
ADAM_WD = 0.01
ADAM_STEP = 10

LANES = 128
TOKEN_TILE = 512
ATTN_TILE = 256
CONV_CHUNK = 512
CONV_W_ROWS = 16

MESH = pl.DeviceIdType.MESH
ANY = pl.BlockSpec(memory_space=pl.ANY)
NT = (((1,), (1,)), ((), ()))
TN = (((0,), (0,)), ((), ()))


def _arb(n):
    return pltpu.CompilerParams(dimension_semantics=("arbitrary",) * n)


def _sds(shape, dtype):
    return jax.ShapeDtypeStruct(shape, dtype)


def _dot(a, b, dims=None):
    if dims is None:
        return jnp.dot(a, b, preferred_element_type=f32)
    return lax.dot_general(a, b, dims, preferred_element_type=f32)


def _split_dot(x, ones):
    hi = x.astype(bf16)
    lo = (x - hi.astype(f32)).astype(bf16)
    return _dot(hi, ones) + _dot(lo, ones)


def _rms_fwd(h, g):
    rstd = lax.rsqrt(jnp.mean(h * h, axis=-1, keepdims=True) + EPS)
    return h * rstd * g, rstd


def _rms_bwd(dy, h, g):
    rstd = lax.rsqrt(jnp.mean(h * h, axis=-1, keepdims=True) + EPS)
    hn = h * rstd
    dyg = dy * g
    dh = rstd * (dyg - hn * jnp.mean(dyg * hn, axis=-1, keepdims=True))
    return dh, jnp.sum(dy * hn, axis=0, keepdims=True)


def _group_ones(n):
    r = lax.broadcasted_iota(jnp.int32, (n, n), 0) // HEAD_DIM
    c = lax.broadcasted_iota(jnp.int32, (n, n), 1) // HEAD_DIM
    return (r == c).astype(bf16)


def _head_rms_fwd(y, g, ones):
    rstd = lax.rsqrt(_split_dot(y * y, ones) * (1.0 / HEAD_DIM) + EPS)
    return y * rstd * g


def _head_rms_bwd(dy, y, g, ones):
    rstd = lax.rsqrt(_split_dot(y * y, ones) * (1.0 / HEAD_DIM) + EPS)
    yn = y * rstd
    dyg = dy * g
    dyy = rstd * (dyg - yn * (_split_dot(dyg * yn, ones) * (1.0 / HEAD_DIM)))
    return dyy, jnp.sum(dy * yn, axis=0, keepdims=True)


def _place():
    return lax.axis_index("x"), lax.axis_index("y"), lax.axis_index("c")


def _other_chips(x, y):
    return [(1 - x, y), (x, 1 - y), (1 - x, 1 - y)]


class _Exchange:
    def __init__(self, arrays, out_shapes, sems, start, finish):
        self.arrays, self.out_shapes, self.sems, self.start, self.finish = arrays, out_shapes, sems, start, finish


def _gather_exchange(shards):
    n = len(shards)
    halves = [s.shape[0] // 2 for s in shards]

    def plan(ins, outs, sems):
        send_sems, recv_sems, local_sems = sems
        x, y, c = _place()
        me = 2 * x + y
        chips = _other_chips(x, y)

        def half(ref, i, which):
            return ref.at[pl.ds(which * halves[i], halves[i]), :]

        def over_ici(i, j, src, slot, to):
            return pltpu.make_async_remote_copy(
                src_ref=src, dst_ref=half(outs[i].at[slot], i, c),
                send_sem=send_sems.at[3 * i + j], recv_sem=recv_sems.at[3 * i + j],
                device_id=to, device_id_type=MESH)

        def to_sibling(i, j, slot, which):
            blk = half(outs[i].at[slot], i, which)
            return pltpu.make_async_remote_copy(
                src_ref=blk, dst_ref=blk,
                send_sem=send_sems.at[3 * n + 3 * i + j], recv_sem=recv_sems.at[3 * n + 3 * i + j],
                device_id=(x, y, 1 - c), device_id_type=MESH)

        pairs = [(i, j, px, py) for i in range(n) for j, (px, py) in enumerate(chips)]
        local = [pltpu.make_async_copy(ins[i], outs[i].at[me], local_sems.at[i]) for i in range(n)]
        sends = [over_ici(i, j, half(ins[i], i, c), me, (px, py, c)) for i, j, px, py in pairs]
        lands = [over_ici(i, j, half(outs[i].at[2 * px + py], i, c), 2 * px + py, (px, py, c)) for i, j, px, py in pairs]
        passes = [to_sibling(i, j, 2 * px + py, c) for i, j, px, py in pairs]
        from_sibling = [to_sibling(i, j, 2 * px + py, 1 - c) for i, j, px, py in pairs]
        return local, sends, lands, passes, from_sibling

    def start(ins, outs, sems):
        local, sends, _, _, _ = plan(ins, outs, sems)
        for cp in local + sends:
            cp.start()

    def finish(ins, outs, sems):
        local, sends, lands, passes, from_sibling = plan(ins, outs, sems)
        for land, on in zip(lands, passes):
            land.wait_recv()
            on.start()
        for cp in from_sibling:
            cp.wait_recv()
        for cp in sends + passes:
            cp.wait_send()
        for cp in local:
            cp.wait()

    return _Exchange(
        shards, [_sds((N_CHIPS,) + s.shape, s.dtype) for s in shards],
        [pltpu.SemaphoreType.DMA((6 * n,)), pltpu.SemaphoreType.DMA((6 * n,)), pltpu.SemaphoreType.DMA((n,))],
        start, finish)


def _call_carrying(ex, body, name, grid, in_specs, out_specs, out_shape, scratch_shapes, args):
    n_in, n_out, n_scr = len(in_specs), len(out_specs), len(scratch_shapes)
    k = 0 if ex is None else len(ex.arrays)

    def wrapped(*refs):
        ins, xin = refs[:n_in], refs[n_in:n_in + k]
        outs, xout = refs[n_in + k:n_in + k + n_out], refs[n_in + k + n_out:n_in + 2 * k + n_out]
        scr, sems = refs[n_in + 2 * k + n_out:n_in + 2 * k + n_out + n_scr], refs[n_in + 2 * k + n_out + n_scr:]
        ids = [pl.program_id(d) for d in range(len(grid))]
        if ex is not None:
            @pl.when(functools.reduce(lambda a, b: a & b, [i == 0 for i in ids]))
            def _():
                ex.start(xin, xout, sems)

        body(*ins, *outs, *scr)
        if ex is not None:
            @pl.when(functools.reduce(lambda a, b: a & b, [i == g - 1 for i, g in zip(ids, grid)]))
            def _():
                ex.finish(xin, xout, sems)

    res = pl.pallas_call(
        wrapped, name=name, grid=grid,
        in_specs=list(in_specs) + [ANY] * k, out_specs=list(out_specs) + [ANY] * k,
        out_shape=list(out_shape) + ([] if ex is None else list(ex.out_shapes)),
        scratch_shapes=list(scratch_shapes) + ([] if ex is None else list(ex.sems)),
        compiler_params=_arb(len(grid)),
    )(*args, *([] if ex is None else ex.arrays))
    return res[:n_out], res[n_out:]


def _run_exchange(ex, name):
    n = len(ex.arrays)

    def body(*refs):
        ins, outs, sems = refs[:n], refs[n:2 * n], refs[2 * n:]
        ex.start(ins, outs, sems)
        ex.finish(ins, outs, sems)

    return pl.pallas_call(
        body, name=name, out_shape=ex.out_shapes, in_specs=[ANY] * n, out_specs=[ANY] * n,
        scratch_shapes=ex.sems,
    )(*ex.arrays)


def _pair_exchange(grads, name):
    n = len(grads)

    def body(*refs):
        ins, outs = refs[:n], refs[n:2 * n]
        send_sems, recv_sems = refs[2 * n:]
        x, y, c = _place()
        sent = []
        for i in range(n):
            cp = pltpu.make_async_remote_copy(
                src_ref=ins[i].at[:, 1 - c], dst_ref=outs[i],
                send_sem=send_sems.at[i], recv_sem=recv_sems.at[i],
                device_id=(x, y, 1 - c), device_id_type=MESH)
            cp.start()
            sent.append(cp)
        for cp in sent:
            cp.wait()

    views = [g.reshape(N_CHIPS, 2, g.shape[1] // 2, g.shape[2]) for g in grads]
    return pl.pallas_call(
        body, name=name,
        out_shape=[_sds((N_CHIPS, v.shape[2], v.shape[3]), f32) for v in views],
        in_specs=[ANY] * n, out_specs=[ANY] * n,
        scratch_shapes=[pltpu.SemaphoreType.DMA((n,)), pltpu.SemaphoreType.DMA((n,))],
    )(*views)


def _chip_exchange(parts):
    n = len(parts)

    def plan(ins, outs, sems):
        send_sems, recv_sems, local_sems = sems
        x, y, c = _place()
        me = 2 * x + y
        pairs = [(i, j, px, py) for i in range(n) for j, (px, py) in enumerate(_other_chips(x, y))]

        def copy(i, j, src, slot, px, py):
            return pltpu.make_async_remote_copy(
                src_ref=src, dst_ref=outs[i].at[slot],
                send_sem=send_sems.at[3 * i + j], recv_sem=recv_sems.at[3 * i + j],
                device_id=(px, py, c), device_id_type=MESH)

        local = [pltpu.make_async_copy(ins[i].at[me], outs[i].at[me], local_sems.at[i]) for i in range(n)]
        sends = [copy(i, j, ins[i].at[2 * px + py], me, px, py) for i, j, px, py in pairs]
        lands = [copy(i, j, outs[i].at[2 * px + py], 2 * px + py, px, py) for i, j, px, py in pairs]
        return local, sends, lands

    def start(ins, outs, sems):
        local, sends, _ = plan(ins, outs, sems)
        for cp in local + sends:
            cp.start()

    def finish(ins, outs, sems):
        local, sends, lands = plan(ins, outs, sems)
        for cp in lands:
            cp.wait_recv()
        for cp in sends:
            cp.wait_send()
        for cp in local:
            cp.wait()

    return _Exchange(
        parts, [_sds(p.shape, p.dtype) for p in parts],
        [pltpu.SemaphoreType.DMA((3 * n,)), pltpu.SemaphoreType.DMA((3 * n,)), pltpu.SemaphoreType.DMA((n,))],
        start, finish)


def _sibling_exchange(halves):
    n = len(halves)

    def body(*refs):
        ins, outs = refs[:n], refs[n:2 * n]
        send_sems, recv_sems, local_sems = refs[2 * n:]
        x, y, c = _place()
        started = []
        for i in range(n):
            cp = pltpu.make_async_copy(ins[i], outs[i].at[c], local_sems.at[i])
            cp.start()
            started.append(cp)
            cp = pltpu.make_async_remote_copy(
                src_ref=ins[i], dst_ref=outs[i].at[c],
                send_sem=send_sems.at[i], recv_sem=recv_sems.at[i],
                device_id=(x, y, 1 - c), device_id_type=MESH)
            cp.start()
            started.append(cp)
        for cp in started:
            cp.wait()

    return pl.pallas_call(
        body, name="grad_sibling_exchange",
        out_shape=[_sds((2,) + h.shape, f32) for h in halves],
        in_specs=[ANY] * n, out_specs=[ANY] * n,
        scratch_shapes=[pltpu.SemaphoreType.DMA((n,)), pltpu.SemaphoreType.DMA((n,)),
                        pltpu.SemaphoreType.DMA((n,))],
    )(*halves)


def _row_tile(rows, cols):
    t = rows
    while t * cols * 4 > (1 << 20) and t % 16 == 0:
        t //= 2
    return t


def _pair_sum(grad, recv, core, name):
    _, r, c = grad.shape
    hr = r // 2
    tr = _row_tile(hr, c)
    view = grad.reshape(N_CHIPS, 2, hr, c)

    def body(core_ref, mine_ref, recv_ref, out_ref):
        out_ref[...] = (mine_ref[...] + recv_ref[...]).astype(bf16)

    return pl.pallas_call(
        body, name=name,
        grid_spec=pltpu.PrefetchScalarGridSpec(
            num_scalar_prefetch=1, grid=(N_CHIPS, hr // tr),
            in_specs=[pl.BlockSpec((None, None, tr, c), lambda k, t, core_ref: (k, core_ref[0], t, 0)),
                      pl.BlockSpec((None, tr, c), lambda k, t, core_ref: (k, t, 0))],
            out_specs=pl.BlockSpec((None, tr, c), lambda k, t, core_ref: (k, t, 0))),
        out_shape=_sds((N_CHIPS, hr, c), bf16),
        compiler_params=_arb(2),
    )(core, view, recv)


def _chip_sum(parts, name):
    _, hr, c = parts.shape
    tr = _row_tile(hr, c)

    def body(p_ref, out_ref):
        out_ref[...] = ((p_ref[0].astype(f32) + p_ref[1].astype(f32)) + p_ref[2].astype(f32)) + p_ref[3].astype(f32)

    return pl.pallas_call(
        body, name=name, grid=(hr // tr,),
        in_specs=[pl.BlockSpec((N_CHIPS, tr, c), lambda t: (0, t, 0))],
        out_specs=pl.BlockSpec((tr, c), lambda t: (t, 0)),
        out_shape=_sds((hr, c), f32), compiler_params=_arb(1),
    )(parts)


def _adamw(g, w, m, v, name):
    r, c = g.shape
    tr = _row_tile(r, c)

    def body(g_ref, w_ref, m_ref, v_ref, d_ref, nm_ref, nv_ref):
        gv = g_ref[...]
        mv = ADAM_B1 * m_ref[...] + (1.0 - ADAM_B1) * gv
        vv = ADAM_B2 * v_ref[...] + (1.0 - ADAM_B2) * jnp.square(gv)
        m_hat = mv / (1.0 - ADAM_B1 ** ADAM_STEP)
        v_hat = vv / (1.0 - ADAM_B2 ** ADAM_STEP)
        d_ref[...] = -ADAM_LR * (m_hat / (jnp.sqrt(v_hat) + ADAM_EPS) + ADAM_WD * w_ref[...])
        nm_ref[...] = mv
        nv_ref[...] = vv

    spec = pl.BlockSpec((tr, c), lambda t: (t, 0))
    return pl.pallas_call(
        body, name=name, grid=(r // tr,), in_specs=[spec] * 4, out_specs=[spec] * 3,
        out_shape=[_sds((r, c), f32)] * 3, compiler_params=_arb(1),
    )(g, w, m, v)


def _allreduce_small(packed):
    shape = packed.shape

    def body(x_ref, out_ref, buf, send_sems, recv_sems):
        x, y, c = _place()
        me = 4 * x + 2 * y + c
        buf[me] = x_ref[...]
        sent = []
        for r in range(1, 8):
            dx, dy, dc = (r >> 2) & 1, (r >> 1) & 1, r & 1
            peer = ((1 - x) if dx else x, (1 - y) if dy else y, (1 - c) if dc else c)
            cp = pltpu.make_async_remote_copy(
                src_ref=x_ref, dst_ref=buf.at[me],
                send_sem=send_sems.at[r], recv_sem=recv_sems.at[r],
                device_id=peer, device_id_type=MESH)
            cp.start()
            sent.append((cp, peer))
        for r, (cp, peer) in enumerate(sent, start=1):
            src = 4 * peer[0] + 2 * peer[1] + peer[2]
            pltpu.make_async_remote_copy(
                src_ref=x_ref, dst_ref=buf.at[src],
                send_sem=send_sems.at[r], recv_sem=recv_sems.at[r],
                device_id=peer, device_id_type=MESH).wait_recv()
        for cp, _ in sent:
            cp.wait_send()
        total = buf[0]
        for k in range(1, 8):
            total = total + buf[k]
        out_ref[...] = total

    vmem = pl.BlockSpec(memory_space=pltpu.VMEM)
    return pl.pallas_call(
        body, name="allreduce_small", out_shape=_sds(shape, f32),
        in_specs=[vmem], out_specs=vmem,
        scratch_shapes=[pltpu.VMEM((8,) + shape, f32), pltpu.SemaphoreType.DMA((8,)),
                        pltpu.SemaphoreType.DMA((8,))],
    )(packed)


def _in_proj(x, g_mix, w_in, tm):
    s = x.shape[0]
    ncol = IN_COLS // N_CHIPS

    def body(x_ref, g_ref, w_ref, proj_ref, a_ref):
        a, _ = _rms_fwd(x_ref[...], g_ref[...])
        ab = a.astype(bf16)
        a_ref[...] = ab
        for k in range(N_CHIPS):
            proj_ref[:, k * ncol:(k + 1) * ncol] = _dot(ab, w_ref[k])

    return pl.pallas_call(
        body, name="in_proj", grid=(s // tm,),
        in_specs=[pl.BlockSpec((tm, D_MODEL), lambda i: (i, 0)),
                  pl.BlockSpec((1, D_MODEL), lambda i: (0, 0)),
                  pl.BlockSpec((N_CHIPS, D_MODEL, ncol), lambda i: (0, 0, 0))],
        out_specs=[pl.BlockSpec((tm, IN_COLS), lambda i: (i, 0)),
                   pl.BlockSpec((tm, D_MODEL), lambda i: (i, 0))],
        out_shape=[_sds((s, IN_COLS), f32), _sds((s, D_MODEL), bf16)],
        compiler_params=_arb(1),
    )(x, g_mix, w_in)


def _shifted(prev8, cur, shift):
    ext = jnp.concatenate([prev8, cur], axis=0)
    return pltpu.roll(ext, shift, axis=0)[8:]


def _conv_fwd(proj, conv_w, g_conv):
    s = proj.shape[0]
    nblk = W_CONV // LANES
    rc = min(CONV_CHUNK, s)

    def body(cb_ref, cc_ref, cu_ref, w_ref, g_ref, out_ref):
        ones = _group_ones(LANES)
        w0, w1, w2 = w_ref[0:1, :], w_ref[1:2, :], w_ref[2:3, :]
        g = g_ref[...]

        def chunk(i, carry):
            r0 = pl.multiple_of(i * rc, rc)
            rows = pl.ds(r0, rc)
            prev = pl.ds(pl.multiple_of(jnp.maximum(r0 - 8, 0), 8), 8)
            v = cc_ref[rows, :] * cu_ref[rows, :]
            vp = jnp.where(i > 0, cc_ref[prev, :] * cu_ref[prev, :], 0.0)
            y = w2 * v + w1 * _shifted(vp, v, 1) + w0 * _shifted(vp, v, 2)
            out_ref[rows, :] = _head_rms_fwd(cb_ref[rows, :] * y, g, ones).astype(bf16)
            return carry

        lax.fori_loop(0, s // rc, chunk, 0)

    def col(off):
        return pl.BlockSpec((s, LANES), lambda j: (0, off + j))

    return pl.pallas_call(
        body, name="conv_fwd", grid=(nblk,),
        in_specs=[col(0), col(nblk), col(2 * nblk),
                  pl.BlockSpec((None, CONV_W_ROWS, LANES), lambda j: (j, 0, 0)),
                  pl.BlockSpec((1, LANES), lambda j: (0, j))],
        out_specs=pl.BlockSpec((s, LANES), lambda j: (0, j)),
        out_shape=_sds((s, W_CONV), bf16), compiler_params=_arb(1),
    )(proj, proj, proj, conv_w, g_conv)


LOG2_E = 1.4426950408889634


def _log2_keep(z2):
    nz2 = -z2
    return jnp.minimum(nz2, 0.0) - jnp.log2(1.0 + jnp.exp2(jnp.minimum(z2, nz2)))


def _head_pair_masks(rows):
    lane = lax.broadcasted_iota(jnp.int32, (rows, LANES), 1)
    return lane < HEAD_DIM


SUBLANES = 8
KEY_RUN = ATTN_TILE // SUBLANES


def _permute_keys(a):
    s, w = a.shape
    return a.reshape(s // ATTN_TILE, SUBLANES, KEY_RUN, w).transpose(0, 2, 1, 3).reshape(s, w)


def _unpermute_keys(a):
    s, w = a.shape
    return a.reshape(s // ATTN_TILE, KEY_RUN, SUBLANES, w).transpose(0, 2, 1, 3).reshape(s, w)


def _causal_tiles():
    r = lax.broadcasted_iota(jnp.int32, (ATTN_TILE, ATTN_TILE), 0)
    key = (r % SUBLANES) * KEY_RUN + r // SUBLANES
    below = key < lax.broadcasted_iota(jnp.int32, (ATTN_TILE, ATTN_TILE), 1)
    return below.astype(f32), jnp.where(below, 0.0, -1e30).astype(f32)


def _sublane_scan(x, reverse):
    row = lax.broadcasted_iota(jnp.int32, x.shape, 0)
    inc = x
    for sh in (1, 2, 4):
        if reverse:
            inc = inc + jnp.where(row < SUBLANES - sh, pltpu.roll(inc, SUBLANES - sh, axis=0), 0.0)
        else:
            inc = inc + jnp.where(row >= sh, pltpu.roll(inc, sh, axis=0), 0.0)
    return inc - x


def _attn_fwd(proj, kp, vt, ex=None):
    s = proj.shape[0]
    t = ATTN_TILE
    nblk = s // t
    npair = W_ATTN // LANES
    qoff = 3 * W_CONV // LANES
    keep01, keepneg = _causal_tiles()

    def body(q_ref, k_ref, vt_ref, m01_ref, neg_ref, o_ref, tot_ref, w_s, a_s, acc):
        qb = pl.program_id(1)
        first = _head_pair_masks(t)
        q = q_ref[...] * (HEAD_DIM ** -0.5)
        qh = (jnp.where(first, q, 0.0).astype(bf16), jnp.where(first, 0.0, q).astype(bf16))
        acc[...] = jnp.zeros_like(acc)
        a_s[1] = jnp.zeros((t, t), bf16)

        def scores(kb, h):
            w_s[h] = _dot(k_ref[kb], qh[h], NT)

        def weigh(kb, h):
            acc[h] += _dot(vt_ref[kb], a_s[h])

        def weights(h, diagonal, later):
            run = jnp.zeros((SUBLANES, t), f32)
            for a in reversed(range(KEY_RUN)):
                rows = slice(SUBLANES * a, SUBLANES * (a + 1))
                z2 = w_s[h, rows, :] * LOG2_E
                lk = _log2_keep(z2)
                if diagonal:
                    lk = lk * m01_ref[rows, :]
                run = run + lk
                w_s[h, rows, :] = z2 + run
            off = _sublane_scan(run, reverse=True) + later
            off2 = jnp.concatenate([off, off], axis=0)
            for a in range(t // (2 * SUBLANES)):
                rows = slice(2 * SUBLANES * a, 2 * SUBLANES * (a + 1))
                w = w_s[h, rows, :] + off2
                if diagonal:
                    w = w + neg_ref[rows, :]
                a_s[h, rows, :] = jnp.exp2(w).astype(bf16)
            return later + jnp.sum(run, axis=0, keepdims=True)

        def block(kb, before, after, diagonal, later):
            scores(kb, 1)
            weigh(before, 1)
            l0 = weights(0, diagonal, later[0])
            scores(after, 0)
            weigh(kb, 0)
            l1 = weights(1, diagonal, later[1])
            return l0, l1

        zero = jnp.zeros((1, t), f32)
        scores(qb, 0)
        later = block(qb, qb, jnp.maximum(qb - 1, 0), True, (zero, zero))

        def earlier(i, c):
            kb = qb - 1 - i
            return block(kb, kb + 1, jnp.maximum(kb - 1, 0), False, c)

        later = lax.fori_loop(0, qb, earlier, later)
        weigh(0, 1)
        top = lax.broadcasted_iota(jnp.int32, (LANES, t), 0) < HEAD_DIM
        o_ref[...] = jnp.where(top, acc[0], acc[1]).T
        tot_ref[...] = jnp.concatenate([later[0], later[1], jnp.zeros((SUBLANES - 2, t), f32)], axis=0)

    return _call_carrying(
        ex, body, "attn_fwd", (npair, nblk),
        in_specs=[pl.BlockSpec((t, LANES), lambda p, i: (i, qoff + p)),
                  pl.BlockSpec((nblk, t, LANES), lambda p, i: (0, 0, p)),
                  pl.BlockSpec((nblk, LANES, t), lambda p, i: (0, p, 0)),
                  pl.BlockSpec((t, t), lambda p, i: (0, 0)),
                  pl.BlockSpec((t, t), lambda p, i: (0, 0))],
        out_specs=[pl.BlockSpec((t, LANES), lambda p, i: (i, p)),
                   pl.BlockSpec((None, SUBLANES, t), lambda p, i: (p, 0, i))],
        out_shape=[_sds((s, W_ATTN), f32), _sds((npair, SUBLANES, s), f32)],
        scratch_shapes=[pltpu.VMEM((2, t, t), f32), pltpu.VMEM((2, t, t), bf16), pltpu.VMEM((2, LANES, t), f32)],
        args=(proj, kp, vt, keep01, keepneg))


def _out_proj(o, conv_n, x, g_attn, w_out, tm):
    s = x.shape[0]

    def body(o_ref, c_ref, x_ref, g_ref, w_ref, h_ref, cat_ref):
        ones = _group_ones(LANES)
        cat_ref[:, :W_CONV] = c_ref[...]
        for j in range(W_ATTN // LANES):
            cols = slice(j * LANES, (j + 1) * LANES)
            cat_ref[:, W_CONV + j * LANES:W_CONV + (j + 1) * LANES] = _head_rms_fwd(
                o_ref[:, cols], g_ref[:, cols], ones).astype(bf16)
        h_ref[...] = x_ref[...] + _dot(cat_ref[...], w_ref[...])

    return pl.pallas_call(
        body, name="out_proj", grid=(s // tm,),
        in_specs=[pl.BlockSpec((tm, W_ATTN), lambda i: (i, 0)),
                  pl.BlockSpec((tm, W_CONV), lambda i: (i, 0)),
                  pl.BlockSpec((tm, D_MODEL), lambda i: (i, 0)),
                  pl.BlockSpec((1, W_ATTN), lambda i: (0, 0)),
                  pl.BlockSpec((D_MODEL, D_MODEL), lambda i: (0, 0))],
        out_specs=[pl.BlockSpec((tm, D_MODEL), lambda i: (i, 0)),
                   pl.BlockSpec((tm, D_MODEL), lambda i: (i, 0))],
        out_shape=[_sds((s, D_MODEL), f32), _sds((s, D_MODEL), bf16)],
        compiler_params=_arb(1),
    )(o, conv_n, x, g_attn, w_out)


def _mlp_fwd(h1, g_mlp, w_up, w_down, tm):
    s = h1.shape[0]
    fc = D_FF // N_CHIPS

    def body(h_ref, g_ref, wu_ref, wd_ref, h2_ref, u_ref, m_ref):
        j = pl.program_id(1)

        @pl.when(j == 0)
        def _():
            m, _ = _rms_fwd(h_ref[...], g_ref[...])
            m_ref[...] = m.astype(bf16)
            h2_ref[...] = h_ref[...]

        u = _dot(m_ref[...], wu_ref[...])
        u_ref[...] = u.astype(bf16)
        h2_ref[...] += _dot(jnp.square(jnp.maximum(u, 0.0)).astype(bf16), wd_ref[...])

    return pl.pallas_call(
        body, name="mlp_fwd", grid=(s // tm, N_CHIPS),
        in_specs=[pl.BlockSpec((tm, D_MODEL), lambda i, j: (i, 0)),
                  pl.BlockSpec((1, D_MODEL), lambda i, j: (0, 0)),
                  pl.BlockSpec((None, D_MODEL, fc), lambda i, j: (j, 0, 0)),
                  pl.BlockSpec((None, fc, D_MODEL), lambda i, j: (j, 0, 0))],
        out_specs=[pl.BlockSpec((tm, D_MODEL), lambda i, j: (i, 0)),
                   pl.BlockSpec((tm, fc), lambda i, j: (i, j)),
                   pl.BlockSpec((tm, D_MODEL), lambda i, j: (i, 0))],
        out_shape=[_sds((s, D_MODEL), f32), _sds((s, D_FF), bf16), _sds((s, D_MODEL), bf16)],
        compiler_params=_arb(2),
    )(h1, g_mlp, w_up, w_down)


def _tail(h2, p, target, g_ple, g_final, w_gate, w_proj, tm):
    s = h2.shape[0]
    pc = D_MODEL // N_CHIPS

    def body(h_ref, p_ref, t_ref, gp_ref, gf_ref, wg_ref, wp_ref,
             dh_ref, dhb_ref, n3_ref, dgl_ref, dpp_ref, pb_ref, ggp_ref, ggf_ref, loss_ref, pp_ref):
        i = pl.program_id(0)
        h2v = h_ref[...]
        n3, _ = _rms_fwd(h2v, gp_ref[...])
        n3b = n3.astype(bf16)
        n3_ref[...] = n3b
        gate = jax.nn.sigmoid(_dot(n3b, wg_ref[...]))
        pb = p_ref[...].astype(bf16)
        pb_ref[...] = pb
        for k in range(N_CHIPS):
            pp_ref[:, k * pc:(k + 1) * pc] = _dot(pb, wp_ref[k])
        pp = pp_ref[...]
        h3 = h2v + gate * pp
        yv, _ = _rms_fwd(h3, gf_ref[...])
        err = yv - t_ref[...]
        loss = 0.5 * jnp.sum(err * err) * (1.0 / D_MODEL)
        dh3, ggf = _rms_bwd(err * (1.0 / D_MODEL), h3, gf_ref[...])
        dpp_ref[...] = (dh3 * gate).astype(bf16)
        dgl = (dh3 * pp * gate * (1.0 - gate)).astype(bf16)
        dgl_ref[...] = dgl
        dn3 = _dot(dgl, wg_ref[...], NT)
        dh2n, ggp = _rms_bwd(dn3, h2v, gp_ref[...])
        dh2 = dh3 + dh2n
        dh_ref[...] = dh2
        dhb_ref[...] = dh2.astype(bf16)

        @pl.when(i == 0)
        def _():
            ggp_ref[...] = jnp.zeros_like(ggp_ref)
            ggf_ref[...] = jnp.zeros_like(ggf_ref)
            loss_ref[...] = jnp.zeros_like(loss_ref)

        ggp_ref[...] += ggp
        ggf_ref[...] += ggf
        loss_ref[...] += jnp.full(loss_ref.shape, loss, f32)

    tok = lambda w: pl.BlockSpec((tm, w), lambda i: (i, 0))
    vec = lambda w: pl.BlockSpec((1, w), lambda i: (0, 0))
    return pl.pallas_call(
        body, name="tail", grid=(s // tm,),
        in_specs=[tok(D_MODEL), tok(PLE_DIM), tok(D_MODEL), vec(D_MODEL), vec(D_MODEL),
                  pl.BlockSpec((D_MODEL, D_MODEL), lambda i: (0, 0)),
                  pl.BlockSpec((N_CHIPS, PLE_DIM, pc), lambda i: (0, 0, 0))],
        out_specs=[tok(D_MODEL), tok(D_MODEL), tok(D_MODEL), tok(D_MODEL), tok(D_MODEL), tok(PLE_DIM),
                   vec(D_MODEL), vec(D_MODEL), vec(LANES)],
        out_shape=[_sds((s, D_MODEL), f32), _sds((s, D_MODEL), bf16), _sds((s, D_MODEL), bf16),
                   _sds((s, D_MODEL), bf16), _sds((s, D_MODEL), bf16), _sds((s, PLE_DIM), bf16),
                   _sds((1, D_MODEL), f32), _sds((1, D_MODEL), f32), _sds((1, LANES), f32)],
        scratch_shapes=[pltpu.VMEM((tm, D_MODEL), f32)],
        compiler_params=_arb(1),
    )(h2, p, target, g_ple, g_final, w_gate, w_proj)


def _mlp_bwd(dh2, dh2b, h1, u, g_mlp, w_up, w_down, tm):
    s = h1.shape[0]
    fc = D_FF // N_CHIPS

    def body(dh_ref, dhb_ref, h_ref, u_ref, g_ref, wu_ref, wd_ref, dh1_ref, dh1b_ref, du_ref, gg_ref, dm):
        i, j = pl.program_id(0), pl.program_id(1)

        @pl.when(j == 0)
        def _():
            dm[...] = jnp.zeros_like(dm)

        dr = _dot(dhb_ref[...], wd_ref[...], NT)
        du = (dr * (2.0 * jnp.maximum(u_ref[...].astype(f32), 0.0))).astype(bf16)
        du_ref[...] = du
        dm[...] += _dot(du, wu_ref[...], NT)

        @pl.when((i == 0) & (j == 0))
        def _():
            gg_ref[...] = jnp.zeros_like(gg_ref)

        @pl.when(j == N_CHIPS - 1)
        def _():
            dh1n, gg = _rms_bwd(dm[...], h_ref[...], g_ref[...])
            dh1 = dh_ref[...] + dh1n
            dh1_ref[...] = dh1
            dh1b_ref[...] = dh1.astype(bf16)
            gg_ref[...] += gg

    tok = pl.BlockSpec((tm, D_MODEL), lambda i, j: (i, 0))
    ffb = pl.BlockSpec((tm, fc), lambda i, j: (i, j))
    vec = pl.BlockSpec((1, D_MODEL), lambda i, j: (0, 0))
    return pl.pallas_call(
        body, name="mlp_bwd", grid=(s // tm, N_CHIPS),
        in_specs=[tok, tok, tok, ffb, vec,
                  pl.BlockSpec((None, D_MODEL, fc), lambda i, j: (j, 0, 0)),
                  pl.BlockSpec((None, fc, D_MODEL), lambda i, j: (j, 0, 0))],
        out_specs=[tok, tok, ffb, vec],
        out_shape=[_sds((s, D_MODEL), f32), _sds((s, D_MODEL), bf16), _sds((s, D_FF), bf16),
                   _sds((1, D_MODEL), f32)],
        scratch_shapes=[pltpu.VMEM((tm, D_MODEL), f32)],
        compiler_params=_arb(2),
    )(dh2, dh2b, h1, u, g_mlp, w_up, w_down)


def _out_proj_bwd(dh1b, o, g_attn, w_out, tm):
    s = o.shape[0]

    def body(dh_ref, o_ref, g_ref, w_ref, dc_ref, do_ref, gg_ref, dcat):
        i = pl.program_id(0)
        ones = _group_ones(LANES)
        dcat[...] = _dot(dh_ref[...], w_ref[...], NT)
        dc_ref[...] = dcat[:, :W_CONV]

        @pl.when(i == 0)
        def _():
            gg_ref[...] = jnp.zeros_like(gg_ref)

        for j in range(W_ATTN // LANES):
            cols = slice(j * LANES, (j + 1) * LANES)
            d, gg = _head_rms_bwd(dcat[:, W_CONV + j * LANES:W_CONV + (j + 1) * LANES],
                                  o_ref[:, cols], g_ref[:, cols], ones)
            do_ref[:, cols] = d
            gg_ref[:, cols] += gg

    return pl.pallas_call(
        body, name="out_proj_bwd", grid=(s // tm,),
        in_specs=[pl.BlockSpec((tm, D_MODEL), lambda i: (i, 0)),
                  pl.BlockSpec((tm, W_ATTN), lambda i: (i, 0)),
                  pl.BlockSpec((1, W_ATTN), lambda i: (0, 0)),
                  pl.BlockSpec((D_MODEL, D_MODEL), lambda i: (0, 0))],
        out_specs=[pl.BlockSpec((tm, W_CONV), lambda i: (i, 0)),
                   pl.BlockSpec((tm, W_ATTN), lambda i: (i, 0)),
                   pl.BlockSpec((1, W_ATTN), lambda i: (0, 0))],
        out_shape=[_sds((s, W_CONV), f32), _sds((s, W_ATTN), f32), _sds((1, W_ATTN), f32)],
        scratch_shapes=[pltpu.VMEM((tm, D_MODEL), f32)],
        compiler_params=_arb(1),
    )(dh1b, o, g_attn, w_out)


def _attn_bwd(proj, kp, vp, kt, do, tot, ex=None):
    s = proj.shape[0]
    t = ATTN_TILE
    nblk = s // t
    npair = W_ATTN // LANES
    qoff = 3 * W_CONV // LANES
    keep01, keepneg = _causal_tiles()

    def body(q_ref, k_ref, v_ref, kt_ref, do_ref, tot_ref, m01_ref, neg_ref, dq_ref, dk_ref, dv_ref,
             w_s, da_s, b_s, g_s, a_s, dz_s, dq_acc, dk_acc, dv_acc):
        qb = pl.program_id(1)
        first = _head_pair_masks(t)
        q = q_ref[...] * (HEAD_DIM ** -0.5)
        qh = (jnp.where(first, q, 0.0).astype(bf16), jnp.where(first, 0.0, q).astype(bf16))
        dov = do_ref[...]
        doh = (jnp.where(first, dov, 0.0).astype(bf16), jnp.where(first, 0.0, dov).astype(bf16))
        total = (tot_ref[0:1, :], tot_ref[1:2, :])

        @pl.when(qb == 0)
        def _():
            dk_acc[...] = jnp.zeros_like(dk_acc)
            dv_acc[...] = jnp.zeros_like(dv_acc)

        dq_acc[...] = jnp.zeros_like(dq_acc)
        a_s[1] = jnp.zeros((t, t), bf16)
        dz_s[1] = jnp.zeros((t, t), bf16)

        def scores(kb, h):
            w_s[h] = _dot(k_ref[kb], qh[h], NT)
            da_s[h] = _dot(v_ref[kb], doh[h], NT)

        def spread(kb, h):
            dq_acc[h] += _dot(kt_ref[kb], dz_s[h])
            dk_acc[kb] += _dot(dz_s[h], qh[h])
            dv_acc[kb] += _dot(a_s[h], doh[h])

        def grads(h, diagonal, lk_before, g_before):
            run = jnp.zeros((SUBLANES, t), f32)
            for a in range(KEY_RUN):
                rows = slice(SUBLANES * a, SUBLANES * (a + 1))
                z2 = w_s[h, rows, :] * LOG2_E
                lk = _log2_keep(z2)
                if diagonal:
                    lk = lk * m01_ref[rows, :]
                log_beta = jnp.minimum(z2 + lk, 0.0)
                run = run + lk
                b_s[h, rows, :] = jnp.exp2(log_beta)
                w_s[h, rows, :] = log_beta - run
            off = total[h] - lk_before - _sublane_scan(run, reverse=False)
            lk_sum = jnp.sum(run, axis=0, keepdims=True)
            run = jnp.zeros((SUBLANES, t), f32)
            for a in range(KEY_RUN // 2):
                parts = []
                for r in (slice(2 * SUBLANES * a, 2 * SUBLANES * a + SUBLANES),
                          slice(2 * SUBLANES * a + SUBLANES, 2 * SUBLANES * (a + 1))):
                    w = w_s[h, r, :] + off
                    if diagonal:
                        w = w + neg_ref[r, :]
                    av = jnp.exp2(w)
                    g = av * da_s[h, r, :]
                    run = run + g
                    da_s[h, r, :] = g
                    g_s[h, r, :] = run
                    parts.append(av)
                a_s[h, 2 * SUBLANES * a:2 * SUBLANES * (a + 1), :] = jnp.concatenate(parts, axis=0).astype(bf16)
            goff = g_before + _sublane_scan(run, reverse=False)
            goff2 = jnp.concatenate([goff, goff], axis=0)
            for a in range(KEY_RUN // 2):
                rows = slice(2 * SUBLANES * a, 2 * SUBLANES * (a + 1))
                dz = da_s[h, rows, :] - b_s[h, rows, :] * (g_s[h, rows, :] + goff2)
                if diagonal:
                    dz = dz * m01_ref[rows, :]
                dz_s[h, rows, :] = dz.astype(bf16)
            return lk_before + lk_sum, g_before + jnp.sum(run, axis=0, keepdims=True)

        def block(kb, before, after, diagonal, carry):
            scores(kb, 1)
            spread(before, 1)
            c0 = grads(0, diagonal, carry[0], carry[1])
            if after is not None:
                scores(after, 0)
            spread(kb, 0)
            c1 = grads(1, diagonal, carry[2], carry[3])
            return c0 + c1

        zero = jnp.zeros((1, t), f32)
        scores(0, 0)
        carry = lax.fori_loop(0, qb, lambda kb, c: block(kb, jnp.maximum(kb - 1, 0), kb + 1, False, c),
                              (zero, zero, zero, zero))
        block(qb, jnp.maximum(qb - 1, 0), None, True, carry)
        spread(qb, 1)
        top = lax.broadcasted_iota(jnp.int32, (LANES, t), 0) < HEAD_DIM
        dq_ref[...] = (jnp.where(top, dq_acc[0], dq_acc[1]).T * (HEAD_DIM ** -0.5)).astype(bf16)

        @pl.when(qb == nblk - 1)
        def _():
            dk_ref[...] = dk_acc[...].astype(bf16)
            dv_ref[...] = dv_acc[...].astype(bf16)

    keys = pl.BlockSpec((nblk, t, LANES), lambda p, i: (0, 0, p))
    tile = pl.BlockSpec((t, t), lambda p, i: (0, 0))
    return _call_carrying(
        ex, body, "attn_bwd", (npair, nblk),
        in_specs=[pl.BlockSpec((t, LANES), lambda p, i: (i, qoff + p)),
                  keys, keys,
                  pl.BlockSpec((nblk, LANES, t), lambda p, i: (0, p, 0)),
                  pl.BlockSpec((t, LANES), lambda p, i: (i, p)),
                  pl.BlockSpec((None, SUBLANES, t), lambda p, i: (p, 0, i)),
                  tile, tile],
        out_specs=[pl.BlockSpec((t, LANES), lambda p, i: (i, p)), keys, keys],
        out_shape=[_sds((s, W_ATTN), bf16), _sds((nblk, t, W_ATTN), bf16), _sds((nblk, t, W_ATTN), bf16)],
        scratch_shapes=[pltpu.VMEM((2, t, t), f32), pltpu.VMEM((2, t, t), f32), pltpu.VMEM((2, t, t), f32),
                        pltpu.VMEM((2, t, t), f32), pltpu.VMEM((2, t, t), bf16), pltpu.VMEM((2, t, t), bf16),
                        pltpu.VMEM((2, LANES, t), f32), pltpu.VMEM((nblk, t, LANES), f32),
                        pltpu.VMEM((nblk, t, LANES), f32)],
        args=(proj, kp, vp, kt, do, tot, keep01, keepneg))


def _conv_bwd(proj, dcn, conv_w, g_conv):
    s = proj.shape[0]
    nblk = W_CONV // LANES
    rc = min(CONV_CHUNK, s)
    nchunk = s // rc

    def body(cb_ref, cc_ref, cu_ref, d_ref, w_ref, g_ref, dcb_ref, dcc_ref, dcu_ref, gw_ref, gg_ref, dy_buf):
        ones = _group_ones(LANES)
        w0, w1, w2 = w_ref[0:1, :], w_ref[1:2, :], w_ref[2:3, :]
        g = g_ref[...]

        def first_pass(i, carry):
            gw0, gw1, gw2, gg = carry
            r0 = pl.multiple_of(i * rc, rc)
            rows = pl.ds(r0, rc)
            prev = pl.ds(pl.multiple_of(jnp.maximum(r0 - 8, 0), 8), 8)
            v = cc_ref[rows, :] * cu_ref[rows, :]
            vp = jnp.where(i > 0, cc_ref[prev, :] * cu_ref[prev, :], 0.0)
            v1, v2 = _shifted(vp, v, 1), _shifted(vp, v, 2)
            y = w2 * v + w1 * v1 + w0 * v2
            cb = cb_ref[rows, :]
            dcy, ggi = _head_rms_bwd(d_ref[rows, :], cb * y, g, ones)
            dcb_ref[rows, :] = (dcy * y).astype(bf16)
            dy = dcy * cb
            dy_buf[rows, :] = dy
            return (gw0 + jnp.sum(dy * v2, axis=0, keepdims=True), gw1 + jnp.sum(dy * v1, axis=0, keepdims=True),
                    gw2 + jnp.sum(dy * v, axis=0, keepdims=True), gg + ggi)

        zero = jnp.zeros((1, LANES), f32)
        gw0, gw1, gw2, gg = lax.fori_loop(0, nchunk, first_pass, (zero, zero, zero, zero))
        gw_ref[...] = jnp.zeros_like(gw_ref)
        gw_ref[0:1, :] = gw0
        gw_ref[1:2, :] = gw1
        gw_ref[2:3, :] = gw2
        gg_ref[...] = gg

        def second_pass(i, carry):
            r0 = pl.multiple_of(i * rc, rc)
            rows = pl.ds(r0, rc)
            nxt = pl.ds(pl.multiple_of(jnp.minimum(r0 + rc, s - 8), 8), 8)
            dy = dy_buf[rows, :]
            dyn = jnp.where(i < nchunk - 1, dy_buf[nxt, :], 0.0)
            ext = jnp.concatenate([dy, dyn], axis=0)
            up1 = pltpu.roll(ext, rc + 8 - 1, axis=0)[:rc]
            up2 = pltpu.roll(ext, rc + 8 - 2, axis=0)[:rc]
            dv = w2 * dy + w1 * up1 + w0 * up2
            dcc_ref[rows, :] = (dv * cu_ref[rows, :]).astype(bf16)
            dcu_ref[rows, :] = (dv * cc_ref[rows, :]).astype(bf16)
            return carry

        lax.fori_loop(0, nchunk, second_pass, 0)

    def col(off):
        return pl.BlockSpec((s, LANES), lambda j: (0, off + j))

    return pl.pallas_call(
        body, name="conv_bwd", grid=(nblk,),
        in_specs=[col(0), col(nblk), col(2 * nblk), col(0),
                  pl.BlockSpec((None, CONV_W_ROWS, LANES), lambda j: (j, 0, 0)),
                  pl.BlockSpec((1, LANES), lambda j: (0, j))],
        out_specs=[col(0), col(0), col(0),
                   pl.BlockSpec((None, CONV_W_ROWS, LANES), lambda j: (j, 0, 0)),
                   pl.BlockSpec((1, LANES), lambda j: (0, j))],
        out_shape=[_sds((s, W_CONV), bf16)] * 3 + [_sds((nblk, CONV_W_ROWS, LANES), f32), _sds((1, W_CONV), f32)],
        scratch_shapes=[pltpu.VMEM((s, LANES), f32)],
        compiler_params=_arb(1),
    )(proj, proj, proj, dcn, conv_w, g_conv)


def _in_proj_bwd(dproj, dh1, x, g_mix, w_in, tm, ex=None):
    s = x.shape[0]
    ncol = IN_COLS // N_CHIPS

    def body(dp_ref, dh_ref, x_ref, g_ref, w_ref, dx_ref, gg_ref):
        i = pl.program_id(0)
        da = _dot(dp_ref[:, 0:ncol], w_ref[0], NT)
        for k in range(1, N_CHIPS):
            da += _dot(dp_ref[:, k * ncol:(k + 1) * ncol], w_ref[k], NT)
        dxn, gg = _rms_bwd(da, x_ref[...], g_ref[...])
        dx_ref[...] = dh_ref[...] + dxn

        @pl.when(i == 0)
        def _():
            gg_ref[...] = jnp.zeros_like(gg_ref)

        gg_ref[...] += gg

    return _call_carrying(
        ex, body, "in_proj_bwd", (s // tm,),
        in_specs=[pl.BlockSpec((tm, IN_COLS), lambda i: (i, 0)),
                  pl.BlockSpec((tm, D_MODEL), lambda i: (i, 0)),
                  pl.BlockSpec((tm, D_MODEL), lambda i: (i, 0)),
                  pl.BlockSpec((1, D_MODEL), lambda i: (0, 0)),
                  pl.BlockSpec((N_CHIPS, D_MODEL, ncol), lambda i: (0, 0, 0))],
        out_specs=[pl.BlockSpec((tm, D_MODEL), lambda i: (i, 0)),
                   pl.BlockSpec((1, D_MODEL), lambda i: (0, 0))],
        out_shape=[_sds((s, D_MODEL), f32), _sds((1, D_MODEL), f32)],
        scratch_shapes=[],
        args=(dproj, dh1, x, g_mix, w_in))


def _weight_grad(a, b, bm, bn, ts, name, relu_sq=False):
    s, m = a.shape
    n = b.shape[1]
    nn = n // bn

    def body(a_ref, b_ref, o_ref):
        @pl.when(pl.program_id(2) == 0)
        def _():
            o_ref[...] = jnp.zeros_like(o_ref)

        av = a_ref[...]
        if relu_sq:
            av = jnp.square(jnp.maximum(av.astype(f32), 0.0)).astype(bf16)
        o_ref[...] += _dot(av, b_ref[...], TN)

    return pl.pallas_call(
        body, name=name, grid=(m // bm, nn, s // ts),
        in_specs=[pl.BlockSpec((ts, bm), lambda i, j, k: (k, i)),
                  pl.BlockSpec((ts, bn), lambda i, j, k: (k, j))],
        out_specs=pl.BlockSpec((None, bm, bn), lambda i, j, k: (i * nn + j, 0, 0)),
        out_shape=_sds(((m // bm) * nn, bm, bn), f32),
        compiler_params=_arb(3),
    )(a, b)


def kernel(x, p, g_mix, w_in, conv_w, g_conv_out, g_attn_out, w_out, g_mlp, w_up, w_down, g_ple, w_ple_gate, w_ple_proj, g_final, loss_target, m_g_mix, m_w_in, m_conv_w, m_g_conv_out, m_g_attn_out, m_w_out, m_g_mlp, m_w_up, m_w_down, m_g_ple, m_w_ple_gate, m_w_ple_proj, m_g_final, v_g_mix, v_w_in, v_conv_w, v_g_conv_out, v_g_attn_out, v_w_out, v_g_mlp, v_w_up, v_w_down, v_g_ple, v_w_ple_gate, v_w_ple_proj, v_g_final):
    s = x.shape[1]
    tm = min(TOKEN_TILE, s)
    xs = x.reshape(s, D_MODEL)
    ps = p.reshape(s, PLE_DIM)
    target = loss_target.reshape(s, D_MODEL)
    core = lax.axis_index("c").astype(jnp.int32).reshape(1)
    chip = 2 * lax.axis_index("x") + lax.axis_index("y")

    big = {"w_in": w_in[0], "w_out": w_out[0], "w_up": w_up[0], "w_down": w_down[0],
           "w_ple_gate": w_ple_gate[0], "w_ple_proj": w_ple_proj[0]}
    names = list(big)
    conv_shard = jnp.pad(conv_w[0], ((0, CONV_W_ROWS - conv_w.shape[1]), (0, 0)))
    later_names = names[1:]
    w_in_f, conv_f = _run_exchange(_gather_exchange([big["w_in"].astype(bf16), conv_shard]), "gather_w_in")

    proj, a_b = _in_proj(xs, g_mix, w_in_f, tm)
    conv_n = _conv_fwd(proj, conv_f, g_conv_out)
    nblk = s // ATTN_TILE
    koff = 3 * W_CONV + W_ATTN
    kp = _permute_keys(proj[:, koff:koff + W_ATTN].astype(bf16)).reshape(nblk, ATTN_TILE, W_ATTN)
    vp = _permute_keys(proj[:, koff + W_ATTN:].astype(bf16)).reshape(nblk, ATTN_TILE, W_ATTN)
    kt, vt = kp.transpose(0, 2, 1), vp.transpose(0, 2, 1)
    (o, tot), gathered = _attn_fwd(proj, kp, vt, _gather_exchange([big[k].astype(bf16) for k in later_names]))
    w_out_f, w_up_f, w_down_f, w_gate_f, w_proj_f = gathered
    w_out_f = w_out_f.reshape(D_MODEL, D_MODEL)
    w_gate_f = w_gate_f.reshape(D_MODEL, D_MODEL)
    h1, cat_b = _out_proj(o, conv_n, xs, g_attn_out, w_out_f, tm)
    h2, u_b, m_b = _mlp_fwd(h1, g_mlp, w_up_f, w_down_f, tm)

    dh2, dh2_b, n3_b, dgl_b, dpp_b, p_b, gg_ple, gg_final, loss_row = _tail(
        h2, ps, target, g_ple, g_final.reshape(1, D_MODEL), w_gate_f, w_proj_f, tm)
    dh1, dh1_b, du_b, gg_mlp = _mlp_bwd(dh2, dh2_b, h1, u_b, g_mlp, w_up_f, w_down_f, tm)
    part = {
        "w_out": _weight_grad(cat_b, dh1_b, D_MODEL, D_MODEL, tm, "grad_w_out").reshape(N_CHIPS, D_MODEL // N_CHIPS, D_MODEL),
        "w_up": _weight_grad(m_b, du_b, D_MODEL, D_FF // N_CHIPS, tm, "grad_w_up"),
        "w_down": _weight_grad(u_b, dh2_b, D_FF // N_CHIPS, D_MODEL, tm, "grad_w_down", relu_sq=True),
        "w_ple_gate": _weight_grad(n3_b, dgl_b, D_MODEL, D_MODEL, tm, "grad_w_ple_gate").reshape(N_CHIPS, D_MODEL // N_CHIPS, D_MODEL),
        "w_ple_proj": _weight_grad(p_b, dpp_b, PLE_DIM, D_MODEL // N_CHIPS, tm, "grad_w_ple_proj"),
    }
    from_sibling = _pair_exchange([part[k] for k in later_names], "grad_pair_exchange")
    pair = [_pair_sum(part[k], r, core, "pair_sum_" + k) for k, r in zip(later_names, from_sibling)]
    dcn, do, gg_attn = _out_proj_bwd(dh1_b, o, g_attn_out, w_out_f, tm)
    dcb, dcc, dcu, g_conv_w, gg_conv = _conv_bwd(proj, dcn, conv_f, g_conv_out)
    (dq, dk, dv), from_chips = _attn_bwd(proj, kp, vp, kt, do, tot, _chip_exchange(pair))
    dk = _unpermute_keys(dk.reshape(s, W_ATTN))
    dv = _unpermute_keys(dv.reshape(s, W_ATTN))
    dproj = jnp.concatenate([dcb, dcc, dcu, dq, dk, dv], axis=1)

    part["w_in"] = _weight_grad(a_b, dproj, D_MODEL, IN_COLS // N_CHIPS, tm, "grad_w_in")
    in_sibling = _pair_exchange([part["w_in"]], "grad_pair_exchange_w_in")
    in_pair = _pair_sum(part["w_in"], in_sibling[0], core, "pair_sum_w_in")
    (grad_x, gg_mix), in_chips = _in_proj_bwd(dproj, dh1, xs, g_mix, w_in_f, tm, _chip_exchange([in_pair]))

    half = [_chip_sum(c, "chip_sum_" + k) for k, c in zip(names, list(in_chips) + list(from_chips))]
    both = _sibling_exchange(half)
    grad = {k: b.reshape(big[k].shape) for k, b in zip(names, both)}

    gcw = g_conv_w[:, :3, :].transpose(1, 0, 2).reshape(3, W_CONV)
    row = lambda *parts: jnp.concatenate(parts, axis=1)
    packed = jnp.concatenate([
        gg_mix, gg_mlp, gg_ple, gg_final, row(gg_conv, gg_attn), row(gcw[0:1], gcw[1:2]),
        row(gcw[2:3], loss_row, jnp.zeros((1, W_CONV - LANES), f32)), jnp.zeros((1, D_MODEL), f32)], axis=0)
    summed = _allreduce_small(packed)
    loss = summed[6, W_CONV]
    gcw_full = jnp.stack([summed[5, :W_CONV], summed[5, W_CONV:], summed[6, :W_CONV]])
    grad["conv_w"] = lax.dynamic_slice(gcw_full, (0, chip * LANES), (3, LANES))
    vec_names = ["g_mix", "g_mlp", "g_ple", "g_final"]
    vec_w = {"g_mix": g_mix, "g_mlp": g_mlp, "g_ple": g_ple, "g_final": g_final.reshape(1, D_MODEL)}
    vec_m = {"g_mix": m_g_mix, "g_mlp": m_g_mlp, "g_ple": m_g_ple, "g_final": m_g_final.reshape(1, D_MODEL)}
    vec_v = {"g_mix": v_g_mix, "g_mlp": v_g_mlp, "g_ple": v_g_ple, "g_final": v_g_final.reshape(1, D_MODEL)}

    def pack_vec(d, conv, attn):
        return jnp.concatenate([d[k] for k in vec_names] + [row(conv, attn)], axis=0)

    vec_g = summed[0:5]
    vec_d, vec_nm, vec_nv = _adamw(vec_g, pack_vec(vec_w, g_conv_out, g_attn_out),
                                   pack_vec(vec_m, m_g_conv_out, m_g_attn_out),
                                   pack_vec(vec_v, v_g_conv_out, v_g_attn_out), "adamw_vectors")

    given_w = dict(big, conv_w=conv_w[0])
    given_m = {"w_in": m_w_in[0], "w_out": m_w_out[0], "w_up": m_w_up[0], "w_down": m_w_down[0],
               "w_ple_gate": m_w_ple_gate[0], "w_ple_proj": m_w_ple_proj[0], "conv_w": m_conv_w[0]}
    given_v = {"w_in": v_w_in[0], "w_out": v_w_out[0], "w_up": v_w_up[0], "w_down": v_w_down[0],
               "w_ple_gate": v_w_ple_gate[0], "w_ple_proj": v_w_ple_proj[0], "conv_w": v_conv_w[0]}
    delta, new_m, new_v = {}, {}, {}
    for k in names + ["conv_w"]:
        delta[k], new_m[k], new_v[k] = _adamw(grad[k], given_w[k], given_m[k], given_v[k], "adamw_" + k)

    def unpack(vals, kind):
        out = {k: vals[i:i + 1] for i, k in enumerate(vec_names)}
        out["g_final"] = out["g_final"].reshape(D_MODEL)
        out["g_conv_out"] = vals[4:5, :W_CONV]
        out["g_attn_out"] = vals[4:5, W_CONV:]
        out.update({k: v[None] for k, v in kind.items()})
        return out

    order = ["g_mix", "w_in", "conv_w", "g_conv_out", "g_attn_out", "w_out", "g_mlp", "w_up", "w_down",
             "g_ple", "w_ple_gate", "w_ple_proj", "g_final"]
    groups = [unpack(vec_g, grad), unpack(vec_d, delta), unpack(vec_nm, new_m), unpack(vec_nv, new_v)]
    return (loss, grad_x[None]) + tuple(g[k] for g in groups for k in order)
```

```python
import functools

import jax
import jax.numpy as jnp
from jax import lax
from jax.experimental import pallas as pl
from jax.experimental.pallas import tpu as pltpu

f32 = jnp.float32
bf16 = jnp.bfloat16

D_MODEL = 1024
HEAD_DIM = 64
W_CONV = 512
W_ATTN = 512
D_FF = 4096
PLE_DIM = 256
IN_COLS = 3 * W_CONV + 3 * W_ATTN
N_CHIPS = 4
EPS = 1e-6
ADAM_LR = 0.001
ADAM_B1 = 0.9
ADAM_B2 = 0.999
ADAM_EPS = 1e-08
ADAM_WD = 0.01
ADAM_STEP = 10

LANES = 128
TOKEN_TILE = 512
ATTN_TILE = 256
CONV_CHUNK = 512
CONV_W_ROWS = 16

MESH = pl.DeviceIdType.MESH
ANY = pl.BlockSpec(memory_space=pl.ANY)
NT = (((1,), (1,)), ((), ()))
TN = (((0,), (0,)), ((), ()))


def _arb(n):
    return pltpu.CompilerParams(dimension_semantics=("arbitrary",) * n)


def _sds(shape, dtype):
    return jax.ShapeDtypeStruct(shape, dtype)


def _dot(a, b, dims=None):
    if dims is None:
        return jnp.dot(a, b, preferred_element_type=f32)
    return lax.dot_general(a, b, dims, preferred_element_type=f32)


def _split_dot(x, ones):
    hi = x.astype(bf16)
    lo = (x - hi.astype(f32)).astype(bf16)
    return _dot(hi, ones) + _dot(lo, ones)


def _rms_fwd(h, g):
    rstd = lax.rsqrt(jnp.mean(h * h, axis=-1, keepdims=True) + EPS)
    return h * rstd * g, rstd


def _rms_bwd(dy, h, g):
    rstd = lax.rsqrt(jnp.mean(h * h, axis=-1, keepdims=True) + EPS)
    hn = h * rstd
    dyg = dy * g
    dh = rstd * (dyg - hn * jnp.mean(dyg * hn, axis=-1, keepdims=True))
    return dh, jnp.sum(dy * hn, axis=0, keepdims=True)


def _group_ones(n):
    r = lax.broadcasted_iota(jnp.int32, (n, n), 0) // HEAD_DIM
    c = lax.broadcasted_iota(jnp.int32, (n, n), 1) // HEAD_DIM
    return (r == c).astype(bf16)


def _head_rms_fwd(y, g, ones):
    rstd = lax.rsqrt(_split_dot(y * y, ones) * (1.0 / HEAD_DIM) + EPS)
    return y * rstd * g


def _head_rms_bwd(dy, y, g, ones):
    rstd = lax.rsqrt(_split_dot(y * y, ones) * (1.0 / HEAD_DIM) + EPS)
    yn = y * rstd
    dyg = dy * g
    dyy = rstd * (dyg - yn * (_split_dot(dyg * yn, ones) * (1.0 / HEAD_DIM)))
    return dyy, jnp.sum(dy * yn, axis=0, keepdims=True)


def _place():
    return lax.axis_index("x"), lax.axis_index("y"), lax.axis_index("c")


def _other_chips(x, y):
    return [(1 - x, y), (x, 1 - y), (1 - x, 1 - y)]


class _Exchange:
    def __init__(self, arrays, out_shapes, sems, start, finish):
        self.arrays, self.out_shapes, self.sems, self.start, self.finish = arrays, out_shapes, sems, start, finish


def _gather_exchange(shards):
    n = len(shards)
    halves = [s.shape[0] // 2 for s in shards]

    def plan(ins, outs, sems):
        send_sems, recv_sems, own_sems = sems
        x, y, c = _place()
        me = 2 * x + y
        chips = _other_chips(x, y)

        def half(ref, i, which):
            return ref.at[pl.ds(which * halves[i], halves[i]), :]

        def over_ici(i, j, src, slot, to):
            return pltpu.make_async_remote_copy(
                src_ref=src, dst_ref=half(outs[i].at[slot], i, c),
                send_sem=send_sems.at[3 * i + j], recv_sem=recv_sems.at[3 * i + j],
                device_id=to, device_id_type=MESH)

        def to_sibling(i, j, slot, which):
            blk = half(outs[i].at[slot], i, which)
            return pltpu.make_async_remote_copy(
                src_ref=blk, dst_ref=blk,
                send_sem=send_sems.at[3 * n + 3 * i + j], recv_sem=recv_sems.at[3 * n + 3 * i + j],
                device_id=(x, y, 1 - c), device_id_type=MESH)

        own = [pltpu.make_async_remote_copy(
            src_ref=ins[i], dst_ref=outs[i].at[me], send_sem=own_sems.at[i], recv_sem=own_sems.at[n + i],
            device_id=(x, y, 1 - c), device_id_type=MESH) for i in range(n)]
        pairs = [(i, j, px, py) for i in range(n) for j, (px, py) in enumerate(chips)]
        sends = [over_ici(i, j, half(ins[i], i, c), me, (px, py, c)) for i, j, px, py in pairs]
        lands = [over_ici(i, j, half(outs[i].at[2 * px + py], i, c), 2 * px + py, (px, py, c)) for i, j, px, py in pairs]
        passes = [to_sibling(i, j, 2 * px + py, c) for i, j, px, py in pairs]
        from_sibling = [to_sibling(i, j, 2 * px + py, 1 - c) for i, j, px, py in pairs]
        return own, sends, lands, passes, from_sibling

    def start(ins, outs, sems):
        own, sends, _, _, _ = plan(ins, outs, sems)
        for cp in own + sends:
            cp.start()

    def finish(ins, outs, sems):
        own, sends, lands, passes, from_sibling = plan(ins, outs, sems)
        for land, on in zip(lands, passes):
            land.wait_recv()
            on.start()
        for cp in from_sibling:
            cp.wait_recv()
        for cp in sends + passes:
            cp.wait_send()
        for cp in own:
            cp.wait()

    return _Exchange(
        shards, [_sds((N_CHIPS,) + s.shape, s.dtype) for s in shards],
        [pltpu.SemaphoreType.DMA((6 * n,)), pltpu.SemaphoreType.DMA((6 * n,)), pltpu.SemaphoreType.DMA((2 * n,))],
        start, finish)


def _call_carrying(ex, body, name, grid, in_specs, out_specs, out_shape, scratch_shapes, args):
    n_in, n_out, n_scr = len(in_specs), len(out_specs), len(scratch_shapes)
    k = 0 if ex is None else len(ex.arrays)

    def wrapped(*refs):
        ins, xin = refs[:n_in], refs[n_in:n_in + k]
        outs, xout = refs[n_in + k:n_in + k + n_out], refs[n_in + k + n_out:n_in + 2 * k + n_out]
        scr, sems = refs[n_in + 2 * k + n_out:n_in + 2 * k + n_out + n_scr], refs[n_in + 2 * k + n_out + n_scr:]
        ids = [pl.program_id(d) for d in range(len(grid))]
        if ex is not None:
            @pl.when(functools.reduce(lambda a, b: a & b, [i == 0 for i in ids]))
            def _():
                ex.start(xin, xout, sems)

        body(*ins, *outs, *scr)
        if ex is not None:
            @pl.when(functools.reduce(lambda a, b: a & b, [i == g - 1 for i, g in zip(ids, grid)]))
            def _():
                ex.finish(xin, xout, sems)

    res = pl.pallas_call(
        wrapped, name=name, grid=grid,
        in_specs=list(in_specs) + [ANY] * k, out_specs=list(out_specs) + [ANY] * k,
        out_shape=list(out_shape) + ([] if ex is None else list(ex.out_shapes)),
        scratch_shapes=list(scratch_shapes) + ([] if ex is None else list(ex.sems)),
        compiler_params=_arb(len(grid)),
    )(*args, *([] if ex is None else ex.arrays))
    return res[:n_out], res[n_out:]


def _run_exchange(ex, name):
    n = len(ex.arrays)

    def body(*refs):
        ins, outs, sems = refs[:n], refs[n:2 * n], refs[2 * n:]
        ex.start(ins, outs, sems)
        ex.finish(ins, outs, sems)

    return pl.pallas_call(
        body, name=name, out_shape=ex.out_shapes, in_specs=[ANY] * n, out_specs=[ANY] * n,
        scratch_shapes=ex.sems,
    )(*ex.arrays)


def _pair_exchange(grads, name):
    n = len(grads)

    def body(*refs):
        ins, outs = refs[:n], refs[n:2 * n]
        send_sems, recv_sems = refs[2 * n:]
        x, y, c = _place()
        sent = []
        for i in range(n):
            cp = pltpu.make_async_remote_copy(
                src_ref=ins[i].at[:, 1 - c], dst_ref=outs[i],
                send_sem=send_sems.at[i], recv_sem=recv_sems.at[i],
                device_id=(x, y, 1 - c), device_id_type=MESH)
            cp.start()
            sent.append(cp)
        for cp in sent:
            cp.wait()

    views = [g.reshape(N_CHIPS, 2, g.shape[1] // 2, g.shape[2]) for g in grads]
    return pl.pallas_call(
        body, name=name,
        out_shape=[_sds((N_CHIPS, v.shape[2], v.shape[3]), f32) for v in views],
        in_specs=[ANY] * n, out_specs=[ANY] * n,
        scratch_shapes=[pltpu.SemaphoreType.DMA((n,)), pltpu.SemaphoreType.DMA((n,))],
    )(*views)


def _chip_exchange(parts):
    n = len(parts)

    def plan(ins, outs, sems):
        send_sems, recv_sems = sems
        x, y, c = _place()
        me = 2 * x + y
        pairs = [(i, j, px, py) for i in range(n) for j, (px, py) in enumerate(_other_chips(x, y))]

        def copy(i, j, src, slot, px, py):
            return pltpu.make_async_remote_copy(
                src_ref=src, dst_ref=outs[i].at[slot],
                send_sem=send_sems.at[3 * i + j], recv_sem=recv_sems.at[3 * i + j],
                device_id=(px, py, c), device_id_type=MESH)

        sends = [copy(i, j, ins[i].at[2 * px + py], me, px, py) for i, j, px, py in pairs]
        lands = [copy(i, j, outs[i].at[2 * px + py], 2 * px + py, px, py) for i, j, px, py in pairs]
        return sends, lands

    def start(ins, outs, sems):
        sends, _ = plan(ins, outs, sems)
        for cp in sends:
            cp.start()

    def finish(ins, outs, sems):
        sends, lands = plan(ins, outs, sems)
        for cp in lands:
            cp.wait_recv()
        for cp in sends:
            cp.wait_send()

    return _Exchange(
        parts, [_sds(p.shape, p.dtype) for p in parts],
        [pltpu.SemaphoreType.DMA((3 * n,)), pltpu.SemaphoreType.DMA((3 * n,))],
        start, finish)


def _sibling_exchange(both):
    n = len(both)

    def body(*refs):
        outs = refs[n:2 * n]
        send_sems, recv_sems = refs[2 * n:]
        x, y, c = _place()
        sent = []
        for i in range(n):
            cp = pltpu.make_async_remote_copy(
                src_ref=outs[i].at[c], dst_ref=outs[i].at[c],
                send_sem=send_sems.at[i], recv_sem=recv_sems.at[i],
                device_id=(x, y, 1 - c), device_id_type=MESH)
            cp.start()
            sent.append(cp)
        for cp in sent:
            cp.wait()

    return pl.pallas_call(
        body, name="grad_sibling_exchange",
        out_shape=[_sds(b.shape, f32) for b in both],
        in_specs=[ANY] * n, out_specs=[ANY] * n,
        input_output_aliases={i: i for i in range(n)},
        scratch_shapes=[pltpu.SemaphoreType.DMA((n,)), pltpu.SemaphoreType.DMA((n,))],
    )(*both)


def _row_tile(rows, cols):
    t = rows
    while t * cols * 4 > (1 << 20) and t % 16 == 0:
        t //= 2
    return t


def _pair_sum(grad, recv, core, name):
    _, r, c = grad.shape
    hr = r // 2
    tr = _row_tile(hr, c)
    view = grad.reshape(N_CHIPS, 2, hr, c)

    def body(core_ref, mine_ref, recv_ref, out_ref):
        out_ref[...] = (mine_ref[...] + recv_ref[...]).astype(bf16)

    return pl.pallas_call(
        body, name=name,
        grid_spec=pltpu.PrefetchScalarGridSpec(
            num_scalar_prefetch=1, grid=(N_CHIPS, hr // tr),
            in_specs=[pl.BlockSpec((None, None, tr, c), lambda k, t, core_ref: (k, core_ref[0], t, 0)),
                      pl.BlockSpec((None, tr, c), lambda k, t, core_ref: (k, t, 0))],
            out_specs=pl.BlockSpec((None, tr, c), lambda k, t, core_ref: (k, t, 0))),
        out_shape=_sds((N_CHIPS, hr, c), bf16),
        compiler_params=_arb(2),
    )(core, view, recv)


def _chip_sum(mine, landed, place, name):
    _, hr, c = mine.shape
    tr = _row_tile(hr, c)

    def body(place_ref, a_ref, b_ref, c_ref, d_ref, out_ref):
        out_ref[...] = ((a_ref[...].astype(f32) + b_ref[...].astype(f32)) + c_ref[...].astype(f32)) + d_ref[...].astype(f32)

    def slot(k):
        return pl.BlockSpec((None, tr, c), lambda t, place_ref: ((place_ref[1] + k) % N_CHIPS, t, 0))

    return pl.pallas_call(
        body, name=name,
        grid_spec=pltpu.PrefetchScalarGridSpec(
            num_scalar_prefetch=1, grid=(hr // tr,),
            in_specs=[slot(0), slot(1), slot(2), slot(3)],
            out_specs=pl.BlockSpec((None, tr, c), lambda t, place_ref: (place_ref[0], t, 0))),
        out_shape=_sds((2, hr, c), f32), compiler_params=_arb(1),
    )(place, mine, landed, landed, landed)


def _adamw(g, w, m, v, name):
    r, c = g.shape
    tr = _row_tile(r, c)

    def body(g_ref, w_ref, m_ref, v_ref, d_ref, nm_ref, nv_ref):
        gv = g_ref[...]
        mv = ADAM_B1 * m_ref[...] + (1.0 - ADAM_B1) * gv
        vv = ADAM_B2 * v_ref[...] + (1.0 - ADAM_B2) * jnp.square(gv)
        m_hat = mv / (1.0 - ADAM_B1 ** ADAM_STEP)
        v_hat = vv / (1.0 - ADAM_B2 ** ADAM_STEP)
        d_ref[...] = -ADAM_LR * (m_hat / (jnp.sqrt(v_hat) + ADAM_EPS) + ADAM_WD * w_ref[...])
        nm_ref[...] = mv
        nv_ref[...] = vv

    spec = pl.BlockSpec((tr, c), lambda t: (t, 0))
    return pl.pallas_call(
        body, name=name, grid=(r // tr,), in_specs=[spec] * 4, out_specs=[spec] * 3,
        out_shape=[_sds((r, c), f32)] * 3, compiler_params=_arb(1),
    )(g, w, m, v)


def _allreduce_small(packed):
    shape = packed.shape

    def body(x_ref, out_ref, buf, send_sems, recv_sems):
        x, y, c = _place()
        me = 4 * x + 2 * y + c
        buf[me] = x_ref[...]
        sent = []
        for r in range(1, 8):
            dx, dy, dc = (r >> 2) & 1, (r >> 1) & 1, r & 1
            peer = ((1 - x) if dx else x, (1 - y) if dy else y, (1 - c) if dc else c)
            cp = pltpu.make_async_remote_copy(
                src_ref=x_ref, dst_ref=buf.at[me],
                send_sem=send_sems.at[r], recv_sem=recv_sems.at[r],
                device_id=peer, device_id_type=MESH)
            cp.start()
            sent.append((cp, peer))
        for r, (cp, peer) in enumerate(sent, start=1):
            src = 4 * peer[0] + 2 * peer[1] + peer[2]
            pltpu.make_async_remote_copy(
                src_ref=x_ref, dst_ref=buf.at[src],
                send_sem=send_sems.at[r], recv_sem=recv_sems.at[r],
                device_id=peer, device_id_type=MESH).wait_recv()
        for cp, _ in sent:
            cp.wait_send()
        total = buf[0]
        for k in range(1, 8):
            total = total + buf[k]
        out_ref[...] = total

    vmem = pl.BlockSpec(memory_space=pltpu.VMEM)
    return pl.pallas_call(
        body, name="allreduce_small", out_shape=_sds(shape, f32),
        in_specs=[vmem], out_specs=vmem,
        scratch_shapes=[pltpu.VMEM((8,) + shape, f32), pltpu.SemaphoreType.DMA((8,)),
                        pltpu.SemaphoreType.DMA((8,))],
    )(packed)


def _in_proj(x, g_mix, w_in, tm):
    s = x.shape[0]
    ncol = IN_COLS // N_CHIPS

    def body(x_ref, g_ref, w_ref, proj_ref, a_ref):
        a, _ = _rms_fwd(x_ref[...], g_ref[...])
        ab = a.astype(bf16)
        a_ref[...] = ab
        for k in range(N_CHIPS):
            proj_ref[:, k * ncol:(k + 1) * ncol] = _dot(ab, w_ref[k])

    return pl.pallas_call(
        body, name="in_proj", grid=(s // tm,),
        in_specs=[pl.BlockSpec((tm, D_MODEL), lambda i: (i, 0)),
                  pl.BlockSpec((1, D_MODEL), lambda i: (0, 0)),
                  pl.BlockSpec((N_CHIPS, D_MODEL, ncol), lambda i: (0, 0, 0))],
        out_specs=[pl.BlockSpec((tm, IN_COLS), lambda i: (i, 0)),
                   pl.BlockSpec((tm, D_MODEL), lambda i: (i, 0))],
        out_shape=[_sds((s, IN_COLS), f32), _sds((s, D_MODEL), bf16)],
        compiler_params=_arb(1),
    )(x, g_mix, w_in)


def _shifted(prev8, cur, shift):
    ext = jnp.concatenate([prev8, cur], axis=0)
    return pltpu.roll(ext, shift, axis=0)[8:]


def _conv_fwd(proj, conv_w, g_conv):
    s = proj.shape[0]
    nblk = W_CONV // LANES
    rc = min(CONV_CHUNK, s)

    def body(cb_ref, cc_ref, cu_ref, w_ref, g_ref, out_ref):
        ones = _group_ones(LANES)
        w0, w1, w2 = w_ref[0:1, :], w_ref[1:2, :], w_ref[2:3, :]
        g = g_ref[...]

        def chunk(i, carry):
            r0 = pl.multiple_of(i * rc, rc)
            rows = pl.ds(r0, rc)
            prev = pl.ds(pl.multiple_of(jnp.maximum(r0 - 8, 0), 8), 8)
            v = cc_ref[rows, :] * cu_ref[rows, :]
            vp = jnp.where(i > 0, cc_ref[prev, :] * cu_ref[prev, :], 0.0)
            y = w2 * v + w1 * _shifted(vp, v, 1) + w0 * _shifted(vp, v, 2)
            out_ref[rows, :] = _head_rms_fwd(cb_ref[rows, :] * y, g, ones).astype(bf16)
            return carry

        lax.fori_loop(0, s // rc, chunk, 0)

    def col(off):
        return pl.BlockSpec((s, LANES), lambda j: (0, off + j))

    return pl.pallas_call(
        body, name="conv_fwd", grid=(nblk,),
        in_specs=[col(0), col(nblk), col(2 * nblk),
                  pl.BlockSpec((None, CONV_W_ROWS, LANES), lambda j: (j, 0, 0)),
                  pl.BlockSpec((1, LANES), lambda j: (0, j))],
        out_specs=pl.BlockSpec((s, LANES), lambda j: (0, j)),
        out_shape=_sds((s, W_CONV), bf16), compiler_params=_arb(1),
    )(proj, proj, proj, conv_w, g_conv)


LOG2_E = 1.4426950408889634


def _log2_keep(z2):
    nz2 = -z2
    return jnp.minimum(nz2, 0.0) - jnp.log2(1.0 + jnp.exp2(jnp.minimum(z2, nz2)))


def _head_pair_masks(rows):
    lane = lax.broadcasted_iota(jnp.int32, (rows, LANES), 1)
    return lane < HEAD_DIM


SUBLANES = 8
KEY_RUN = ATTN_TILE // SUBLANES


def _permute_keys(a):
    s, w = a.shape
    return a.reshape(s // ATTN_TILE, SUBLANES, KEY_RUN, w).transpose(0, 2, 1, 3).reshape(s, w)


def _unpermute_keys(a):
    s, w = a.shape
    return a.reshape(s // ATTN_TILE, KEY_RUN, SUBLANES, w).transpose(0, 2, 1, 3).reshape(s, w)


def _causal_tiles():
    r = lax.broadcasted_iota(jnp.int32, (ATTN_TILE, ATTN_TILE), 0)
    key = (r % SUBLANES) * KEY_RUN + r // SUBLANES
    below = key < lax.broadcasted_iota(jnp.int32, (ATTN_TILE, ATTN_TILE), 1)
    return below.astype(f32), jnp.where(below, 0.0, -1e30).astype(f32)


def _sublane_scan(x, reverse):
    row = lax.broadcasted_iota(jnp.int32, x.shape, 0)
    inc = x
    for sh in (1, 2, 4):
        if reverse:
            inc = inc + jnp.where(row < SUBLANES - sh, pltpu.roll(inc, SUBLANES - sh, axis=0), 0.0)
        else:
            inc = inc + jnp.where(row >= sh, pltpu.roll(inc, sh, axis=0), 0.0)
    return inc - x


def _attn_fwd(proj, kp, vt, ex=None):
    s = proj.shape[0]
    t = ATTN_TILE
    nblk = s // t
    npair = W_ATTN // LANES
    qoff = 3 * W_CONV // LANES
    keep01, keepneg = _causal_tiles()

    def body(q_ref, k_ref, vt_ref, m01_ref, neg_ref, o_ref, tot_ref, w_s, a_s, acc):
        qb = pl.program_id(1)
        first = _head_pair_masks(t)
        q = q_ref[...] * (HEAD_DIM ** -0.5)
        qh = (jnp.where(first, q, 0.0).astype(bf16), jnp.where(first, 0.0, q).astype(bf16))
        acc[...] = jnp.zeros_like(acc)
        a_s[1] = jnp.zeros((t, t), bf16)

        def scores(kb, h):
            w_s[h] = _dot(k_ref[kb], qh[h], NT)

        def weigh(kb, h):
            acc[h] += _dot(vt_ref[kb], a_s[h])

        def weights(h, diagonal, later):
            run = jnp.zeros((SUBLANES, t), f32)
            for a in reversed(range(KEY_RUN)):
                rows = slice(SUBLANES * a, SUBLANES * (a + 1))
                z2 = w_s[h, rows, :] * LOG2_E
                lk = _log2_keep(z2)
                if diagonal:
                    lk = lk * m01_ref[rows, :]
                run = run + lk
                w_s[h, rows, :] = z2 + run
            off = _sublane_scan(run, reverse=True) + later
            off2 = jnp.concatenate([off, off], axis=0)
            for a in range(t // (2 * SUBLANES)):
                rows = slice(2 * SUBLANES * a, 2 * SUBLANES * (a + 1))
                w = w_s[h, rows, :] + off2
                if diagonal:
                    w = w + neg_ref[rows, :]
                a_s[h, rows, :] = jnp.exp2(w).astype(bf16)
            return later + jnp.sum(run, axis=0, keepdims=True)

        def block(kb, before, after, diagonal, later):
            scores(kb, 1)
            weigh(before, 1)
            l0 = weights(0, diagonal, later[0])
            scores(after, 0)
            weigh(kb, 0)
            l1 = weights(1, diagonal, later[1])
            return l0, l1

        zero = jnp.zeros((1, t), f32)
        scores(qb, 0)
        later = block(qb, qb, jnp.maximum(qb - 1, 0), True, (zero, zero))

        def earlier(i, c):
            kb = qb - 1 - i
            return block(kb, kb + 1, jnp.maximum(kb - 1, 0), False, c)

        later = lax.fori_loop(0, qb, earlier, later)
        weigh(0, 1)
        top = lax.broadcasted_iota(jnp.int32, (LANES, t), 0) < HEAD_DIM
        o_ref[...] = jnp.where(top, acc[0], acc[1]).T
        tot_ref[...] = jnp.concatenate([later[0], later[1], jnp.zeros((SUBLANES - 2, t), f32)], axis=0)

    return _call_carrying(
        ex, body, "attn_fwd", (npair, nblk),
        in_specs=[pl.BlockSpec((t, LANES), lambda p, i: (i, qoff + p)),
                  pl.BlockSpec((nblk, t, LANES), lambda p, i: (0, 0, p)),
                  pl.BlockSpec((nblk, LANES, t), lambda p, i: (0, p, 0)),
                  pl.BlockSpec((t, t), lambda p, i: (0, 0)),
                  pl.BlockSpec((t, t), lambda p, i: (0, 0))],
        out_specs=[pl.BlockSpec((t, LANES), lambda p, i: (i, p)),
                   pl.BlockSpec((None, SUBLANES, t), lambda p, i: (p, 0, i))],
        out_shape=[_sds((s, W_ATTN), f32), _sds((npair, SUBLANES, s), f32)],
        scratch_shapes=[pltpu.VMEM((2, t, t), f32), pltpu.VMEM((2, t, t), bf16), pltpu.VMEM((2, LANES, t), f32)],
        args=(proj, kp, vt, keep01, keepneg))


def _out_proj(o, conv_n, x, g_attn, w_out, tm):
    s = x.shape[0]

    def body(o_ref, c_ref, x_ref, g_ref, w_ref, h_ref, cat_ref):
        ones = _group_ones(LANES)
        cat_ref[:, :W_CONV] = c_ref[...]
        for j in range(W_ATTN // LANES):
            cols = slice(j * LANES, (j + 1) * LANES)
            cat_ref[:, W_CONV + j * LANES:W_CONV + (j + 1) * LANES] = _head_rms_fwd(
                o_ref[:, cols], g_ref[:, cols], ones).astype(bf16)
        h_ref[...] = x_ref[...] + _dot(cat_ref[...], w_ref[...])

    return pl.pallas_call(
        body, name="out_proj", grid=(s // tm,),
        in_specs=[pl.BlockSpec((tm, W_ATTN), lambda i: (i, 0)),
                  pl.BlockSpec((tm, W_CONV), lambda i: (i, 0)),
                  pl.BlockSpec((tm, D_MODEL), lambda i: (i, 0)),
                  pl.BlockSpec((1, W_ATTN), lambda i: (0, 0)),
                  pl.BlockSpec((D_MODEL, D_MODEL), lambda i: (0, 0))],
        out_specs=[pl.BlockSpec((tm, D_MODEL), lambda i: (i, 0)),
                   pl.BlockSpec((tm, D_MODEL), lambda i: (i, 0))],
        out_shape=[_sds((s, D_MODEL), f32), _sds((s, D_MODEL), bf16)],
        compiler_params=_arb(1),
    )(o, conv_n, x, g_attn, w_out)


def _mlp_fwd(h1, g_mlp, w_up, w_down, tm):
    s = h1.shape[0]
    fc = D_FF // N_CHIPS

    def body(h_ref, g_ref, wu_ref, wd_ref, h2_ref, u_ref, m_ref):
        j = pl.program_id(1)

        @pl.when(j == 0)
        def _():
            m, _ = _rms_fwd(h_ref[...], g_ref[...])
            m_ref[...] = m.astype(bf16)
            h2_ref[...] = h_ref[...]

        u = _dot(m_ref[...], wu_ref[...])
        u_ref[...] = u.astype(bf16)
        h2_ref[...] += _dot(jnp.square(jnp.maximum(u, 0.0)).astype(bf16), wd_ref[...])

    return pl.pallas_call(
        body, name="mlp_fwd", grid=(s // tm, N_CHIPS),
        in_specs=[pl.BlockSpec((tm, D_MODEL), lambda i, j: (i, 0)),
                  pl.BlockSpec((1, D_MODEL), lambda i, j: (0, 0)),
                  pl.BlockSpec((None, D_MODEL, fc), lambda i, j: (j, 0, 0)),
                  pl.BlockSpec((None, fc, D_MODEL), lambda i, j: (j, 0, 0))],
        out_specs=[pl.BlockSpec((tm, D_MODEL), lambda i, j: (i, 0)),
                   pl.BlockSpec((tm, fc), lambda i, j: (i, j)),
                   pl.BlockSpec((tm, D_MODEL), lambda i, j: (i, 0))],
        out_shape=[_sds((s, D_MODEL), f32), _sds((s, D_FF), bf16), _sds((s, D_MODEL), bf16)],
        compiler_params=_arb(2),
    )(h1, g_mlp, w_up, w_down)


def _tail(h2, p, target, g_ple, g_final, w_gate, w_proj, tm):
    s = h2.shape[0]
    pc = D_MODEL // N_CHIPS

    def body(h_ref, p_ref, t_ref, gp_ref, gf_ref, wg_ref, wp_ref,
             dh_ref, dhb_ref, n3_ref, dgl_ref, dpp_ref, pb_ref, ggp_ref, ggf_ref, loss_ref, pp_ref):
        i = pl.program_id(0)
        h2v = h_ref[...]
        n3, _ = _rms_fwd(h2v, gp_ref[...])
        n3b = n3.astype(bf16)
        n3_ref[...] = n3b
        gate = jax.nn.sigmoid(_dot(n3b, wg_ref[...]))
        pb = p_ref[...].astype(bf16)
        pb_ref[...] = pb
        for k in range(N_CHIPS):
            pp_ref[:, k * pc:(k + 1) * pc] = _dot(pb, wp_ref[k])
        pp = pp_ref[...]
        h3 = h2v + gate * pp
        yv, _ = _rms_fwd(h3, gf_ref[...])
        err = yv - t_ref[...]
        loss = 0.5 * jnp.sum(err * err) * (1.0 / D_MODEL)
        dh3, ggf = _rms_bwd(err * (1.0 / D_MODEL), h3, gf_ref[...])
        dpp_ref[...] = (dh3 * gate).astype(bf16)
        dgl = (dh3 * pp * gate * (1.0 - gate)).astype(bf16)
        dgl_ref[...] = dgl
        dn3 = _dot(dgl, wg_ref[...], NT)
        dh2n, ggp = _rms_bwd(dn3, h2v, gp_ref[...])
        dh2 = dh3 + dh2n
        dh_ref[...] = dh2
        dhb_ref[...] = dh2.astype(bf16)

        @pl.when(i == 0)
        def _():
            ggp_ref[...] = jnp.zeros_like(ggp_ref)
            ggf_ref[...] = jnp.zeros_like(ggf_ref)
            loss_ref[...] = jnp.zeros_like(loss_ref)

        ggp_ref[...] += ggp
        ggf_ref[...] += ggf
        loss_ref[...] += jnp.full(loss_ref.shape, loss, f32)

    tok = lambda w: pl.BlockSpec((tm, w), lambda i: (i, 0))
    vec = lambda w: pl.BlockSpec((1, w), lambda i: (0, 0))
    return pl.pallas_call(
        body, name="tail", grid=(s // tm,),
        in_specs=[tok(D_MODEL), tok(PLE_DIM), tok(D_MODEL), vec(D_MODEL), vec(D_MODEL),
                  pl.BlockSpec((D_MODEL, D_MODEL), lambda i: (0, 0)),
                  pl.BlockSpec((N_CHIPS, PLE_DIM, pc), lambda i: (0, 0, 0))],
        out_specs=[tok(D_MODEL), tok(D_MODEL), tok(D_MODEL), tok(D_MODEL), tok(D_MODEL), tok(PLE_DIM),
                   vec(D_MODEL), vec(D_MODEL), vec(LANES)],
        out_shape=[_sds((s, D_MODEL), f32), _sds((s, D_MODEL), bf16), _sds((s, D_MODEL), bf16),
                   _sds((s, D_MODEL), bf16), _sds((s, D_MODEL), bf16), _sds((s, PLE_DIM), bf16),
                   _sds((1, D_MODEL), f32), _sds((1, D_MODEL), f32), _sds((1, LANES), f32)],
        scratch_shapes=[pltpu.VMEM((tm, D_MODEL), f32)],
        compiler_params=_arb(1),
    )(h2, p, target, g_ple, g_final, w_gate, w_proj)


def _mlp_bwd(dh2, dh2b, h1, u, g_mlp, w_up, w_down, tm):
    s = h1.shape[0]
    fc = D_FF // N_CHIPS

    def body(dh_ref, dhb_ref, h_ref, u_ref, g_ref, wu_ref, wd_ref, dh1_ref, dh1b_ref, du_ref, gg_ref, dm):
        i, j = pl.program_id(0), pl.program_id(1)

        @pl.when(j == 0)
        def _():
            dm[...] = jnp.zeros_like(dm)

        dr = _dot(dhb_ref[...], wd_ref[...], NT)
        du = (dr * (2.0 * jnp.maximum(u_ref[...].astype(f32), 0.0))).astype(bf16)
        du_ref[...] = du
        dm[...] += _dot(du, wu_ref[...], NT)

        @pl.when((i == 0) & (j == 0))
        def _():
            gg_ref[...] = jnp.zeros_like(gg_ref)

        @pl.when(j == N_CHIPS - 1)
        def _():
            dh1n, gg = _rms_bwd(dm[...], h_ref[...], g_ref[...])
            dh1 = dh_ref[...] + dh1n
            dh1_ref[...] = dh1
            dh1b_ref[...] = dh1.astype(bf16)
            gg_ref[...] += gg

    tok = pl.BlockSpec((tm, D_MODEL), lambda i, j: (i, 0))
    ffb = pl.BlockSpec((tm, fc), lambda i, j: (i, j))
    vec = pl.BlockSpec((1, D_MODEL), lambda i, j: (0, 0))
    return pl.pallas_call(
        body, name="mlp_bwd", grid=(s // tm, N_CHIPS),
        in_specs=[tok, tok, tok, ffb, vec,
                  pl.BlockSpec((None, D_MODEL, fc), lambda i, j: (j, 0, 0)),
                  pl.BlockSpec((None, fc, D_MODEL), lambda i, j: (j, 0, 0))],
        out_specs=[tok, tok, ffb, vec],
        out_shape=[_sds((s, D_MODEL), f32), _sds((s, D_MODEL), bf16), _sds((s, D_FF), bf16),
                   _sds((1, D_MODEL), f32)],
        scratch_shapes=[pltpu.VMEM((tm, D_MODEL), f32)],
        compiler_params=_arb(2),
    )(dh2, dh2b, h1, u, g_mlp, w_up, w_down)


def _out_proj_bwd(dh1b, o, g_attn, w_out, tm):
    s = o.shape[0]

    def body(dh_ref, o_ref, g_ref, w_ref, dc_ref, do_ref, gg_ref, dcat):
        i = pl.program_id(0)
        ones = _group_ones(LANES)
        dcat[...] = _dot(dh_ref[...], w_ref[...], NT)
        dc_ref[...] = dcat[:, :W_CONV]

        @pl.when(i == 0)
        def _():
            gg_ref[...] = jnp.zeros_like(gg_ref)

        for j in range(W_ATTN // LANES):
            cols = slice(j * LANES, (j + 1) * LANES)
            d, gg = _head_rms_bwd(dcat[:, W_CONV + j * LANES:W_CONV + (j + 1) * LANES],
                                  o_ref[:, cols], g_ref[:, cols], ones)
            do_ref[:, cols] = d
            gg_ref[:, cols] += gg

    return pl.pallas_call(
        body, name="out_proj_bwd", grid=(s // tm,),
        in_specs=[pl.BlockSpec((tm, D_MODEL), lambda i: (i, 0)),
                  pl.BlockSpec((tm, W_ATTN), lambda i: (i, 0)),
                  pl.BlockSpec((1, W_ATTN), lambda i: (0, 0)),
                  pl.BlockSpec((D_MODEL, D_MODEL), lambda i: (0, 0))],
        out_specs=[pl.BlockSpec((tm, W_CONV), lambda i: (i, 0)),
                   pl.BlockSpec((tm, W_ATTN), lambda i: (i, 0)),
                   pl.BlockSpec((1, W_ATTN), lambda i: (0, 0))],
        out_shape=[_sds((s, W_CONV), f32), _sds((s, W_ATTN), f32), _sds((1, W_ATTN), f32)],
        scratch_shapes=[pltpu.VMEM((tm, D_MODEL), f32)],
        compiler_params=_arb(1),
    )(dh1b, o, g_attn, w_out)


def _attn_bwd(proj, kp, vp, kt, do, tot, ex=None):
    s = proj.shape[0]
    t = ATTN_TILE
    nblk = s // t
    npair = W_ATTN // LANES
    qoff = 3 * W_CONV // LANES
    keep01, keepneg = _causal_tiles()

    def body(q_ref, k_ref, v_ref, kt_ref, do_ref, tot_ref, m01_ref, neg_ref, dq_ref, dk_ref, dv_ref,
             w_s, da_s, b_s, g_s, a_s, dz_s, dq_acc, dk_acc, dv_acc):
        qb = pl.program_id(1)
        first = _head_pair_masks(t)
        q = q_ref[...] * (HEAD_DIM ** -0.5)
        qh = (jnp.where(first, q, 0.0).astype(bf16), jnp.where(first, 0.0, q).astype(bf16))
        dov = do_ref[...]
        doh = (jnp.where(first, dov, 0.0).astype(bf16), jnp.where(first, 0.0, dov).astype(bf16))
        total = (tot_ref[0:1, :], tot_ref[1:2, :])

        @pl.when(qb == 0)
        def _():
            dk_acc[...] = jnp.zeros_like(dk_acc)
            dv_acc[...] = jnp.zeros_like(dv_acc)

        dq_acc[...] = jnp.zeros_like(dq_acc)
        a_s[1] = jnp.zeros((t, t), bf16)
        dz_s[1] = jnp.zeros((t, t), bf16)

        def scores(kb, h):
            w_s[h] = _dot(k_ref[kb], qh[h], NT)
            da_s[h] = _dot(v_ref[kb], doh[h], NT)

        def spread(kb, h):
            dq_acc[h] += _dot(kt_ref[kb], dz_s[h])
            dk_acc[kb] += _dot(dz_s[h], qh[h])
            dv_acc[kb] += _dot(a_s[h], doh[h])

        def grads(h, diagonal, lk_before, g_before):
            run = jnp.zeros((SUBLANES, t), f32)
            for a in range(KEY_RUN):
                rows = slice(SUBLANES * a, SUBLANES * (a + 1))
                z2 = w_s[h, rows, :] * LOG2_E
                lk = _log2_keep(z2)
                if diagonal:
                    lk = lk * m01_ref[rows, :]
                log_beta = jnp.minimum(z2 + lk, 0.0)
                run = run + lk
                b_s[h, rows, :] = jnp.exp2(log_beta)
                w_s[h, rows, :] = log_beta - run
            off = total[h] - lk_before - _sublane_scan(run, reverse=False)
            lk_sum = jnp.sum(run, axis=0, keepdims=True)
            run = jnp.zeros((SUBLANES, t), f32)
            for a in range(KEY_RUN // 2):
                parts = []
                for r in (slice(2 * SUBLANES * a, 2 * SUBLANES * a + SUBLANES),
                          slice(2 * SUBLANES * a + SUBLANES, 2 * SUBLANES * (a + 1))):
                    w = w_s[h, r, :] + off
                    if diagonal:
                        w = w + neg_ref[r, :]
                    av = jnp.exp2(w)
                    g = av * da_s[h, r, :]
                    run = run + g
                    da_s[h, r, :] = g
                    g_s[h, r, :] = run
                    parts.append(av)
                a_s[h, 2 * SUBLANES * a:2 * SUBLANES * (a + 1), :] = jnp.concatenate(parts, axis=0).astype(bf16)
            goff = g_before + _sublane_scan(run, reverse=False)
            goff2 = jnp.concatenate([goff, goff], axis=0)
            for a in range(KEY_RUN // 2):
                rows = slice(2 * SUBLANES * a, 2 * SUBLANES * (a + 1))
                dz = da_s[h, rows, :] - b_s[h, rows, :] * (g_s[h, rows, :] + goff2)
                if diagonal:
                    dz = dz * m01_ref[rows, :]
                dz_s[h, rows, :] = dz.astype(bf16)
            return lk_before + lk_sum, g_before + jnp.sum(run, axis=0, keepdims=True)

        def block(kb, before, after, diagonal, carry):
            scores(kb, 1)
            spread(before, 1)
            c0 = grads(0, diagonal, carry[0], carry[1])
            if after is not None:
                scores(after, 0)
            spread(kb, 0)
            c1 = grads(1, diagonal, carry[2], carry[3])
            return c0 + c1

        zero = jnp.zeros((1, t), f32)
        scores(0, 0)
        carry = lax.fori_loop(0, qb, lambda kb, c: block(kb, jnp.maximum(kb - 1, 0), kb + 1, False, c),
                              (zero, zero, zero, zero))
        block(qb, jnp.maximum(qb - 1, 0), None, True, carry)
        spread(qb, 1)
        top = lax.broadcasted_iota(jnp.int32, (LANES, t), 0) < HEAD_DIM
        dq_ref[...] = (jnp.where(top, dq_acc[0], dq_acc[1]).T * (HEAD_DIM ** -0.5)).astype(bf16)

        @pl.when(qb == nblk - 1)
        def _():
            dk_ref[...] = dk_acc[...].astype(bf16)
            dv_ref[...] = dv_acc[...].astype(bf16)

    keys = pl.BlockSpec((nblk, t, LANES), lambda p, i: (0, 0, p))
    tile = pl.BlockSpec((t, t), lambda p, i: (0, 0))
    return _call_carrying(
        ex, body, "attn_bwd", (npair, nblk),
        in_specs=[pl.BlockSpec((t, LANES), lambda p, i: (i, qoff + p)),
                  keys, keys,
                  pl.BlockSpec((nblk, LANES, t), lambda p, i: (0, p, 0)),
                  pl.BlockSpec((t, LANES), lambda p, i: (i, p)),
                  pl.BlockSpec((None, SUBLANES, t), lambda p, i: (p, 0, i)),
                  tile, tile],
        out_specs=[pl.BlockSpec((t, LANES), lambda p, i: (i, p)), keys, keys],
        out_shape=[_sds((s, W_ATTN), bf16), _sds((nblk, t, W_ATTN), bf16), _sds((nblk, t, W_ATTN), bf16)],
        scratch_shapes=[pltpu.VMEM((2, t, t), f32), pltpu.VMEM((2, t, t), f32), pltpu.VMEM((2, t, t), f32),
                        pltpu.VMEM((2, t, t), f32), pltpu.VMEM((2, t, t), bf16), pltpu.VMEM((2, t, t), bf16),
                        pltpu.VMEM((2, LANES, t), f32), pltpu.VMEM((nblk, t, LANES), f32),
                        pltpu.VMEM((nblk, t, LANES), f32)],
        args=(proj, kp, vp, kt, do, tot, keep01, keepneg))


def _conv_bwd(proj, dcn, conv_w, g_conv):
    s = proj.shape[0]
    nblk = W_CONV // LANES
    rc = min(CONV_CHUNK, s)
    nchunk = s // rc

    def body(cb_ref, cc_ref, cu_ref, d_ref, w_ref, g_ref, dcb_ref, dcc_ref, dcu_ref, gw_ref, gg_ref, dy_buf):
        ones = _group_ones(LANES)
        w0, w1, w2 = w_ref[0:1, :], w_ref[1:2, :], w_ref[2:3, :]
        g = g_ref[...]

        def first_pass(i, carry):
            gw0, gw1, gw2, gg = carry
            r0 = pl.multiple_of(i * rc, rc)
            rows = pl.ds(r0, rc)
            prev = pl.ds(pl.multiple_of(jnp.maximum(r0 - 8, 0), 8), 8)
            v = cc_ref[rows, :] * cu_ref[rows, :]
            vp = jnp.where(i > 0, cc_ref[prev, :] * cu_ref[prev, :], 0.0)
            v1, v2 = _shifted(vp, v, 1), _shifted(vp, v, 2)
            y = w2 * v + w1 * v1 + w0 * v2
            cb = cb_ref[rows, :]
            dcy, ggi = _head_rms_bwd(d_ref[rows, :], cb * y, g, ones)
            dcb_ref[rows, :] = (dcy * y).astype(bf16)
            dy = dcy * cb
            dy_buf[rows, :] = dy
            return (gw0 + jnp.sum(dy * v2, axis=0, keepdims=True), gw1 + jnp.sum(dy * v1, axis=0, keepdims=True),
                    gw2 + jnp.sum(dy * v, axis=0, keepdims=True), gg + ggi)

        zero = jnp.zeros((1, LANES), f32)
        gw0, gw1, gw2, gg = lax.fori_loop(0, nchunk, first_pass, (zero, zero, zero, zero))
        gw_ref[...] = jnp.zeros_like(gw_ref)
        gw_ref[0:1, :] = gw0
        gw_ref[1:2, :] = gw1
        gw_ref[2:3, :] = gw2
        gg_ref[...] = gg

        def second_pass(i, carry):
            r0 = pl.multiple_of(i * rc, rc)
            rows = pl.ds(r0, rc)
            nxt = pl.ds(pl.multiple_of(jnp.minimum(r0 + rc, s - 8), 8), 8)
            dy = dy_buf[rows, :]
            dyn = jnp.where(i < nchunk - 1, dy_buf[nxt, :], 0.0)
            ext = jnp.concatenate([dy, dyn], axis=0)
            up1 = pltpu.roll(ext, rc + 8 - 1, axis=0)[:rc]
            up2 = pltpu.roll(ext, rc + 8 - 2, axis=0)[:rc]
            dv = w2 * dy + w1 * up1 + w0 * up2
            dcc_ref[rows, :] = (dv * cu_ref[rows, :]).astype(bf16)
            dcu_ref[rows, :] = (dv * cc_ref[rows, :]).astype(bf16)
            return carry

        lax.fori_loop(0, nchunk, second_pass, 0)

    def col(off):
        return pl.BlockSpec((s, LANES), lambda j: (0, off + j))

    return pl.pallas_call(
        body, name="conv_bwd", grid=(nblk,),
        in_specs=[col(0), col(nblk), col(2 * nblk), col(0),
                  pl.BlockSpec((None, CONV_W_ROWS, LANES), lambda j: (j, 0, 0)),
                  pl.BlockSpec((1, LANES), lambda j: (0, j))],
        out_specs=[col(0), col(0), col(0),
                   pl.BlockSpec((None, CONV_W_ROWS, LANES), lambda j: (j, 0, 0)),
                   pl.BlockSpec((1, LANES), lambda j: (0, j))],
        out_shape=[_sds((s, W_CONV), bf16)] * 3 + [_sds((nblk, CONV_W_ROWS, LANES), f32), _sds((1, W_CONV), f32)],
        scratch_shapes=[pltpu.VMEM((s, LANES), f32)],
        compiler_params=_arb(1),
    )(proj, proj, proj, dcn, conv_w, g_conv)


def _in_proj_bwd(dproj, dh1, x, g_mix, w_in, tm, ex=None):
    s = x.shape[0]
    ncol = IN_COLS // N_CHIPS

    def body(dp_ref, dh_ref, x_ref, g_ref, w_ref, dx_ref, gg_ref):
        i = pl.program_id(0)
        da = _dot(dp_ref[:, 0:ncol], w_ref[0], NT)
        for k in range(1, N_CHIPS):
            da += _dot(dp_ref[:, k * ncol:(k + 1) * ncol], w_ref[k], NT)
        dxn, gg = _rms_bwd(da, x_ref[...], g_ref[...])
        dx_ref[...] = dh_ref[...] + dxn

        @pl.when(i == 0)
        def _():
            gg_ref[...] = jnp.zeros_like(gg_ref)

        gg_ref[...] += gg

    return _call_carrying(
        ex, body, "in_proj_bwd", (s // tm,),
        in_specs=[pl.BlockSpec((tm, IN_COLS), lambda i: (i, 0)),
                  pl.BlockSpec((tm, D_MODEL), lambda i: (i, 0)),
                  pl.BlockSpec((tm, D_MODEL), lambda i: (i, 0)),
                  pl.BlockSpec((1, D_MODEL), lambda i: (0, 0)),
                  pl.BlockSpec((N_CHIPS, D_MODEL, ncol), lambda i: (0, 0, 0))],
        out_specs=[pl.BlockSpec((tm, D_MODEL), lambda i: (i, 0)),
                   pl.BlockSpec((1, D_MODEL), lambda i: (0, 0))],
        out_shape=[_sds((s, D_MODEL), f32), _sds((1, D_MODEL), f32)],
        scratch_shapes=[],
        args=(dproj, dh1, x, g_mix, w_in))


def _weight_grad(a, b, bm, bn, ts, name, relu_sq=False):
    s, m = a.shape
    n = b.shape[1]
    nn = n // bn

    def body(a_ref, b_ref, o_ref):
        @pl.when(pl.program_id(2) == 0)
        def _():
            o_ref[...] = jnp.zeros_like(o_ref)

        av = a_ref[...]
        if relu_sq:
            av = jnp.square(jnp.maximum(av.astype(f32), 0.0)).astype(bf16)
        o_ref[...] += _dot(av, b_ref[...], TN)

    return pl.pallas_call(
        body, name=name, grid=(m // bm, nn, s // ts),
        in_specs=[pl.BlockSpec((ts, bm), lambda i, j, k: (k, i)),
                  pl.BlockSpec((ts, bn), lambda i, j, k: (k, j))],
        out_specs=pl.BlockSpec((None, bm, bn), lambda i, j, k: (i * nn + j, 0, 0)),
        out_shape=_sds(((m // bm) * nn, bm, bn), f32),
        compiler_params=_arb(3),
    )(a, b)


def kernel(x, p, g_mix, w_in, conv_w, g_conv_out, g_attn_out, w_out, g_mlp, w_up, w_down, g_ple, w_ple_gate, w_ple_proj, g_final, loss_target, m_g_mix, m_w_in, m_conv_w, m_g_conv_out, m_g_attn_out, m_w_out, m_g_mlp, m_w_up, m_w_down, m_g_ple, m_w_ple_gate, m_w_ple_proj, m_g_final, v_g_mix, v_w_in, v_conv_w, v_g_conv_out, v_g_attn_out, v_w_out, v_g_mlp, v_w_up, v_w_down, v_g_ple, v_w_ple_gate, v_w_ple_proj, v_g_final):
    s = x.shape[1]
    tm = min(TOKEN_TILE, s)
    xs = x.reshape(s, D_MODEL)
    ps = p.reshape(s, PLE_DIM)
    target = loss_target.reshape(s, D_MODEL)
    core = lax.axis_index("c").astype(jnp.int32).reshape(1)
    chip = 2 * lax.axis_index("x") + lax.axis_index("y")

    big = {"w_in": w_in[0], "w_out": w_out[0], "w_up": w_up[0], "w_down": w_down[0],
           "w_ple_gate": w_ple_gate[0], "w_ple_proj": w_ple_proj[0]}
    names = list(big)
    conv_shard = jnp.pad(conv_w[0], ((0, CONV_W_ROWS - conv_w.shape[1]), (0, 0)))
    later_names = names[1:]
    w_in_f, conv_f = _run_exchange(_gather_exchange([big["w_in"].astype(bf16), conv_shard]), "gather_w_in")

    proj, a_b = _in_proj(xs, g_mix, w_in_f, tm)
    conv_n = _conv_fwd(proj, conv_f, g_conv_out)
    nblk = s // ATTN_TILE
    koff = 3 * W_CONV + W_ATTN
    kp = _permute_keys(proj[:, koff:koff + W_ATTN].astype(bf16)).reshape(nblk, ATTN_TILE, W_ATTN)
    vp = _permute_keys(proj[:, koff + W_ATTN:].astype(bf16)).reshape(nblk, ATTN_TILE, W_ATTN)
    kt, vt = kp.transpose(0, 2, 1), vp.transpose(0, 2, 1)
    (o, tot), gathered = _attn_fwd(proj, kp, vt, _gather_exchange([big[k].astype(bf16) for k in later_names]))
    w_out_f, w_up_f, w_down_f, w_gate_f, w_proj_f = gathered
    w_out_f = w_out_f.reshape(D_MODEL, D_MODEL)
    w_gate_f = w_gate_f.reshape(D_MODEL, D_MODEL)
    h1, cat_b = _out_proj(o, conv_n, xs, g_attn_out, w_out_f, tm)
    h2, u_b, m_b = _mlp_fwd(h1, g_mlp, w_up_f, w_down_f, tm)

    dh2, dh2_b, n3_b, dgl_b, dpp_b, p_b, gg_ple, gg_final, loss_row = _tail(
        h2, ps, target, g_ple, g_final.reshape(1, D_MODEL), w_gate_f, w_proj_f, tm)
    dh1, dh1_b, du_b, gg_mlp = _mlp_bwd(dh2, dh2_b, h1, u_b, g_mlp, w_up_f, w_down_f, tm)
    part = {
        "w_out": _weight_grad(cat_b, dh1_b, D_MODEL, D_MODEL, tm, "grad_w_out").reshape(N_CHIPS, D_MODEL // N_CHIPS, D_MODEL),
        "w_up": _weight_grad(m_b, du_b, D_MODEL, D_FF // N_CHIPS, tm, "grad_w_up"),
        "w_down": _weight_grad(u_b, dh2_b, D_FF // N_CHIPS, D_MODEL, tm, "grad_w_down", relu_sq=True),
        "w_ple_gate": _weight_grad(n3_b, dgl_b, D_MODEL, D_MODEL, tm, "grad_w_ple_gate").reshape(N_CHIPS, D_MODEL // N_CHIPS, D_MODEL),
        "w_ple_proj": _weight_grad(p_b, dpp_b, PLE_DIM, D_MODEL // N_CHIPS, tm, "grad_w_ple_proj"),
    }
    from_sibling = _pair_exchange([part[k] for k in later_names], "grad_pair_exchange")
    pair = [_pair_sum(part[k], r, core, "pair_sum_" + k) for k, r in zip(later_names, from_sibling)]
    dcn, do, gg_attn = _out_proj_bwd(dh1_b, o, g_attn_out, w_out_f, tm)
    dcb, dcc, dcu, g_conv_w, gg_conv = _conv_bwd(proj, dcn, conv_f, g_conv_out)
    (dq, dk, dv), from_chips = _attn_bwd(proj, kp, vp, kt, do, tot, _chip_exchange(pair))
    dk = _unpermute_keys(dk.reshape(s, W_ATTN))
    dv = _unpermute_keys(dv.reshape(s, W_ATTN))
    dproj = jnp.concatenate([dcb, dcc, dcu, dq, dk, dv], axis=1)

    part["w_in"] = _weight_grad(a_b, dproj, D_MODEL, IN_COLS // N_CHIPS, tm, "grad_w_in")
    in_sibling = _pair_exchange([part["w_in"]], "grad_pair_exchange_w_in")
    in_pair = _pair_sum(part["w_in"], in_sibling[0], core, "pair_sum_w_in")
    (grad_x, gg_mix), in_chips = _in_proj_bwd(dproj, dh1, xs, g_mix, w_in_f, tm, _chip_exchange([in_pair]))

    place = jnp.stack([lax.axis_index("c"), chip]).astype(jnp.int32)
    half = [_chip_sum(mine, landed, place, "chip_sum_" + k)
            for k, mine, landed in zip(names, [in_pair] + pair, list(in_chips) + list(from_chips))]
    both = _sibling_exchange(half)
    grad = {k: b.reshape(big[k].shape) for k, b in zip(names, both)}

    gcw = g_conv_w[:, :3, :].transpose(1, 0, 2).reshape(3, W_CONV)
    row = lambda *parts: jnp.concatenate(parts, axis=1)
    packed = jnp.concatenate([
        gg_mix, gg_mlp, gg_ple, gg_final, row(gg_conv, gg_attn), row(gcw[0:1], gcw[1:2]),
        row(gcw[2:3], loss_row, jnp.zeros((1, W_CONV - LANES), f32)), jnp.zeros((1, D_MODEL), f32)], axis=0)
    summed = _allreduce_small(packed)
    loss = summed[6, W_CONV]
    gcw_full = jnp.stack([summed[5, :W_CONV], summed[5, W_CONV:], summed[6, :W_CONV]])
    grad["conv_w"] = lax.dynamic_slice(gcw_full, (0, chip * LANES), (3, LANES))
    vec_names = ["g_mix", "g_mlp", "g_ple", "g_final"]
    vec_w = {"g_mix": g_mix, "g_mlp": g_mlp, "g_ple": g_ple, "g_final": g_final.reshape(1, D_MODEL)}
    vec_m = {"g_mix": m_g_mix, "g_mlp": m_g_mlp, "g_ple": m_g_ple, "g_final": m_g_final.reshape(1, D_MODEL)}
    vec_v = {"g_mix": v_g_mix, "g_mlp": v_g_mlp, "g_ple": v_g_ple, "g_final": v_g_final.reshape(1, D_MODEL)}

    def pack_vec(d, conv, attn):
        return jnp.concatenate([d[k] for k in vec_names] + [row(conv, attn)], axis=0)

    vec_g = summed[0:5]
    vec_d, vec_nm, vec_nv = _adamw(vec_g, pack_vec(vec_w, g_conv_out, g_attn_out),
                                   pack_vec(vec_m, m_g_conv_out, m_g_attn_out),
                                   pack_vec(vec_v, v_g_conv_out, v_g_attn_out), "adamw_vectors")

    given_w = dict(big, conv_w=conv_w[0])
    given_m = {"w_in": m_w_in[0], "w_out": m_w_out[0], "w_up": m_w_up[0], "w_down": m_w_down[0],
               "w_ple_gate": m_w_ple_gate[0], "w_ple_proj": m_w_ple_proj[0], "conv_w": m_conv_w[0]}
    given_v = {"w_in": v_w_in[0], "w_out": v_w_out[0], "w_up": v_w_up[0], "w_down": v_w_down[0],
               "w_ple_gate": v_w_ple_gate[0], "w_ple_proj": v_w_ple_proj[0], "conv_w": v_conv_w[0]}
    delta, new_m, new_v = {}, {}, {}
    for k in names + ["conv_w"]:
        delta[k], new_m[k], new_v[k] = _adamw(grad[k], given_w[k], given_m[k], given_v[k], "adamw_" + k)

    def unpack(vals, kind):
        out = {k: vals[i:i + 1] for i, k in enumerate(vec_names)}
        out["g_final"] = out["g_final"].reshape(D_MODEL)
        out["g_conv_out"] = vals[4:5, :W_CONV]
        out["g_attn_out"] = vals[4:5, W_CONV:]
        out.update({k: v[None] for k, v in kind.items()})
        return out

    order = ["g_mix", "w_in", "conv_w", "g_conv_out", "g_attn_out", "w_out", "g_mlp", "w_up", "w_down",
             "g_ple", "w_ple_gate", "w_ple_proj", "g_final"]
    groups = [unpack(vec_g, grad), unpack(vec_d, delta), unpack(vec_nm, new_m), unpack(vec_nv, new_v)]
    return (loss, grad_x[None]) + tuple(g[k] for g in groups for k in order)
```

```python
import functools

import jax
import jax.numpy as jnp
from jax import lax
from jax.experimental import pallas as pl
from jax.experimental.pallas import tpu as pltpu

f32 = jnp.float32
bf16 = jnp.bfloat16

D_MODEL = 1024
HEAD_DIM = 64
W_CONV = 512
W_ATTN = 512
D_FF = 4096
PLE_DIM = 256
IN_COLS = 3 * W_CONV + 3 * W_ATTN
N_CHIPS = 4
EPS = 1e-6
ADAM_LR = 0.001
ADAM_B1 = 0.9
ADAM_B2 = 0.999
ADAM_EPS = 1e-08
ADAM_WD = 0.01
ADAM_STEP = 10

LANES = 128
TOKEN_TILE = 512
MLP_TOKEN_TILE = 1024
GRAD_TOKEN_TILE = 2048
ATTN_TILE = 256
CONV_CHUNK = 512
CONV_W_ROWS = 16

MESH = pl.DeviceIdType.MESH
ANY = pl.BlockSpec(memory_space=pl.ANY)
NT = (((1,), (1,)), ((), ()))
TN = (((0,), (0,)), ((), ()))


def _arb(n):
    return pltpu.CompilerParams(dimension_semantics=("arbitrary",) * n)


def _sds(shape, dtype):
    return jax.ShapeDtypeStruct(shape, dtype)


def _dot(a, b, dims=None):
    if dims is None:
        return jnp.dot(a, b, preferred_element_type=f32)
    return lax.dot_general(a, b, dims, preferred_element_type=f32)


def _split_dot(x, ones):
    hi = x.astype(bf16)
    lo = (x - hi.astype(f32)).astype(bf16)
    return _dot(hi, ones) + _dot(lo, ones)


def _rms_fwd(h, g):
    rstd = lax.rsqrt(jnp.mean(h * h, axis=-1, keepdims=True) + EPS)
    return h * rstd * g, rstd


def _rms_bwd(dy, h, g):
    rstd = lax.rsqrt(jnp.mean(h * h, axis=-1, keepdims=True) + EPS)
    hn = h * rstd
    dyg = dy * g
    dh = rstd * (dyg - hn * jnp.mean(dyg * hn, axis=-1, keepdims=True))
    return dh, jnp.sum(dy * hn, axis=0, keepdims=True)


def _group_ones(n):
    r = lax.broadcasted_iota(jnp.int32, (n, n), 0) // HEAD_DIM
    c = lax.broadcasted_iota(jnp.int32, (n, n), 1) // HEAD_DIM
    return (r == c).astype(bf16)


def _head_rms_fwd(y, g, ones):
    rstd = lax.rsqrt(_split_dot(y * y, ones) * (1.0 / HEAD_DIM) + EPS)
    return y * rstd * g


def _head_rms_bwd(dy, y, g, ones):
    rstd = lax.rsqrt(_split_dot(y * y, ones) * (1.0 / HEAD_DIM) + EPS)
    yn = y * rstd
    dyg = dy * g
    dyy = rstd * (dyg - yn * (_split_dot(dyg * yn, ones) * (1.0 / HEAD_DIM)))
    return dyy, jnp.sum(dy * yn, axis=0, keepdims=True)


def _place():
    return lax.axis_index("x"), lax.axis_index("y"), lax.axis_index("c")


def _other_chips(x, y):
    return [(1 - x, y), (x, 1 - y), (1 - x, 1 - y)]


class _Exchange:
    def __init__(self, arrays, out_shapes, sems, start, finish):
        self.arrays, self.out_shapes, self.sems, self.start, self.finish = arrays, out_shapes, sems, start, finish


def _gather_exchange(shards):
    n = len(shards)
    halves = [s.shape[0] // 2 for s in shards]

    def plan(ins, outs, sems):
        send_sems, recv_sems, own_sems = sems
        x, y, c = _place()
        me = 2 * x + y
        chips = _other_chips(x, y)

        def half(ref, i, which):
            return ref.at[pl.ds(which * halves[i], halves[i]), :]

        def over_ici(i, j, src, slot, to):
            return pltpu.make_async_remote_copy(
                src_ref=src, dst_ref=half(outs[i].at[slot], i, c),
                send_sem=send_sems.at[3 * i + j], recv_sem=recv_sems.at[3 * i + j],
                device_id=to, device_id_type=MESH)

        def to_sibling(i, j, slot, which):
            blk = half(outs[i].at[slot], i, which)
            return pltpu.make_async_remote_copy(
                src_ref=blk, dst_ref=blk,
                send_sem=send_sems.at[3 * n + 3 * i + j], recv_sem=recv_sems.at[3 * n + 3 * i + j],
                device_id=(x, y, 1 - c), device_id_type=MESH)

        own = [pltpu.make_async_remote_copy(
            src_ref=ins[i], dst_ref=outs[i].at[me], send_sem=own_sems.at[i], recv_sem=own_sems.at[n + i],
            device_id=(x, y, 1 - c), device_id_type=MESH) for i in range(n)]
        pairs = [(i, j, px, py) for i in range(n) for j, (px, py) in enumerate(chips)]
        sends = [over_ici(i, j, half(ins[i], i, c), me, (px, py, c)) for i, j, px, py in pairs]
        lands = [over_ici(i, j, half(outs[i].at[2 * px + py], i, c), 2 * px + py, (px, py, c)) for i, j, px, py in pairs]
        passes = [to_sibling(i, j, 2 * px + py, c) for i, j, px, py in pairs]
        from_sibling = [to_sibling(i, j, 2 * px + py, 1 - c) for i, j, px, py in pairs]
        return own, sends, lands, passes, from_sibling

    def start(ins, outs, sems):
        own, sends, _, _, _ = plan(ins, outs, sems)
        for cp in own + sends:
            cp.start()

    def finish(ins, outs, sems):
        own, sends, lands, passes, from_sibling = plan(ins, outs, sems)
        for land, on in zip(lands, passes):
            land.wait_recv()
            on.start()
        for cp in from_sibling:
            cp.wait_recv()
        for cp in sends + passes:
            cp.wait_send()
        for cp in own:
            cp.wait()

    return _Exchange(
        shards, [_sds((N_CHIPS,) + s.shape, s.dtype) for s in shards],
        [pltpu.SemaphoreType.DMA((6 * n,)), pltpu.SemaphoreType.DMA((6 * n,)), pltpu.SemaphoreType.DMA((2 * n,))],
        start, finish)


def _call_carrying(ex, body, name, grid, in_specs, out_specs, out_shape, scratch_shapes, args):
    n_in, n_out, n_scr = len(in_specs), len(out_specs), len(scratch_shapes)
    k = 0 if ex is None else len(ex.arrays)

    def wrapped(*refs):
        ins, xin = refs[:n_in], refs[n_in:n_in + k]
        outs, xout = refs[n_in + k:n_in + k + n_out], refs[n_in + k + n_out:n_in + 2 * k + n_out]
        scr, sems = refs[n_in + 2 * k + n_out:n_in + 2 * k + n_out + n_scr], refs[n_in + 2 * k + n_out + n_scr:]
        ids = [pl.program_id(d) for d in range(len(grid))]
        if ex is not None:
            @pl.when(functools.reduce(lambda a, b: a & b, [i == 0 for i in ids]))
            def _():
                ex.start(xin, xout, sems)

        body(*ins, *outs, *scr)
        if ex is not None:
            @pl.when(functools.reduce(lambda a, b: a & b, [i == g - 1 for i, g in zip(ids, grid)]))
            def _():
                ex.finish(xin, xout, sems)

    res = pl.pallas_call(
        wrapped, name=name, grid=grid,
        in_specs=list(in_specs) + [ANY] * k, out_specs=list(out_specs) + [ANY] * k,
        out_shape=list(out_shape) + ([] if ex is None else list(ex.out_shapes)),
        scratch_shapes=list(scratch_shapes) + ([] if ex is None else list(ex.sems)),
        compiler_params=_arb(len(grid)),
    )(*args, *([] if ex is None else ex.arrays))
    return res[:n_out], res[n_out:]


def _run_exchange(ex, name):
    n = len(ex.arrays)

    def body(*refs):
        ins, outs, sems = refs[:n], refs[n:2 * n], refs[2 * n:]
        ex.start(ins, outs, sems)
        ex.finish(ins, outs, sems)

    return pl.pallas_call(
        body, name=name, out_shape=ex.out_shapes, in_specs=[ANY] * n, out_specs=[ANY] * n,
        scratch_shapes=ex.sems,
    )(*ex.arrays)


def _pair_exchange(grads):
    n = len(grads)

    def plan(ins, outs, sems):
        send_sems, recv_sems = sems
        x, y, c = _place()
        return [pltpu.make_async_remote_copy(
            src_ref=ins[i].at[:, 1 - c], dst_ref=outs[i],
            send_sem=send_sems.at[i], recv_sem=recv_sems.at[i],
            device_id=(x, y, 1 - c), device_id_type=MESH) for i in range(n)]

    def start(ins, outs, sems):
        for cp in plan(ins, outs, sems):
            cp.start()

    def finish(ins, outs, sems):
        for cp in plan(ins, outs, sems):
            cp.wait()

    views = [g.reshape(N_CHIPS, 2, g.shape[1] // 2, g.shape[2]) for g in grads]
    return _Exchange(
        views, [_sds((N_CHIPS, v.shape[2], v.shape[3]), f32) for v in views],
        [pltpu.SemaphoreType.DMA((n,)), pltpu.SemaphoreType.DMA((n,))], start, finish)


def _chip_exchange(parts):
    n = len(parts)

    def plan(ins, outs, sems):
        send_sems, recv_sems = sems
        x, y, c = _place()
        me = 2 * x + y
        pairs = [(i, j, px, py) for i in range(n) for j, (px, py) in enumerate(_other_chips(x, y))]

        def copy(i, j, src, slot, px, py):
            return pltpu.make_async_remote_copy(
                src_ref=src, dst_ref=outs[i].at[slot],
                send_sem=send_sems.at[3 * i + j], recv_sem=recv_sems.at[3 * i + j],
                device_id=(px, py, c), device_id_type=MESH)

        sends = [copy(i, j, ins[i].at[2 * px + py], me, px, py) for i, j, px, py in pairs]
        lands = [copy(i, j, outs[i].at[2 * px + py], 2 * px + py, px, py) for i, j, px, py in pairs]
        return sends, lands

    def start(ins, outs, sems):
        sends, _ = plan(ins, outs, sems)
        for cp in sends:
            cp.start()

    def finish(ins, outs, sems):
        sends, lands = plan(ins, outs, sems)
        for cp in lands:
            cp.wait_recv()
        for cp in sends:
            cp.wait_send()

    return _Exchange(
        parts, [_sds(p.shape, p.dtype) for p in parts],
        [pltpu.SemaphoreType.DMA((3 * n,)), pltpu.SemaphoreType.DMA((3 * n,))],
        start, finish)


def _sibling_exchange(both):
    n = len(both)

    def body(*refs):
        outs = refs[n:2 * n]
        send_sems, recv_sems = refs[2 * n:]
        x, y, c = _place()
        sent = []
        for i in range(n):
            cp = pltpu.make_async_remote_copy(
                src_ref=outs[i].at[c], dst_ref=outs[i].at[c],
                send_sem=send_sems.at[i], recv_sem=recv_sems.at[i],
                device_id=(x, y, 1 - c), device_id_type=MESH)
            cp.start()
            sent.append(cp)
        for cp in sent:
            cp.wait()

    return pl.pallas_call(
        body, name="grad_sibling_exchange",
        out_shape=[_sds(b.shape, f32) for b in both],
        in_specs=[ANY] * n, out_specs=[ANY] * n,
        input_output_aliases={i: i for i in range(n)},
        scratch_shapes=[pltpu.SemaphoreType.DMA((n,)), pltpu.SemaphoreType.DMA((n,))],
    )(*both)


def _row_tile(rows, cols):
    t = rows
    while t * cols * 4 > (1 << 20) and t % 16 == 0:
        t //= 2
    return t


def _pair_sum(grad, recv, core, name):
    _, r, c = grad.shape
    hr = r // 2
    tr = _row_tile(hr, c)
    view = grad.reshape(N_CHIPS, 2, hr, c)

    def body(core_ref, mine_ref, recv_ref, out_ref):
        out_ref[...] = (mine_ref[...] + recv_ref[...]).astype(bf16)

    return pl.pallas_call(
        body, name=name,
        grid_spec=pltpu.PrefetchScalarGridSpec(
            num_scalar_prefetch=1, grid=(N_CHIPS, hr // tr),
            in_specs=[pl.BlockSpec((None, None, tr, c), lambda k, t, core_ref: (k, core_ref[0], t, 0)),
                      pl.BlockSpec((None, tr, c), lambda k, t, core_ref: (k, t, 0))],
            out_specs=pl.BlockSpec((None, tr, c), lambda k, t, core_ref: (k, t, 0))),
        out_shape=_sds((N_CHIPS, hr, c), bf16),
        compiler_params=_arb(2),
    )(core, view, recv)


def _chip_sum(mine, landed, place, name):
    _, hr, c = mine.shape
    tr = _row_tile(hr, c)

    def body(place_ref, a_ref, b_ref, c_ref, d_ref, out_ref):
        out_ref[...] = ((a_ref[...].astype(f32) + b_ref[...].astype(f32)) + c_ref[...].astype(f32)) + d_ref[...].astype(f32)

    def slot(k):
        return pl.BlockSpec((None, tr, c), lambda t, place_ref: ((place_ref[1] + k) % N_CHIPS, t, 0))

    return pl.pallas_call(
        body, name=name,
        grid_spec=pltpu.PrefetchScalarGridSpec(
            num_scalar_prefetch=1, grid=(hr // tr,),
            in_specs=[slot(0), slot(1), slot(2), slot(3)],
            out_specs=pl.BlockSpec((None, tr, c), lambda t, place_ref: (place_ref[0], t, 0))),
        out_shape=_sds((2, hr, c), f32), compiler_params=_arb(1),
    )(place, mine, landed, landed, landed)


def _adamw(g, w, m, v, name):
    r, c = g.shape
    tr = _row_tile(r, c)

    def body(g_ref, w_ref, m_ref, v_ref, d_ref, nm_ref, nv_ref):
        gv = g_ref[...]
        mv = ADAM_B1 * m_ref[...] + (1.0 - ADAM_B1) * gv
        vv = ADAM_B2 * v_ref[...] + (1.0 - ADAM_B2) * jnp.square(gv)
        m_hat = mv / (1.0 - ADAM_B1 ** ADAM_STEP)
        v_hat = vv / (1.0 - ADAM_B2 ** ADAM_STEP)
        d_ref[...] = -ADAM_LR * (m_hat / (jnp.sqrt(v_hat) + ADAM_EPS) + ADAM_WD * w_ref[...])
        nm_ref[...] = mv
        nv_ref[...] = vv

    spec = pl.BlockSpec((tr, c), lambda t: (t, 0))
    return pl.pallas_call(
        body, name=name, grid=(r // tr,), in_specs=[spec] * 4, out_specs=[spec] * 3,
        out_shape=[_sds((r, c), f32)] * 3, compiler_params=_arb(1),
    )(g, w, m, v)


def _allreduce_small(packed):
    shape = packed.shape

    def body(x_ref, out_ref, buf, send_sems, recv_sems):
        x, y, c = _place()
        me = 4 * x + 2 * y + c
        buf[me] = x_ref[...]
        sent = []
        for r in range(1, 8):
            dx, dy, dc = (r >> 2) & 1, (r >> 1) & 1, r & 1
            peer = ((1 - x) if dx else x, (1 - y) if dy else y, (1 - c) if dc else c)
            cp = pltpu.make_async_remote_copy(
                src_ref=x_ref, dst_ref=buf.at[me],
                send_sem=send_sems.at[r], recv_sem=recv_sems.at[r],
                device_id=peer, device_id_type=MESH)
            cp.start()
            sent.append((cp, peer))
        for r, (cp, peer) in enumerate(sent, start=1):
            src = 4 * peer[0] + 2 * peer[1] + peer[2]
            pltpu.make_async_remote_copy(
                src_ref=x_ref, dst_ref=buf.at[src],
                send_sem=send_sems.at[r], recv_sem=recv_sems.at[r],
                device_id=peer, device_id_type=MESH).wait_recv()
        for cp, _ in sent:
            cp.wait_send()
        total = buf[0]
        for k in range(1, 8):
            total = total + buf[k]
        out_ref[...] = total

    vmem = pl.BlockSpec(memory_space=pltpu.VMEM)
    return pl.pallas_call(
        body, name="allreduce_small", out_shape=_sds(shape, f32),
        in_specs=[vmem], out_specs=vmem,
        scratch_shapes=[pltpu.VMEM((8,) + shape, f32), pltpu.SemaphoreType.DMA((8,)),
                        pltpu.SemaphoreType.DMA((8,))],
    )(packed)


def _in_proj(x, g_mix, w_in, tm):
    s = x.shape[0]
    ncol = IN_COLS // N_CHIPS

    def body(x_ref, g_ref, w_ref, proj_ref, a_ref):
        a, _ = _rms_fwd(x_ref[...], g_ref[...])
        ab = a.astype(bf16)
        a_ref[...] = ab
        for k in range(N_CHIPS):
            proj_ref[:, k * ncol:(k + 1) * ncol] = _dot(ab, w_ref[k])

    return pl.pallas_call(
        body, name="in_proj", grid=(s // tm,),
        in_specs=[pl.BlockSpec((tm, D_MODEL), lambda i: (i, 0)),
                  pl.BlockSpec((1, D_MODEL), lambda i: (0, 0)),
                  pl.BlockSpec((N_CHIPS, D_MODEL, ncol), lambda i: (0, 0, 0))],
        out_specs=[pl.BlockSpec((tm, IN_COLS), lambda i: (i, 0)),
                   pl.BlockSpec((tm, D_MODEL), lambda i: (i, 0))],
        out_shape=[_sds((s, IN_COLS), f32), _sds((s, D_MODEL), bf16)],
        compiler_params=_arb(1),
    )(x, g_mix, w_in)


def _shifted(prev8, cur, shift):
    ext = jnp.concatenate([prev8, cur], axis=0)
    return pltpu.roll(ext, shift, axis=0)[8:]


def _conv_fwd(proj, conv_w, g_conv):
    s = proj.shape[0]
    nblk = W_CONV // LANES
    rc = min(CONV_CHUNK, s)

    def body(cb_ref, cc_ref, cu_ref, w_ref, g_ref, out_ref):
        ones = _group_ones(LANES)
        w0, w1, w2 = w_ref[0:1, :], w_ref[1:2, :], w_ref[2:3, :]
        g = g_ref[...]

        def chunk(i, carry):
            r0 = pl.multiple_of(i * rc, rc)
            rows = pl.ds(r0, rc)
            prev = pl.ds(pl.multiple_of(jnp.maximum(r0 - 8, 0), 8), 8)
            v = cc_ref[rows, :] * cu_ref[rows, :]
            vp = jnp.where(i > 0, cc_ref[prev, :] * cu_ref[prev, :], 0.0)
            y = w2 * v + w1 * _shifted(vp, v, 1) + w0 * _shifted(vp, v, 2)
            out_ref[rows, :] = _head_rms_fwd(cb_ref[rows, :] * y, g, ones).astype(bf16)
            return carry

        lax.fori_loop(0, s // rc, chunk, 0)

    def col(off):
        return pl.BlockSpec((s, LANES), lambda j: (0, off + j))

    return pl.pallas_call(
        body, name="conv_fwd", grid=(nblk,),
        in_specs=[col(0), col(nblk), col(2 * nblk),
                  pl.BlockSpec((None, CONV_W_ROWS, LANES), lambda j: (j, 0, 0)),
                  pl.BlockSpec((1, LANES), lambda j: (0, j))],
        out_specs=pl.BlockSpec((s, LANES), lambda j: (0, j)),
        out_shape=_sds((s, W_CONV), bf16), compiler_params=_arb(1),
    )(proj, proj, proj, conv_w, g_conv)


LOG2_E = 1.4426950408889634


def _log2_keep(z2):
    nz2 = -z2
    return jnp.minimum(nz2, 0.0) - jnp.log2(1.0 + jnp.exp2(jnp.minimum(z2, nz2)))


def _head_pair_masks(rows):
    lane = lax.broadcasted_iota(jnp.int32, (rows, LANES), 1)
    return lane < HEAD_DIM


SUBLANES = 8
KEY_RUN = ATTN_TILE // SUBLANES


def _permute_keys(a):
    s, w = a.shape
    return a.reshape(s // ATTN_TILE, SUBLANES, KEY_RUN, w).transpose(0, 2, 1, 3).reshape(s, w)


def _unpermute_keys(a):
    s, w = a.shape
    return a.reshape(s // ATTN_TILE, KEY_RUN, SUBLANES, w).transpose(0, 2, 1, 3).reshape(s, w)


def _causal_tiles():
    r = lax.broadcasted_iota(jnp.int32, (ATTN_TILE, ATTN_TILE), 0)
    key = (r % SUBLANES) * KEY_RUN + r // SUBLANES
    below = key < lax.broadcasted_iota(jnp.int32, (ATTN_TILE, ATTN_TILE), 1)
    return below.astype(f32), jnp.where(below, 0.0, -1e30).astype(f32)


def _sublane_scan(x, reverse):
    row = lax.broadcasted_iota(jnp.int32, x.shape, 0)
    inc = x
    for sh in (1, 2, 4):
        if reverse:
            inc = inc + jnp.where(row < SUBLANES - sh, pltpu.roll(inc, SUBLANES - sh, axis=0), 0.0)
        else:
            inc = inc + jnp.where(row >= sh, pltpu.roll(inc, sh, axis=0), 0.0)
    return inc - x


def _attn_fwd(proj, kp, vt, ex=None):
    s = proj.shape[0]
    t = ATTN_TILE
    nblk = s // t
    npair = W_ATTN // LANES
    qoff = 3 * W_CONV // LANES
    keep01, keepneg = _causal_tiles()

    def body(q_ref, k_ref, vt_ref, m01_ref, neg_ref, o_ref, tot_ref, w_s, a_s, acc):
        qb = pl.program_id(1)
        first = _head_pair_masks(t)
        q = q_ref[...] * (HEAD_DIM ** -0.5)
        qh = (jnp.where(first, q, 0.0).astype(bf16), jnp.where(first, 0.0, q).astype(bf16))
        acc[...] = jnp.zeros_like(acc)
        a_s[1] = jnp.zeros((t, t), bf16)

        def scores(kb, h):
            w_s[h] = _dot(k_ref[kb], qh[h], NT)

        def weigh(kb, h):
            acc[h] += _dot(vt_ref[kb], a_s[h])

        def weights(h, diagonal, later):
            run = jnp.zeros((SUBLANES, t), f32)
            for a in reversed(range(KEY_RUN)):
                rows = slice(SUBLANES * a, SUBLANES * (a + 1))
                z2 = w_s[h, rows, :] * LOG2_E
                lk = _log2_keep(z2)
                if diagonal:
                    lk = lk * m01_ref[rows, :]
                run = run + lk
                w_s[h, rows, :] = z2 + run
            off = _sublane_scan(run, reverse=True) + later
            off2 = jnp.concatenate([off, off], axis=0)
            for a in range(t // (2 * SUBLANES)):
                rows = slice(2 * SUBLANES * a, 2 * SUBLANES * (a + 1))
                w = w_s[h, rows, :] + off2
                if diagonal:
                    w = w + neg_ref[rows, :]
                a_s[h, rows, :] = jnp.exp2(w).astype(bf16)
            return later + jnp.sum(run, axis=0, keepdims=True)

        def block(kb, before, after, diagonal, later):
            scores(kb, 1)
            weigh(before, 1)
            l0 = weights(0, diagonal, later[0])
            scores(after, 0)
            weigh(kb, 0)
            l1 = weights(1, diagonal, later[1])
            return l0, l1

        zero = jnp.zeros((1, t), f32)
        scores(qb, 0)
        later = block(qb, qb, jnp.maximum(qb - 1, 0), True, (zero, zero))

        def earlier(i, c):
            kb = qb - 1 - i
            return block(kb, kb + 1, jnp.maximum(kb - 1, 0), False, c)

        later = lax.fori_loop(0, qb, earlier, later)
        weigh(0, 1)
        top = lax.broadcasted_iota(jnp.int32, (LANES, t), 0) < HEAD_DIM
        o_ref[...] = jnp.where(top, acc[0], acc[1]).T
        tot_ref[...] = jnp.concatenate([later[0], later[1], jnp.zeros((SUBLANES - 2, t), f32)], axis=0)

    return _call_carrying(
        ex, body, "attn_fwd", (npair, nblk),
        in_specs=[pl.BlockSpec((t, LANES), lambda p, i: (i, qoff + p)),
                  pl.BlockSpec((nblk, t, LANES), lambda p, i: (0, 0, p)),
                  pl.BlockSpec((nblk, LANES, t), lambda p, i: (0, p, 0)),
                  pl.BlockSpec((t, t), lambda p, i: (0, 0)),
                  pl.BlockSpec((t, t), lambda p, i: (0, 0))],
        out_specs=[pl.BlockSpec((t, LANES), lambda p, i: (i, p)),
                   pl.BlockSpec((None, SUBLANES, t), lambda p, i: (p, 0, i))],
        out_shape=[_sds((s, W_ATTN), f32), _sds((npair, SUBLANES, s), f32)],
        scratch_shapes=[pltpu.VMEM((2, t, t), f32), pltpu.VMEM((2, t, t), bf16), pltpu.VMEM((2, LANES, t), f32)],
        args=(proj, kp, vt, keep01, keepneg))


def _out_proj(o, conv_n, x, g_attn, w_out, tm):
    s = x.shape[0]

    def body(o_ref, c_ref, x_ref, g_ref, w_ref, h_ref, cat_ref):
        ones = _group_ones(LANES)
        cat_ref[:, :W_CONV] = c_ref[...]
        for j in range(W_ATTN // LANES):
            cols = slice(j * LANES, (j + 1) * LANES)
            cat_ref[:, W_CONV + j * LANES:W_CONV + (j + 1) * LANES] = _head_rms_fwd(
                o_ref[:, cols], g_ref[:, cols], ones).astype(bf16)
        h_ref[...] = x_ref[...] + _dot(cat_ref[...], w_ref[...])

    return pl.pallas_call(
        body, name="out_proj", grid=(s // tm,),
        in_specs=[pl.BlockSpec((tm, W_ATTN), lambda i: (i, 0)),
                  pl.BlockSpec((tm, W_CONV), lambda i: (i, 0)),
                  pl.BlockSpec((tm, D_MODEL), lambda i: (i, 0)),
                  pl.BlockSpec((1, W_ATTN), lambda i: (0, 0)),
                  pl.BlockSpec((D_MODEL, D_MODEL), lambda i: (0, 0))],
        out_specs=[pl.BlockSpec((tm, D_MODEL), lambda i: (i, 0)),
                   pl.BlockSpec((tm, D_MODEL), lambda i: (i, 0))],
        out_shape=[_sds((s, D_MODEL), f32), _sds((s, D_MODEL), bf16)],
        compiler_params=_arb(1),
    )(o, conv_n, x, g_attn, w_out)


def _mlp_fwd(h1, g_mlp, w_up, w_down, tm):
    s = h1.shape[0]
    fc = D_FF // N_CHIPS

    def body(h_ref, g_ref, wu_ref, wd_ref, h2_ref, u_ref, m_ref):
        j = pl.program_id(1)

        @pl.when(j == 0)
        def _():
            m, _ = _rms_fwd(h_ref[...], g_ref[...])
            m_ref[...] = m.astype(bf16)
            h2_ref[...] = h_ref[...]

        u = _dot(m_ref[...], wu_ref[...])
        u_ref[...] = u.astype(bf16)
        h2_ref[...] += _dot(jnp.square(jnp.maximum(u, 0.0)).astype(bf16), wd_ref[...])

    return pl.pallas_call(
        body, name="mlp_fwd", grid=(s // tm, N_CHIPS),
        in_specs=[pl.BlockSpec((tm, D_MODEL), lambda i, j: (i, 0)),
                  pl.BlockSpec((1, D_MODEL), lambda i, j: (0, 0)),
                  pl.BlockSpec((None, D_MODEL, fc), lambda i, j: (j, 0, 0)),
                  pl.BlockSpec((None, fc, D_MODEL), lambda i, j: (j, 0, 0))],
        out_specs=[pl.BlockSpec((tm, D_MODEL), lambda i, j: (i, 0)),
                   pl.BlockSpec((tm, fc), lambda i, j: (i, j)),
                   pl.BlockSpec((tm, D_MODEL), lambda i, j: (i, 0))],
        out_shape=[_sds((s, D_MODEL), f32), _sds((s, D_FF), bf16), _sds((s, D_MODEL), bf16)],
        compiler_params=_arb(2),
    )(h1, g_mlp, w_up, w_down)


def _tail(h2, p, target, g_ple, g_final, w_gate, w_proj, tm):
    s = h2.shape[0]
    pc = D_MODEL // N_CHIPS

    def body(h_ref, p_ref, t_ref, gp_ref, gf_ref, wg_ref, wp_ref,
             dh_ref, dhb_ref, n3_ref, dgl_ref, dpp_ref, pb_ref, ggp_ref, ggf_ref, loss_ref, pp_ref):
        i = pl.program_id(0)
        h2v = h_ref[...]
        n3, _ = _rms_fwd(h2v, gp_ref[...])
        n3b = n3.astype(bf16)
        n3_ref[...] = n3b
        gate = jax.nn.sigmoid(_dot(n3b, wg_ref[...]))
        pb = p_ref[...].astype(bf16)
        pb_ref[...] = pb
        for k in range(N_CHIPS):
            pp_ref[:, k * pc:(k + 1) * pc] = _dot(pb, wp_ref[k])
        pp = pp_ref[...]
        h3 = h2v + gate * pp
        yv, _ = _rms_fwd(h3, gf_ref[...])
        err = yv - t_ref[...]
        loss = 0.5 * jnp.sum(err * err) * (1.0 / D_MODEL)
        dh3, ggf = _rms_bwd(err * (1.0 / D_MODEL), h3, gf_ref[...])
        dpp_ref[...] = (dh3 * gate).astype(bf16)
        dgl = (dh3 * pp * gate * (1.0 - gate)).astype(bf16)
        dgl_ref[...] = dgl
        dn3 = _dot(dgl, wg_ref[...], NT)
        dh2n, ggp = _rms_bwd(dn3, h2v, gp_ref[...])
        dh2 = dh3 + dh2n
        dh_ref[...] = dh2
        dhb_ref[...] = dh2.astype(bf16)

        @pl.when(i == 0)
        def _():
            ggp_ref[...] = jnp.zeros_like(ggp_ref)
            ggf_ref[...] = jnp.zeros_like(ggf_ref)
            loss_ref[...] = jnp.zeros_like(loss_ref)

        ggp_ref[...] += ggp
        ggf_ref[...] += ggf
        loss_ref[...] += jnp.full(loss_ref.shape, loss, f32)

    tok = lambda w: pl.BlockSpec((tm, w), lambda i: (i, 0))
    vec = lambda w: pl.BlockSpec((1, w), lambda i: (0, 0))
    return pl.pallas_call(
        body, name="tail", grid=(s // tm,),
        in_specs=[tok(D_MODEL), tok(PLE_DIM), tok(D_MODEL), vec(D_MODEL), vec(D_MODEL),
                  pl.BlockSpec((D_MODEL, D_MODEL), lambda i: (0, 0)),
                  pl.BlockSpec((N_CHIPS, PLE_DIM, pc), lambda i: (0, 0, 0))],
        out_specs=[tok(D_MODEL), tok(D_MODEL), tok(D_MODEL), tok(D_MODEL), tok(D_MODEL), tok(PLE_DIM),
                   vec(D_MODEL), vec(D_MODEL), vec(LANES)],
        out_shape=[_sds((s, D_MODEL), f32), _sds((s, D_MODEL), bf16), _sds((s, D_MODEL), bf16),
                   _sds((s, D_MODEL), bf16), _sds((s, D_MODEL), bf16), _sds((s, PLE_DIM), bf16),
                   _sds((1, D_MODEL), f32), _sds((1, D_MODEL), f32), _sds((1, LANES), f32)],
        scratch_shapes=[pltpu.VMEM((tm, D_MODEL), f32)],
        compiler_params=_arb(1),
    )(h2, p, target, g_ple, g_final, w_gate, w_proj)


def _mlp_bwd(dh2, dh2b, h1, u, g_mlp, w_up, w_down, tm):
    s = h1.shape[0]
    fc = D_FF // N_CHIPS

    def body(dh_ref, dhb_ref, h_ref, u_ref, g_ref, wu_ref, wd_ref, dh1_ref, dh1b_ref, du_ref, gg_ref, dm):
        i, j = pl.program_id(0), pl.program_id(1)

        @pl.when(j == 0)
        def _():
            dm[...] = jnp.zeros_like(dm)

        dr = _dot(dhb_ref[...], wd_ref[...], NT)
        du = (dr * (2.0 * jnp.maximum(u_ref[...].astype(f32), 0.0))).astype(bf16)
        du_ref[...] = du
        dm[...] += _dot(du, wu_ref[...], NT)

        @pl.when((i == 0) & (j == 0))
        def _():
            gg_ref[...] = jnp.zeros_like(gg_ref)

        @pl.when(j == N_CHIPS - 1)
        def _():
            dh1n, gg = _rms_bwd(dm[...], h_ref[...], g_ref[...])
            dh1 = dh_ref[...] + dh1n
            dh1_ref[...] = dh1
            dh1b_ref[...] = dh1.astype(bf16)
            gg_ref[...] += gg

    tok = pl.BlockSpec((tm, D_MODEL), lambda i, j: (i, 0))
    ffb = pl.BlockSpec((tm, fc), lambda i, j: (i, j))
    vec = pl.BlockSpec((1, D_MODEL), lambda i, j: (0, 0))
    return pl.pallas_call(
        body, name="mlp_bwd", grid=(s // tm, N_CHIPS),
        in_specs=[tok, tok, tok, ffb, vec,
                  pl.BlockSpec((None, D_MODEL, fc), lambda i, j: (j, 0, 0)),
                  pl.BlockSpec((None, fc, D_MODEL), lambda i, j: (j, 0, 0))],
        out_specs=[tok, tok, ffb, vec],
        out_shape=[_sds((s, D_MODEL), f32), _sds((s, D_MODEL), bf16), _sds((s, D_FF), bf16),
                   _sds((1, D_MODEL), f32)],
        scratch_shapes=[pltpu.VMEM((tm, D_MODEL), f32)],
        compiler_params=_arb(2),
    )(dh2, dh2b, h1, u, g_mlp, w_up, w_down)


def _out_proj_bwd(dh1b, o, g_attn, w_out, tm, ex=None):
    s = o.shape[0]

    def body(dh_ref, o_ref, g_ref, w_ref, dc_ref, do_ref, gg_ref, dcat):
        i = pl.program_id(0)
        ones = _group_ones(LANES)
        dcat[...] = _dot(dh_ref[...], w_ref[...], NT)
        dc_ref[...] = dcat[:, :W_CONV]

        @pl.when(i == 0)
        def _():
            gg_ref[...] = jnp.zeros_like(gg_ref)

        for j in range(W_ATTN // LANES):
            cols = slice(j * LANES, (j + 1) * LANES)
            d, gg = _head_rms_bwd(dcat[:, W_CONV + j * LANES:W_CONV + (j + 1) * LANES],
                                  o_ref[:, cols], g_ref[:, cols], ones)
            do_ref[:, cols] = d
            gg_ref[:, cols] += gg

    return _call_carrying(
        ex, body, "out_proj_bwd", (s // tm,),
        in_specs=[pl.BlockSpec((tm, D_MODEL), lambda i: (i, 0)),
                  pl.BlockSpec((tm, W_ATTN), lambda i: (i, 0)),
                  pl.BlockSpec((1, W_ATTN), lambda i: (0, 0)),
                  pl.BlockSpec((D_MODEL, D_MODEL), lambda i: (0, 0))],
        out_specs=[pl.BlockSpec((tm, W_CONV), lambda i: (i, 0)),
                   pl.BlockSpec((tm, W_ATTN), lambda i: (i, 0)),
                   pl.BlockSpec((1, W_ATTN), lambda i: (0, 0))],
        out_shape=[_sds((s, W_CONV), f32), _sds((s, W_ATTN), f32), _sds((1, W_ATTN), f32)],
        scratch_shapes=[pltpu.VMEM((tm, D_MODEL), f32)],
        args=(dh1b, o, g_attn, w_out))


def _attn_bwd(proj, kp, vp, kt, do, tot, ex=None):
    s = proj.shape[0]
    t = ATTN_TILE
    nblk = s // t
    npair = W_ATTN // LANES
    qoff = 3 * W_CONV // LANES
    keep01, keepneg = _causal_tiles()

    def body(q_ref, k_ref, v_ref, kt_ref, do_ref, tot_ref, m01_ref, neg_ref, dq_ref, dk_ref, dv_ref,
             w_s, da_s, b_s, g_s, a_s, dz_s, dq_acc, dk_acc, dv_acc):
        qb = pl.program_id(1)
        first = _head_pair_masks(t)
        q = q_ref[...] * (HEAD_DIM ** -0.5)
        qh = (jnp.where(first, q, 0.0).astype(bf16), jnp.where(first, 0.0, q).astype(bf16))
        dov = do_ref[...]
        doh = (jnp.where(first, dov, 0.0).astype(bf16), jnp.where(first, 0.0, dov).astype(bf16))
        total = (tot_ref[0:1, :], tot_ref[1:2, :])

        @pl.when(qb == 0)
        def _():
            dk_acc[...] = jnp.zeros_like(dk_acc)
            dv_acc[...] = jnp.zeros_like(dv_acc)

        dq_acc[...] = jnp.zeros_like(dq_acc)
        a_s[1] = jnp.zeros((t, t), bf16)
        dz_s[1] = jnp.zeros((t, t), bf16)

        def scores(kb, h):
            w_s[h] = _dot(k_ref[kb], qh[h], NT)
            da_s[h] = _dot(v_ref[kb], doh[h], NT)

        def spread(kb, h):
            dq_acc[h] += _dot(kt_ref[kb], dz_s[h])
            dk_acc[kb] += _dot(dz_s[h], qh[h])
            dv_acc[kb] += _dot(a_s[h], doh[h])

        def grads(h, diagonal, lk_before, g_before):
            run = jnp.zeros((SUBLANES, t), f32)
            for a in range(KEY_RUN):
                rows = slice(SUBLANES * a, SUBLANES * (a + 1))
                z2 = w_s[h, rows, :] * LOG2_E
                lk = _log2_keep(z2)
                if diagonal:
                    lk = lk * m01_ref[rows, :]
                log_beta = jnp.minimum(z2 + lk, 0.0)
                run = run + lk
                b_s[h, rows, :] = jnp.exp2(log_beta)
                w_s[h, rows, :] = log_beta - run
            off = total[h] - lk_before - _sublane_scan(run, reverse=False)
            lk_sum = jnp.sum(run, axis=0, keepdims=True)
            run = jnp.zeros((SUBLANES, t), f32)
            for a in range(KEY_RUN // 2):
                parts = []
                for r in (slice(2 * SUBLANES * a, 2 * SUBLANES * a + SUBLANES),
                          slice(2 * SUBLANES * a + SUBLANES, 2 * SUBLANES * (a + 1))):
                    w = w_s[h, r, :] + off
                    if diagonal:
                        w = w + neg_ref[r, :]
                    av = jnp.exp2(w)
                    g = av * da_s[h, r, :]
                    run = run + g
                    da_s[h, r, :] = g
                    g_s[h, r, :] = run
                    parts.append(av)
                a_s[h, 2 * SUBLANES * a:2 * SUBLANES * (a + 1), :] = jnp.concatenate(parts, axis=0).astype(bf16)
            goff = g_before + _sublane_scan(run, reverse=False)
            goff2 = jnp.concatenate([goff, goff], axis=0)
            for a in range(KEY_RUN // 2):
                rows = slice(2 * SUBLANES * a, 2 * SUBLANES * (a + 1))
                dz = da_s[h, rows, :] - b_s[h, rows, :] * (g_s[h, rows, :] + goff2)
                if diagonal:
                    dz = dz * m01_ref[rows, :]
                dz_s[h, rows, :] = dz.astype(bf16)
            return lk_before + lk_sum, g_before + jnp.sum(run, axis=0, keepdims=True)

        def block(kb, before, after, diagonal, carry):
            scores(kb, 1)
            spread(before, 1)
            c0 = grads(0, diagonal, carry[0], carry[1])
            if after is not None:
                scores(after, 0)
            spread(kb, 0)
            c1 = grads(1, diagonal, carry[2], carry[3])
            return c0 + c1

        zero = jnp.zeros((1, t), f32)
        scores(0, 0)
        carry = lax.fori_loop(0, qb, lambda kb, c: block(kb, jnp.maximum(kb - 1, 0), kb + 1, False, c),
                              (zero, zero, zero, zero))
        block(qb, jnp.maximum(qb - 1, 0), None, True, carry)
        spread(qb, 1)
        top = lax.broadcasted_iota(jnp.int32, (LANES, t), 0) < HEAD_DIM
        dq_ref[...] = (jnp.where(top, dq_acc[0], dq_acc[1]).T * (HEAD_DIM ** -0.5)).astype(bf16)

        @pl.when(qb == nblk - 1)
        def _():
            dk_ref[...] = dk_acc[...].astype(bf16)
            dv_ref[...] = dv_acc[...].astype(bf16)

    keys = pl.BlockSpec((nblk, t, LANES), lambda p, i: (0, 0, p))
    tile = pl.BlockSpec((t, t), lambda p, i: (0, 0))
    return _call_carrying(
        ex, body, "attn_bwd", (npair, nblk),
        in_specs=[pl.BlockSpec((t, LANES), lambda p, i: (i, qoff + p)),
                  keys, keys,
                  pl.BlockSpec((nblk, LANES, t), lambda p, i: (0, p, 0)),
                  pl.BlockSpec((t, LANES), lambda p, i: (i, p)),
                  pl.BlockSpec((None, SUBLANES, t), lambda p, i: (p, 0, i)),
                  tile, tile],
        out_specs=[pl.BlockSpec((t, LANES), lambda p, i: (i, p)), keys, keys],
        out_shape=[_sds((s, W_ATTN), bf16), _sds((nblk, t, W_ATTN), bf16), _sds((nblk, t, W_ATTN), bf16)],
        scratch_shapes=[pltpu.VMEM((2, t, t), f32), pltpu.VMEM((2, t, t), f32), pltpu.VMEM((2, t, t), f32),
                        pltpu.VMEM((2, t, t), f32), pltpu.VMEM((2, t, t), bf16), pltpu.VMEM((2, t, t), bf16),
                        pltpu.VMEM((2, LANES, t), f32), pltpu.VMEM((nblk, t, LANES), f32),
                        pltpu.VMEM((nblk, t, LANES), f32)],
        args=(proj, kp, vp, kt, do, tot, keep01, keepneg))


def _conv_bwd(proj, dcn, conv_w, g_conv):
    s = proj.shape[0]
    nblk = W_CONV // LANES
    rc = min(CONV_CHUNK, s)
    nchunk = s // rc

    def body(cb_ref, cc_ref, cu_ref, d_ref, w_ref, g_ref, dcb_ref, dcc_ref, dcu_ref, gw_ref, gg_ref, dy_buf):
        ones = _group_ones(LANES)
        w0, w1, w2 = w_ref[0:1, :], w_ref[1:2, :], w_ref[2:3, :]
        g = g_ref[...]

        def first_pass(i, carry):
            gw0, gw1, gw2, gg = carry
            r0 = pl.multiple_of(i * rc, rc)
            rows = pl.ds(r0, rc)
            prev = pl.ds(pl.multiple_of(jnp.maximum(r0 - 8, 0), 8), 8)
            v = cc_ref[rows, :] * cu_ref[rows, :]
            vp = jnp.where(i > 0, cc_ref[prev, :] * cu_ref[prev, :], 0.0)
            v1, v2 = _shifted(vp, v, 1), _shifted(vp, v, 2)
            y = w2 * v + w1 * v1 + w0 * v2
            cb = cb_ref[rows, :]
            dcy, ggi = _head_rms_bwd(d_ref[rows, :], cb * y, g, ones)
            dcb_ref[rows, :] = (dcy * y).astype(bf16)
            dy = dcy * cb
            dy_buf[rows, :] = dy
            return (gw0 + jnp.sum(dy * v2, axis=0, keepdims=True), gw1 + jnp.sum(dy * v1, axis=0, keepdims=True),
                    gw2 + jnp.sum(dy * v, axis=0, keepdims=True), gg + ggi)

        zero = jnp.zeros((1, LANES), f32)
        gw0, gw1, gw2, gg = lax.fori_loop(0, nchunk, first_pass, (zero, zero, zero, zero))
        gw_ref[...] = jnp.zeros_like(gw_ref)
        gw_ref[0:1, :] = gw0
        gw_ref[1:2, :] = gw1
        gw_ref[2:3, :] = gw2
        gg_ref[...] = gg

        def second_pass(i, carry):
            r0 = pl.multiple_of(i * rc, rc)
            rows = pl.ds(r0, rc)
            nxt = pl.ds(pl.multiple_of(jnp.minimum(r0 + rc, s - 8), 8), 8)
            dy = dy_buf[rows, :]
            dyn = jnp.where(i < nchunk - 1, dy_buf[nxt, :], 0.0)
            ext = jnp.concatenate([dy, dyn], axis=0)
            up1 = pltpu.roll(ext, rc + 8 - 1, axis=0)[:rc]
            up2 = pltpu.roll(ext, rc + 8 - 2, axis=0)[:rc]
            dv = w2 * dy + w1 * up1 + w0 * up2
            dcc_ref[rows, :] = (dv * cu_ref[rows, :]).astype(bf16)
            dcu_ref[rows, :] = (dv * cc_ref[rows, :]).astype(bf16)
            return carry

        lax.fori_loop(0, nchunk, second_pass, 0)

    def col(off):
        return pl.BlockSpec((s, LANES), lambda j: (0, off + j))

    return pl.pallas_call(
        body, name="conv_bwd", grid=(nblk,),
        in_specs=[col(0), col(nblk), col(2 * nblk), col(0),
                  pl.BlockSpec((None, CONV_W_ROWS, LANES), lambda j: (j, 0, 0)),
                  pl.BlockSpec((1, LANES), lambda j: (0, j))],
        out_specs=[col(0), col(0), col(0),
                   pl.BlockSpec((None, CONV_W_ROWS, LANES), lambda j: (j, 0, 0)),
                   pl.BlockSpec((1, LANES), lambda j: (0, j))],
        out_shape=[_sds((s, W_CONV), bf16)] * 3 + [_sds((nblk, CONV_W_ROWS, LANES), f32), _sds((1, W_CONV), f32)],
        scratch_shapes=[pltpu.VMEM((s, LANES), f32)],
        compiler_params=_arb(1),
    )(proj, proj, proj, dcn, conv_w, g_conv)


def _in_proj_bwd(dproj, dh1, x, g_mix, w_in, tm, ex=None):
    s = x.shape[0]
    ncol = IN_COLS // N_CHIPS

    def body(dp_ref, dh_ref, x_ref, g_ref, w_ref, dx_ref, gg_ref):
        i = pl.program_id(0)
        da = _dot(dp_ref[:, 0:ncol], w_ref[0], NT)
        for k in range(1, N_CHIPS):
            da += _dot(dp_ref[:, k * ncol:(k + 1) * ncol], w_ref[k], NT)
        dxn, gg = _rms_bwd(da, x_ref[...], g_ref[...])
        dx_ref[...] = dh_ref[...] + dxn

        @pl.when(i == 0)
        def _():
            gg_ref[...] = jnp.zeros_like(gg_ref)

        gg_ref[...] += gg

    return _call_carrying(
        ex, body, "in_proj_bwd", (s // tm,),
        in_specs=[pl.BlockSpec((tm, IN_COLS), lambda i: (i, 0)),
                  pl.BlockSpec((tm, D_MODEL), lambda i: (i, 0)),
                  pl.BlockSpec((tm, D_MODEL), lambda i: (i, 0)),
                  pl.BlockSpec((1, D_MODEL), lambda i: (0, 0)),
                  pl.BlockSpec((N_CHIPS, D_MODEL, ncol), lambda i: (0, 0, 0))],
        out_specs=[pl.BlockSpec((tm, D_MODEL), lambda i: (i, 0)),
                   pl.BlockSpec((1, D_MODEL), lambda i: (0, 0))],
        out_shape=[_sds((s, D_MODEL), f32), _sds((1, D_MODEL), f32)],
        scratch_shapes=[],
        args=(dproj, dh1, x, g_mix, w_in))


def _weight_grad(a, b, bm, bn, ts, name, relu_sq=False):
    s, m = a.shape
    n = b.shape[1]
    nn = n // bn

    def body(a_ref, b_ref, o_ref):
        @pl.when(pl.program_id(2) == 0)
        def _():
            o_ref[...] = jnp.zeros_like(o_ref)

        av = a_ref[...]
        if relu_sq:
            av = jnp.square(jnp.maximum(av.astype(f32), 0.0)).astype(bf16)
        o_ref[...] += _dot(av, b_ref[...], TN)

    return pl.pallas_call(
        body, name=name, grid=(m // bm, nn, s // ts),
        in_specs=[pl.BlockSpec((ts, bm), lambda i, j, k: (k, i)),
                  pl.BlockSpec((ts, bn), lambda i, j, k: (k, j))],
        out_specs=pl.BlockSpec((None, bm, bn), lambda i, j, k: (i * nn + j, 0, 0)),
        out_shape=_sds(((m // bm) * nn, bm, bn), f32),
        compiler_params=_arb(3),
    )(a, b)


def kernel(x, p, g_mix, w_in, conv_w, g_conv_out, g_attn_out, w_out, g_mlp, w_up, w_down, g_ple, w_ple_gate, w_ple_proj, g_final, loss_target, m_g_mix, m_w_in, m_conv_w, m_g_conv_out, m_g_attn_out, m_w_out, m_g_mlp, m_w_up, m_w_down, m_g_ple, m_w_ple_gate, m_w_ple_proj, m_g_final, v_g_mix, v_w_in, v_conv_w, v_g_conv_out, v_g_attn_out, v_w_out, v_g_mlp, v_w_up, v_w_down, v_g_ple, v_w_ple_gate, v_w_ple_proj, v_g_final):
    s = x.shape[1]
    tm = min(TOKEN_TILE, s)
    tg = min(GRAD_TOKEN_TILE, s)
    xs = x.reshape(s, D_MODEL)
    ps = p.reshape(s, PLE_DIM)
    target = loss_target.reshape(s, D_MODEL)
    core = lax.axis_index("c").astype(jnp.int32).reshape(1)
    chip = 2 * lax.axis_index("x") + lax.axis_index("y")

    big = {"w_in": w_in[0], "w_out": w_out[0], "w_up": w_up[0], "w_down": w_down[0],
           "w_ple_gate": w_ple_gate[0], "w_ple_proj": w_ple_proj[0]}
    names = list(big)
    conv_shard = jnp.pad(conv_w[0], ((0, CONV_W_ROWS - conv_w.shape[1]), (0, 0)))
    later_names = names[1:]
    w_in_f, conv_f = _run_exchange(_gather_exchange([big["w_in"].astype(bf16), conv_shard]), "gather_w_in")

    proj, a_b = _in_proj(xs, g_mix, w_in_f, tm)
    conv_n = _conv_fwd(proj, conv_f, g_conv_out)
    nblk = s // ATTN_TILE
    koff = 3 * W_CONV + W_ATTN
    kp = _permute_keys(proj[:, koff:koff + W_ATTN].astype(bf16)).reshape(nblk, ATTN_TILE, W_ATTN)
    vp = _permute_keys(proj[:, koff + W_ATTN:].astype(bf16)).reshape(nblk, ATTN_TILE, W_ATTN)
    kt, vt = kp.transpose(0, 2, 1), vp.transpose(0, 2, 1)
    (o, tot), gathered = _attn_fwd(proj, kp, vt, _gather_exchange([big[k].astype(bf16) for k in later_names]))
    w_out_f, w_up_f, w_down_f, w_gate_f, w_proj_f = gathered
    w_out_f = w_out_f.reshape(D_MODEL, D_MODEL)
    w_gate_f = w_gate_f.reshape(D_MODEL, D_MODEL)
    h1, cat_b = _out_proj(o, conv_n, xs, g_attn_out, w_out_f, tm)
    h2, u_b, m_b = _mlp_fwd(h1, g_mlp, w_up_f, w_down_f, min(MLP_TOKEN_TILE, s))

    dh2, dh2_b, n3_b, dgl_b, dpp_b, p_b, gg_ple, gg_final, loss_row = _tail(
        h2, ps, target, g_ple, g_final.reshape(1, D_MODEL), w_gate_f, w_proj_f, tm)
    dh1, dh1_b, du_b, gg_mlp = _mlp_bwd(dh2, dh2_b, h1, u_b, g_mlp, w_up_f, w_down_f, tm)
    part = {
        "w_out": _weight_grad(cat_b, dh1_b, D_MODEL, D_MODEL, tg, "grad_w_out").reshape(N_CHIPS, D_MODEL // N_CHIPS, D_MODEL),
        "w_up": _weight_grad(m_b, du_b, D_MODEL, D_FF // N_CHIPS, tg, "grad_w_up"),
        "w_down": _weight_grad(u_b, dh2_b, D_FF // N_CHIPS, D_MODEL, tg, "grad_w_down", relu_sq=True),
        "w_ple_gate": _weight_grad(n3_b, dgl_b, D_MODEL, D_MODEL, tg, "grad_w_ple_gate").reshape(N_CHIPS, D_MODEL // N_CHIPS, D_MODEL),
        "w_ple_proj": _weight_grad(p_b, dpp_b, PLE_DIM, D_MODEL // N_CHIPS, tg, "grad_w_ple_proj"),
    }
    (dcn, do, gg_attn), from_sibling = _out_proj_bwd(
        dh1_b, o, g_attn_out, w_out_f, tm, _pair_exchange([part[k] for k in later_names]))
    pair = [_pair_sum(part[k], r, core, "pair_sum_" + k) for k, r in zip(later_names, from_sibling)]
    dcb, dcc, dcu, g_conv_w, gg_conv = _conv_bwd(proj, dcn, conv_f, g_conv_out)
    (dq, dk, dv), from_chips = _attn_bwd(proj, kp, vp, kt, do, tot, _chip_exchange(pair))
    dk = _unpermute_keys(dk.reshape(s, W_ATTN))
    dv = _unpermute_keys(dv.reshape(s, W_ATTN))
    dproj = jnp.concatenate([dcb, dcc, dcu, dq, dk, dv], axis=1)

    part["w_in"] = _weight_grad(a_b, dproj, D_MODEL, IN_COLS // N_CHIPS, tg, "grad_w_in")
    in_sibling = _run_exchange(_pair_exchange([part["w_in"]]), "grad_pair_exchange_w_in")
    in_pair = _pair_sum(part["w_in"], in_sibling[0], core, "pair_sum_w_in")
    (grad_x, gg_mix), in_chips = _in_proj_bwd(dproj, dh1, xs, g_mix, w_in_f, tm, _chip_exchange([in_pair]))

    place = jnp.stack([lax.axis_index("c"), chip]).astype(jnp.int32)
    half = [_chip_sum(mine, landed, place, "chip_sum_" + k)
            for k, mine, landed in zip(names, [in_pair] + pair, list(in_chips) + list(from_chips))]
    both = _sibling_exchange(half)
    grad = {k: b.reshape(big[k].shape) for k, b in zip(names, both)}

    gcw = g_conv_w[:, :3, :].transpose(1, 0, 2).reshape(3, W_CONV)
    row = lambda *parts: jnp.concatenate(parts, axis=1)
    packed = jnp.concatenate([
        gg_mix, gg_mlp, gg_ple, gg_final, row(gg_conv, gg_attn), row(gcw[0:1], gcw[1:2]),
        row(gcw[2:3], loss_row, jnp.zeros((1, W_CONV - LANES), f32)), jnp.zeros((1, D_MODEL), f32)], axis=0)
    summed = _allreduce_small(packed)
    loss = summed[6, W_CONV]
    gcw_full = jnp.stack([summed[5, :W_CONV], summed[5, W_CONV:], summed[6, :W_CONV]])
    grad["conv_w"] = lax.dynamic_slice(gcw_full, (0, chip * LANES), (3, LANES))
    vec_names = ["g_mix", "g_mlp", "g_ple", "g_final"]
    vec_w = {"g_mix": g_mix, "g_mlp": g_mlp, "g_ple": g_ple, "g_final": g_final.reshape(1, D_MODEL)}
    vec_m = {"g_mix": m_g_mix, "g_mlp": m_g_mlp, "g_ple": m_g_ple, "g_final": m_g_final.reshape(1, D_MODEL)}
    vec_v = {"g_mix": v_g_mix, "g_mlp": v_g_mlp, "g_ple": v_g_ple, "g_final": v_g_final.reshape(1, D_MODEL)}

    def pack_vec(d, conv, attn):
        return jnp.concatenate([d[k] for k in vec_names] + [row(conv, attn)], axis=0)

    vec_g = summed[0:5]
    vec_d, vec_nm, vec_nv = _adamw(vec_g, pack_vec(vec_w, g_conv_out, g_attn_out),
                                   pack_vec(vec_m, m_g_conv_out, m_g_attn_out),
                                   pack_vec(vec_v, v_g_conv_out, v_g_attn_out), "adamw_vectors")

    given_w = dict(big, conv_w=conv_w[0])
    given_m = {"w_in": m_w_in[0], "w_out": m_w_out[0], "w_up": m_w_up[0], "w_down": m_w_down[0],
               "w_ple_gate": m_w_ple_gate[0], "w_ple_proj": m_w_ple_proj[0], "conv_w": m_conv_w[0]}
    given_v = {"w_in": v_w_in[0], "w_out": v_w_out[0], "w_up": v_w_up[0], "w_down": v_w_down[0],
               "w_ple_gate": v_w_ple_gate[0], "w_ple_proj": v_w_ple_proj[0], "conv_w": v_conv_w[0]}
    delta, new_m, new_v = {}, {}, {}
    for k in names + ["conv_w"]:
        delta[k], new_m[k], new_v[k] = _adamw(grad[k], given_w[k], given_m[k], given_v[k], "adamw_" + k)

    def unpack(vals, kind):
        out = {k: vals[i:i + 1] for i, k in enumerate(vec_names)}
        out["g_final"] = out["g_final"].reshape(D_MODEL)
        out["g_conv_out"] = vals[4:5, :W_CONV]
        out["g_attn_out"] = vals[4:5, W_CONV:]
        out.update({k: v[None] for k, v in kind.items()})
        return out

    order = ["g_mix", "w_in", "conv_w", "g_conv_out", "g_attn_out", "w_out", "g_mlp", "w_up", "w_down",
             "g_ple", "w_ple_gate", "w_ple_proj", "g_final"]
    groups = [unpack(vec_g, grad), unpack(vec_d, delta), unpack(vec_nm, new_m), unpack(vec_nv, new_v)]
    return (loss, grad_x[None]) + tuple(g[k] for g in groups for k in order)
```

```python
import functools

import jax
import jax.numpy as jnp
from jax import lax
from jax.experimental import pallas as pl
from jax.experimental.pallas import tpu as pltpu

f32 = jnp.float32
bf16 = jnp.bfloat16

D_MODEL = 1024
HEAD_DIM = 64
W_CONV = 512
W_ATTN = 512
D_FF = 4096
PLE_DIM = 256
IN_COLS = 3 * W_CONV + 3 * W_ATTN
N_CHIPS = 4
EPS = 1e-6
ADAM_LR = 0.001
ADAM_B1 = 0.9
ADAM_B2 = 0.999
ADAM_EPS = 1e-08
ADAM_WD = 0.01
ADAM_STEP = 10

LANES = 128
TOKEN_TILE = 512
MLP_TOKEN_TILE = 1024
GRAD_TOKEN_TILE = 2048
ATTN_TILE = 256
CONV_CHUNK = 512
CONV_W_ROWS = 16

MESH = pl.DeviceIdType.MESH
ANY = pl.BlockSpec(memory_space=pl.ANY)
NT = (((1,), (1,)), ((), ()))
TN = (((0,), (0,)), ((), ()))


def _arb(n):
    return pltpu.CompilerParams(dimension_semantics=("arbitrary",) * n)


def _sds(shape, dtype):
    return jax.ShapeDtypeStruct(shape, dtype)


def _dot(a, b, dims=None):
    if dims is None:
        return jnp.dot(a, b, preferred_element_type=f32)
    return lax.dot_general(a, b, dims, preferred_element_type=f32)


def _split_dot(x, ones):
    hi = x.astype(bf16)
    lo = (x - hi.astype(f32)).astype(bf16)
    return _dot(hi, ones) + _dot(lo, ones)


def _rms_fwd(h, g):
    rstd = lax.rsqrt(jnp.mean(h * h, axis=-1, keepdims=True) + EPS)
    return h * rstd * g, rstd


def _rms_bwd(dy, h, g):
    rstd = lax.rsqrt(jnp.mean(h * h, axis=-1, keepdims=True) + EPS)
    hn = h * rstd
    dyg = dy * g
    dh = rstd * (dyg - hn * jnp.mean(dyg * hn, axis=-1, keepdims=True))
    return dh, jnp.sum(dy * hn, axis=0, keepdims=True)


def _group_ones(n):
    r = lax.broadcasted_iota(jnp.int32, (n, n), 0) // HEAD_DIM
    c = lax.broadcasted_iota(jnp.int32, (n, n), 1) // HEAD_DIM
    return (r == c).astype(bf16)


def _head_rms_fwd(y, g, ones):
    rstd = lax.rsqrt(_split_dot(y * y, ones) * (1.0 / HEAD_DIM) + EPS)
    return y * rstd * g


def _head_rms_bwd(dy, y, g, ones):
    rstd = lax.rsqrt(_split_dot(y * y, ones) * (1.0 / HEAD_DIM) + EPS)
    yn = y * rstd
    dyg = dy * g
    dyy = rstd * (dyg - yn * (_split_dot(dyg * yn, ones) * (1.0 / HEAD_DIM)))
    return dyy, jnp.sum(dy * yn, axis=0, keepdims=True)


def _place():
    return lax.axis_index("x"), lax.axis_index("y"), lax.axis_index("c")


def _other_chips(x, y):
    return [(1 - x, y), (x, 1 - y), (1 - x, 1 - y)]


class _Exchange:
    def __init__(self, arrays, out_shapes, sems, start, finish):
        self.arrays, self.out_shapes, self.sems, self.start, self.finish = arrays, out_shapes, sems, start, finish


def _gather_exchange(shards):
    n = len(shards)
    halves = [s.shape[0] // 2 for s in shards]

    def plan(ins, outs, sems):
        send_sems, recv_sems, own_sems = sems
        x, y, c = _place()
        me = 2 * x + y
        chips = _other_chips(x, y)

        def half(ref, i, which):
            return ref.at[pl.ds(which * halves[i], halves[i]), :]

        def over_ici(i, j, src, slot, to):
            return pltpu.make_async_remote_copy(
                src_ref=src, dst_ref=half(outs[i].at[slot], i, c),
                send_sem=send_sems.at[3 * i + j], recv_sem=recv_sems.at[3 * i + j],
                device_id=to, device_id_type=MESH)

        def to_sibling(i, j, slot, which):
            blk = half(outs[i].at[slot], i, which)
            return pltpu.make_async_remote_copy(
                src_ref=blk, dst_ref=blk,
                send_sem=send_sems.at[3 * n + 3 * i + j], recv_sem=recv_sems.at[3 * n + 3 * i + j],
                device_id=(x, y, 1 - c), device_id_type=MESH)

        own = [pltpu.make_async_remote_copy(
            src_ref=ins[i], dst_ref=outs[i].at[me], send_sem=own_sems.at[i], recv_sem=own_sems.at[n + i],
            device_id=(x, y, 1 - c), device_id_type=MESH) for i in range(n)]
        pairs = [(i, j, px, py) for i in range(n) for j, (px, py) in enumerate(chips)]
        sends = [over_ici(i, j, half(ins[i], i, c), me, (px, py, c)) for i, j, px, py in pairs]
        lands = [over_ici(i, j, half(outs[i].at[2 * px + py], i, c), 2 * px + py, (px, py, c)) for i, j, px, py in pairs]
        passes = [to_sibling(i, j, 2 * px + py, c) for i, j, px, py in pairs]
        from_sibling = [to_sibling(i, j, 2 * px + py, 1 - c) for i, j, px, py in pairs]
        return own, sends, lands, passes, from_sibling

    def start(ins, outs, sems):
        own, sends, _, _, _ = plan(ins, outs, sems)
        for cp in own + sends:
            cp.start()

    def finish(ins, outs, sems):
        own, sends, lands, passes, from_sibling = plan(ins, outs, sems)
        for land, on in zip(lands, passes):
            land.wait_recv()
            on.start()
        for cp in from_sibling:
            cp.wait_recv()
        for cp in sends + passes:
            cp.wait_send()
        for cp in own:
            cp.wait()

    return _Exchange(
        shards, [_sds((N_CHIPS,) + s.shape, s.dtype) for s in shards],
        [pltpu.SemaphoreType.DMA((6 * n,)), pltpu.SemaphoreType.DMA((6 * n,)), pltpu.SemaphoreType.DMA((2 * n,))],
        start, finish)


def _call_carrying(ex, body, name, grid, in_specs, out_specs, out_shape, scratch_shapes, args):
    n_in, n_out, n_scr = len(in_specs), len(out_specs), len(scratch_shapes)
    k = 0 if ex is None else len(ex.arrays)

    def wrapped(*refs):
        ins, xin = refs[:n_in], refs[n_in:n_in + k]
        outs, xout = refs[n_in + k:n_in + k + n_out], refs[n_in + k + n_out:n_in + 2 * k + n_out]
        scr, sems = refs[n_in + 2 * k + n_out:n_in + 2 * k + n_out + n_scr], refs[n_in + 2 * k + n_out + n_scr:]
        ids = [pl.program_id(d) for d in range(len(grid))]
        if ex is not None:
            @pl.when(functools.reduce(lambda a, b: a & b, [i == 0 for i in ids]))
            def _():
                ex.start(xin, xout, sems)

        body(*ins, *outs, *scr)
        if ex is not None:
            @pl.when(functools.reduce(lambda a, b: a & b, [i == g - 1 for i, g in zip(ids, grid)]))
            def _():
                ex.finish(xin, xout, sems)

    res = pl.pallas_call(
        wrapped, name=name, grid=grid,
        in_specs=list(in_specs) + [ANY] * k, out_specs=list(out_specs) + [ANY] * k,
        out_shape=list(out_shape) + ([] if ex is None else list(ex.out_shapes)),
        scratch_shapes=list(scratch_shapes) + ([] if ex is None else list(ex.sems)),
        compiler_params=_arb(len(grid)),
    )(*args, *([] if ex is None else ex.arrays))
    return res[:n_out], res[n_out:]


def _run_exchange(ex, name):
    n = len(ex.arrays)

    def body(*refs):
        ins, outs, sems = refs[:n], refs[n:2 * n], refs[2 * n:]
        ex.start(ins, outs, sems)
        ex.finish(ins, outs, sems)

    return pl.pallas_call(
        body, name=name, out_shape=ex.out_shapes, in_specs=[ANY] * n, out_specs=[ANY] * n,
        scratch_shapes=ex.sems,
    )(*ex.arrays)


def _pair_exchange(grads):
    n = len(grads)

    def plan(ins, outs, sems):
        send_sems, recv_sems = sems
        x, y, c = _place()
        return [pltpu.make_async_remote_copy(
            src_ref=ins[i].at[:, 1 - c], dst_ref=outs[i],
            send_sem=send_sems.at[i], recv_sem=recv_sems.at[i],
            device_id=(x, y, 1 - c), device_id_type=MESH) for i in range(n)]

    def start(ins, outs, sems):
        for cp in plan(ins, outs, sems):
            cp.start()

    def finish(ins, outs, sems):
        for cp in plan(ins, outs, sems):
            cp.wait()

    views = [g.reshape(N_CHIPS, 2, g.shape[1] // 2, g.shape[2]) for g in grads]
    return _Exchange(
        views, [_sds((N_CHIPS, v.shape[2], v.shape[3]), f32) for v in views],
        [pltpu.SemaphoreType.DMA((n,)), pltpu.SemaphoreType.DMA((n,))], start, finish)


def _chip_exchange(parts):
    n = len(parts)

    def plan(ins, outs, sems):
        send_sems, recv_sems = sems
        x, y, c = _place()
        me = 2 * x + y
        pairs = [(i, j, px, py) for i in range(n) for j, (px, py) in enumerate(_other_chips(x, y))]

        def copy(i, j, src, slot, px, py):
            return pltpu.make_async_remote_copy(
                src_ref=src, dst_ref=outs[i].at[slot],
                send_sem=send_sems.at[3 * i + j], recv_sem=recv_sems.at[3 * i + j],
                device_id=(px, py, c), device_id_type=MESH)

        sends = [copy(i, j, ins[i].at[2 * px + py], me, px, py) for i, j, px, py in pairs]
        lands = [copy(i, j, outs[i].at[2 * px + py], 2 * px + py, px, py) for i, j, px, py in pairs]
        return sends, lands

    def start(ins, outs, sems):
        sends, _ = plan(ins, outs, sems)
        for cp in sends:
            cp.start()

    def finish(ins, outs, sems):
        sends, lands = plan(ins, outs, sems)
        for cp in lands:
            cp.wait_recv()
        for cp in sends:
            cp.wait_send()

    return _Exchange(
        parts, [_sds(p.shape, p.dtype) for p in parts],
        [pltpu.SemaphoreType.DMA((3 * n,)), pltpu.SemaphoreType.DMA((3 * n,))],
        start, finish)


def _sibling_exchange(both):
    n = len(both)

    def body(*refs):
        outs = refs[n:2 * n]
        send_sems, recv_sems = refs[2 * n:]
        x, y, c = _place()
        sent = []
        for i in range(n):
            cp = pltpu.make_async_remote_copy(
                src_ref=outs[i].at[c], dst_ref=outs[i].at[c],
                send_sem=send_sems.at[i], recv_sem=recv_sems.at[i],
                device_id=(x, y, 1 - c), device_id_type=MESH)
            cp.start()
            sent.append(cp)
        for cp in sent:
            cp.wait()

    return pl.pallas_call(
        body, name="grad_sibling_exchange",
        out_shape=[_sds(b.shape, f32) for b in both],
        in_specs=[ANY] * n, out_specs=[ANY] * n,
        input_output_aliases={i: i for i in range(n)},
        scratch_shapes=[pltpu.SemaphoreType.DMA((n,)), pltpu.SemaphoreType.DMA((n,))],
    )(*both)


def _row_tile(rows, cols):
    t = rows
    while t * cols * 4 > (2 << 20) and t % 16 == 0:
        t //= 2
    return t


def _pair_sum(grad, recv, core, name):
    _, r, c = grad.shape
    hr = r // 2
    tr = _row_tile(hr, c)
    view = grad.reshape(N_CHIPS, 2, hr, c)

    def body(core_ref, mine_ref, recv_ref, out_ref):
        out_ref[...] = (mine_ref[...] + recv_ref[...]).astype(bf16)

    return pl.pallas_call(
        body, name=name,
        grid_spec=pltpu.PrefetchScalarGridSpec(
            num_scalar_prefetch=1, grid=(N_CHIPS, hr // tr),
            in_specs=[pl.BlockSpec((None, None, tr, c), lambda k, t, core_ref: (k, core_ref[0], t, 0)),
                      pl.BlockSpec((None, tr, c), lambda k, t, core_ref: (k, t, 0))],
            out_specs=pl.BlockSpec((None, tr, c), lambda k, t, core_ref: (k, t, 0))),
        out_shape=_sds((N_CHIPS, hr, c), bf16),
        compiler_params=_arb(2),
    )(core, view, recv)


def _chip_sum(mine, landed, place, name):
    _, hr, c = mine.shape
    tr = _row_tile(hr, c)

    def body(place_ref, a_ref, b_ref, c_ref, d_ref, out_ref):
        out_ref[...] = ((a_ref[...].astype(f32) + b_ref[...].astype(f32)) + c_ref[...].astype(f32)) + d_ref[...].astype(f32)

    def slot(k):
        return pl.BlockSpec((None, tr, c), lambda t, place_ref: ((place_ref[1] + k) % N_CHIPS, t, 0))

    return pl.pallas_call(
        body, name=name,
        grid_spec=pltpu.PrefetchScalarGridSpec(
            num_scalar_prefetch=1, grid=(hr // tr,),
            in_specs=[slot(0), slot(1), slot(2), slot(3)],
            out_specs=pl.BlockSpec((None, tr, c), lambda t, place_ref: (place_ref[0], t, 0))),
        out_shape=_sds((2, hr, c), f32), compiler_params=_arb(1),
    )(place, mine, landed, landed, landed)


def _adamw(g, w, m, v, name):
    r, c = g.shape
    tr = _row_tile(r, c)

    def body(g_ref, w_ref, m_ref, v_ref, d_ref, nm_ref, nv_ref):
        gv = g_ref[...]
        mv = ADAM_B1 * m_ref[...] + (1.0 - ADAM_B1) * gv
        vv = ADAM_B2 * v_ref[...] + (1.0 - ADAM_B2) * jnp.square(gv)
        m_hat = mv / (1.0 - ADAM_B1 ** ADAM_STEP)
        v_hat = vv / (1.0 - ADAM_B2 ** ADAM_STEP)
        d_ref[...] = -ADAM_LR * (m_hat / (jnp.sqrt(v_hat) + ADAM_EPS) + ADAM_WD * w_ref[...])
        nm_ref[...] = mv
        nv_ref[...] = vv

    spec = pl.BlockSpec((tr, c), lambda t: (t, 0))
    return pl.pallas_call(
        body, name=name, grid=(r // tr,), in_specs=[spec] * 4, out_specs=[spec] * 3,
        out_shape=[_sds((r, c), f32)] * 3, compiler_params=_arb(1),
    )(g, w, m, v)


def _allreduce_small(packed):
    shape = packed.shape

    def body(x_ref, out_ref, buf, send_sems, recv_sems):
        x, y, c = _place()
        me = 4 * x + 2 * y + c
        buf[me] = x_ref[...]
        sent = []
        for r in range(1, 8):
            dx, dy, dc = (r >> 2) & 1, (r >> 1) & 1, r & 1
            peer = ((1 - x) if dx else x, (1 - y) if dy else y, (1 - c) if dc else c)
            cp = pltpu.make_async_remote_copy(
                src_ref=x_ref, dst_ref=buf.at[me],
                send_sem=send_sems.at[r], recv_sem=recv_sems.at[r],
                device_id=peer, device_id_type=MESH)
            cp.start()
            sent.append((cp, peer))
        for r, (cp, peer) in enumerate(sent, start=1):
            src = 4 * peer[0] + 2 * peer[1] + peer[2]
            pltpu.make_async_remote_copy(
                src_ref=x_ref, dst_ref=buf.at[src],
                send_sem=send_sems.at[r], recv_sem=recv_sems.at[r],
                device_id=peer, device_id_type=MESH).wait_recv()
        for cp, _ in sent:
            cp.wait_send()
        total = buf[0]
        for k in range(1, 8):
            total = total + buf[k]
        out_ref[...] = total

    vmem = pl.BlockSpec(memory_space=pltpu.VMEM)
    return pl.pallas_call(
        body, name="allreduce_small", out_shape=_sds(shape, f32),
        in_specs=[vmem], out_specs=vmem,
        scratch_shapes=[pltpu.VMEM((8,) + shape, f32), pltpu.SemaphoreType.DMA((8,)),
                        pltpu.SemaphoreType.DMA((8,))],
    )(packed)


def _in_proj(x, g_mix, w_in, tm):
    s = x.shape[0]
    ncol = IN_COLS // N_CHIPS
    t = ATTN_TILE
    nb = tm // t
    koff = 3 * W_CONV + W_ATTN

    def body(x_ref, g_ref, w_ref, proj_ref, a_ref, kp_ref, vp_ref, kt_ref, vt_ref, perm, kv):
        a, _ = _rms_fwd(x_ref[...], g_ref[...])
        ab = a.astype(bf16)
        a_ref[...] = ab
        for k in range(N_CHIPS):
            proj_ref[:, k * ncol:(k + 1) * ncol] = _dot(ab, w_ref[k])
        for j in range(2 * W_ATTN // LANES):
            kv[j] = proj_ref[:, koff + j * LANES:koff + (j + 1) * LANES]
        for b in range(nb):
            for r in range(KEY_RUN):
                for j in range(2 * W_ATTN // LANES):
                    perm[SUBLANES * r:SUBLANES * (r + 1), j * LANES:(j + 1) * LANES] = kv[
                        j, pl.ds(b * t + r, SUBLANES, stride=KEY_RUN), :]
            kp_ref[b] = perm[:, :W_ATTN].astype(bf16)
            vp_ref[b] = perm[:, W_ATTN:].astype(bf16)
            kt_ref[b] = perm[:, :W_ATTN].T.astype(bf16)
            vt_ref[b] = perm[:, W_ATTN:].T.astype(bf16)

    keys = pl.BlockSpec((nb, t, W_ATTN), lambda i: (i, 0, 0))
    keys_t = pl.BlockSpec((nb, W_ATTN, t), lambda i: (i, 0, 0))
    return pl.pallas_call(
        body, name="in_proj", grid=(s // tm,),
        in_specs=[pl.BlockSpec((tm, D_MODEL), lambda i: (i, 0)),
                  pl.BlockSpec((1, D_MODEL), lambda i: (0, 0)),
                  pl.BlockSpec((N_CHIPS, D_MODEL, ncol), lambda i: (0, 0, 0))],
        out_specs=[pl.BlockSpec((tm, IN_COLS), lambda i: (i, 0)),
                   pl.BlockSpec((tm, D_MODEL), lambda i: (i, 0)), keys, keys, keys_t, keys_t],
        out_shape=[_sds((s, IN_COLS), f32), _sds((s, D_MODEL), bf16),
                   _sds((s // t, t, W_ATTN), bf16), _sds((s // t, t, W_ATTN), bf16),
                   _sds((s // t, W_ATTN, t), bf16), _sds((s // t, W_ATTN, t), bf16)],
        scratch_shapes=[pltpu.VMEM((t, 2 * W_ATTN), f32), pltpu.VMEM((2 * W_ATTN // LANES, tm, LANES), f32)],
        compiler_params=_arb(1),
    )(x, g_mix, w_in)


def _shifted(prev8, cur, shift):
    ext = jnp.concatenate([prev8, cur], axis=0)
    return pltpu.roll(ext, shift, axis=0)[8:]


def _conv_fwd(proj, conv_w, g_conv):
    s = proj.shape[0]
    nblk = W_CONV // LANES
    rc = min(CONV_CHUNK, s)

    def body(cb_ref, cc_ref, cu_ref, w_ref, g_ref, out_ref):
        ones = _group_ones(LANES)
        w0, w1, w2 = w_ref[0:1, :], w_ref[1:2, :], w_ref[2:3, :]
        g = g_ref[...]

        def chunk(i, carry):
            r0 = pl.multiple_of(i * rc, rc)
            rows = pl.ds(r0, rc)
            prev = pl.ds(pl.multiple_of(jnp.maximum(r0 - 8, 0), 8), 8)
            v = cc_ref[rows, :] * cu_ref[rows, :]
            vp = jnp.where(i > 0, cc_ref[prev, :] * cu_ref[prev, :], 0.0)
            y = w2 * v + w1 * _shifted(vp, v, 1) + w0 * _shifted(vp, v, 2)
            out_ref[rows, :] = _head_rms_fwd(cb_ref[rows, :] * y, g, ones).astype(bf16)
            return carry

        lax.fori_loop(0, s // rc, chunk, 0)

    def col(off):
        return pl.BlockSpec((s, LANES), lambda j: (0, off + j))

    return pl.pallas_call(
        body, name="conv_fwd", grid=(nblk,),
        in_specs=[col(0), col(nblk), col(2 * nblk),
                  pl.BlockSpec((None, CONV_W_ROWS, LANES), lambda j: (j, 0, 0)),
                  pl.BlockSpec((1, LANES), lambda j: (0, j))],
        out_specs=pl.BlockSpec((s, LANES), lambda j: (0, j)),
        out_shape=_sds((s, W_CONV), bf16), compiler_params=_arb(1),
    )(proj, proj, proj, conv_w, g_conv)


LOG2_E = 1.4426950408889634


def _log2_keep(z2):
    nz2 = -z2
    return jnp.minimum(nz2, 0.0) - jnp.log2(1.0 + jnp.exp2(jnp.minimum(z2, nz2)))


def _head_pair_masks(rows):
    lane = lax.broadcasted_iota(jnp.int32, (rows, LANES), 1)
    return lane < HEAD_DIM


SUBLANES = 8
KEY_RUN = ATTN_TILE // SUBLANES


def _causal_tiles():
    r = lax.broadcasted_iota(jnp.int32, (ATTN_TILE, ATTN_TILE), 0)
    key = (r % SUBLANES) * KEY_RUN + r // SUBLANES
    below = key < lax.broadcasted_iota(jnp.int32, (ATTN_TILE, ATTN_TILE), 1)
    return below.astype(f32), jnp.where(below, 0.0, -1e30).astype(f32)


def _sublane_scan(x, reverse):
    row = lax.broadcasted_iota(jnp.int32, x.shape, 0)
    inc = x
    for sh in (1, 2, 4):
        if reverse:
            inc = inc + jnp.where(row < SUBLANES - sh, pltpu.roll(inc, SUBLANES - sh, axis=0), 0.0)
        else:
            inc = inc + jnp.where(row >= sh, pltpu.roll(inc, sh, axis=0), 0.0)
    return inc - x


def _attn_fwd(proj, kp, vt, ex=None):
    s = proj.shape[0]
    t = ATTN_TILE
    nblk = s // t
    npair = W_ATTN // LANES
    qoff = 3 * W_CONV // LANES
    keep01, keepneg = _causal_tiles()

    def body(q_ref, k_ref, vt_ref, m01_ref, neg_ref, o_ref, tot_ref, w_s, a_s, acc):
        qb = pl.program_id(1)
        first = _head_pair_masks(t)
        q = q_ref[...] * (HEAD_DIM ** -0.5)
        qh = (jnp.where(first, q, 0.0).astype(bf16), jnp.where(first, 0.0, q).astype(bf16))
        acc[...] = jnp.zeros_like(acc)
        a_s[1] = jnp.zeros((t, t), bf16)

        def scores(kb, h):
            w_s[h] = _dot(k_ref[kb], qh[h], NT)

        def weigh(kb, h):
            acc[h] += _dot(vt_ref[kb], a_s[h])

        def weights(h, diagonal, later):
            run = jnp.zeros((SUBLANES, t), f32)
            for a in reversed(range(KEY_RUN)):
                rows = slice(SUBLANES * a, SUBLANES * (a + 1))
                z2 = w_s[h, rows, :] * LOG2_E
                lk = _log2_keep(z2)
                if diagonal:
                    lk = lk * m01_ref[rows, :]
                run = run + lk
                w_s[h, rows, :] = z2 + run
            off = _sublane_scan(run, reverse=True) + later
            off2 = jnp.concatenate([off, off], axis=0)
            for a in range(t // (2 * SUBLANES)):
                rows = slice(2 * SUBLANES * a, 2 * SUBLANES * (a + 1))
                w = w_s[h, rows, :] + off2
                if diagonal:
                    w = w + neg_ref[rows, :]
                a_s[h, rows, :] = jnp.exp2(w).astype(bf16)
            return later + jnp.sum(run, axis=0, keepdims=True)

        def block(kb, before, after, diagonal, later):
            scores(kb, 1)
            weigh(before, 1)
            l0 = weights(0, diagonal, later[0])
            scores(after, 0)
            weigh(kb, 0)
            l1 = weights(1, diagonal, later[1])
            return l0, l1

        zero = jnp.zeros((1, t), f32)
        scores(qb, 0)
        later = block(qb, qb, jnp.maximum(qb - 1, 0), True, (zero, zero))

        def earlier(i, c):
            kb = qb - 1 - i
            return block(kb, kb + 1, jnp.maximum(kb - 1, 0), False, c)

        later = lax.fori_loop(0, qb, earlier, later)
        weigh(0, 1)
        top = lax.broadcasted_iota(jnp.int32, (LANES, t), 0) < HEAD_DIM
        o_ref[...] = jnp.where(top, acc[0], acc[1]).T
        tot_ref[...] = jnp.concatenate([later[0], later[1], jnp.zeros((SUBLANES - 2, t), f32)], axis=0)

    return _call_carrying(
        ex, body, "attn_fwd", (npair, nblk),
        in_specs=[pl.BlockSpec((t, LANES), lambda p, i: (i, qoff + p)),
                  pl.BlockSpec((nblk, t, LANES), lambda p, i: (0, 0, p)),
                  pl.BlockSpec((nblk, LANES, t), lambda p, i: (0, p, 0)),
                  pl.BlockSpec((t, t), lambda p, i: (0, 0)),
                  pl.BlockSpec((t, t), lambda p, i: (0, 0))],
        out_specs=[pl.BlockSpec((t, LANES), lambda p, i: (i, p)),
                   pl.BlockSpec((None, SUBLANES, t), lambda p, i: (p, 0, i))],
        out_shape=[_sds((s, W_ATTN), f32), _sds((npair, SUBLANES, s), f32)],
        scratch_shapes=[pltpu.VMEM((2, t, t), f32), pltpu.VMEM((2, t, t), bf16), pltpu.VMEM((2, LANES, t), f32)],
        args=(proj, kp, vt, keep01, keepneg))


def _out_proj(o, conv_n, x, g_attn, w_out, tm):
    s = x.shape[0]

    def body(o_ref, c_ref, x_ref, g_ref, w_ref, h_ref, cat_ref):
        ones = _group_ones(LANES)
        cat_ref[:, :W_CONV] = c_ref[...]
        for j in range(W_ATTN // LANES):
            cols = slice(j * LANES, (j + 1) * LANES)
            cat_ref[:, W_CONV + j * LANES:W_CONV + (j + 1) * LANES] = _head_rms_fwd(
                o_ref[:, cols], g_ref[:, cols], ones).astype(bf16)
        h_ref[...] = x_ref[...] + _dot(cat_ref[...], w_ref[...])

    return pl.pallas_call(
        body, name="out_proj", grid=(s // tm,),
        in_specs=[pl.BlockSpec((tm, W_ATTN), lambda i: (i, 0)),
                  pl.BlockSpec((tm, W_CONV), lambda i: (i, 0)),
                  pl.BlockSpec((tm, D_MODEL), lambda i: (i, 0)),
                  pl.BlockSpec((1, W_ATTN), lambda i: (0, 0)),
                  pl.BlockSpec((D_MODEL, D_MODEL), lambda i: (0, 0))],
        out_specs=[pl.BlockSpec((tm, D_MODEL), lambda i: (i, 0)),
                   pl.BlockSpec((tm, D_MODEL), lambda i: (i, 0))],
        out_shape=[_sds((s, D_MODEL), f32), _sds((s, D_MODEL), bf16)],
        compiler_params=_arb(1),
    )(o, conv_n, x, g_attn, w_out)


def _mlp_fwd(h1, g_mlp, w_up, w_down, tm):
    s = h1.shape[0]
    fc = D_FF // N_CHIPS

    def body(h_ref, g_ref, wu_ref, wd_ref, h2_ref, u_ref, m_ref):
        j = pl.program_id(1)

        @pl.when(j == 0)
        def _():
            m, _ = _rms_fwd(h_ref[...], g_ref[...])
            m_ref[...] = m.astype(bf16)
            h2_ref[...] = h_ref[...]

        u = _dot(m_ref[...], wu_ref[...])
        u_ref[...] = u.astype(bf16)
        h2_ref[...] += _dot(jnp.square(jnp.maximum(u, 0.0)).astype(bf16), wd_ref[...])

    return pl.pallas_call(
        body, name="mlp_fwd", grid=(s // tm, N_CHIPS),
        in_specs=[pl.BlockSpec((tm, D_MODEL), lambda i, j: (i, 0)),
                  pl.BlockSpec((1, D_MODEL), lambda i, j: (0, 0)),
                  pl.BlockSpec((None, D_MODEL, fc), lambda i, j: (j, 0, 0)),
                  pl.BlockSpec((None, fc, D_MODEL), lambda i, j: (j, 0, 0))],
        out_specs=[pl.BlockSpec((tm, D_MODEL), lambda i, j: (i, 0)),
                   pl.BlockSpec((tm, fc), lambda i, j: (i, j)),
                   pl.BlockSpec((tm, D_MODEL), lambda i, j: (i, 0))],
        out_shape=[_sds((s, D_MODEL), f32), _sds((s, D_FF), bf16), _sds((s, D_MODEL), bf16)],
        compiler_params=_arb(2),
    )(h1, g_mlp, w_up, w_down)


def _tail(h2, p, target, g_ple, g_final, w_gate, w_proj, tm):
    s = h2.shape[0]
    pc = D_MODEL // N_CHIPS

    def body(h_ref, p_ref, t_ref, gp_ref, gf_ref, wg_ref, wp_ref,
             dh_ref, dhb_ref, n3_ref, dgl_ref, dpp_ref, pb_ref, ggp_ref, ggf_ref, loss_ref, pp_ref):
        i = pl.program_id(0)
        h2v = h_ref[...]
        n3, _ = _rms_fwd(h2v, gp_ref[...])
        n3b = n3.astype(bf16)
        n3_ref[...] = n3b
        gate = jax.nn.sigmoid(_dot(n3b, wg_ref[...]))
        pb = p_ref[...].astype(bf16)
        pb_ref[...] = pb
        for k in range(N_CHIPS):
            pp_ref[:, k * pc:(k + 1) * pc] = _dot(pb, wp_ref[k])
        pp = pp_ref[...]
        h3 = h2v + gate * pp
        yv, _ = _rms_fwd(h3, gf_ref[...])
        err = yv - t_ref[...]
        loss = 0.5 * jnp.sum(err * err) * (1.0 / D_MODEL)
        dh3, ggf = _rms_bwd(err * (1.0 / D_MODEL), h3, gf_ref[...])
        dpp_ref[...] = (dh3 * gate).astype(bf16)
        dgl = (dh3 * pp * gate * (1.0 - gate)).astype(bf16)
        dgl_ref[...] = dgl
        dn3 = _dot(dgl, wg_ref[...], NT)
        dh2n, ggp = _rms_bwd(dn3, h2v, gp_ref[...])
        dh2 = dh3 + dh2n
        dh_ref[...] = dh2
        dhb_ref[...] = dh2.astype(bf16)

        @pl.when(i == 0)
        def _():
            ggp_ref[...] = jnp.zeros_like(ggp_ref)
            ggf_ref[...] = jnp.zeros_like(ggf_ref)
            loss_ref[...] = jnp.zeros_like(loss_ref)

        ggp_ref[...] += ggp
        ggf_ref[...] += ggf
        loss_ref[...] += jnp.full(loss_ref.shape, loss, f32)

    tok = lambda w: pl.BlockSpec((tm, w), lambda i: (i, 0))
    vec = lambda w: pl.BlockSpec((1, w), lambda i: (0, 0))
    return pl.pallas_call(
        body, name="tail", grid=(s // tm,),
        in_specs=[tok(D_MODEL), tok(PLE_DIM), tok(D_MODEL), vec(D_MODEL), vec(D_MODEL),
                  pl.BlockSpec((D_MODEL, D_MODEL), lambda i: (0, 0)),
                  pl.BlockSpec((N_CHIPS, PLE_DIM, pc), lambda i: (0, 0, 0))],
        out_specs=[tok(D_MODEL), tok(D_MODEL), tok(D_MODEL), tok(D_MODEL), tok(D_MODEL), tok(PLE_DIM),
                   vec(D_MODEL), vec(D_MODEL), vec(LANES)],
        out_shape=[_sds((s, D_MODEL), f32), _sds((s, D_MODEL), bf16), _sds((s, D_MODEL), bf16),
                   _sds((s, D_MODEL), bf16), _sds((s, D_MODEL), bf16), _sds((s, PLE_DIM), bf16),
                   _sds((1, D_MODEL), f32), _sds((1, D_MODEL), f32), _sds((1, LANES), f32)],
        scratch_shapes=[pltpu.VMEM((tm, D_MODEL), f32)],
        compiler_params=_arb(1),
    )(h2, p, target, g_ple, g_final, w_gate, w_proj)


def _mlp_bwd(dh2, dh2b, h1, u, g_mlp, w_up, w_down, tm):
    s = h1.shape[0]
    fc = D_FF // N_CHIPS

    def body(dh_ref, dhb_ref, h_ref, u_ref, g_ref, wu_ref, wd_ref, dh1_ref, dh1b_ref, du_ref, gg_ref, dm):
        i, j = pl.program_id(0), pl.program_id(1)

        @pl.when(j == 0)
        def _():
            dm[...] = jnp.zeros_like(dm)

        dr = _dot(dhb_ref[...], wd_ref[...], NT)
        du = (dr * (2.0 * jnp.maximum(u_ref[...].astype(f32), 0.0))).astype(bf16)
        du_ref[...] = du
        dm[...] += _dot(du, wu_ref[...], NT)

        @pl.when((i == 0) & (j == 0))
        def _():
            gg_ref[...] = jnp.zeros_like(gg_ref)

        @pl.when(j == N_CHIPS - 1)
        def _():
            dh1n, gg = _rms_bwd(dm[...], h_ref[...], g_ref[...])
            dh1 = dh_ref[...] + dh1n
            dh1_ref[...] = dh1
            dh1b_ref[...] = dh1.astype(bf16)
            gg_ref[...] += gg

    tok = pl.BlockSpec((tm, D_MODEL), lambda i, j: (i, 0))
    ffb = pl.BlockSpec((tm, fc), lambda i, j: (i, j))
    vec = pl.BlockSpec((1, D_MODEL), lambda i, j: (0, 0))
    return pl.pallas_call(
        body, name="mlp_bwd", grid=(s // tm, N_CHIPS),
        in_specs=[tok, tok, tok, ffb, vec,
                  pl.BlockSpec((None, D_MODEL, fc), lambda i, j: (j, 0, 0)),
                  pl.BlockSpec((None, fc, D_MODEL), lambda i, j: (j, 0, 0))],
        out_specs=[tok, tok, ffb, vec],
        out_shape=[_sds((s, D_MODEL), f32), _sds((s, D_MODEL), bf16), _sds((s, D_FF), bf16),
                   _sds((1, D_MODEL), f32)],
        scratch_shapes=[pltpu.VMEM((tm, D_MODEL), f32)],
        compiler_params=_arb(2),
    )(dh2, dh2b, h1, u, g_mlp, w_up, w_down)


def _out_proj_bwd(dh1b, o, g_attn, w_out, tm, ex=None):
    s = o.shape[0]

    def body(dh_ref, o_ref, g_ref, w_ref, dc_ref, do_ref, gg_ref, dcat):
        i = pl.program_id(0)
        ones = _group_ones(LANES)
        dcat[...] = _dot(dh_ref[...], w_ref[...], NT)
        dc_ref[...] = dcat[:, :W_CONV]

        @pl.when(i == 0)
        def _():
            gg_ref[...] = jnp.zeros_like(gg_ref)

        for j in range(W_ATTN // LANES):
            cols = slice(j * LANES, (j + 1) * LANES)
            d, gg = _head_rms_bwd(dcat[:, W_CONV + j * LANES:W_CONV + (j + 1) * LANES],
                                  o_ref[:, cols], g_ref[:, cols], ones)
            do_ref[:, cols] = d
            gg_ref[:, cols] += gg

    return _call_carrying(
        ex, body, "out_proj_bwd", (s // tm,),
        in_specs=[pl.BlockSpec((tm, D_MODEL), lambda i: (i, 0)),
                  pl.BlockSpec((tm, W_ATTN), lambda i: (i, 0)),
                  pl.BlockSpec((1, W_ATTN), lambda i: (0, 0)),
                  pl.BlockSpec((D_MODEL, D_MODEL), lambda i: (0, 0))],
        out_specs=[pl.BlockSpec((tm, W_CONV), lambda i: (i, 0)),
                   pl.BlockSpec((tm, W_ATTN), lambda i: (i, 0)),
                   pl.BlockSpec((1, W_ATTN), lambda i: (0, 0))],
        out_shape=[_sds((s, W_CONV), f32), _sds((s, W_ATTN), f32), _sds((1, W_ATTN), f32)],
        scratch_shapes=[pltpu.VMEM((tm, D_MODEL), f32)],
        args=(dh1b, o, g_attn, w_out))


def _attn_bwd(proj, kp, vp, kt, do, tot, ex=None):
    s = proj.shape[0]
    t = ATTN_TILE
    nblk = s // t
    npair = W_ATTN // LANES
    qoff = 3 * W_CONV // LANES
    keep01, keepneg = _causal_tiles()

    def body(q_ref, k_ref, v_ref, kt_ref, do_ref, tot_ref, m01_ref, neg_ref, dq_ref, dk_ref, dv_ref,
             w_s, da_s, b_s, g_s, a_s, dz_s, dq_acc, dk_acc, dv_acc):
        qb = pl.program_id(1)
        first = _head_pair_masks(t)
        q = q_ref[...] * (HEAD_DIM ** -0.5)
        qh = (jnp.where(first, q, 0.0).astype(bf16), jnp.where(first, 0.0, q).astype(bf16))
        dov = do_ref[...]
        doh = (jnp.where(first, dov, 0.0).astype(bf16), jnp.where(first, 0.0, dov).astype(bf16))
        total = (tot_ref[0:1, :], tot_ref[1:2, :])

        @pl.when(qb == 0)
        def _():
            dk_acc[...] = jnp.zeros_like(dk_acc)
            dv_acc[...] = jnp.zeros_like(dv_acc)

        dq_acc[...] = jnp.zeros_like(dq_acc)
        a_s[1] = jnp.zeros((t, t), bf16)
        dz_s[1] = jnp.zeros((t, t), bf16)

        def scores(kb, h):
            w_s[h] = _dot(k_ref[kb], qh[h], NT)
            da_s[h] = _dot(v_ref[kb], doh[h], NT)

        def spread(kb, h):
            dq_acc[h] += _dot(kt_ref[kb], dz_s[h])
            dk_acc[kb] += _dot(dz_s[h], qh[h])
            dv_acc[kb] += _dot(a_s[h], doh[h])

        def grads(h, diagonal, lk_before, g_before):
            run = jnp.zeros((SUBLANES, t), f32)
            for a in range(KEY_RUN):
                rows = slice(SUBLANES * a, SUBLANES * (a + 1))
                z2 = w_s[h, rows, :] * LOG2_E
                lk = _log2_keep(z2)
                if diagonal:
                    lk = lk * m01_ref[rows, :]
                log_beta = jnp.minimum(z2 + lk, 0.0)
                run = run + lk
                b_s[h, rows, :] = jnp.exp2(log_beta)
                w_s[h, rows, :] = log_beta - run
            off = total[h] - lk_before - _sublane_scan(run, reverse=False)
            lk_sum = jnp.sum(run, axis=0, keepdims=True)
            run = jnp.zeros((SUBLANES, t), f32)
            for a in range(KEY_RUN // 2):
                parts = []
                for r in (slice(2 * SUBLANES * a, 2 * SUBLANES * a + SUBLANES),
                          slice(2 * SUBLANES * a + SUBLANES, 2 * SUBLANES * (a + 1))):
                    w = w_s[h, r, :] + off
                    if diagonal:
                        w = w + neg_ref[r, :]
                    av = jnp.exp2(w)
                    g = av * da_s[h, r, :]
                    run = run + g
                    da_s[h, r, :] = g
                    g_s[h, r, :] = run
                    parts.append(av)
                a_s[h, 2 * SUBLANES * a:2 * SUBLANES * (a + 1), :] = jnp.concatenate(parts, axis=0).astype(bf16)
            goff = g_before + _sublane_scan(run, reverse=False)
            goff2 = jnp.concatenate([goff, goff], axis=0)
            for a in range(KEY_RUN // 2):
                rows = slice(2 * SUBLANES * a, 2 * SUBLANES * (a + 1))
                dz = da_s[h, rows, :] - b_s[h, rows, :] * (g_s[h, rows, :] + goff2)
                if diagonal:
                    dz = dz * m01_ref[rows, :]
                dz_s[h, rows, :] = dz.astype(bf16)
            return lk_before + lk_sum, g_before + jnp.sum(run, axis=0, keepdims=True)

        def block(kb, before, after, diagonal, carry):
            scores(kb, 1)
            spread(before, 1)
            c0 = grads(0, diagonal, carry[0], carry[1])
            if after is not None:
                scores(after, 0)
            spread(kb, 0)
            c1 = grads(1, diagonal, carry[2], carry[3])
            return c0 + c1

        zero = jnp.zeros((1, t), f32)
        scores(0, 0)
        def two_blocks(i, c):
            kb = 2 * i
            c = block(kb, jnp.maximum(kb - 1, 0), kb + 1, False, c)
            return block(kb + 1, kb, kb + 2, False, c)

        carry = lax.fori_loop(0, qb // 2, two_blocks, (zero, zero, zero, zero))
        carry = lax.fori_loop(qb - qb % 2, qb, lambda kb, c: block(kb, jnp.maximum(kb - 1, 0), kb + 1, False, c), carry)
        block(qb, jnp.maximum(qb - 1, 0), None, True, carry)
        spread(qb, 1)
        top = lax.broadcasted_iota(jnp.int32, (LANES, t), 0) < HEAD_DIM
        dq_ref[...] = (jnp.where(top, dq_acc[0], dq_acc[1]).T * (HEAD_DIM ** -0.5)).astype(bf16)

        @pl.when(qb == nblk - 1)
        def _():
            for kb in range(nblk):
                for b in range(SUBLANES):
                    keys_b = slice(KEY_RUN * b, KEY_RUN * (b + 1))
                    dk_ref[kb, keys_b, :] = dk_acc[kb, pl.ds(b, KEY_RUN, stride=SUBLANES), :].astype(bf16)
                    dv_ref[kb, keys_b, :] = dv_acc[kb, pl.ds(b, KEY_RUN, stride=SUBLANES), :].astype(bf16)

    keys = pl.BlockSpec((nblk, t, LANES), lambda p, i: (0, 0, p))
    tile = pl.BlockSpec((t, t), lambda p, i: (0, 0))
    return _call_carrying(
        ex, body, "attn_bwd", (npair, nblk),
        in_specs=[pl.BlockSpec((t, LANES), lambda p, i: (i, qoff + p)),
                  keys, keys,
                  pl.BlockSpec((nblk, LANES, t), lambda p, i: (0, p, 0)),
                  pl.BlockSpec((t, LANES), lambda p, i: (i, p)),
                  pl.BlockSpec((None, SUBLANES, t), lambda p, i: (p, 0, i)),
                  tile, tile],
        out_specs=[pl.BlockSpec((t, LANES), lambda p, i: (i, p)), keys, keys],
        out_shape=[_sds((s, W_ATTN), bf16), _sds((nblk, t, W_ATTN), bf16), _sds((nblk, t, W_ATTN), bf16)],
        scratch_shapes=[pltpu.VMEM((2, t, t), f32), pltpu.VMEM((2, t, t), f32), pltpu.VMEM((2, t, t), f32),
                        pltpu.VMEM((2, t, t), f32), pltpu.VMEM((2, t, t), bf16), pltpu.VMEM((2, t, t), bf16),
                        pltpu.VMEM((2, LANES, t), f32), pltpu.VMEM((nblk, t, LANES), f32),
                        pltpu.VMEM((nblk, t, LANES), f32)],
        args=(proj, kp, vp, kt, do, tot, keep01, keepneg))


def _conv_bwd(proj, dcn, conv_w, g_conv):
    s = proj.shape[0]
    nblk = W_CONV // LANES
    rc = min(CONV_CHUNK, s)
    nchunk = s // rc

    def body(cb_ref, cc_ref, cu_ref, d_ref, w_ref, g_ref, dcb_ref, dcc_ref, dcu_ref, gw_ref, gg_ref, dy_buf):
        ones = _group_ones(LANES)
        w0, w1, w2 = w_ref[0:1, :], w_ref[1:2, :], w_ref[2:3, :]
        g = g_ref[...]

        def first_pass(i, carry):
            gw0, gw1, gw2, gg = carry
            r0 = pl.multiple_of(i * rc, rc)
            rows = pl.ds(r0, rc)
            prev = pl.ds(pl.multiple_of(jnp.maximum(r0 - 8, 0), 8), 8)
            v = cc_ref[rows, :] * cu_ref[rows, :]
            vp = jnp.where(i > 0, cc_ref[prev, :] * cu_ref[prev, :], 0.0)
            v1, v2 = _shifted(vp, v, 1), _shifted(vp, v, 2)
            y = w2 * v + w1 * v1 + w0 * v2
            cb = cb_ref[rows, :]
            dcy, ggi = _head_rms_bwd(d_ref[rows, :], cb * y, g, ones)
            dcb_ref[rows, :] = (dcy * y).astype(bf16)
            dy = dcy * cb
            dy_buf[rows, :] = dy
            return (gw0 + jnp.sum(dy * v2, axis=0, keepdims=True), gw1 + jnp.sum(dy * v1, axis=0, keepdims=True),
                    gw2 + jnp.sum(dy * v, axis=0, keepdims=True), gg + ggi)

        zero = jnp.zeros((1, LANES), f32)
        gw0, gw1, gw2, gg = lax.fori_loop(0, nchunk, first_pass, (zero, zero, zero, zero))
        gw_ref[...] = jnp.zeros_like(gw_ref)
        gw_ref[0:1, :] = gw0
        gw_ref[1:2, :] = gw1
        gw_ref[2:3, :] = gw2
        gg_ref[...] = gg

        def second_pass(i, carry):
            r0 = pl.multiple_of(i * rc, rc)
            rows = pl.ds(r0, rc)
            nxt = pl.ds(pl.multiple_of(jnp.minimum(r0 + rc, s - 8), 8), 8)
            dy = dy_buf[rows, :]
            dyn = jnp.where(i < nchunk - 1, dy_buf[nxt, :], 0.0)
            ext = jnp.concatenate([dy, dyn], axis=0)
            up1 = pltpu.roll(ext, rc + 8 - 1, axis=0)[:rc]
            up2 = pltpu.roll(ext, rc + 8 - 2, axis=0)[:rc]
            dv = w2 * dy + w1 * up1 + w0 * up2
            dcc_ref[rows, :] = (dv * cu_ref[rows, :]).astype(bf16)
            dcu_ref[rows, :] = (dv * cc_ref[rows, :]).astype(bf16)
            return carry

        lax.fori_loop(0, nchunk, second_pass, 0)

    def col(off):
        return pl.BlockSpec((s, LANES), lambda j: (0, off + j))

    return pl.pallas_call(
        body, name="conv_bwd", grid=(nblk,),
        in_specs=[col(0), col(nblk), col(2 * nblk), col(0),
                  pl.BlockSpec((None, CONV_W_ROWS, LANES), lambda j: (j, 0, 0)),
                  pl.BlockSpec((1, LANES), lambda j: (0, j))],
        out_specs=[col(0), col(0), col(0),
                   pl.BlockSpec((None, CONV_W_ROWS, LANES), lambda j: (j, 0, 0)),
                   pl.BlockSpec((1, LANES), lambda j: (0, j))],
        out_shape=[_sds((s, W_CONV), bf16)] * 3 + [_sds((nblk, CONV_W_ROWS, LANES), f32), _sds((1, W_CONV), f32)],
        scratch_shapes=[pltpu.VMEM((s, LANES), f32)],
        compiler_params=_arb(1),
    )(proj, proj, proj, dcn, conv_w, g_conv)


def _in_proj_bwd(dproj, dh1, x, g_mix, w_in, tm, ex=None):
    s = x.shape[0]
    ncol = IN_COLS // N_CHIPS

    def body(dp_ref, dh_ref, x_ref, g_ref, w_ref, dx_ref, gg_ref):
        i = pl.program_id(0)
        da = _dot(dp_ref[:, 0:ncol], w_ref[0], NT)
        for k in range(1, N_CHIPS):
            da += _dot(dp_ref[:, k * ncol:(k + 1) * ncol], w_ref[k], NT)
        dxn, gg = _rms_bwd(da, x_ref[...], g_ref[...])
        dx_ref[...] = dh_ref[...] + dxn

        @pl.when(i == 0)
        def _():
            gg_ref[...] = jnp.zeros_like(gg_ref)

        gg_ref[...] += gg

    return _call_carrying(
        ex, body, "in_proj_bwd", (s // tm,),
        in_specs=[pl.BlockSpec((tm, IN_COLS), lambda i: (i, 0)),
                  pl.BlockSpec((tm, D_MODEL), lambda i: (i, 0)),
                  pl.BlockSpec((tm, D_MODEL), lambda i: (i, 0)),
                  pl.BlockSpec((1, D_MODEL), lambda i: (0, 0)),
                  pl.BlockSpec((N_CHIPS, D_MODEL, ncol), lambda i: (0, 0, 0))],
        out_specs=[pl.BlockSpec((tm, D_MODEL), lambda i: (i, 0)),
                   pl.BlockSpec((1, D_MODEL), lambda i: (0, 0))],
        out_shape=[_sds((s, D_MODEL), f32), _sds((1, D_MODEL), f32)],
        scratch_shapes=[],
        args=(dproj, dh1, x, g_mix, w_in))


def _weight_grad(a, b, bm, bn, ts, name, relu_sq=False):
    s, m = a.shape
    n = b.shape[1]
    nn = n // bn

    def body(a_ref, b_ref, o_ref):
        @pl.when(pl.program_id(2) == 0)
        def _():
            o_ref[...] = jnp.zeros_like(o_ref)

        av = a_ref[...]
        if relu_sq:
            av = jnp.square(jnp.maximum(av.astype(f32), 0.0)).astype(bf16)
        o_ref[...] += _dot(av, b_ref[...], TN)

    return pl.pallas_call(
        body, name=name, grid=(m // bm, nn, s // ts),
        in_specs=[pl.BlockSpec((ts, bm), lambda i, j, k: (k, i)),
                  pl.BlockSpec((ts, bn), lambda i, j, k: (k, j))],
        out_specs=pl.BlockSpec((None, bm, bn), lambda i, j, k: (i * nn + j, 0, 0)),
        out_shape=_sds(((m // bm) * nn, bm, bn), f32),
        compiler_params=_arb(3),
    )(a, b)


def kernel(x, p, g_mix, w_in, conv_w, g_conv_out, g_attn_out, w_out, g_mlp, w_up, w_down, g_ple, w_ple_gate, w_ple_proj, g_final, loss_target, m_g_mix, m_w_in, m_conv_w, m_g_conv_out, m_g_attn_out, m_w_out, m_g_mlp, m_w_up, m_w_down, m_g_ple, m_w_ple_gate, m_w_ple_proj, m_g_final, v_g_mix, v_w_in, v_conv_w, v_g_conv_out, v_g_attn_out, v_w_out, v_g_mlp, v_w_up, v_w_down, v_g_ple, v_w_ple_gate, v_w_ple_proj, v_g_final):
    s = x.shape[1]
    tm = min(TOKEN_TILE, s)
    tg = min(GRAD_TOKEN_TILE, s)
    xs = x.reshape(s, D_MODEL)
    ps = p.reshape(s, PLE_DIM)
    target = loss_target.reshape(s, D_MODEL)
    core = lax.axis_index("c").astype(jnp.int32).reshape(1)
    chip = 2 * lax.axis_index("x") + lax.axis_index("y")

    big = {"w_in": w_in[0], "w_out": w_out[0], "w_up": w_up[0], "w_down": w_down[0],
           "w_ple_gate": w_ple_gate[0], "w_ple_proj": w_ple_proj[0]}
    names = list(big)
    conv_shard = jnp.pad(conv_w[0], ((0, CONV_W_ROWS - conv_w.shape[1]), (0, 0)))
    later_names = names[1:]
    w_in_f, conv_f = _run_exchange(_gather_exchange([big["w_in"].astype(bf16), conv_shard]), "gather_w_in")

    proj, a_b, kp, vp, kt, vt = _in_proj(xs, g_mix, w_in_f, tm)
    conv_n = _conv_fwd(proj, conv_f, g_conv_out)
    (o, tot), gathered = _attn_fwd(proj, kp, vt, _gather_exchange([big[k].astype(bf16) for k in later_names]))
    w_out_f, w_up_f, w_down_f, w_gate_f, w_proj_f = gathered
    w_out_f = w_out_f.reshape(D_MODEL, D_MODEL)
    w_gate_f = w_gate_f.reshape(D_MODEL, D_MODEL)
    h1, cat_b = _out_proj(o, conv_n, xs, g_attn_out, w_out_f, tm)
    h2, u_b, m_b = _mlp_fwd(h1, g_mlp, w_up_f, w_down_f, min(MLP_TOKEN_TILE, s))

    dh2, dh2_b, n3_b, dgl_b, dpp_b, p_b, gg_ple, gg_final, loss_row = _tail(
        h2, ps, target, g_ple, g_final.reshape(1, D_MODEL), w_gate_f, w_proj_f, tm)
    dh1, dh1_b, du_b, gg_mlp = _mlp_bwd(dh2, dh2_b, h1, u_b, g_mlp, w_up_f, w_down_f, tm)
    part = {
        "w_out": _weight_grad(cat_b, dh1_b, D_MODEL, D_MODEL, tg, "grad_w_out").reshape(N_CHIPS, D_MODEL // N_CHIPS, D_MODEL),
        "w_up": _weight_grad(m_b, du_b, D_MODEL, D_FF // N_CHIPS, tg, "grad_w_up"),
        "w_down": _weight_grad(u_b, dh2_b, D_FF // N_CHIPS, D_MODEL, tg, "grad_w_down", relu_sq=True),
        "w_ple_gate": _weight_grad(n3_b, dgl_b, D_MODEL, D_MODEL, tg, "grad_w_ple_gate").reshape(N_CHIPS, D_MODEL // N_CHIPS, D_MODEL),
        "w_ple_proj": _weight_grad(p_b, dpp_b, PLE_DIM, D_MODEL // N_CHIPS, tg, "grad_w_ple_proj"),
    }
    (dcn, do, gg_attn), from_sibling = _out_proj_bwd(
        dh1_b, o, g_attn_out, w_out_f, tm, _pair_exchange([part[k] for k in later_names]))
    pair = [_pair_sum(part[k], r, core, "pair_sum_" + k) for k, r in zip(later_names, from_sibling)]
    dcb, dcc, dcu, g_conv_w, gg_conv = _conv_bwd(proj, dcn, conv_f, g_conv_out)
    (dq, dk, dv), from_chips = _attn_bwd(proj, kp, vp, kt, do, tot, _chip_exchange(pair))
    dproj = jnp.concatenate([dcb, dcc, dcu, dq, dk.reshape(s, W_ATTN), dv.reshape(s, W_ATTN)], axis=1)

    part["w_in"] = _weight_grad(a_b, dproj, D_MODEL, IN_COLS // N_CHIPS, tg, "grad_w_in")
    in_sibling = _run_exchange(_pair_exchange([part["w_in"]]), "grad_pair_exchange_w_in")
    in_pair = _pair_sum(part["w_in"], in_sibling[0], core, "pair_sum_w_in")
    (grad_x, gg_mix), in_chips = _in_proj_bwd(dproj, dh1, xs, g_mix, w_in_f, tm, _chip_exchange([in_pair]))

    place = jnp.stack([lax.axis_index("c"), chip]).astype(jnp.int32)
    half = [_chip_sum(mine, landed, place, "chip_sum_" + k)
            for k, mine, landed in zip(names, [in_pair] + pair, list(in_chips) + list(from_chips))]
    both = _sibling_exchange(half)
    grad = {k: b.reshape(big[k].shape) for k, b in zip(names, both)}

    gcw = g_conv_w[:, :3, :].transpose(1, 0, 2).reshape(3, W_CONV)
    row = lambda *parts: jnp.concatenate(parts, axis=1)
    packed = jnp.concatenate([
        gg_mix, gg_mlp, gg_ple, gg_final, row(gg_conv, gg_attn), row(gcw[0:1], gcw[1:2]),
        row(gcw[2:3], loss_row, jnp.zeros((1, W_CONV - LANES), f32)), jnp.zeros((1, D_MODEL), f32)], axis=0)
    summed = _allreduce_small(packed)
    loss = summed[6, W_CONV]
    gcw_full = jnp.stack([summed[5, :W_CONV], summed[5, W_CONV:], summed[6, :W_CONV]])
    grad["conv_w"] = lax.dynamic_slice(gcw_full, (0, chip * LANES), (3, LANES))
    vec_names = ["g_mix", "g_mlp", "g_ple", "g_final"]
    vec_w = {"g_mix": g_mix, "g_mlp": g_mlp, "g_ple": g_ple, "g_final": g_final.reshape(1, D_MODEL)}
    vec_m = {"g_mix": m_g_mix, "g_mlp": m_g_mlp, "g_ple": m_g_ple, "g_final": m_g_final.reshape(1, D_MODEL)}
    vec_v = {"g_mix": v_g_mix, "g_mlp": v_g_mlp, "g_ple": v_g_ple, "g_final": v_g_final.reshape(1, D_MODEL)}

    def pack_vec(d, conv, attn):
        return jnp.concatenate([d[k] for k in vec_names] + [row(conv, attn)], axis=0)

    vec_g = summed[0:5]
    vec_d, vec_nm, vec_nv = _adamw(vec_g, pack_vec(vec_w, g_conv_out, g_attn_out),
                                   pack_vec(vec_m, m_g_conv_out, m_g_attn_out),
                                   pack_vec(vec_v, v_g_conv_out, v_g_attn_out), "adamw_vectors")

    given_w = dict(big, conv_w=conv_w[0])
    given_m = {"w_in": m_w_in[0], "w_out": m_w_out[0], "w_up": m_w_up[0], "w_down": m_w_down[0],
               "w_ple_gate": m_w_ple_gate[0], "w_ple_proj": m_w_ple_proj[0], "conv_w": m_conv_w[0]}
    given_v = {"w_in": v_w_in[0], "w_out": v_w_out[0], "w_up": v_w_up[0], "w_down": v_w_down[0],
               "w_ple_gate": v_w_ple_gate[0], "w_ple_proj": v_w_ple_proj[0], "conv_w": v_conv_w[0]}
    delta, new_m, new_v = {}, {}, {}
    for k in names + ["conv_w"]:
        delta[k], new_m[k], new_v[k] = _adamw(grad[k], given_w[k], given_m[k], given_v[k], "adamw_" + k)

    def unpack(vals, kind):
        out = {k: vals[i:i + 1] for i, k in enumerate(vec_names)}
        out["g_final"] = out["g_final"].reshape(D_MODEL)
        out["g_conv_out"] = vals[4:5, :W_CONV]
        out["g_attn_out"] = vals[4:5, W_CONV:]
        out.update({k: v[None] for k, v in kind.items()})
        return out

    order = ["g_mix", "w_in", "conv_w", "g_conv_out", "g_attn_out", "w_out", "g_mlp", "w_up", "w_down",
             "g_ple", "w_ple_gate", "w_ple_proj", "g_final"]
    groups = [unpack(vec_g, grad), unpack(vec_d, delta), unpack(vec_nm, new_m), unpack(vec_nv, new_v)]
    return (loss, grad_x[None]) + tuple(g[k] for g in groups for k in order)
```

```python
import functools

import jax
import jax.numpy as jnp
from jax import lax
from jax.experimental import pallas as pl
from jax.experimental.pallas import tpu as pltpu

f32 = jnp.float32
bf16 = jnp.bfloat16

D_MODEL = 1024
HEAD_DIM = 64
W_CONV = 512
W_ATTN = 512
D_FF = 4096
PLE_DIM = 256
IN_COLS = 3 * W_CONV + 3 * W_ATTN
N_CHIPS = 4
EPS = 1e-6
ADAM_LR = 0.001
ADAM_B1 = 0.9
ADAM_B2 = 0.999
ADAM_EPS = 1e-08
ADAM_WD = 0.01
ADAM_STEP = 10

LANES = 128
TOKEN_TILE = 512
MLP_TOKEN_TILE = 1024
GRAD_TOKEN_TILE = 2048
ATTN_TILE = 256
CONV_CHUNK = 512
CONV_W_ROWS = 16

MESH = pl.DeviceIdType.MESH
ANY = pl.BlockSpec(memory_space=pl.ANY)
NT = (((1,), (1,)), ((), ()))
TN = (((0,), (0,)), ((), ()))


def _arb(n):
    return pltpu.CompilerParams(dimension_semantics=("arbitrary",) * n)


def _sds(shape, dtype):
    return jax.ShapeDtypeStruct(shape, dtype)


def _dot(a, b, dims=None):
    if dims is None:
        return jnp.dot(a, b, preferred_element_type=f32)
    return lax.dot_general(a, b, dims, preferred_element_type=f32)


def _split_dot(x, ones):
    hi = x.astype(bf16)
    lo = (x - hi.astype(f32)).astype(bf16)
    return _dot(hi, ones) + _dot(lo, ones)


def _rms_fwd(h, g):
    rstd = lax.rsqrt(jnp.mean(h * h, axis=-1, keepdims=True) + EPS)
    return h * rstd * g, rstd


def _rms_bwd(dy, h, g):
    rstd = lax.rsqrt(jnp.mean(h * h, axis=-1, keepdims=True) + EPS)
    hn = h * rstd
    dyg = dy * g
    dh = rstd * (dyg - hn * jnp.mean(dyg * hn, axis=-1, keepdims=True))
    return dh, jnp.sum(dy * hn, axis=0, keepdims=True)


def _group_ones(n):
    r = lax.broadcasted_iota(jnp.int32, (n, n), 0) // HEAD_DIM
    c = lax.broadcasted_iota(jnp.int32, (n, n), 1) // HEAD_DIM
    return (r == c).astype(bf16)


def _head_rms_fwd(y, g, ones):
    rstd = lax.rsqrt(_split_dot(y * y, ones) * (1.0 / HEAD_DIM) + EPS)
    return y * rstd * g


def _head_rms_bwd(dy, y, g, ones):
    rstd = lax.rsqrt(_split_dot(y * y, ones) * (1.0 / HEAD_DIM) + EPS)
    yn = y * rstd
    dyg = dy * g
    dyy = rstd * (dyg - yn * (_split_dot(dyg * yn, ones) * (1.0 / HEAD_DIM)))
    return dyy, jnp.sum(dy * yn, axis=0, keepdims=True)


def _place():
    return lax.axis_index("x"), lax.axis_index("y"), lax.axis_index("c")


def _other_chips(x, y):
    return [(1 - x, y), (x, 1 - y), (1 - x, 1 - y)]


class _Exchange:
    def __init__(self, arrays, out_shapes, sems, start, finish):
        self.arrays, self.out_shapes, self.sems, self.start, self.finish = arrays, out_shapes, sems, start, finish


def _gather_exchange(shards):
    n = len(shards)
    halves = [s.shape[0] // 2 for s in shards]

    def plan(ins, outs, sems):
        send_sems, recv_sems, own_sems = sems
        x, y, c = _place()
        me = 2 * x + y
        chips = _other_chips(x, y)

        def half(ref, i, which):
            return ref.at[pl.ds(which * halves[i], halves[i]), :]

        def over_ici(i, j, src, slot, to):
            return pltpu.make_async_remote_copy(
                src_ref=src, dst_ref=half(outs[i].at[slot], i, c),
                send_sem=send_sems.at[3 * i + j], recv_sem=recv_sems.at[3 * i + j],
                device_id=to, device_id_type=MESH)

        def to_sibling(i, j, slot, which):
            blk = half(outs[i].at[slot], i, which)
            return pltpu.make_async_remote_copy(
                src_ref=blk, dst_ref=blk,
                send_sem=send_sems.at[3 * n + 3 * i + j], recv_sem=recv_sems.at[3 * n + 3 * i + j],
                device_id=(x, y, 1 - c), device_id_type=MESH)

        own = [pltpu.make_async_remote_copy(
            src_ref=ins[i], dst_ref=outs[i].at[me], send_sem=own_sems.at[i], recv_sem=own_sems.at[n + i],
            device_id=(x, y, 1 - c), device_id_type=MESH) for i in range(n)]
        pairs = [(i, j, px, py) for i in range(n) for j, (px, py) in enumerate(chips)]
        sends = [over_ici(i, j, half(ins[i], i, c), me, (px, py, c)) for i, j, px, py in pairs]
        lands = [over_ici(i, j, half(outs[i].at[2 * px + py], i, c), 2 * px + py, (px, py, c)) for i, j, px, py in pairs]
        passes = [to_sibling(i, j, 2 * px + py, c) for i, j, px, py in pairs]
        from_sibling = [to_sibling(i, j, 2 * px + py, 1 - c) for i, j, px, py in pairs]
        return own, sends, lands, passes, from_sibling

    def start(ins, outs, sems):
        own, sends, _, _, _ = plan(ins, outs, sems)
        for cp in own + sends:
            cp.start()

    def finish(ins, outs, sems):
        own, sends, lands, passes, from_sibling = plan(ins, outs, sems)
        for land, on in zip(lands, passes):
            land.wait_recv()
            on.start()
        for cp in from_sibling:
            cp.wait_recv()
        for cp in sends + passes:
            cp.wait_send()
        for cp in own:
            cp.wait()

    return _Exchange(
        shards, [_sds((N_CHIPS,) + s.shape, s.dtype) for s in shards],
        [pltpu.SemaphoreType.DMA((6 * n,)), pltpu.SemaphoreType.DMA((6 * n,)), pltpu.SemaphoreType.DMA((2 * n,))],
        start, finish)


def _call_carrying(ex, body, name, grid, in_specs, out_specs, out_shape, scratch_shapes, args):
    n_in, n_out, n_scr = len(in_specs), len(out_specs), len(scratch_shapes)
    k = 0 if ex is None else len(ex.arrays)

    def wrapped(*refs):
        ins, xin = refs[:n_in], refs[n_in:n_in + k]
        outs, xout = refs[n_in + k:n_in + k + n_out], refs[n_in + k + n_out:n_in + 2 * k + n_out]
        scr, sems = refs[n_in + 2 * k + n_out:n_in + 2 * k + n_out + n_scr], refs[n_in + 2 * k + n_out + n_scr:]
        ids = [pl.program_id(d) for d in range(len(grid))]
        if ex is not None:
            @pl.when(functools.reduce(lambda a, b: a & b, [i == 0 for i in ids]))
            def _():
                ex.start(xin, xout, sems)

        body(*ins, *outs, *scr)
        if ex is not None:
            @pl.when(functools.reduce(lambda a, b: a & b, [i == g - 1 for i, g in zip(ids, grid)]))
            def _():
                ex.finish(xin, xout, sems)

    res = pl.pallas_call(
        wrapped, name=name, grid=grid,
        in_specs=list(in_specs) + [ANY] * k, out_specs=list(out_specs) + [ANY] * k,
        out_shape=list(out_shape) + ([] if ex is None else list(ex.out_shapes)),
        scratch_shapes=list(scratch_shapes) + ([] if ex is None else list(ex.sems)),
        compiler_params=_arb(len(grid)),
    )(*args, *([] if ex is None else ex.arrays))
    return res[:n_out], res[n_out:]


def _run_exchange(ex, name):
    n = len(ex.arrays)

    def body(*refs):
        ins, outs, sems = refs[:n], refs[n:2 * n], refs[2 * n:]
        ex.start(ins, outs, sems)
        ex.finish(ins, outs, sems)

    return pl.pallas_call(
        body, name=name, out_shape=ex.out_shapes, in_specs=[ANY] * n, out_specs=[ANY] * n,
        scratch_shapes=ex.sems,
    )(*ex.arrays)


def _pair_exchange(grads):
    n = len(grads)

    def plan(ins, outs, sems):
        send_sems, recv_sems = sems
        x, y, c = _place()
        return [pltpu.make_async_remote_copy(
            src_ref=ins[i].at[:, 1 - c], dst_ref=outs[i],
            send_sem=send_sems.at[i], recv_sem=recv_sems.at[i],
            device_id=(x, y, 1 - c), device_id_type=MESH) for i in range(n)]

    def start(ins, outs, sems):
        for cp in plan(ins, outs, sems):
            cp.start()

    def finish(ins, outs, sems):
        for cp in plan(ins, outs, sems):
            cp.wait()

    views = [g.reshape(N_CHIPS, 2, g.shape[1] // 2, g.shape[2]) for g in grads]
    return _Exchange(
        views, [_sds((N_CHIPS, v.shape[2], v.shape[3]), f32) for v in views],
        [pltpu.SemaphoreType.DMA((n,)), pltpu.SemaphoreType.DMA((n,))], start, finish)


def _chip_exchange(parts):
    n = len(parts)

    def plan(ins, outs, sems):
        send_sems, recv_sems = sems
        x, y, c = _place()
        me = 2 * x + y
        pairs = [(i, j, px, py) for i in range(n) for j, (px, py) in enumerate(_other_chips(x, y))]

        def copy(i, j, src, slot, px, py):
            return pltpu.make_async_remote_copy(
                src_ref=src, dst_ref=outs[i].at[slot],
                send_sem=send_sems.at[3 * i + j], recv_sem=recv_sems.at[3 * i + j],
                device_id=(px, py, c), device_id_type=MESH)

        sends = [copy(i, j, ins[i].at[2 * px + py], me, px, py) for i, j, px, py in pairs]
        lands = [copy(i, j, outs[i].at[2 * px + py], 2 * px + py, px, py) for i, j, px, py in pairs]
        return sends, lands

    def start(ins, outs, sems):
        sends, _ = plan(ins, outs, sems)
        for cp in sends:
            cp.start()

    def finish(ins, outs, sems):
        sends, lands = plan(ins, outs, sems)
        for cp in lands:
            cp.wait_recv()
        for cp in sends:
            cp.wait_send()

    return _Exchange(
        parts, [_sds(p.shape, p.dtype) for p in parts],
        [pltpu.SemaphoreType.DMA((3 * n,)), pltpu.SemaphoreType.DMA((3 * n,))],
        start, finish)


def _sibling_exchange(both):
    n = len(both)

    def body(*refs):
        outs = refs[n:2 * n]
        send_sems, recv_sems = refs[2 * n:]
        x, y, c = _place()
        sent = []
        for i in range(n):
            cp = pltpu.make_async_remote_copy(
                src_ref=outs[i].at[c], dst_ref=outs[i].at[c],
                send_sem=send_sems.at[i], recv_sem=recv_sems.at[i],
                device_id=(x, y, 1 - c), device_id_type=MESH)
            cp.start()
            sent.append(cp)
        for cp in sent:
            cp.wait()

    return pl.pallas_call(
        body, name="grad_sibling_exchange",
        out_shape=[_sds(b.shape, f32) for b in both],
        in_specs=[ANY] * n, out_specs=[ANY] * n,
        input_output_aliases={i: i for i in range(n)},
        scratch_shapes=[pltpu.SemaphoreType.DMA((n,)), pltpu.SemaphoreType.DMA((n,))],
    )(*both)


def _row_tile(rows, cols):
    t = rows
    while t * cols * 4 > (2 << 20) and t % 16 == 0:
        t //= 2
    return t


def _pair_sum(grad, recv, core, name):
    _, r, c = grad.shape
    hr = r // 2
    tr = _row_tile(hr, c)
    view = grad.reshape(N_CHIPS, 2, hr, c)

    def body(core_ref, mine_ref, recv_ref, out_ref):
        out_ref[...] = (mine_ref[...] + recv_ref[...]).astype(bf16)

    return pl.pallas_call(
        body, name=name,
        grid_spec=pltpu.PrefetchScalarGridSpec(
            num_scalar_prefetch=1, grid=(N_CHIPS, hr // tr),
            in_specs=[pl.BlockSpec((None, None, tr, c), lambda k, t, core_ref: (k, core_ref[0], t, 0)),
                      pl.BlockSpec((None, tr, c), lambda k, t, core_ref: (k, t, 0))],
            out_specs=pl.BlockSpec((None, tr, c), lambda k, t, core_ref: (k, t, 0))),
        out_shape=_sds((N_CHIPS, hr, c), bf16),
        compiler_params=_arb(2),
    )(core, view, recv)


def _chip_sum(mine, landed, place, name):
    _, hr, c = mine.shape
    tr = _row_tile(hr, c)

    def body(place_ref, a_ref, b_ref, c_ref, d_ref, out_ref):
        out_ref[...] = ((a_ref[...].astype(f32) + b_ref[...].astype(f32)) + c_ref[...].astype(f32)) + d_ref[...].astype(f32)

    def slot(k):
        return pl.BlockSpec((None, tr, c), lambda t, place_ref: ((place_ref[1] + k) % N_CHIPS, t, 0))

    return pl.pallas_call(
        body, name=name,
        grid_spec=pltpu.PrefetchScalarGridSpec(
            num_scalar_prefetch=1, grid=(hr // tr,),
            in_specs=[slot(0), slot(1), slot(2), slot(3)],
            out_specs=pl.BlockSpec((None, tr, c), lambda t, place_ref: (place_ref[0], t, 0))),
        out_shape=_sds((2, hr, c), f32), compiler_params=_arb(1),
    )(place, mine, landed, landed, landed)


def _adamw(g, w, m, v, name):
    r, c = g.shape
    tr = _row_tile(r, c)

    def body(g_ref, w_ref, m_ref, v_ref, d_ref, nm_ref, nv_ref):
        gv = g_ref[...]
        mv = ADAM_B1 * m_ref[...] + (1.0 - ADAM_B1) * gv
        vv = ADAM_B2 * v_ref[...] + (1.0 - ADAM_B2) * jnp.square(gv)
        m_hat = mv / (1.0 - ADAM_B1 ** ADAM_STEP)
        v_hat = vv / (1.0 - ADAM_B2 ** ADAM_STEP)
        d_ref[...] = -ADAM_LR * (m_hat / (jnp.sqrt(v_hat) + ADAM_EPS) + ADAM_WD * w_ref[...])
        nm_ref[...] = mv
        nv_ref[...] = vv

    spec = pl.BlockSpec((tr, c), lambda t: (t, 0))
    return pl.pallas_call(
        body, name=name, grid=(r // tr,), in_specs=[spec] * 4, out_specs=[spec] * 3,
        out_shape=[_sds((r, c), f32)] * 3, compiler_params=_arb(1),
    )(g, w, m, v)


def _allreduce_small(packed):
    shape = packed.shape

    def body(x_ref, out_ref, buf, send_sems, recv_sems):
        x, y, c = _place()
        me = 4 * x + 2 * y + c
        buf[me] = x_ref[...]
        sent = []
        for r in range(1, 8):
            dx, dy, dc = (r >> 2) & 1, (r >> 1) & 1, r & 1
            peer = ((1 - x) if dx else x, (1 - y) if dy else y, (1 - c) if dc else c)
            cp = pltpu.make_async_remote_copy(
                src_ref=x_ref, dst_ref=buf.at[me],
                send_sem=send_sems.at[r], recv_sem=recv_sems.at[r],
                device_id=peer, device_id_type=MESH)
            cp.start()
            sent.append((cp, peer))
        for r, (cp, peer) in enumerate(sent, start=1):
            src = 4 * peer[0] + 2 * peer[1] + peer[2]
            pltpu.make_async_remote_copy(
                src_ref=x_ref, dst_ref=buf.at[src],
                send_sem=send_sems.at[r], recv_sem=recv_sems.at[r],
                device_id=peer, device_id_type=MESH).wait_recv()
        for cp, _ in sent:
            cp.wait_send()
        total = buf[0]
        for k in range(1, 8):
            total = total + buf[k]
        out_ref[...] = total

    vmem = pl.BlockSpec(memory_space=pltpu.VMEM)
    return pl.pallas_call(
        body, name="allreduce_small", out_shape=_sds(shape, f32),
        in_specs=[vmem], out_specs=vmem,
        scratch_shapes=[pltpu.VMEM((8,) + shape, f32), pltpu.SemaphoreType.DMA((8,)),
                        pltpu.SemaphoreType.DMA((8,))],
    )(packed)


def _in_proj(x, g_mix, w_in, tm):
    s = x.shape[0]
    ncol = IN_COLS // N_CHIPS
    t = ATTN_TILE
    nb = tm // t
    koff = 3 * W_CONV + W_ATTN

    def body(x_ref, g_ref, w_ref, proj_ref, a_ref, kp_ref, vp_ref, kt_ref, vt_ref, perm, kv):
        a, _ = _rms_fwd(x_ref[...], g_ref[...])
        ab = a.astype(bf16)
        a_ref[...] = ab
        for k in range(N_CHIPS):
            proj_ref[:, k * ncol:(k + 1) * ncol] = _dot(ab, w_ref[k])
        for j in range(2 * W_ATTN // LANES):
            kv[j] = proj_ref[:, koff + j * LANES:koff + (j + 1) * LANES]
        for b in range(nb):
            for r in range(KEY_RUN):
                for j in range(2 * W_ATTN // LANES):
                    perm[SUBLANES * r:SUBLANES * (r + 1), j * LANES:(j + 1) * LANES] = kv[
                        j, pl.ds(b * t + r, SUBLANES, stride=KEY_RUN), :]
            kp_ref[b] = perm[:, :W_ATTN].astype(bf16)
            vp_ref[b] = perm[:, W_ATTN:].astype(bf16)
            kt_ref[b] = perm[:, :W_ATTN].T.astype(bf16)
            vt_ref[b] = perm[:, W_ATTN:].T.astype(bf16)

    keys = pl.BlockSpec((nb, t, W_ATTN), lambda i: (i, 0, 0))
    keys_t = pl.BlockSpec((nb, W_ATTN, t), lambda i: (i, 0, 0))
    return pl.pallas_call(
        body, name="in_proj", grid=(s // tm,),
        in_specs=[pl.BlockSpec((tm, D_MODEL), lambda i: (i, 0)),
                  pl.BlockSpec((1, D_MODEL), lambda i: (0, 0)),
                  pl.BlockSpec((N_CHIPS, D_MODEL, ncol), lambda i: (0, 0, 0))],
        out_specs=[pl.BlockSpec((tm, IN_COLS), lambda i: (i, 0)),
                   pl.BlockSpec((tm, D_MODEL), lambda i: (i, 0)), keys, keys, keys_t, keys_t],
        out_shape=[_sds((s, IN_COLS), f32), _sds((s, D_MODEL), bf16),
                   _sds((s // t, t, W_ATTN), bf16), _sds((s // t, t, W_ATTN), bf16),
                   _sds((s // t, W_ATTN, t), bf16), _sds((s // t, W_ATTN, t), bf16)],
        scratch_shapes=[pltpu.VMEM((t, 2 * W_ATTN), f32), pltpu.VMEM((2 * W_ATTN // LANES, tm, LANES), f32)],
        compiler_params=_arb(1),
    )(x, g_mix, w_in)


def _shifted(prev8, cur, shift):
    ext = jnp.concatenate([prev8, cur], axis=0)
    return pltpu.roll(ext, shift, axis=0)[8:]


def _conv_fwd(proj, conv_w, g_conv):
    s = proj.shape[0]
    nblk = W_CONV // LANES
    rc = min(CONV_CHUNK, s)

    def body(cb_ref, cc_ref, cu_ref, w_ref, g_ref, out_ref):
        ones = _group_ones(LANES)
        w0, w1, w2 = w_ref[0:1, :], w_ref[1:2, :], w_ref[2:3, :]
        g = g_ref[...]

        def chunk(i, carry):
            r0 = pl.multiple_of(i * rc, rc)
            rows = pl.ds(r0, rc)
            prev = pl.ds(pl.multiple_of(jnp.maximum(r0 - 8, 0), 8), 8)
            v = cc_ref[rows, :] * cu_ref[rows, :]
            vp = jnp.where(i > 0, cc_ref[prev, :] * cu_ref[prev, :], 0.0)
            y = w2 * v + w1 * _shifted(vp, v, 1) + w0 * _shifted(vp, v, 2)
            out_ref[rows, :] = _head_rms_fwd(cb_ref[rows, :] * y, g, ones).astype(bf16)
            return carry

        lax.fori_loop(0, s // rc, chunk, 0)

    def col(off):
        return pl.BlockSpec((s, LANES), lambda j: (0, off + j))

    return pl.pallas_call(
        body, name="conv_fwd", grid=(nblk,),
        in_specs=[col(0), col(nblk), col(2 * nblk),
                  pl.BlockSpec((None, CONV_W_ROWS, LANES), lambda j: (j, 0, 0)),
                  pl.BlockSpec((1, LANES), lambda j: (0, j))],
        out_specs=pl.BlockSpec((s, LANES), lambda j: (0, j)),
        out_shape=_sds((s, W_CONV), bf16), compiler_params=_arb(1),
    )(proj, proj, proj, conv_w, g_conv)


LOG2_E = 1.4426950408889634


def _log2_keep(z2):
    nz2 = -z2
    return jnp.minimum(nz2, 0.0) - jnp.log2(1.0 + jnp.exp2(jnp.minimum(z2, nz2)))


def _head_pair_masks(rows):
    lane = lax.broadcasted_iota(jnp.int32, (rows, LANES), 1)
    return lane < HEAD_DIM


SUBLANES = 8
KEY_RUN = ATTN_TILE // SUBLANES


def _causal_tiles():
    r = lax.broadcasted_iota(jnp.int32, (ATTN_TILE, ATTN_TILE), 0)
    key = (r % SUBLANES) * KEY_RUN + r // SUBLANES
    below = key < lax.broadcasted_iota(jnp.int32, (ATTN_TILE, ATTN_TILE), 1)
    return below.astype(f32), jnp.where(below, 0.0, -1e30).astype(f32)


def _sublane_scan(x, reverse):
    row = lax.broadcasted_iota(jnp.int32, x.shape, 0)
    inc = x
    for sh in (1, 2, 4):
        if reverse:
            inc = inc + jnp.where(row < SUBLANES - sh, pltpu.roll(inc, SUBLANES - sh, axis=0), 0.0)
        else:
            inc = inc + jnp.where(row >= sh, pltpu.roll(inc, sh, axis=0), 0.0)
    return inc - x


def _attn_fwd(proj, kp, vt, ex=None):
    s = proj.shape[0]
    t = ATTN_TILE
    nblk = s // t
    npair = W_ATTN // LANES
    qoff = 3 * W_CONV // LANES
    keep01, keepneg = _causal_tiles()

    def body(q_ref, k_ref, vt_ref, m01_ref, neg_ref, o_ref, tot_ref, w_s, a_s, acc):
        qb = pl.program_id(1)
        first = _head_pair_masks(t)
        q = q_ref[...] * (HEAD_DIM ** -0.5)
        qh = (jnp.where(first, q, 0.0).astype(bf16), jnp.where(first, 0.0, q).astype(bf16))
        acc[...] = jnp.zeros_like(acc)
        a_s[1] = jnp.zeros((t, t), bf16)

        def scores(kb, h):
            w_s[h] = _dot(k_ref[kb], qh[h], NT)

        def weigh(kb, h):
            acc[h] += _dot(vt_ref[kb], a_s[h])

        def weights(h, diagonal, later):
            run = jnp.zeros((SUBLANES, t), f32)
            for a in reversed(range(KEY_RUN)):
                rows = slice(SUBLANES * a, SUBLANES * (a + 1))
                z2 = w_s[h, rows, :] * LOG2_E
                lk = _log2_keep(z2)
                if diagonal:
                    lk = lk * m01_ref[rows, :]
                run = run + lk
                w_s[h, rows, :] = z2 + run
            off = _sublane_scan(run, reverse=True) + later
            off2 = jnp.concatenate([off, off], axis=0)
            for a in range(t // (2 * SUBLANES)):
                rows = slice(2 * SUBLANES * a, 2 * SUBLANES * (a + 1))
                w = w_s[h, rows, :] + off2
                if diagonal:
                    w = w + neg_ref[rows, :]
                a_s[h, rows, :] = jnp.exp2(w).astype(bf16)
            return later + jnp.sum(run, axis=0, keepdims=True)

        def block(kb, before, after, diagonal, later):
            scores(kb, 1)
            weigh(before, 1)
            l0 = weights(0, diagonal, later[0])
            scores(after, 0)
            weigh(kb, 0)
            l1 = weights(1, diagonal, later[1])
            return l0, l1

        zero = jnp.zeros((1, t), f32)
        scores(qb, 0)
        later = block(qb, qb, jnp.maximum(qb - 1, 0), True, (zero, zero))

        def earlier(i, c):
            kb = qb - 1 - i
            return block(kb, kb + 1, jnp.maximum(kb - 1, 0), False, c)

        later = lax.fori_loop(0, qb, earlier, later)
        weigh(0, 1)
        top = lax.broadcasted_iota(jnp.int32, (LANES, t), 0) < HEAD_DIM
        o_ref[...] = jnp.where(top, acc[0], acc[1]).T
        tot_ref[...] = jnp.concatenate([later[0], later[1], jnp.zeros((SUBLANES - 2, t), f32)], axis=0)

    return _call_carrying(
        ex, body, "attn_fwd", (npair, nblk),
        in_specs=[pl.BlockSpec((t, LANES), lambda p, i: (i, qoff + p)),
                  pl.BlockSpec((nblk, t, LANES), lambda p, i: (0, 0, p)),
                  pl.BlockSpec((nblk, LANES, t), lambda p, i: (0, p, 0)),
                  pl.BlockSpec((t, t), lambda p, i: (0, 0)),
                  pl.BlockSpec((t, t), lambda p, i: (0, 0))],
        out_specs=[pl.BlockSpec((t, LANES), lambda p, i: (i, p)),
                   pl.BlockSpec((None, SUBLANES, t), lambda p, i: (p, 0, i))],
        out_shape=[_sds((s, W_ATTN), f32), _sds((npair, SUBLANES, s), f32)],
        scratch_shapes=[pltpu.VMEM((2, t, t), f32), pltpu.VMEM((2, t, t), bf16), pltpu.VMEM((2, LANES, t), f32)],
        args=(proj, kp, vt, keep01, keepneg))


def _out_proj(o, conv_n, x, g_attn, w_out, tm):
    s = x.shape[0]

    def body(o_ref, c_ref, x_ref, g_ref, w_ref, h_ref, cat_ref):
        ones = _group_ones(LANES)
        cat_ref[:, :W_CONV] = c_ref[...]
        for j in range(W_ATTN // LANES):
            cols = slice(j * LANES, (j + 1) * LANES)
            cat_ref[:, W_CONV + j * LANES:W_CONV + (j + 1) * LANES] = _head_rms_fwd(
                o_ref[:, cols], g_ref[:, cols], ones).astype(bf16)
        h_ref[...] = x_ref[...] + _dot(cat_ref[...], w_ref[...])

    return pl.pallas_call(
        body, name="out_proj", grid=(s // tm,),
        in_specs=[pl.BlockSpec((tm, W_ATTN), lambda i: (i, 0)),
                  pl.BlockSpec((tm, W_CONV), lambda i: (i, 0)),
                  pl.BlockSpec((tm, D_MODEL), lambda i: (i, 0)),
                  pl.BlockSpec((1, W_ATTN), lambda i: (0, 0)),
                  pl.BlockSpec((D_MODEL, D_MODEL), lambda i: (0, 0))],
        out_specs=[pl.BlockSpec((tm, D_MODEL), lambda i: (i, 0)),
                   pl.BlockSpec((tm, D_MODEL), lambda i: (i, 0))],
        out_shape=[_sds((s, D_MODEL), f32), _sds((s, D_MODEL), bf16)],
        compiler_params=_arb(1),
    )(o, conv_n, x, g_attn, w_out)


def _mlp_fwd(h1, g_mlp, w_up, w_down, tm):
    s = h1.shape[0]
    fc = D_FF // N_CHIPS

    def body(h_ref, g_ref, wu_ref, wd_ref, h2_ref, u_ref, m_ref):
        j = pl.program_id(1)

        @pl.when(j == 0)
        def _():
            m, _ = _rms_fwd(h_ref[...], g_ref[...])
            m_ref[...] = m.astype(bf16)
            h2_ref[...] = h_ref[...]

        u = _dot(m_ref[...], wu_ref[...])
        u_ref[...] = u.astype(bf16)
        h2_ref[...] += _dot(jnp.square(jnp.maximum(u, 0.0)).astype(bf16), wd_ref[...])

    return pl.pallas_call(
        body, name="mlp_fwd", grid=(s // tm, N_CHIPS),
        in_specs=[pl.BlockSpec((tm, D_MODEL), lambda i, j: (i, 0)),
                  pl.BlockSpec((1, D_MODEL), lambda i, j: (0, 0)),
                  pl.BlockSpec((None, D_MODEL, fc), lambda i, j: (j, 0, 0)),
                  pl.BlockSpec((None, fc, D_MODEL), lambda i, j: (j, 0, 0))],
        out_specs=[pl.BlockSpec((tm, D_MODEL), lambda i, j: (i, 0)),
                   pl.BlockSpec((tm, fc), lambda i, j: (i, j)),
                   pl.BlockSpec((tm, D_MODEL), lambda i, j: (i, 0))],
        out_shape=[_sds((s, D_MODEL), f32), _sds((s, D_FF), bf16), _sds((s, D_MODEL), bf16)],
        compiler_params=_arb(2),
    )(h1, g_mlp, w_up, w_down)


def _tail(h2, p, target, g_ple, g_final, w_gate, w_proj, tm):
    s = h2.shape[0]
    pc = D_MODEL // N_CHIPS

    def body(h_ref, p_ref, t_ref, gp_ref, gf_ref, wg_ref, wp_ref,
             dh_ref, dhb_ref, n3_ref, dgl_ref, dpp_ref, pb_ref, ggp_ref, ggf_ref, loss_ref, pp_ref):
        i = pl.program_id(0)
        h2v = h_ref[...]
        n3, _ = _rms_fwd(h2v, gp_ref[...])
        n3b = n3.astype(bf16)
        n3_ref[...] = n3b
        gate = jax.nn.sigmoid(_dot(n3b, wg_ref[...]))
        pb = p_ref[...].astype(bf16)
        pb_ref[...] = pb
        for k in range(N_CHIPS):
            pp_ref[:, k * pc:(k + 1) * pc] = _dot(pb, wp_ref[k])
        pp = pp_ref[...]
        h3 = h2v + gate * pp
        yv, _ = _rms_fwd(h3, gf_ref[...])
        err = yv - t_ref[...]
        loss = 0.5 * jnp.sum(err * err) * (1.0 / D_MODEL)
        dh3, ggf = _rms_bwd(err * (1.0 / D_MODEL), h3, gf_ref[...])
        dpp_ref[...] = (dh3 * gate).astype(bf16)
        dgl = (dh3 * pp * gate * (1.0 - gate)).astype(bf16)
        dgl_ref[...] = dgl
        dn3 = _dot(dgl, wg_ref[...], NT)
        dh2n, ggp = _rms_bwd(dn3, h2v, gp_ref[...])
        dh2 = dh3 + dh2n
        dh_ref[...] = dh2
        dhb_ref[...] = dh2.astype(bf16)

        @pl.when(i == 0)
        def _():
            ggp_ref[...] = jnp.zeros_like(ggp_ref)
            ggf_ref[...] = jnp.zeros_like(ggf_ref)
            loss_ref[...] = jnp.zeros_like(loss_ref)

        ggp_ref[...] += ggp
        ggf_ref[...] += ggf
        loss_ref[...] += jnp.full(loss_ref.shape, loss, f32)

    tok = lambda w: pl.BlockSpec((tm, w), lambda i: (i, 0))
    vec = lambda w: pl.BlockSpec((1, w), lambda i: (0, 0))
    return pl.pallas_call(
        body, name="tail", grid=(s // tm,),
        in_specs=[tok(D_MODEL), tok(PLE_DIM), tok(D_MODEL), vec(D_MODEL), vec(D_MODEL),
                  pl.BlockSpec((D_MODEL, D_MODEL), lambda i: (0, 0)),
                  pl.BlockSpec((N_CHIPS, PLE_DIM, pc), lambda i: (0, 0, 0))],
        out_specs=[tok(D_MODEL), tok(D_MODEL), tok(D_MODEL), tok(D_MODEL), tok(D_MODEL), tok(PLE_DIM),
                   vec(D_MODEL), vec(D_MODEL), vec(LANES)],
        out_shape=[_sds((s, D_MODEL), f32), _sds((s, D_MODEL), bf16), _sds((s, D_MODEL), bf16),
                   _sds((s, D_MODEL), bf16), _sds((s, D_MODEL), bf16), _sds((s, PLE_DIM), bf16),
                   _sds((1, D_MODEL), f32), _sds((1, D_MODEL), f32), _sds((1, LANES), f32)],
        scratch_shapes=[pltpu.VMEM((tm, D_MODEL), f32)],
        compiler_params=_arb(1),
    )(h2, p, target, g_ple, g_final, w_gate, w_proj)


def _mlp_bwd(dh2, dh2b, h1, u, g_mlp, w_up, w_down, tm):
    s = h1.shape[0]
    fc = D_FF // N_CHIPS

    def body(dh_ref, dhb_ref, h_ref, u_ref, g_ref, wu_ref, wd_ref, dh1_ref, dh1b_ref, du_ref, gg_ref, dm):
        i, j = pl.program_id(0), pl.program_id(1)

        @pl.when(j == 0)
        def _():
            dm[...] = jnp.zeros_like(dm)

        dr = _dot(dhb_ref[...], wd_ref[...], NT)
        du = (dr * (2.0 * jnp.maximum(u_ref[...].astype(f32), 0.0))).astype(bf16)
        du_ref[...] = du
        dm[...] += _dot(du, wu_ref[...], NT)

        @pl.when((i == 0) & (j == 0))
        def _():
            gg_ref[...] = jnp.zeros_like(gg_ref)

        @pl.when(j == N_CHIPS - 1)
        def _():
            dh1n, gg = _rms_bwd(dm[...], h_ref[...], g_ref[...])
            dh1 = dh_ref[...] + dh1n
            dh1_ref[...] = dh1
            dh1b_ref[...] = dh1.astype(bf16)
            gg_ref[...] += gg

    tok = pl.BlockSpec((tm, D_MODEL), lambda i, j: (i, 0))
    ffb = pl.BlockSpec((tm, fc), lambda i, j: (i, j))
    vec = pl.BlockSpec((1, D_MODEL), lambda i, j: (0, 0))
    return pl.pallas_call(
        body, name="mlp_bwd", grid=(s // tm, N_CHIPS),
        in_specs=[tok, tok, tok, ffb, vec,
                  pl.BlockSpec((None, D_MODEL, fc), lambda i, j: (j, 0, 0)),
                  pl.BlockSpec((None, fc, D_MODEL), lambda i, j: (j, 0, 0))],
        out_specs=[tok, tok, ffb, vec],
        out_shape=[_sds((s, D_MODEL), f32), _sds((s, D_MODEL), bf16), _sds((s, D_FF), bf16),
                   _sds((1, D_MODEL), f32)],
        scratch_shapes=[pltpu.VMEM((tm, D_MODEL), f32)],
        compiler_params=_arb(2),
    )(dh2, dh2b, h1, u, g_mlp, w_up, w_down)


def _out_proj_bwd(dh1b, o, g_attn, w_out, tm, ex=None):
    s = o.shape[0]

    def body(dh_ref, o_ref, g_ref, w_ref, dc_ref, do_ref, gg_ref, dcat):
        i = pl.program_id(0)
        ones = _group_ones(LANES)
        dcat[...] = _dot(dh_ref[...], w_ref[...], NT)
        dc_ref[...] = dcat[:, :W_CONV]

        @pl.when(i == 0)
        def _():
            gg_ref[...] = jnp.zeros_like(gg_ref)

        for j in range(W_ATTN // LANES):
            cols = slice(j * LANES, (j + 1) * LANES)
            d, gg = _head_rms_bwd(dcat[:, W_CONV + j * LANES:W_CONV + (j + 1) * LANES],
                                  o_ref[:, cols], g_ref[:, cols], ones)
            do_ref[:, cols] = d
            gg_ref[:, cols] += gg

    return _call_carrying(
        ex, body, "out_proj_bwd", (s // tm,),
        in_specs=[pl.BlockSpec((tm, D_MODEL), lambda i: (i, 0)),
                  pl.BlockSpec((tm, W_ATTN), lambda i: (i, 0)),
                  pl.BlockSpec((1, W_ATTN), lambda i: (0, 0)),
                  pl.BlockSpec((D_MODEL, D_MODEL), lambda i: (0, 0))],
        out_specs=[pl.BlockSpec((tm, W_CONV), lambda i: (i, 0)),
                   pl.BlockSpec((tm, W_ATTN), lambda i: (i, 0)),
                   pl.BlockSpec((1, W_ATTN), lambda i: (0, 0))],
        out_shape=[_sds((s, W_CONV), f32), _sds((s, W_ATTN), f32), _sds((1, W_ATTN), f32)],
        scratch_shapes=[pltpu.VMEM((tm, D_MODEL), f32)],
        args=(dh1b, o, g_attn, w_out))


def _attn_bwd(proj, kp, vp, kt, do, tot, ex=None):
    s = proj.shape[0]
    t = ATTN_TILE
    nblk = s // t
    npair = W_ATTN // LANES
    qoff = 3 * W_CONV // LANES
    keep01, keepneg = _causal_tiles()

    def body(q_ref, k_ref, v_ref, kt_ref, do_ref, tot_ref, m01_ref, neg_ref, dq_ref, dk_ref, dv_ref,
             w_s, da_s, b_s, g_s, a_s, dz_s, dq_acc, dk_acc, dv_acc):
        qb = pl.program_id(1)
        first = _head_pair_masks(t)
        q = q_ref[...] * (HEAD_DIM ** -0.5)
        qh = (jnp.where(first, q, 0.0).astype(bf16), jnp.where(first, 0.0, q).astype(bf16))
        dov = do_ref[...]
        doh = (jnp.where(first, dov, 0.0).astype(bf16), jnp.where(first, 0.0, dov).astype(bf16))
        total = (tot_ref[0:1, :], tot_ref[1:2, :])

        @pl.when(qb == 0)
        def _():
            dk_acc[...] = jnp.zeros_like(dk_acc)
            dv_acc[...] = jnp.zeros_like(dv_acc)

        dq_acc[...] = jnp.zeros_like(dq_acc)
        a_s[1] = jnp.zeros((t, t), bf16)
        dz_s[1] = jnp.zeros((t, t), bf16)

        def scores(kb, h):
            w_s[h] = _dot(k_ref[kb], qh[h], NT)
            da_s[h] = _dot(v_ref[kb], doh[h], NT)

        def spread(kb, h):
            dq_acc[h] += _dot(kt_ref[kb], dz_s[h])
            dk_acc[kb] += _dot(dz_s[h], qh[h])
            dv_acc[kb] += _dot(a_s[h], doh[h])

        def grads(h, diagonal, lk_before, g_before):
            run = jnp.zeros((SUBLANES, t), f32)
            for a in range(KEY_RUN):
                rows = slice(SUBLANES * a, SUBLANES * (a + 1))
                z2 = w_s[h, rows, :] * LOG2_E
                lk = _log2_keep(z2)
                if diagonal:
                    lk = lk * m01_ref[rows, :]
                log_beta = jnp.minimum(z2 + lk, 0.0)
                run = run + lk
                b_s[h, rows, :] = jnp.exp2(log_beta)
                w_s[h, rows, :] = log_beta - run
            off = total[h] - lk_before - _sublane_scan(run, reverse=False)
            lk_sum = jnp.sum(run, axis=0, keepdims=True)
            run = jnp.zeros((SUBLANES, t), f32)
            for a in range(KEY_RUN // 2):
                parts = []
                for r in (slice(2 * SUBLANES * a, 2 * SUBLANES * a + SUBLANES),
                          slice(2 * SUBLANES * a + SUBLANES, 2 * SUBLANES * (a + 1))):
                    w = w_s[h, r, :] + off
                    if diagonal:
                        w = w + neg_ref[r, :]
                    av = jnp.exp2(w)
                    g = av * da_s[h, r, :]
                    run = run + g
                    da_s[h, r, :] = g
                    g_s[h, r, :] = run
                    parts.append(av)
                a_s[h, 2 * SUBLANES * a:2 * SUBLANES * (a + 1), :] = jnp.concatenate(parts, axis=0).astype(bf16)
            goff = g_before + _sublane_scan(run, reverse=False)
            goff2 = jnp.concatenate([goff, goff], axis=0)
            for a in range(KEY_RUN // 2):
                rows = slice(2 * SUBLANES * a, 2 * SUBLANES * (a + 1))
                dz = da_s[h, rows, :] - b_s[h, rows, :] * (g_s[h, rows, :] + goff2)
                if diagonal:
                    dz = dz * m01_ref[rows, :]
                dz_s[h, rows, :] = dz.astype(bf16)
            return lk_before + lk_sum, g_before + jnp.sum(run, axis=0, keepdims=True)

        def block(kb, before, after, diagonal, carry):
            scores(kb, 1)
            spread(before, 1)
            c0 = grads(0, diagonal, carry[0], carry[1])
            if after is not None:
                scores(after, 0)
            spread(kb, 0)
            c1 = grads(1, diagonal, carry[2], carry[3])
            return c0 + c1

        zero = jnp.zeros((1, t), f32)
        scores(0, 0)
        def two_blocks(i, c):
            kb = 2 * i
            c = block(kb, jnp.maximum(kb - 1, 0), kb + 1, False, c)
            return block(kb + 1, kb, kb + 2, False, c)

        carry = lax.fori_loop(0, qb // 2, two_blocks, (zero, zero, zero, zero))
        carry = lax.fori_loop(qb - qb % 2, qb, lambda kb, c: block(kb, jnp.maximum(kb - 1, 0), kb + 1, False, c), carry)
        block(qb, jnp.maximum(qb - 1, 0), None, True, carry)
        spread(qb, 1)
        top = lax.broadcasted_iota(jnp.int32, (LANES, t), 0) < HEAD_DIM
        dq_ref[...] = (jnp.where(top, dq_acc[0], dq_acc[1]).T * (HEAD_DIM ** -0.5)).astype(bf16)

        @pl.when(qb == nblk - 1)
        def _():
            for kb in range(nblk):
                for b in range(SUBLANES):
                    keys_b = slice(KEY_RUN * b, KEY_RUN * (b + 1))
                    dk_ref[kb, keys_b, :] = dk_acc[kb, pl.ds(b, KEY_RUN, stride=SUBLANES), :].astype(bf16)
                    dv_ref[kb, keys_b, :] = dv_acc[kb, pl.ds(b, KEY_RUN, stride=SUBLANES), :].astype(bf16)

    keys = pl.BlockSpec((nblk, t, LANES), lambda p, i: (0, 0, p))
    tile = pl.BlockSpec((t, t), lambda p, i: (0, 0))
    return _call_carrying(
        ex, body, "attn_bwd", (npair, nblk),
        in_specs=[pl.BlockSpec((t, LANES), lambda p, i: (i, qoff + p)),
                  keys, keys,
                  pl.BlockSpec((nblk, LANES, t), lambda p, i: (0, p, 0)),
                  pl.BlockSpec((t, LANES), lambda p, i: (i, p)),
                  pl.BlockSpec((None, SUBLANES, t), lambda p, i: (p, 0, i)),
                  tile, tile],
        out_specs=[pl.BlockSpec((t, LANES), lambda p, i: (i, p)), keys, keys],
        out_shape=[_sds((s, W_ATTN), bf16), _sds((nblk, t, W_ATTN), bf16), _sds((nblk, t, W_ATTN), bf16)],
        scratch_shapes=[pltpu.VMEM((2, t, t), f32), pltpu.VMEM((2, t, t), f32), pltpu.VMEM((2, t, t), f32),
                        pltpu.VMEM((2, t, t), f32), pltpu.VMEM((2, t, t), bf16), pltpu.VMEM((2, t, t), bf16),
                        pltpu.VMEM((2, LANES, t), f32), pltpu.VMEM((nblk, t, LANES), f32),
                        pltpu.VMEM((nblk, t, LANES), f32)],
        args=(proj, kp, vp, kt, do, tot, keep01, keepneg))


def _conv_bwd(proj, dcn, conv_w, g_conv):
    s = proj.shape[0]
    nblk = W_CONV // LANES
    rc = min(CONV_CHUNK, s)
    nchunk = s // rc

    def body(cb_ref, cc_ref, cu_ref, d_ref, w_ref, g_ref, d3_ref, gw_ref, gg_ref, dy_buf):
        ones = _group_ones(LANES)
        w0, w1, w2 = w_ref[0:1, :], w_ref[1:2, :], w_ref[2:3, :]
        g = g_ref[...]

        def first_pass(i, carry):
            gw0, gw1, gw2, gg = carry
            r0 = pl.multiple_of(i * rc, rc)
            rows = pl.ds(r0, rc)
            prev = pl.ds(pl.multiple_of(jnp.maximum(r0 - 8, 0), 8), 8)
            v = cc_ref[rows, :] * cu_ref[rows, :]
            vp = jnp.where(i > 0, cc_ref[prev, :] * cu_ref[prev, :], 0.0)
            v1, v2 = _shifted(vp, v, 1), _shifted(vp, v, 2)
            y = w2 * v + w1 * v1 + w0 * v2
            cb = cb_ref[rows, :]
            dcy, ggi = _head_rms_bwd(d_ref[rows, :], cb * y, g, ones)
            d3_ref[0, rows, :] = (dcy * y).astype(bf16)
            dy = dcy * cb
            dy_buf[rows, :] = dy
            return (gw0 + jnp.sum(dy * v2, axis=0, keepdims=True), gw1 + jnp.sum(dy * v1, axis=0, keepdims=True),
                    gw2 + jnp.sum(dy * v, axis=0, keepdims=True), gg + ggi)

        zero = jnp.zeros((1, LANES), f32)
        gw0, gw1, gw2, gg = lax.fori_loop(0, nchunk, first_pass, (zero, zero, zero, zero))
        gw_ref[...] = jnp.zeros_like(gw_ref)
        gw_ref[0:1, :] = gw0
        gw_ref[1:2, :] = gw1
        gw_ref[2:3, :] = gw2
        gg_ref[...] = gg

        def second_pass(i, carry):
            r0 = pl.multiple_of(i * rc, rc)
            rows = pl.ds(r0, rc)
            nxt = pl.ds(pl.multiple_of(jnp.minimum(r0 + rc, s - 8), 8), 8)
            dy = dy_buf[rows, :]
            dyn = jnp.where(i < nchunk - 1, dy_buf[nxt, :], 0.0)
            ext = jnp.concatenate([dy, dyn], axis=0)
            up1 = pltpu.roll(ext, rc + 8 - 1, axis=0)[:rc]
            up2 = pltpu.roll(ext, rc + 8 - 2, axis=0)[:rc]
            dv = w2 * dy + w1 * up1 + w0 * up2
            d3_ref[1, rows, :] = (dv * cu_ref[rows, :]).astype(bf16)
            d3_ref[2, rows, :] = (dv * cc_ref[rows, :]).astype(bf16)
            return carry

        lax.fori_loop(0, nchunk, second_pass, 0)

    def col(off):
        return pl.BlockSpec((s, LANES), lambda j: (0, off + j))

    return pl.pallas_call(
        body, name="conv_bwd", grid=(nblk,),
        in_specs=[col(0), col(nblk), col(2 * nblk), col(0),
                  pl.BlockSpec((None, CONV_W_ROWS, LANES), lambda j: (j, 0, 0)),
                  pl.BlockSpec((1, LANES), lambda j: (0, j))],
        out_specs=[pl.BlockSpec((3, s, LANES), lambda j: (0, 0, j)),
                   pl.BlockSpec((None, CONV_W_ROWS, LANES), lambda j: (j, 0, 0)),
                   pl.BlockSpec((1, LANES), lambda j: (0, j))],
        out_shape=[_sds((3, s, W_CONV), bf16), _sds((nblk, CONV_W_ROWS, LANES), f32), _sds((1, W_CONV), f32)],
        scratch_shapes=[pltpu.VMEM((s, LANES), f32)],
        compiler_params=_arb(1),
    )(proj, proj, proj, dcn, conv_w, g_conv)


PIECE = W_CONV


def _piece_spans():
    ncol = IN_COLS // N_CHIPS
    spans = []
    for p in range(IN_COLS // PIECE):
        for k in range(N_CHIPS):
            lo, hi = max(p * PIECE, k * ncol), min((p + 1) * PIECE, (k + 1) * ncol)
            if lo < hi:
                spans.append((p, lo - p * PIECE, hi - p * PIECE, k, lo - k * ncol, hi - k * ncol))
    return spans


def _in_proj_bwd(dconv, dq, dk, dv, dh1, x, g_mix, w_in, tm, ex=None):
    s = x.shape[0]
    ncol = IN_COLS // N_CHIPS

    def body(dc_ref, dq_ref, dk_ref, dv_ref, dh_ref, x_ref, g_ref, w_ref, dx_ref, gg_ref):
        i = pl.program_id(0)
        pieces = [dc_ref.at[0], dc_ref.at[1], dc_ref.at[2], dq_ref, dk_ref, dv_ref]
        da = jnp.zeros((tm, D_MODEL), f32)
        for p, plo, phi, k, wlo, whi in _piece_spans():
            da += _dot(pieces[p][:, plo:phi], w_ref[k, :, wlo:whi], NT)
        dxn, gg = _rms_bwd(da, x_ref[...], g_ref[...])
        dx_ref[...] = dh_ref[...] + dxn

        @pl.when(i == 0)
        def _():
            gg_ref[...] = jnp.zeros_like(gg_ref)

        gg_ref[...] += gg

    return _call_carrying(
        ex, body, "in_proj_bwd", (s // tm,),
        in_specs=[pl.BlockSpec((3, tm, PIECE), lambda i: (0, i, 0)),
                  pl.BlockSpec((tm, PIECE), lambda i: (i, 0)),
                  pl.BlockSpec((tm, PIECE), lambda i: (i, 0)),
                  pl.BlockSpec((tm, PIECE), lambda i: (i, 0)),
                  pl.BlockSpec((tm, D_MODEL), lambda i: (i, 0)),
                  pl.BlockSpec((tm, D_MODEL), lambda i: (i, 0)),
                  pl.BlockSpec((1, D_MODEL), lambda i: (0, 0)),
                  pl.BlockSpec((N_CHIPS, D_MODEL, ncol), lambda i: (0, 0, 0))],
        out_specs=[pl.BlockSpec((tm, D_MODEL), lambda i: (i, 0)),
                   pl.BlockSpec((1, D_MODEL), lambda i: (0, 0))],
        out_shape=[_sds((s, D_MODEL), f32), _sds((1, D_MODEL), f32)],
        scratch_shapes=[],
        args=(dconv, dq, dk, dv, dh1, x, g_mix, w_in))


def _grad_w_in(a, dconv, dq, dk, dv, ts):
    s = a.shape[0]
    ncol = IN_COLS // N_CHIPS

    def body(a_ref, dc_ref, dq_ref, dk_ref, dv_ref, o_ref):
        @pl.when(pl.program_id(0) == 0)
        def _():
            o_ref[...] = jnp.zeros_like(o_ref)

        pieces = [dc_ref.at[0], dc_ref.at[1], dc_ref.at[2], dq_ref, dk_ref, dv_ref]
        av = a_ref[...]
        for p, plo, phi, k, wlo, whi in _piece_spans():
            o_ref[k, :, wlo:whi] += _dot(av, pieces[p][:, plo:phi], TN)

    tok = pl.BlockSpec((ts, PIECE), lambda i: (i, 0))
    return pl.pallas_call(
        body, name="grad_w_in", grid=(s // ts,),
        in_specs=[pl.BlockSpec((ts, D_MODEL), lambda i: (i, 0)),
                  pl.BlockSpec((3, ts, PIECE), lambda i: (0, i, 0)), tok, tok, tok],
        out_specs=pl.BlockSpec((N_CHIPS, D_MODEL, ncol), lambda i: (0, 0, 0)),
        out_shape=_sds((N_CHIPS, D_MODEL, ncol), f32),
        compiler_params=_arb(1),
    )(a, dconv, dq, dk, dv)


def _weight_grad(a, b, bm, bn, ts, name, relu_sq=False):
    s, m = a.shape
    n = b.shape[1]
    nn = n // bn

    def body(a_ref, b_ref, o_ref):
        @pl.when(pl.program_id(2) == 0)
        def _():
            o_ref[...] = jnp.zeros_like(o_ref)

        av = a_ref[...]
        if relu_sq:
            av = jnp.square(jnp.maximum(av.astype(f32), 0.0)).astype(bf16)
        o_ref[...] += _dot(av, b_ref[...], TN)

    return pl.pallas_call(
        body, name=name, grid=(m // bm, nn, s // ts),
        in_specs=[pl.BlockSpec((ts, bm), lambda i, j, k: (k, i)),
                  pl.BlockSpec((ts, bn), lambda i, j, k: (k, j))],
        out_specs=pl.BlockSpec((None, bm, bn), lambda i, j, k: (i * nn + j, 0, 0)),
        out_shape=_sds(((m // bm) * nn, bm, bn), f32),
        compiler_params=_arb(3),
    )(a, b)


def kernel(x, p, g_mix, w_in, conv_w, g_conv_out, g_attn_out, w_out, g_mlp, w_up, w_down, g_ple, w_ple_gate, w_ple_proj, g_final, loss_target, m_g_mix, m_w_in, m_conv_w, m_g_conv_out, m_g_attn_out, m_w_out, m_g_mlp, m_w_up, m_w_down, m_g_ple, m_w_ple_gate, m_w_ple_proj, m_g_final, v_g_mix, v_w_in, v_conv_w, v_g_conv_out, v_g_attn_out, v_w_out, v_g_mlp, v_w_up, v_w_down, v_g_ple, v_w_ple_gate, v_w_ple_proj, v_g_final):
    s = x.shape[1]
    tm = min(TOKEN_TILE, s)
    tg = min(GRAD_TOKEN_TILE, s)
    xs = x.reshape(s, D_MODEL)
    ps = p.reshape(s, PLE_DIM)
    target = loss_target.reshape(s, D_MODEL)
    core = lax.axis_index("c").astype(jnp.int32).reshape(1)
    chip = 2 * lax.axis_index("x") + lax.axis_index("y")

    big = {"w_in": w_in[0], "w_out": w_out[0], "w_up": w_up[0], "w_down": w_down[0],
           "w_ple_gate": w_ple_gate[0], "w_ple_proj": w_ple_proj[0]}
    names = list(big)
    conv_shard = jnp.pad(conv_w[0], ((0, CONV_W_ROWS - conv_w.shape[1]), (0, 0)))
    later_names = names[1:]
    w_in_f, conv_f = _run_exchange(_gather_exchange([big["w_in"].astype(bf16), conv_shard]), "gather_w_in")

    proj, a_b, kp, vp, kt, vt = _in_proj(xs, g_mix, w_in_f, tm)
    conv_n = _conv_fwd(proj, conv_f, g_conv_out)
    (o, tot), gathered = _attn_fwd(proj, kp, vt, _gather_exchange([big[k].astype(bf16) for k in later_names]))
    w_out_f, w_up_f, w_down_f, w_gate_f, w_proj_f = gathered
    w_out_f = w_out_f.reshape(D_MODEL, D_MODEL)
    w_gate_f = w_gate_f.reshape(D_MODEL, D_MODEL)
    h1, cat_b = _out_proj(o, conv_n, xs, g_attn_out, w_out_f, tm)
    h2, u_b, m_b = _mlp_fwd(h1, g_mlp, w_up_f, w_down_f, min(MLP_TOKEN_TILE, s))

    dh2, dh2_b, n3_b, dgl_b, dpp_b, p_b, gg_ple, gg_final, loss_row = _tail(
        h2, ps, target, g_ple, g_final.reshape(1, D_MODEL), w_gate_f, w_proj_f, tm)
    dh1, dh1_b, du_b, gg_mlp = _mlp_bwd(dh2, dh2_b, h1, u_b, g_mlp, w_up_f, w_down_f, tm)
    part = {
        "w_out": _weight_grad(cat_b, dh1_b, D_MODEL, D_MODEL, tg, "grad_w_out").reshape(N_CHIPS, D_MODEL // N_CHIPS, D_MODEL),
        "w_up": _weight_grad(m_b, du_b, D_MODEL, D_FF // N_CHIPS, tg, "grad_w_up"),
        "w_down": _weight_grad(u_b, dh2_b, D_FF // N_CHIPS, D_MODEL, tg, "grad_w_down", relu_sq=True),
        "w_ple_gate": _weight_grad(n3_b, dgl_b, D_MODEL, D_MODEL, tg, "grad_w_ple_gate").reshape(N_CHIPS, D_MODEL // N_CHIPS, D_MODEL),
        "w_ple_proj": _weight_grad(p_b, dpp_b, PLE_DIM, D_MODEL // N_CHIPS, tg, "grad_w_ple_proj"),
    }
    (dcn, do, gg_attn), from_sibling = _out_proj_bwd(
        dh1_b, o, g_attn_out, w_out_f, tm, _pair_exchange([part[k] for k in later_names]))
    pair = [_pair_sum(part[k], r, core, "pair_sum_" + k) for k, r in zip(later_names, from_sibling)]
    dconv, g_conv_w, gg_conv = _conv_bwd(proj, dcn, conv_f, g_conv_out)
    (dq, dk, dv), from_chips = _attn_bwd(proj, kp, vp, kt, do, tot, _chip_exchange(pair))
    dk, dv = dk.reshape(s, W_ATTN), dv.reshape(s, W_ATTN)

    part["w_in"] = _grad_w_in(a_b, dconv, dq, dk, dv, min(MLP_TOKEN_TILE, s))
    in_sibling = _run_exchange(_pair_exchange([part["w_in"]]), "grad_pair_exchange_w_in")
    in_pair = _pair_sum(part["w_in"], in_sibling[0], core, "pair_sum_w_in")
    (grad_x, gg_mix), in_chips = _in_proj_bwd(
        dconv, dq, dk, dv, dh1, xs, g_mix, w_in_f, tm, _chip_exchange([in_pair]))

    place = jnp.stack([lax.axis_index("c"), chip]).astype(jnp.int32)
    half = [_chip_sum(mine, landed, place, "chip_sum_" + k)
            for k, mine, landed in zip(names, [in_pair] + pair, list(in_chips) + list(from_chips))]
    both = _sibling_exchange(half)
    grad = {k: b.reshape(big[k].shape) for k, b in zip(names, both)}

    gcw = g_conv_w[:, :3, :].transpose(1, 0, 2).reshape(3, W_CONV)
    row = lambda *parts: jnp.concatenate(parts, axis=1)
    packed = jnp.concatenate([
        gg_mix, gg_mlp, gg_ple, gg_final, row(gg_conv, gg_attn), row(gcw[0:1], gcw[1:2]),
        row(gcw[2:3], loss_row, jnp.zeros((1, W_CONV - LANES), f32)), jnp.zeros((1, D_MODEL), f32)], axis=0)
    summed = _allreduce_small(packed)
    loss = summed[6, W_CONV]
    gcw_full = jnp.stack([summed[5, :W_CONV], summed[5, W_CONV:], summed[6, :W_CONV]])
    grad["conv_w"] = lax.dynamic_slice(gcw_full, (0, chip * LANES), (3, LANES))
    vec_names = ["g_mix", "g_mlp", "g_ple", "g_final"]
    vec_w = {"g_mix": g_mix, "g_mlp": g_mlp, "g_ple": g_ple, "g_final": g_final.reshape(1, D_MODEL)}
    vec_m = {"g_mix": m_g_mix, "g_mlp": m_g_mlp, "g_ple": m_g_ple, "g_final": m_g_final.reshape(1, D_MODEL)}
    vec_v = {"g_mix": v_g_mix, "g_mlp": v_g_mlp, "g_ple": v_g_ple, "g_final": v_g_final.reshape(1, D_MODEL)}

    def pack_vec(d, conv, attn):
        return jnp.concatenate([d[k] for k in vec_names] + [row(conv, attn)], axis=0)

    vec_g = summed[0:5]
    vec_d, vec_nm, vec_nv = _adamw(vec_g, pack_vec(vec_w, g_conv_out, g_attn_out),
                                   pack_vec(vec_m, m_g_conv_out, m_g_attn_out),
                                   pack_vec(vec_v, v_g_conv_out, v_g_attn_out), "adamw_vectors")

    given_w = dict(big, conv_w=conv_w[0])
    given_m = {"w_in": m_w_in[0], "w_out": m_w_out[0], "w_up": m_w_up[0], "w_down": m_w_down[0],
               "w_ple_gate": m_w_ple_gate[0], "w_ple_proj": m_w_ple_proj[0], "conv_w": m_conv_w[0]}
    given_v = {"w_in": v_w_in[0], "w_out": v_w_out[0], "w_up": v_w_up[0], "w_down": v_w_down[0],
               "w_ple_gate": v_w_ple_gate[0], "w_ple_proj": v_w_ple_proj[0], "conv_w": v_conv_w[0]}
    delta, new_m, new_v = {}, {}, {}
    for k in names + ["conv_w"]:
        delta[k], new_m[k], new_v[k] = _adamw(grad[k], given_w[k], given_m[k], given_v[k], "adamw_" + k)

    def unpack(vals, kind):
        out = {k: vals[i:i + 1] for i, k in enumerate(vec_names)}
        out["g_final"] = out["g_final"].reshape(D_MODEL)
        out["g_conv_out"] = vals[4:5, :W_CONV]
        out["g_attn_out"] = vals[4:5, W_CONV:]
        out.update({k: v[None] for k, v in kind.items()})
        return out

    order = ["g_mix", "w_in", "conv_w", "g_conv_out", "g_attn_out", "w_out", "g_mlp", "w_up", "w_down",
             "g_ple", "w_ple_gate", "w_ple_proj", "g_final"]
    groups = [unpack(vec_g, grad), unpack(vec_d, delta), unpack(vec_nm, new_m), unpack(vec_nv, new_v)]
    return (loss, grad_x[None]) + tuple(g[k] for g in groups for k in order)
```

```python
import functools

import jax
import jax.numpy as jnp
from jax import lax
from jax.experimental import pallas as pl
from jax.experimental.pallas import tpu as pltpu

f32 = jnp.float32
bf16 = jnp.bfloat16

D_MODEL = 1024
HEAD_DIM = 64
W_CONV = 512
W_ATTN = 512
D_FF = 4096
PLE_DIM = 256
IN_COLS = 3 * W_CONV + 3 * W_ATTN
N_CHIPS = 4
EPS = 1e-6
ADAM_LR = 0.001
ADAM_B1 = 0.9
ADAM_B2 = 0.999
ADAM_EPS = 1e-08
ADAM_WD = 0.01
ADAM_STEP = 10

LANES = 128
TOKEN_TILE = 512
MLP_TOKEN_TILE = 1024
GRAD_TOKEN_TILE = 2048
ATTN_TILE = 256
CONV_CHUNK = 512
CONV_W_ROWS = 16

MESH = pl.DeviceIdType.MESH
ANY = pl.BlockSpec(memory_space=pl.ANY)
NT = (((1,), (1,)), ((), ()))
TN = (((0,), (0,)), ((), ()))


def _arb(n):
    return pltpu.CompilerParams(dimension_semantics=("arbitrary",) * n)


def _sds(shape, dtype):
    return jax.ShapeDtypeStruct(shape, dtype)


def _dot(a, b, dims=None):
    if dims is None:
        return jnp.dot(a, b, preferred_element_type=f32)
    return lax.dot_general(a, b, dims, preferred_element_type=f32)


def _split_dot(x, ones):
    hi = x.astype(bf16)
    lo = (x - hi.astype(f32)).astype(bf16)
    return _dot(hi, ones) + _dot(lo, ones)


def _rms_fwd(h, g):
    rstd = lax.rsqrt(jnp.mean(h * h, axis=-1, keepdims=True) + EPS)
    return h * rstd * g, rstd


def _rms_bwd(dy, h, g):
    rstd = lax.rsqrt(jnp.mean(h * h, axis=-1, keepdims=True) + EPS)
    hn = h * rstd
    dyg = dy * g
    dh = rstd * (dyg - hn * jnp.mean(dyg * hn, axis=-1, keepdims=True))
    return dh, jnp.sum(dy * hn, axis=0, keepdims=True)


def _group_ones(n):
    r = lax.broadcasted_iota(jnp.int32, (n, n), 0) // HEAD_DIM
    c = lax.broadcasted_iota(jnp.int32, (n, n), 1) // HEAD_DIM
    return (r == c).astype(bf16)


def _head_rms_fwd(y, g, ones):
    rstd = lax.rsqrt(_split_dot(y * y, ones) * (1.0 / HEAD_DIM) + EPS)
    return y * rstd * g


def _head_rms_bwd(dy, y, g, ones):
    rstd = lax.rsqrt(_split_dot(y * y, ones) * (1.0 / HEAD_DIM) + EPS)
    yn = y * rstd
    dyg = dy * g
    dyy = rstd * (dyg - yn * (_split_dot(dyg * yn, ones) * (1.0 / HEAD_DIM)))
    return dyy, jnp.sum(dy * yn, axis=0, keepdims=True)


def _place():
    return lax.axis_index("x"), lax.axis_index("y"), lax.axis_index("c")


def _other_chips(x, y):
    return [(1 - x, y), (x, 1 - y), (1 - x, 1 - y)]


class _Exchange:
    def __init__(self, arrays, out_shapes, sems, start, finish, relay=None):
        self.arrays, self.out_shapes, self.sems, self.start, self.finish = arrays, out_shapes, sems, start, finish
        self.relay = relay


def _gather_exchange(shards):
    n = len(shards)
    halves = [s.shape[0] // 2 for s in shards]

    def plan(ins, outs, sems):
        send_sems, recv_sems, own_sems = sems
        x, y, c = _place()
        me = 2 * x + y
        chips = _other_chips(x, y)

        def half(ref, i, which):
            return ref.at[pl.ds(which * halves[i], halves[i]), :]

        def over_ici(i, j, src, slot, to):
            return pltpu.make_async_remote_copy(
                src_ref=src, dst_ref=half(outs[i].at[slot], i, c),
                send_sem=send_sems.at[3 * i + j], recv_sem=recv_sems.at[3 * i + j],
                device_id=to, device_id_type=MESH)

        def to_sibling(i, j, slot, which):
            blk = half(outs[i].at[slot], i, which)
            return pltpu.make_async_remote_copy(
                src_ref=blk, dst_ref=blk,
                send_sem=send_sems.at[3 * n + 3 * i + j], recv_sem=recv_sems.at[3 * n + 3 * i + j],
                device_id=(x, y, 1 - c), device_id_type=MESH)

        own = [pltpu.make_async_remote_copy(
            src_ref=ins[i], dst_ref=outs[i].at[me], send_sem=own_sems.at[i], recv_sem=own_sems.at[n + i],
            device_id=(x, y, 1 - c), device_id_type=MESH) for i in range(n)]
        pairs = [(i, j, px, py) for i in range(n) for j, (px, py) in enumerate(chips)]
        sends = [over_ici(i, j, half(ins[i], i, c), me, (px, py, c)) for i, j, px, py in pairs]
        lands = [over_ici(i, j, half(outs[i].at[2 * px + py], i, c), 2 * px + py, (px, py, c)) for i, j, px, py in pairs]
        passes = [to_sibling(i, j, 2 * px + py, c) for i, j, px, py in pairs]
        from_sibling = [to_sibling(i, j, 2 * px + py, 1 - c) for i, j, px, py in pairs]
        return own, sends, lands, passes, from_sibling

    def start(ins, outs, sems):
        own, sends, _, _, _ = plan(ins, outs, sems)
        for cp in own + sends:
            cp.start()

    def relay(ins, outs, sems):
        _, _, lands, passes, _ = plan(ins, outs, sems)
        for land, on in zip(lands, passes):
            land.wait_recv()
            on.start()

    def finish(ins, outs, sems):
        own, sends, _, passes, from_sibling = plan(ins, outs, sems)
        for cp in from_sibling:
            cp.wait_recv()
        for cp in sends + passes:
            cp.wait_send()
        for cp in own:
            cp.wait()

    return _Exchange(
        shards, [_sds((N_CHIPS,) + s.shape, s.dtype) for s in shards],
        [pltpu.SemaphoreType.DMA((6 * n,)), pltpu.SemaphoreType.DMA((6 * n,)), pltpu.SemaphoreType.DMA((2 * n,))],
        start, finish, relay)


def _call_carrying(ex, body, name, grid, in_specs, out_specs, out_shape, scratch_shapes, args):
    n_in, n_out, n_scr = len(in_specs), len(out_specs), len(scratch_shapes)
    k = 0 if ex is None else len(ex.arrays)

    def wrapped(*refs):
        ins, xin = refs[:n_in], refs[n_in:n_in + k]
        outs, xout = refs[n_in + k:n_in + k + n_out], refs[n_in + k + n_out:n_in + 2 * k + n_out]
        scr, sems = refs[n_in + 2 * k + n_out:n_in + 2 * k + n_out + n_scr], refs[n_in + 2 * k + n_out + n_scr:]
        ids = [pl.program_id(d) for d in range(len(grid))]
        if ex is not None:
            @pl.when(functools.reduce(lambda a, b: a & b, [i == 0 for i in ids]))
            def _():
                ex.start(xin, xout, sems)

        if ex is not None and ex.relay is not None:
            relay_at = [grid[0] - 1] + [0] * (len(grid) - 1) if len(grid) > 1 else [grid[0] - 1]

            @pl.when(functools.reduce(lambda a, b: a & b, [i == r for i, r in zip(ids, relay_at)]))
            def _():
                ex.relay(xin, xout, sems)

        body(*ins, *outs, *scr)
        if ex is not None:
            @pl.when(functools.reduce(lambda a, b: a & b, [i == g - 1 for i, g in zip(ids, grid)]))
            def _():
                ex.finish(xin, xout, sems)

    res = pl.pallas_call(
        wrapped, name=name, grid=grid,
        in_specs=list(in_specs) + [ANY] * k, out_specs=list(out_specs) + [ANY] * k,
        out_shape=list(out_shape) + ([] if ex is None else list(ex.out_shapes)),
        scratch_shapes=list(scratch_shapes) + ([] if ex is None else list(ex.sems)),
        compiler_params=_arb(len(grid)),
    )(*args, *([] if ex is None else ex.arrays))
    return res[:n_out], res[n_out:]


def _run_exchange(ex, name):
    n = len(ex.arrays)

    def body(*refs):
        ins, outs, sems = refs[:n], refs[n:2 * n], refs[2 * n:]
        ex.start(ins, outs, sems)
        if ex.relay is not None:
            ex.relay(ins, outs, sems)
        ex.finish(ins, outs, sems)

    return pl.pallas_call(
        body, name=name, out_shape=ex.out_shapes, in_specs=[ANY] * n, out_specs=[ANY] * n,
        scratch_shapes=ex.sems,
    )(*ex.arrays)


def _pair_exchange(grads):
    n = len(grads)

    def plan(ins, outs, sems):
        send_sems, recv_sems = sems
        x, y, c = _place()
        return [pltpu.make_async_remote_copy(
            src_ref=ins[i].at[:, 1 - c], dst_ref=outs[i],
            send_sem=send_sems.at[i], recv_sem=recv_sems.at[i],
            device_id=(x, y, 1 - c), device_id_type=MESH) for i in range(n)]

    def start(ins, outs, sems):
        for cp in plan(ins, outs, sems):
            cp.start()

    def finish(ins, outs, sems):
        for cp in plan(ins, outs, sems):
            cp.wait()

    views = [g.reshape(N_CHIPS, 2, g.shape[1] // 2, g.shape[2]) for g in grads]
    return _Exchange(
        views, [_sds((N_CHIPS, v.shape[2], v.shape[3]), f32) for v in views],
        [pltpu.SemaphoreType.DMA((n,)), pltpu.SemaphoreType.DMA((n,))], start, finish)


def _chip_exchange(parts):
    n = len(parts)

    def plan(ins, outs, sems):
        send_sems, recv_sems = sems
        x, y, c = _place()
        me = 2 * x + y
        pairs = [(i, j, px, py) for i in range(n) for j, (px, py) in enumerate(_other_chips(x, y))]

        def copy(i, j, src, slot, px, py):
            return pltpu.make_async_remote_copy(
                src_ref=src, dst_ref=outs[i].at[slot],
                send_sem=send_sems.at[3 * i + j], recv_sem=recv_sems.at[3 * i + j],
                device_id=(px, py, c), device_id_type=MESH)

        sends = [copy(i, j, ins[i].at[2 * px + py], me, px, py) for i, j, px, py in pairs]
        lands = [copy(i, j, outs[i].at[2 * px + py], 2 * px + py, px, py) for i, j, px, py in pairs]
        return sends, lands

    def start(ins, outs, sems):
        sends, _ = plan(ins, outs, sems)
        for cp in sends:
            cp.start()

    def finish(ins, outs, sems):
        sends, lands = plan(ins, outs, sems)
        for cp in lands:
            cp.wait_recv()
        for cp in sends:
            cp.wait_send()

    return _Exchange(
        parts, [_sds(p.shape, p.dtype) for p in parts],
        [pltpu.SemaphoreType.DMA((3 * n,)), pltpu.SemaphoreType.DMA((3 * n,))],
        start, finish)


def _sibling_exchange(both):
    n = len(both)

    def body(*refs):
        outs = refs[n:2 * n]
        send_sems, recv_sems = refs[2 * n:]
        x, y, c = _place()
        sent = []
        for i in range(n):
            cp = pltpu.make_async_remote_copy(
                src_ref=outs[i].at[c], dst_ref=outs[i].at[c],
                send_sem=send_sems.at[i], recv_sem=recv_sems.at[i],
                device_id=(x, y, 1 - c), device_id_type=MESH)
            cp.start()
            sent.append(cp)
        for cp in sent:
            cp.wait()

    return pl.pallas_call(
        body, name="grad_sibling_exchange",
        out_shape=[_sds(b.shape, f32) for b in both],
        in_specs=[ANY] * n, out_specs=[ANY] * n,
        input_output_aliases={i: i for i in range(n)},
        scratch_shapes=[pltpu.SemaphoreType.DMA((n,)), pltpu.SemaphoreType.DMA((n,))],
    )(*both)


def _row_tile(rows, cols):
    t = rows
    while t * cols * 4 > (2 << 20) and t % 16 == 0:
        t //= 2
    return t


def _pair_sum(grad, recv, core, name):
    _, r, c = grad.shape
    hr = r // 2
    tr = _row_tile(hr, c)
    view = grad.reshape(N_CHIPS, 2, hr, c)

    def body(core_ref, mine_ref, recv_ref, out_ref):
        out_ref[...] = (mine_ref[...] + recv_ref[...]).astype(bf16)

    return pl.pallas_call(
        body, name=name,
        grid_spec=pltpu.PrefetchScalarGridSpec(
            num_scalar_prefetch=1, grid=(N_CHIPS, hr // tr),
            in_specs=[pl.BlockSpec((None, None, tr, c), lambda k, t, core_ref: (k, core_ref[0], t, 0)),
                      pl.BlockSpec((None, tr, c), lambda k, t, core_ref: (k, t, 0))],
            out_specs=pl.BlockSpec((None, tr, c), lambda k, t, core_ref: (k, t, 0))),
        out_shape=_sds((N_CHIPS, hr, c), bf16),
        compiler_params=_arb(2),
    )(core, view, recv)


def _chip_sum(mine, landed, place, name):
    _, hr, c = mine.shape
    tr = _row_tile(hr, c)

    def body(place_ref, a_ref, b_ref, c_ref, d_ref, out_ref):
        out_ref[...] = ((a_ref[...].astype(f32) + b_ref[...].astype(f32)) + c_ref[...].astype(f32)) + d_ref[...].astype(f32)

    def slot(k):
        return pl.BlockSpec((None, tr, c), lambda t, place_ref: ((place_ref[1] + k) % N_CHIPS, t, 0))

    return pl.pallas_call(
        body, name=name,
        grid_spec=pltpu.PrefetchScalarGridSpec(
            num_scalar_prefetch=1, grid=(hr // tr,),
            in_specs=[slot(0), slot(1), slot(2), slot(3)],
            out_specs=pl.BlockSpec((None, tr, c), lambda t, place_ref: (place_ref[0], t, 0))),
        out_shape=_sds((2, hr, c), f32), compiler_params=_arb(1),
    )(place, mine, landed, landed, landed)


def _adamw(g, w, m, v, name):
    r, c = g.shape
    tr = _row_tile(r, c)

    def body(g_ref, w_ref, m_ref, v_ref, d_ref, nm_ref, nv_ref):
        gv = g_ref[...]
        mv = ADAM_B1 * m_ref[...] + (1.0 - ADAM_B1) * gv
        vv = ADAM_B2 * v_ref[...] + (1.0 - ADAM_B2) * jnp.square(gv)
        m_hat = mv / (1.0 - ADAM_B1 ** ADAM_STEP)
        v_hat = vv / (1.0 - ADAM_B2 ** ADAM_STEP)
        d_ref[...] = -ADAM_LR * (m_hat / (jnp.sqrt(v_hat) + ADAM_EPS) + ADAM_WD * w_ref[...])
        nm_ref[...] = mv
        nv_ref[...] = vv

    spec = pl.BlockSpec((tr, c), lambda t: (t, 0))
    return pl.pallas_call(
        body, name=name, grid=(r // tr,), in_specs=[spec] * 4, out_specs=[spec] * 3,
        out_shape=[_sds((r, c), f32)] * 3, compiler_params=_arb(1),
    )(g, w, m, v)


def _allreduce_small(packed):
    shape = packed.shape

    def body(x_ref, out_ref, buf, send_sems, recv_sems):
        x, y, c = _place()
        me = 4 * x + 2 * y + c
        buf[me] = x_ref[...]
        sent = []
        for r in range(1, 8):
            dx, dy, dc = (r >> 2) & 1, (r >> 1) & 1, r & 1
            peer = ((1 - x) if dx else x, (1 - y) if dy else y, (1 - c) if dc else c)
            cp = pltpu.make_async_remote_copy(
                src_ref=x_ref, dst_ref=buf.at[me],
                send_sem=send_sems.at[r], recv_sem=recv_sems.at[r],
                device_id=peer, device_id_type=MESH)
            cp.start()
            sent.append((cp, peer))
        for r, (cp, peer) in enumerate(sent, start=1):
            src = 4 * peer[0] + 2 * peer[1] + peer[2]
            pltpu.make_async_remote_copy(
                src_ref=x_ref, dst_ref=buf.at[src],
                send_sem=send_sems.at[r], recv_sem=recv_sems.at[r],
                device_id=peer, device_id_type=MESH).wait_recv()
        for cp, _ in sent:
            cp.wait_send()
        total = buf[0]
        for k in range(1, 8):
            total = total + buf[k]
        out_ref[...] = total

    vmem = pl.BlockSpec(memory_space=pltpu.VMEM)
    return pl.pallas_call(
        body, name="allreduce_small", out_shape=_sds(shape, f32),
        in_specs=[vmem], out_specs=vmem,
        scratch_shapes=[pltpu.VMEM((8,) + shape, f32), pltpu.SemaphoreType.DMA((8,)),
                        pltpu.SemaphoreType.DMA((8,))],
    )(packed)


def _in_proj(x, g_mix, w_in, tm):
    s = x.shape[0]
    ncol = IN_COLS // N_CHIPS
    t = ATTN_TILE
    nb = tm // t
    koff = 3 * W_CONV + W_ATTN

    def body(x_ref, g_ref, w_ref, proj_ref, a_ref, kp_ref, vp_ref, kt_ref, vt_ref, perm, kv):
        a, _ = _rms_fwd(x_ref[...], g_ref[...])
        ab = a.astype(bf16)
        a_ref[...] = ab
        for k in range(N_CHIPS):
            proj_ref[:, k * ncol:(k + 1) * ncol] = _dot(ab, w_ref[k])
        for j in range(2 * W_ATTN // LANES):
            kv[j] = proj_ref[:, koff + j * LANES:koff + (j + 1) * LANES]
        for b in range(nb):
            for r in range(KEY_RUN):
                for j in range(2 * W_ATTN // LANES):
                    perm[SUBLANES * r:SUBLANES * (r + 1), j * LANES:(j + 1) * LANES] = kv[
                        j, pl.ds(b * t + r, SUBLANES, stride=KEY_RUN), :]
            kp_ref[b] = perm[:, :W_ATTN].astype(bf16)
            vp_ref[b] = perm[:, W_ATTN:].astype(bf16)
            kt_ref[b] = perm[:, :W_ATTN].T.astype(bf16)
            vt_ref[b] = perm[:, W_ATTN:].T.astype(bf16)

    keys = pl.BlockSpec((nb, t, W_ATTN), lambda i: (i, 0, 0))
    keys_t = pl.BlockSpec((nb, W_ATTN, t), lambda i: (i, 0, 0))
    return pl.pallas_call(
        body, name="in_proj", grid=(s // tm,),
        in_specs=[pl.BlockSpec((tm, D_MODEL), lambda i: (i, 0)),
                  pl.BlockSpec((1, D_MODEL), lambda i: (0, 0)),
                  pl.BlockSpec((N_CHIPS, D_MODEL, ncol), lambda i: (0, 0, 0))],
        out_specs=[pl.BlockSpec((tm, IN_COLS), lambda i: (i, 0)),
                   pl.BlockSpec((tm, D_MODEL), lambda i: (i, 0)), keys, keys, keys_t, keys_t],
        out_shape=[_sds((s, IN_COLS), f32), _sds((s, D_MODEL), bf16),
                   _sds((s // t, t, W_ATTN), bf16), _sds((s // t, t, W_ATTN), bf16),
                   _sds((s // t, W_ATTN, t), bf16), _sds((s // t, W_ATTN, t), bf16)],
        scratch_shapes=[pltpu.VMEM((t, 2 * W_ATTN), f32), pltpu.VMEM((2 * W_ATTN // LANES, tm, LANES), f32)],
        compiler_params=_arb(1),
    )(x, g_mix, w_in)


def _shifted(prev8, cur, shift):
    ext = jnp.concatenate([prev8, cur], axis=0)
    return pltpu.roll(ext, shift, axis=0)[8:]


def _conv_fwd(proj, conv_w, g_conv):
    s = proj.shape[0]
    nblk = W_CONV // LANES
    rc = min(CONV_CHUNK, s)

    def body(cb_ref, cc_ref, cu_ref, w_ref, g_ref, out_ref):
        ones = _group_ones(LANES)
        w0, w1, w2 = w_ref[0:1, :], w_ref[1:2, :], w_ref[2:3, :]
        g = g_ref[...]

        def chunk(i, carry):
            r0 = pl.multiple_of(i * rc, rc)
            rows = pl.ds(r0, rc)
            prev = pl.ds(pl.multiple_of(jnp.maximum(r0 - 8, 0), 8), 8)
            v = cc_ref[rows, :] * cu_ref[rows, :]
            vp = jnp.where(i > 0, cc_ref[prev, :] * cu_ref[prev, :], 0.0)
            y = w2 * v + w1 * _shifted(vp, v, 1) + w0 * _shifted(vp, v, 2)
            out_ref[rows, :] = _head_rms_fwd(cb_ref[rows, :] * y, g, ones).astype(bf16)
            return carry

        lax.fori_loop(0, s // rc, chunk, 0)

    def col(off):
        return pl.BlockSpec((s, LANES), lambda j: (0, off + j))

    return pl.pallas_call(
        body, name="conv_fwd", grid=(nblk,),
        in_specs=[col(0), col(nblk), col(2 * nblk),
                  pl.BlockSpec((None, CONV_W_ROWS, LANES), lambda j: (j, 0, 0)),
                  pl.BlockSpec((1, LANES), lambda j: (0, j))],
        out_specs=pl.BlockSpec((s, LANES), lambda j: (0, j)),
        out_shape=_sds((s, W_CONV), bf16), compiler_params=_arb(1),
    )(proj, proj, proj, conv_w, g_conv)


LOG2_E = 1.4426950408889634


def _log2_keep(z2):
    nz2 = -z2
    return jnp.minimum(nz2, 0.0) - jnp.log2(1.0 + jnp.exp2(jnp.minimum(z2, nz2)))


def _head_pair_masks(rows):
    lane = lax.broadcasted_iota(jnp.int32, (rows, LANES), 1)
    return lane < HEAD_DIM


SUBLANES = 8
KEY_RUN = ATTN_TILE // SUBLANES


def _causal_tiles():
    r = lax.broadcasted_iota(jnp.int32, (ATTN_TILE, ATTN_TILE), 0)
    key = (r % SUBLANES) * KEY_RUN + r // SUBLANES
    below = key < lax.broadcasted_iota(jnp.int32, (ATTN_TILE, ATTN_TILE), 1)
    return below.astype(f32), jnp.where(below, 0.0, -1e30).astype(f32)


def _sublane_scan(x, reverse):
    row = lax.broadcasted_iota(jnp.int32, x.shape, 0)
    inc = x
    for sh in (1, 2, 4):
        if reverse:
            inc = inc + jnp.where(row < SUBLANES - sh, pltpu.roll(inc, SUBLANES - sh, axis=0), 0.0)
        else:
            inc = inc + jnp.where(row >= sh, pltpu.roll(inc, sh, axis=0), 0.0)
    return inc - x


def _attn_fwd(proj, kp, vt, ex=None):
    s = proj.shape[0]
    t = ATTN_TILE
    nblk = s // t
    npair = W_ATTN // LANES
    qoff = 3 * W_CONV // LANES
    keep01, keepneg = _causal_tiles()

    def body(q_ref, k_ref, vt_ref, m01_ref, neg_ref, o_ref, tot_ref, w_s, a_s, acc):
        qb = pl.program_id(1)
        first = _head_pair_masks(t)
        q = q_ref[...] * (HEAD_DIM ** -0.5)
        qh = (jnp.where(first, q, 0.0).astype(bf16), jnp.where(first, 0.0, q).astype(bf16))
        acc[...] = jnp.zeros_like(acc)
        a_s[1] = jnp.zeros((t, t), bf16)

        def scores(kb, h):
            w_s[h] = _dot(k_ref[kb], qh[h], NT)

        def weigh(kb, h):
            acc[h] += _dot(vt_ref[kb], a_s[h])

        def weights(h, diagonal, later):
            run = jnp.zeros((SUBLANES, t), f32)
            for a in reversed(range(KEY_RUN)):
                rows = slice(SUBLANES * a, SUBLANES * (a + 1))
                z2 = w_s[h, rows, :] * LOG2_E
                lk = _log2_keep(z2)
                if diagonal:
                    lk = lk * m01_ref[rows, :]
                run = run + lk
                w_s[h, rows, :] = z2 + run
            off = _sublane_scan(run, reverse=True) + later
            off2 = jnp.concatenate([off, off], axis=0)
            for a in range(t // (2 * SUBLANES)):
                rows = slice(2 * SUBLANES * a, 2 * SUBLANES * (a + 1))
                w = w_s[h, rows, :] + off2
                if diagonal:
                    w = w + neg_ref[rows, :]
                a_s[h, rows, :] = jnp.exp2(w).astype(bf16)
            return later + jnp.sum(run, axis=0, keepdims=True)

        def block(kb, before, after, diagonal, later):
            scores(kb, 1)
            weigh(before, 1)
            l0 = weights(0, diagonal, later[0])
            scores(after, 0)
            weigh(kb, 0)
            l1 = weights(1, diagonal, later[1])
            return l0, l1

        zero = jnp.zeros((1, t), f32)
        scores(qb, 0)
        later = block(qb, qb, jnp.maximum(qb - 1, 0), True, (zero, zero))

        def earlier(i, c):
            kb = qb - 1 - i
            return block(kb, kb + 1, jnp.maximum(kb - 1, 0), False, c)

        later = lax.fori_loop(0, qb, earlier, later)
        weigh(0, 1)
        top = lax.broadcasted_iota(jnp.int32, (LANES, t), 0) < HEAD_DIM
        o_ref[...] = jnp.where(top, acc[0], acc[1]).T
        tot_ref[...] = jnp.concatenate([later[0], later[1], jnp.zeros((SUBLANES - 2, t), f32)], axis=0)

    return _call_carrying(
        ex, body, "attn_fwd", (npair, nblk),
        in_specs=[pl.BlockSpec((t, LANES), lambda p, i: (i, qoff + p)),
                  pl.BlockSpec((nblk, t, LANES), lambda p, i: (0, 0, p)),
                  pl.BlockSpec((nblk, LANES, t), lambda p, i: (0, p, 0)),
                  pl.BlockSpec((t, t), lambda p, i: (0, 0)),
                  pl.BlockSpec((t, t), lambda p, i: (0, 0))],
        out_specs=[pl.BlockSpec((t, LANES), lambda p, i: (i, p)),
                   pl.BlockSpec((None, SUBLANES, t), lambda p, i: (p, 0, i))],
        out_shape=[_sds((s, W_ATTN), f32), _sds((npair, SUBLANES, s), f32)],
        scratch_shapes=[pltpu.VMEM((2, t, t), f32), pltpu.VMEM((2, t, t), bf16), pltpu.VMEM((2, LANES, t), f32)],
        args=(proj, kp, vt, keep01, keepneg))


def _out_proj(o, conv_n, x, g_attn, w_out, tm):
    s = x.shape[0]

    def body(o_ref, c_ref, x_ref, g_ref, w_ref, h_ref, cat_ref):
        ones = _group_ones(LANES)
        cat_ref[:, :W_CONV] = c_ref[...]
        for j in range(W_ATTN // LANES):
            cols = slice(j * LANES, (j + 1) * LANES)
            cat_ref[:, W_CONV + j * LANES:W_CONV + (j + 1) * LANES] = _head_rms_fwd(
                o_ref[:, cols], g_ref[:, cols], ones).astype(bf16)
        h_ref[...] = x_ref[...] + _dot(cat_ref[...], w_ref[...])

    return pl.pallas_call(
        body, name="out_proj", grid=(s // tm,),
        in_specs=[pl.BlockSpec((tm, W_ATTN), lambda i: (i, 0)),
                  pl.BlockSpec((tm, W_CONV), lambda i: (i, 0)),
                  pl.BlockSpec((tm, D_MODEL), lambda i: (i, 0)),
                  pl.BlockSpec((1, W_ATTN), lambda i: (0, 0)),
                  pl.BlockSpec((D_MODEL, D_MODEL), lambda i: (0, 0))],
        out_specs=[pl.BlockSpec((tm, D_MODEL), lambda i: (i, 0)),
                   pl.BlockSpec((tm, D_MODEL), lambda i: (i, 0))],
        out_shape=[_sds((s, D_MODEL), f32), _sds((s, D_MODEL), bf16)],
        compiler_params=_arb(1),
    )(o, conv_n, x, g_attn, w_out)


def _mlp_fwd(h1, g_mlp, w_up, w_down, tm):
    s = h1.shape[0]
    fc = D_FF // N_CHIPS

    def body(h_ref, g_ref, wu_ref, wd_ref, h2_ref, u_ref, m_ref):
        j = pl.program_id(1)

        @pl.when(j == 0)
        def _():
            m, _ = _rms_fwd(h_ref[...], g_ref[...])
            m_ref[...] = m.astype(bf16)
            h2_ref[...] = h_ref[...]

        u = _dot(m_ref[...], wu_ref[...])
        u_ref[...] = u.astype(bf16)
        h2_ref[...] += _dot(jnp.square(jnp.maximum(u, 0.0)).astype(bf16), wd_ref[...])

    return pl.pallas_call(
        body, name="mlp_fwd", grid=(s // tm, N_CHIPS),
        in_specs=[pl.BlockSpec((tm, D_MODEL), lambda i, j: (i, 0)),
                  pl.BlockSpec((1, D_MODEL), lambda i, j: (0, 0)),
                  pl.BlockSpec((None, D_MODEL, fc), lambda i, j: (j, 0, 0)),
                  pl.BlockSpec((None, fc, D_MODEL), lambda i, j: (j, 0, 0))],
        out_specs=[pl.BlockSpec((tm, D_MODEL), lambda i, j: (i, 0)),
                   pl.BlockSpec((tm, fc), lambda i, j: (i, j)),
                   pl.BlockSpec((tm, D_MODEL), lambda i, j: (i, 0))],
        out_shape=[_sds((s, D_MODEL), f32), _sds((s, D_FF), bf16), _sds((s, D_MODEL), bf16)],
        compiler_params=_arb(2),
    )(h1, g_mlp, w_up, w_down)


def _tail(h2, p, target, g_ple, g_final, w_gate, w_proj, tm):
    s = h2.shape[0]
    pc = D_MODEL // N_CHIPS

    def body(h_ref, p_ref, t_ref, gp_ref, gf_ref, wg_ref, wp_ref,
             dh_ref, dhb_ref, n3_ref, dgl_ref, dpp_ref, pb_ref, ggp_ref, ggf_ref, loss_ref, pp_ref):
        i = pl.program_id(0)
        h2v = h_ref[...]
        n3, _ = _rms_fwd(h2v, gp_ref[...])
        n3b = n3.astype(bf16)
        n3_ref[...] = n3b
        gate = jax.nn.sigmoid(_dot(n3b, wg_ref[...]))
        pb = p_ref[...].astype(bf16)
        pb_ref[...] = pb
        for k in range(N_CHIPS):
            pp_ref[:, k * pc:(k + 1) * pc] = _dot(pb, wp_ref[k])
        pp = pp_ref[...]
        h3 = h2v + gate * pp
        yv, _ = _rms_fwd(h3, gf_ref[...])
        err = yv - t_ref[...]
        loss = 0.5 * jnp.sum(err * err) * (1.0 / D_MODEL)
        dh3, ggf = _rms_bwd(err * (1.0 / D_MODEL), h3, gf_ref[...])
        dpp_ref[...] = (dh3 * gate).astype(bf16)
        dgl = (dh3 * pp * gate * (1.0 - gate)).astype(bf16)
        dgl_ref[...] = dgl
        dn3 = _dot(dgl, wg_ref[...], NT)
        dh2n, ggp = _rms_bwd(dn3, h2v, gp_ref[...])
        dh2 = dh3 + dh2n
        dh_ref[...] = dh2
        dhb_ref[...] = dh2.astype(bf16)

        @pl.when(i == 0)
        def _():
            ggp_ref[...] = jnp.zeros_like(ggp_ref)
            ggf_ref[...] = jnp.zeros_like(ggf_ref)
            loss_ref[...] = jnp.zeros_like(loss_ref)

        ggp_ref[...] += ggp
        ggf_ref[...] += ggf
        loss_ref[...] += jnp.full(loss_ref.shape, loss, f32)

    tok = lambda w: pl.BlockSpec((tm, w), lambda i: (i, 0))
    vec = lambda w: pl.BlockSpec((1, w), lambda i: (0, 0))
    return pl.pallas_call(
        body, name="tail", grid=(s // tm,),
        in_specs=[tok(D_MODEL), tok(PLE_DIM), tok(D_MODEL), vec(D_MODEL), vec(D_MODEL),
                  pl.BlockSpec((D_MODEL, D_MODEL), lambda i: (0, 0)),
                  pl.BlockSpec((N_CHIPS, PLE_DIM, pc), lambda i: (0, 0, 0))],
        out_specs=[tok(D_MODEL), tok(D_MODEL), tok(D_MODEL), tok(D_MODEL), tok(D_MODEL), tok(PLE_DIM),
                   vec(D_MODEL), vec(D_MODEL), vec(LANES)],
        out_shape=[_sds((s, D_MODEL), f32), _sds((s, D_MODEL), bf16), _sds((s, D_MODEL), bf16),
                   _sds((s, D_MODEL), bf16), _sds((s, D_MODEL), bf16), _sds((s, PLE_DIM), bf16),
                   _sds((1, D_MODEL), f32), _sds((1, D_MODEL), f32), _sds((1, LANES), f32)],
        scratch_shapes=[pltpu.VMEM((tm, D_MODEL), f32)],
        compiler_params=_arb(1),
    )(h2, p, target, g_ple, g_final, w_gate, w_proj)


def _mlp_bwd(dh2, dh2b, h1, u, g_mlp, w_up, w_down, tm):
    s = h1.shape[0]
    fc = D_FF // N_CHIPS

    def body(dh_ref, dhb_ref, h_ref, u_ref, g_ref, wu_ref, wd_ref, dh1_ref, dh1b_ref, du_ref, gg_ref, dm):
        i, j = pl.program_id(0), pl.program_id(1)

        @pl.when(j == 0)
        def _():
            dm[...] = jnp.zeros_like(dm)

        dr = _dot(dhb_ref[...], wd_ref[...], NT)
        du = (dr * (2.0 * jnp.maximum(u_ref[...].astype(f32), 0.0))).astype(bf16)
        du_ref[...] = du
        dm[...] += _dot(du, wu_ref[...], NT)

        @pl.when((i == 0) & (j == 0))
        def _():
            gg_ref[...] = jnp.zeros_like(gg_ref)

        @pl.when(j == N_CHIPS - 1)
        def _():
            dh1n, gg = _rms_bwd(dm[...], h_ref[...], g_ref[...])
            dh1 = dh_ref[...] + dh1n
            dh1_ref[...] = dh1
            dh1b_ref[...] = dh1.astype(bf16)
            gg_ref[...] += gg

    tok = pl.BlockSpec((tm, D_MODEL), lambda i, j: (i, 0))
    ffb = pl.BlockSpec((tm, fc), lambda i, j: (i, j))
    vec = pl.BlockSpec((1, D_MODEL), lambda i, j: (0, 0))
    return pl.pallas_call(
        body, name="mlp_bwd", grid=(s // tm, N_CHIPS),
        in_specs=[tok, tok, tok, ffb, vec,
                  pl.BlockSpec((None, D_MODEL, fc), lambda i, j: (j, 0, 0)),
                  pl.BlockSpec((None, fc, D_MODEL), lambda i, j: (j, 0, 0))],
        out_specs=[tok, tok, ffb, vec],
        out_shape=[_sds((s, D_MODEL), f32), _sds((s, D_MODEL), bf16), _sds((s, D_FF), bf16),
                   _sds((1, D_MODEL), f32)],
        scratch_shapes=[pltpu.VMEM((tm, D_MODEL), f32)],
        compiler_params=_arb(2),
    )(dh2, dh2b, h1, u, g_mlp, w_up, w_down)


def _out_proj_bwd(dh1b, o, g_attn, w_out, tm, ex=None):
    s = o.shape[0]

    def body(dh_ref, o_ref, g_ref, w_ref, dc_ref, do_ref, gg_ref, dcat):
        i = pl.program_id(0)
        ones = _group_ones(LANES)
        dcat[...] = _dot(dh_ref[...], w_ref[...], NT)
        dc_ref[...] = dcat[:, :W_CONV]

        @pl.when(i == 0)
        def _():
            gg_ref[...] = jnp.zeros_like(gg_ref)

        for j in range(W_ATTN // LANES):
            cols = slice(j * LANES, (j + 1) * LANES)
            d, gg = _head_rms_bwd(dcat[:, W_CONV + j * LANES:W_CONV + (j + 1) * LANES],
                                  o_ref[:, cols], g_ref[:, cols], ones)
            do_ref[:, cols] = d
            gg_ref[:, cols] += gg

    return _call_carrying(
        ex, body, "out_proj_bwd", (s // tm,),
        in_specs=[pl.BlockSpec((tm, D_MODEL), lambda i: (i, 0)),
                  pl.BlockSpec((tm, W_ATTN), lambda i: (i, 0)),
                  pl.BlockSpec((1, W_ATTN), lambda i: (0, 0)),
                  pl.BlockSpec((D_MODEL, D_MODEL), lambda i: (0, 0))],
        out_specs=[pl.BlockSpec((tm, W_CONV), lambda i: (i, 0)),
                   pl.BlockSpec((tm, W_ATTN), lambda i: (i, 0)),
                   pl.BlockSpec((1, W_ATTN), lambda i: (0, 0))],
        out_shape=[_sds((s, W_CONV), f32), _sds((s, W_ATTN), f32), _sds((1, W_ATTN), f32)],
        scratch_shapes=[pltpu.VMEM((tm, D_MODEL), f32)],
        args=(dh1b, o, g_attn, w_out))


def _attn_bwd(proj, kp, vp, kt, do, tot, ex=None):
    s = proj.shape[0]
    t = ATTN_TILE
    nblk = s // t
    npair = W_ATTN // LANES
    qoff = 3 * W_CONV // LANES
    keep01, keepneg = _causal_tiles()

    def body(q_ref, k_ref, v_ref, kt_ref, do_ref, tot_ref, m01_ref, neg_ref, dq_ref, dk_ref, dv_ref,
             w_s, da_s, b_s, g_s, a_s, dz_s, dq_acc, dk_acc, dv_acc):
        qb = pl.program_id(1)
        first = _head_pair_masks(t)
        q = q_ref[...] * (HEAD_DIM ** -0.5)
        qh = (jnp.where(first, q, 0.0).astype(bf16), jnp.where(first, 0.0, q).astype(bf16))
        dov = do_ref[...]
        doh = (jnp.where(first, dov, 0.0).astype(bf16), jnp.where(first, 0.0, dov).astype(bf16))
        total = (tot_ref[0:1, :], tot_ref[1:2, :])

        @pl.when(qb == 0)
        def _():
            dk_acc[...] = jnp.zeros_like(dk_acc)
            dv_acc[...] = jnp.zeros_like(dv_acc)

        dq_acc[...] = jnp.zeros_like(dq_acc)
        a_s[1] = jnp.zeros((t, t), bf16)
        dz_s[1] = jnp.zeros((t, t), bf16)

        def scores(kb, h):
            w_s[h] = _dot(k_ref[kb], qh[h], NT)
            da_s[h] = _dot(v_ref[kb], doh[h], NT)

        def spread(kb, h):
            dq_acc[h] += _dot(kt_ref[kb], dz_s[h])
            dk_acc[kb] += _dot(dz_s[h], qh[h])
            dv_acc[kb] += _dot(a_s[h], doh[h])

        def grads(h, diagonal, lk_before, g_before):
            run = jnp.zeros((SUBLANES, t), f32)
            for a in range(KEY_RUN):
                rows = slice(SUBLANES * a, SUBLANES * (a + 1))
                z2 = w_s[h, rows, :] * LOG2_E
                lk = _log2_keep(z2)
                if diagonal:
                    lk = lk * m01_ref[rows, :]
                log_beta = jnp.minimum(z2 + lk, 0.0)
                run = run + lk
                b_s[h, rows, :] = jnp.exp2(log_beta)
                w_s[h, rows, :] = log_beta - run
            off = total[h] - lk_before - _sublane_scan(run, reverse=False)
            lk_sum = jnp.sum(run, axis=0, keepdims=True)
            run = jnp.zeros((SUBLANES, t), f32)
            for a in range(KEY_RUN // 2):
                parts = []
                for r in (slice(2 * SUBLANES * a, 2 * SUBLANES * a + SUBLANES),
                          slice(2 * SUBLANES * a + SUBLANES, 2 * SUBLANES * (a + 1))):
                    w = w_s[h, r, :] + off
                    if diagonal:
                        w = w + neg_ref[r, :]
                    av = jnp.exp2(w)
                    g = av * da_s[h, r, :]
                    run = run + g
                    da_s[h, r, :] = g
                    g_s[h, r, :] = run
                    parts.append(av)
                a_s[h, 2 * SUBLANES * a:2 * SUBLANES * (a + 1), :] = jnp.concatenate(parts, axis=0).astype(bf16)
            goff = g_before + _sublane_scan(run, reverse=False)
            goff2 = jnp.concatenate([goff, goff], axis=0)
            for a in range(KEY_RUN // 2):
                rows = slice(2 * SUBLANES * a, 2 * SUBLANES * (a + 1))
                dz = da_s[h, rows, :] - b_s[h, rows, :] * (g_s[h, rows, :] + goff2)
                if diagonal:
                    dz = dz * m01_ref[rows, :]
                dz_s[h, rows, :] = dz.astype(bf16)
            return lk_before + lk_sum, g_before + jnp.sum(run, axis=0, keepdims=True)

        def block(kb, before, after, diagonal, carry):
            scores(kb, 1)
            spread(before, 1)
            c0 = grads(0, diagonal, carry[0], carry[1])
            if after is not None:
                scores(after, 0)
            spread(kb, 0)
            c1 = grads(1, diagonal, carry[2], carry[3])
            return c0 + c1

        zero = jnp.zeros((1, t), f32)
        scores(0, 0)
        def two_blocks(i, c):
            kb = 2 * i
            c = block(kb, jnp.maximum(kb - 1, 0), kb + 1, False, c)
            return block(kb + 1, kb, kb + 2, False, c)

        carry = lax.fori_loop(0, qb // 2, two_blocks, (zero, zero, zero, zero))
        carry = lax.fori_loop(qb - qb % 2, qb, lambda kb, c: block(kb, jnp.maximum(kb - 1, 0), kb + 1, False, c), carry)
        block(qb, jnp.maximum(qb - 1, 0), None, True, carry)
        spread(qb, 1)
        top = lax.broadcasted_iota(jnp.int32, (LANES, t), 0) < HEAD_DIM
        dq_ref[...] = (jnp.where(top, dq_acc[0], dq_acc[1]).T * (HEAD_DIM ** -0.5)).astype(bf16)

        @pl.when(qb == nblk - 1)
        def _():
            for kb in range(nblk):
                for b in range(SUBLANES):
                    keys_b = slice(KEY_RUN * b, KEY_RUN * (b + 1))
                    dk_ref[kb, keys_b, :] = dk_acc[kb, pl.ds(b, KEY_RUN, stride=SUBLANES), :].astype(bf16)
                    dv_ref[kb, keys_b, :] = dv_acc[kb, pl.ds(b, KEY_RUN, stride=SUBLANES), :].astype(bf16)

    keys = pl.BlockSpec((nblk, t, LANES), lambda p, i: (0, 0, p))
    tile = pl.BlockSpec((t, t), lambda p, i: (0, 0))
    return _call_carrying(
        ex, body, "attn_bwd", (npair, nblk),
        in_specs=[pl.BlockSpec((t, LANES), lambda p, i: (i, qoff + p)),
                  keys, keys,
                  pl.BlockSpec((nblk, LANES, t), lambda p, i: (0, p, 0)),
                  pl.BlockSpec((t, LANES), lambda p, i: (i, p)),
                  pl.BlockSpec((None, SUBLANES, t), lambda p, i: (p, 0, i)),
                  tile, tile],
        out_specs=[pl.BlockSpec((t, LANES), lambda p, i: (i, p)), keys, keys],
        out_shape=[_sds((s, W_ATTN), bf16), _sds((nblk, t, W_ATTN), bf16), _sds((nblk, t, W_ATTN), bf16)],
        scratch_shapes=[pltpu.VMEM((2, t, t), f32), pltpu.VMEM((2, t, t), f32), pltpu.VMEM((2, t, t), f32),
                        pltpu.VMEM((2, t, t), f32), pltpu.VMEM((2, t, t), bf16), pltpu.VMEM((2, t, t), bf16),
                        pltpu.VMEM((2, LANES, t), f32), pltpu.VMEM((nblk, t, LANES), f32),
                        pltpu.VMEM((nblk, t, LANES), f32)],
        args=(proj, kp, vp, kt, do, tot, keep01, keepneg))


def _conv_bwd(proj, dcn, conv_w, g_conv):
    s = proj.shape[0]
    nblk = W_CONV // LANES
    rc = min(CONV_CHUNK, s)
    nchunk = s // rc

    def body(cb_ref, cc_ref, cu_ref, d_ref, w_ref, g_ref, d3_ref, gw_ref, gg_ref, dy_buf):
        ones = _group_ones(LANES)
        w0, w1, w2 = w_ref[0:1, :], w_ref[1:2, :], w_ref[2:3, :]
        g = g_ref[...]

        def first_pass(i, carry):
            gw0, gw1, gw2, gg = carry
            r0 = pl.multiple_of(i * rc, rc)
            rows = pl.ds(r0, rc)
            prev = pl.ds(pl.multiple_of(jnp.maximum(r0 - 8, 0), 8), 8)
            v = cc_ref[rows, :] * cu_ref[rows, :]
            vp = jnp.where(i > 0, cc_ref[prev, :] * cu_ref[prev, :], 0.0)
            v1, v2 = _shifted(vp, v, 1), _shifted(vp, v, 2)
            y = w2 * v + w1 * v1 + w0 * v2
            cb = cb_ref[rows, :]
            dcy, ggi = _head_rms_bwd(d_ref[rows, :], cb * y, g, ones)
            d3_ref[0, rows, :] = (dcy * y).astype(bf16)
            dy = dcy * cb
            dy_buf[rows, :] = dy
            return (gw0 + jnp.sum(dy * v2, axis=0, keepdims=True), gw1 + jnp.sum(dy * v1, axis=0, keepdims=True),
                    gw2 + jnp.sum(dy * v, axis=0, keepdims=True), gg + ggi)

        zero = jnp.zeros((1, LANES), f32)
        gw0, gw1, gw2, gg = lax.fori_loop(0, nchunk, first_pass, (zero, zero, zero, zero))
        gw_ref[...] = jnp.zeros_like(gw_ref)
        gw_ref[0:1, :] = gw0
        gw_ref[1:2, :] = gw1
        gw_ref[2:3, :] = gw2
        gg_ref[...] = gg

        def second_pass(i, carry):
            r0 = pl.multiple_of(i * rc, rc)
            rows = pl.ds(r0, rc)
            nxt = pl.ds(pl.multiple_of(jnp.minimum(r0 + rc, s - 8), 8), 8)
            dy = dy_buf[rows, :]
            dyn = jnp.where(i < nchunk - 1, dy_buf[nxt, :], 0.0)
            ext = jnp.concatenate([dy, dyn], axis=0)
            up1 = pltpu.roll(ext, rc + 8 - 1, axis=0)[:rc]
            up2 = pltpu.roll(ext, rc + 8 - 2, axis=0)[:rc]
            dv = w2 * dy + w1 * up1 + w0 * up2
            d3_ref[1, rows, :] = (dv * cu_ref[rows, :]).astype(bf16)
            d3_ref[2, rows, :] = (dv * cc_ref[rows, :]).astype(bf16)
            return carry

        lax.fori_loop(0, nchunk, second_pass, 0)

    def col(off):
        return pl.BlockSpec((s, LANES), lambda j: (0, off + j))

    return pl.pallas_call(
        body, name="conv_bwd", grid=(nblk,),
        in_specs=[col(0), col(nblk), col(2 * nblk), col(0),
                  pl.BlockSpec((None, CONV_W_ROWS, LANES), lambda j: (j, 0, 0)),
                  pl.BlockSpec((1, LANES), lambda j: (0, j))],
        out_specs=[pl.BlockSpec((3, s, LANES), lambda j: (0, 0, j)),
                   pl.BlockSpec((None, CONV_W_ROWS, LANES), lambda j: (j, 0, 0)),
                   pl.BlockSpec((1, LANES), lambda j: (0, j))],
        out_shape=[_sds((3, s, W_CONV), bf16), _sds((nblk, CONV_W_ROWS, LANES), f32), _sds((1, W_CONV), f32)],
        scratch_shapes=[pltpu.VMEM((s, LANES), f32)],
        compiler_params=_arb(1),
    )(proj, proj, proj, dcn, conv_w, g_conv)


PIECE = W_CONV


def _piece_spans():
    ncol = IN_COLS // N_CHIPS
    spans = []
    for p in range(IN_COLS // PIECE):
        for k in range(N_CHIPS):
            lo, hi = max(p * PIECE, k * ncol), min((p + 1) * PIECE, (k + 1) * ncol)
            if lo < hi:
                spans.append((p, lo - p * PIECE, hi - p * PIECE, k, lo - k * ncol, hi - k * ncol))
    return spans


def _in_proj_bwd(dconv, dq, dk, dv, dh1, x, g_mix, w_in, tm, ex=None):
    s = x.shape[0]
    ncol = IN_COLS // N_CHIPS

    def body(dc_ref, dq_ref, dk_ref, dv_ref, dh_ref, x_ref, g_ref, w_ref, dx_ref, gg_ref):
        i = pl.program_id(0)
        pieces = [dc_ref.at[0], dc_ref.at[1], dc_ref.at[2], dq_ref, dk_ref, dv_ref]
        da = jnp.zeros((tm, D_MODEL), f32)
        for p, plo, phi, k, wlo, whi in _piece_spans():
            da += _dot(pieces[p][:, plo:phi], w_ref[k, :, wlo:whi], NT)
        dxn, gg = _rms_bwd(da, x_ref[...], g_ref[...])
        dx_ref[...] = dh_ref[...] + dxn

        @pl.when(i == 0)
        def _():
            gg_ref[...] = jnp.zeros_like(gg_ref)

        gg_ref[...] += gg

    return _call_carrying(
        ex, body, "in_proj_bwd", (s // tm,),
        in_specs=[pl.BlockSpec((3, tm, PIECE), lambda i: (0, i, 0)),
                  pl.BlockSpec((tm, PIECE), lambda i: (i, 0)),
                  pl.BlockSpec((tm, PIECE), lambda i: (i, 0)),
                  pl.BlockSpec((tm, PIECE), lambda i: (i, 0)),
                  pl.BlockSpec((tm, D_MODEL), lambda i: (i, 0)),
                  pl.BlockSpec((tm, D_MODEL), lambda i: (i, 0)),
                  pl.BlockSpec((1, D_MODEL), lambda i: (0, 0)),
                  pl.BlockSpec((N_CHIPS, D_MODEL, ncol), lambda i: (0, 0, 0))],
        out_specs=[pl.BlockSpec((tm, D_MODEL), lambda i: (i, 0)),
                   pl.BlockSpec((1, D_MODEL), lambda i: (0, 0))],
        out_shape=[_sds((s, D_MODEL), f32), _sds((1, D_MODEL), f32)],
        scratch_shapes=[],
        args=(dconv, dq, dk, dv, dh1, x, g_mix, w_in))


def _grad_w_in(a, dconv, dq, dk, dv, ts):
    s = a.shape[0]
    ncol = IN_COLS // N_CHIPS

    def body(a_ref, dc_ref, dq_ref, dk_ref, dv_ref, o_ref):
        @pl.when(pl.program_id(0) == 0)
        def _():
            o_ref[...] = jnp.zeros_like(o_ref)

        pieces = [dc_ref.at[0], dc_ref.at[1], dc_ref.at[2], dq_ref, dk_ref, dv_ref]
        av = a_ref[...]
        for p, plo, phi, k, wlo, whi in _piece_spans():
            o_ref[k, :, wlo:whi] += _dot(av, pieces[p][:, plo:phi], TN)

    tok = pl.BlockSpec((ts, PIECE), lambda i: (i, 0))
    return pl.pallas_call(
        body, name="grad_w_in", grid=(s // ts,),
        in_specs=[pl.BlockSpec((ts, D_MODEL), lambda i: (i, 0)),
                  pl.BlockSpec((3, ts, PIECE), lambda i: (0, i, 0)), tok, tok, tok],
        out_specs=pl.BlockSpec((N_CHIPS, D_MODEL, ncol), lambda i: (0, 0, 0)),
        out_shape=_sds((N_CHIPS, D_MODEL, ncol), f32),
        compiler_params=_arb(1),
    )(a, dconv, dq, dk, dv)


def _weight_grad(a, b, bm, bn, ts, name, relu_sq=False):
    s, m = a.shape
    n = b.shape[1]
    nn = n // bn

    def body(a_ref, b_ref, o_ref):
        @pl.when(pl.program_id(2) == 0)
        def _():
            o_ref[...] = jnp.zeros_like(o_ref)

        av = a_ref[...]
        if relu_sq:
            av = jnp.square(jnp.maximum(av.astype(f32), 0.0)).astype(bf16)
        o_ref[...] += _dot(av, b_ref[...], TN)

    return pl.pallas_call(
        body, name=name, grid=(m // bm, nn, s // ts),
        in_specs=[pl.BlockSpec((ts, bm), lambda i, j, k: (k, i)),
                  pl.BlockSpec((ts, bn), lambda i, j, k: (k, j))],
        out_specs=pl.BlockSpec((None, bm, bn), lambda i, j, k: (i * nn + j, 0, 0)),
        out_shape=_sds(((m // bm) * nn, bm, bn), f32),
        compiler_params=_arb(3),
    )(a, b)


def kernel(x, p, g_mix, w_in, conv_w, g_conv_out, g_attn_out, w_out, g_mlp, w_up, w_down, g_ple, w_ple_gate, w_ple_proj, g_final, loss_target, m_g_mix, m_w_in, m_conv_w, m_g_conv_out, m_g_attn_out, m_w_out, m_g_mlp, m_w_up, m_w_down, m_g_ple, m_w_ple_gate, m_w_ple_proj, m_g_final, v_g_mix, v_w_in, v_conv_w, v_g_conv_out, v_g_attn_out, v_w_out, v_g_mlp, v_w_up, v_w_down, v_g_ple, v_w_ple_gate, v_w_ple_proj, v_g_final):
    s = x.shape[1]
    tm = min(TOKEN_TILE, s)
    tg = min(GRAD_TOKEN_TILE, s)
    xs = x.reshape(s, D_MODEL)
    ps = p.reshape(s, PLE_DIM)
    target = loss_target.reshape(s, D_MODEL)
    core = lax.axis_index("c").astype(jnp.int32).reshape(1)
    chip = 2 * lax.axis_index("x") + lax.axis_index("y")

    big = {"w_in": w_in[0], "w_out": w_out[0], "w_up": w_up[0], "w_down": w_down[0],
           "w_ple_gate": w_ple_gate[0], "w_ple_proj": w_ple_proj[0]}
    names = list(big)
    conv_shard = jnp.pad(conv_w[0], ((0, CONV_W_ROWS - conv_w.shape[1]), (0, 0)))
    later_names = names[1:]
    w_in_f, conv_f = _run_exchange(_gather_exchange([big["w_in"].astype(bf16), conv_shard]), "gather_w_in")

    proj, a_b, kp, vp, kt, vt = _in_proj(xs, g_mix, w_in_f, tm)
    conv_n = _conv_fwd(proj, conv_f, g_conv_out)
    (o, tot), gathered = _attn_fwd(proj, kp, vt, _gather_exchange([big[k].astype(bf16) for k in later_names]))
    w_out_f, w_up_f, w_down_f, w_gate_f, w_proj_f = gathered
    w_out_f = w_out_f.reshape(D_MODEL, D_MODEL)
    w_gate_f = w_gate_f.reshape(D_MODEL, D_MODEL)
    h1, cat_b = _out_proj(o, conv_n, xs, g_attn_out, w_out_f, tm)
    h2, u_b, m_b = _mlp_fwd(h1, g_mlp, w_up_f, w_down_f, min(MLP_TOKEN_TILE, s))

    dh2, dh2_b, n3_b, dgl_b, dpp_b, p_b, gg_ple, gg_final, loss_row = _tail(
        h2, ps, target, g_ple, g_final.reshape(1, D_MODEL), w_gate_f, w_proj_f, tm)
    dh1, dh1_b, du_b, gg_mlp = _mlp_bwd(dh2, dh2_b, h1, u_b, g_mlp, w_up_f, w_down_f, tm)
    part = {
        "w_out": _weight_grad(cat_b, dh1_b, D_MODEL, D_MODEL, tg, "grad_w_out").reshape(N_CHIPS, D_MODEL // N_CHIPS, D_MODEL),
        "w_up": _weight_grad(m_b, du_b, D_MODEL, D_FF // N_CHIPS, tg, "grad_w_up"),
        "w_down": _weight_grad(u_b, dh2_b, D_FF // N_CHIPS, D_MODEL, tg, "grad_w_down", relu_sq=True),
        "w_ple_gate": _weight_grad(n3_b, dgl_b, D_MODEL, D_MODEL, tg, "grad_w_ple_gate").reshape(N_CHIPS, D_MODEL // N_CHIPS, D_MODEL),
        "w_ple_proj": _weight_grad(p_b, dpp_b, PLE_DIM, D_MODEL // N_CHIPS, tg, "grad_w_ple_proj"),
    }
    (dcn, do, gg_attn), from_sibling = _out_proj_bwd(
        dh1_b, o, g_attn_out, w_out_f, tm, _pair_exchange([part[k] for k in later_names]))
    pair = [_pair_sum(part[k], r, core, "pair_sum_" + k) for k, r in zip(later_names, from_sibling)]
    dconv, g_conv_w, gg_conv = _conv_bwd(proj, dcn, conv_f, g_conv_out)
    (dq, dk, dv), from_chips = _attn_bwd(proj, kp, vp, kt, do, tot, _chip_exchange(pair))
    dk, dv = dk.reshape(s, W_ATTN), dv.reshape(s, W_ATTN)

    part["w_in"] = _grad_w_in(a_b, dconv, dq, dk, dv, min(MLP_TOKEN_TILE, s))
    in_sibling = _run_exchange(_pair_exchange([part["w_in"]]), "grad_pair_exchange_w_in")
    in_pair = _pair_sum(part["w_in"], in_sibling[0], core, "pair_sum_w_in")
    (grad_x, gg_mix), in_chips = _in_proj_bwd(
        dconv, dq, dk, dv, dh1, xs, g_mix, w_in_f, tm, _chip_exchange([in_pair]))

    place = jnp.stack([lax.axis_index("c"), chip]).astype(jnp.int32)
    half = [_chip_sum(mine, landed, place, "chip_sum_" + k)
            for k, mine, landed in zip(names, [in_pair] + pair, list(in_chips) + list(from_chips))]
    both = _sibling_exchange(half)
    grad = {k: b.reshape(big[k].shape) for k, b in zip(names, both)}

    gcw = g_conv_w[:, :3, :].transpose(1, 0, 2).reshape(3, W_CONV)
    row = lambda *parts: jnp.concatenate(parts, axis=1)
    packed = jnp.concatenate([
        gg_mix, gg_mlp, gg_ple, gg_final, row(gg_conv, gg_attn), row(gcw[0:1], gcw[1:2]),
        row(gcw[2:3], loss_row, jnp.zeros((1, W_CONV - LANES), f32)), jnp.zeros((1, D_MODEL), f32)], axis=0)
    summed = _allreduce_small(packed)
    loss = summed[6, W_CONV]
    gcw_full = jnp.stack([summed[5, :W_CONV], summed[5, W_CONV:], summed[6, :W_CONV]])
    grad["conv_w"] = lax.dynamic_slice(gcw_full, (0, chip * LANES), (3, LANES))
    vec_names = ["g_mix", "g_mlp", "g_ple", "g_final"]
    vec_w = {"g_mix": g_mix, "g_mlp": g_mlp, "g_ple": g_ple, "g_final": g_final.reshape(1, D_MODEL)}
    vec_m = {"g_mix": m_g_mix, "g_mlp": m_g_mlp, "g_ple": m_g_ple, "g_final": m_g_final.reshape(1, D_MODEL)}
    vec_v = {"g_mix": v_g_mix, "g_mlp": v_g_mlp, "g_ple": v_g_ple, "g_final": v_g_final.reshape(1, D_MODEL)}

    def pack_vec(d, conv, attn):
        return jnp.concatenate([d[k] for k in vec_names] + [row(conv, attn)], axis=0)

    vec_g = summed[0:5]
    vec_d, vec_nm, vec_nv = _adamw(vec_g, pack_vec(vec_w, g_conv_out, g_attn_out),
                                   pack_vec(vec_m, m_g_conv_out, m_g_attn_out),
                                   pack_vec(vec_v, v_g_conv_out, v_g_attn_out), "adamw_vectors")

    given_w = dict(big, conv_w=conv_w[0])
    given_m = {"w_in": m_w_in[0], "w_out": m_w_out[0], "w_up": m_w_up[0], "w_down": m_w_down[0],
               "w_ple_gate": m_w_ple_gate[0], "w_ple_proj": m_w_ple_proj[0], "conv_w": m_conv_w[0]}
    given_v = {"w_in": v_w_in[0], "w_out": v_w_out[0], "w_up": v_w_up[0], "w_down": v_w_down[0],
               "w_ple_gate": v_w_ple_gate[0], "w_ple_proj": v_w_ple_proj[0], "conv_w": v_conv_w[0]}
    delta, new_m, new_v = {}, {}, {}
    for k in names + ["conv_w"]:
        delta[k], new_m[k], new_v[k] = _adamw(grad[k], given_w[k], given_m[k], given_v[k], "adamw_" + k)

    def unpack(vals, kind):
        out = {k: vals[i:i + 1] for i, k in enumerate(vec_names)}
        out["g_final"] = out["g_final"].reshape(D_MODEL)
        out["g_conv_out"] = vals[4:5, :W_CONV]
        out["g_attn_out"] = vals[4:5, W_CONV:]
        out.update({k: v[None] for k, v in kind.items()})
        return out

    order = ["g_mix", "w_in", "conv_w", "g_conv_out", "g_attn_out", "w_out", "g_mlp", "w_up", "w_down",
             "g_ple", "w_ple_gate", "w_ple_proj", "g_final"]
    groups = [unpack(vec_g, grad), unpack(vec_d, delta), unpack(vec_nm, new_m), unpack(vec_nv, new_v)]
    return (loss, grad_x[None]) + tuple(g[k] for g in groups for k in order)
```

```python
import functools

import jax
import jax.numpy as jnp
from jax import lax
from jax.experimental import pallas as pl
from jax.experimental.pallas import tpu as pltpu

f32 = jnp.float32
bf16 = jnp.bfloat16

D_MODEL = 1024
HEAD_DIM = 64
W_CONV = 512
W_ATTN = 512
D_FF = 4096
PLE_DIM = 256
IN_COLS = 3 * W_CONV + 3 * W_ATTN
N_CHIPS = 4
EPS = 1e-6
ADAM_LR = 0.001
ADAM_B1 = 0.9
ADAM_B2 = 0.999
ADAM_EPS = 1e-08
ADAM_WD = 0.01
ADAM_STEP = 10

LANES = 128
TOKEN_TILE = 512
MLP_TOKEN_TILE = 1024
GRAD_TOKEN_TILE = 2048
ATTN_TILE = 256
CONV_CHUNK = 512
CONV_W_ROWS = 16

MESH = pl.DeviceIdType.MESH
ANY = pl.BlockSpec(memory_space=pl.ANY)
NT = (((1,), (1,)), ((), ()))
TN = (((0,), (0,)), ((), ()))


def _arb(n):
    return pltpu.CompilerParams(dimension_semantics=("arbitrary",) * n)


def _sds(shape, dtype):
    return jax.ShapeDtypeStruct(shape, dtype)


def _dot(a, b, dims=None):
    if dims is None:
        return jnp.dot(a, b, preferred_element_type=f32)
    return lax.dot_general(a, b, dims, preferred_element_type=f32)


def _split_dot(x, ones):
    hi = x.astype(bf16)
    lo = (x - hi.astype(f32)).astype(bf16)
    return _dot(hi, ones) + _dot(lo, ones)


def _rms_fwd(h, g):
    rstd = lax.rsqrt(jnp.mean(h * h, axis=-1, keepdims=True) + EPS)
    return h * rstd * g, rstd


def _rms_bwd(dy, h, g):
    rstd = lax.rsqrt(jnp.mean(h * h, axis=-1, keepdims=True) + EPS)
    hn = h * rstd
    dyg = dy * g
    dh = rstd * (dyg - hn * jnp.mean(dyg * hn, axis=-1, keepdims=True))
    return dh, jnp.sum(dy * hn, axis=0, keepdims=True)


def _group_ones(n):
    r = lax.broadcasted_iota(jnp.int32, (n, n), 0) // HEAD_DIM
    c = lax.broadcasted_iota(jnp.int32, (n, n), 1) // HEAD_DIM
    return (r == c).astype(bf16)


def _head_rms_fwd(y, g, ones):
    rstd = lax.rsqrt(_split_dot(y * y, ones) * (1.0 / HEAD_DIM) + EPS)
    return y * rstd * g


def _head_rms_bwd(dy, y, g, ones):
    rstd = lax.rsqrt(_split_dot(y * y, ones) * (1.0 / HEAD_DIM) + EPS)
    yn = y * rstd
    dyg = dy * g
    dyy = rstd * (dyg - yn * (_split_dot(dyg * yn, ones) * (1.0 / HEAD_DIM)))
    return dyy, jnp.sum(dy * yn, axis=0, keepdims=True)


def _place():
    return lax.axis_index("x"), lax.axis_index("y"), lax.axis_index("c")


def _other_chips(x, y):
    return [(1 - x, y), (x, 1 - y), (1 - x, 1 - y)]


class _Exchange:
    def __init__(self, arrays, out_shapes, sems, start, finish, relay=None):
        self.arrays, self.out_shapes, self.sems, self.start, self.finish = arrays, out_shapes, sems, start, finish
        self.relay = relay


def _gather_exchange(shards):
    n = len(shards)
    halves = [s.shape[0] // 2 for s in shards]

    def plan(ins, outs, sems):
        send_sems, recv_sems, own_sems = sems
        x, y, c = _place()
        me = 2 * x + y
        chips = _other_chips(x, y)

        def half(ref, i, which):
            return ref.at[pl.ds(which * halves[i], halves[i]), :]

        def over_ici(i, j, src, slot, to):
            return pltpu.make_async_remote_copy(
                src_ref=src, dst_ref=half(outs[i].at[slot], i, c),
                send_sem=send_sems.at[3 * i + j], recv_sem=recv_sems.at[3 * i + j],
                device_id=to, device_id_type=MESH)

        def to_sibling(i, j, slot, which):
            blk = half(outs[i].at[slot], i, which)
            return pltpu.make_async_remote_copy(
                src_ref=blk, dst_ref=blk,
                send_sem=send_sems.at[3 * n + 3 * i + j], recv_sem=recv_sems.at[3 * n + 3 * i + j],
                device_id=(x, y, 1 - c), device_id_type=MESH)

        own = [pltpu.make_async_remote_copy(
            src_ref=ins[i], dst_ref=outs[i].at[me], send_sem=own_sems.at[i], recv_sem=own_sems.at[n + i],
            device_id=(x, y, 1 - c), device_id_type=MESH) for i in range(n)]
        pairs = [(i, j, px, py) for i in range(n) for j, (px, py) in enumerate(chips)]
        sends = [over_ici(i, j, half(ins[i], i, c), me, (px, py, c)) for i, j, px, py in pairs]
        lands = [over_ici(i, j, half(outs[i].at[2 * px + py], i, c), 2 * px + py, (px, py, c)) for i, j, px, py in pairs]
        passes = [to_sibling(i, j, 2 * px + py, c) for i, j, px, py in pairs]
        from_sibling = [to_sibling(i, j, 2 * px + py, 1 - c) for i, j, px, py in pairs]
        return own, sends, lands, passes, from_sibling

    def start(ins, outs, sems):
        own, sends, _, _, _ = plan(ins, outs, sems)
        for cp in own + sends:
            cp.start()

    def relay(ins, outs, sems):
        _, _, lands, passes, _ = plan(ins, outs, sems)
        for land, on in zip(lands, passes):
            land.wait_recv()
            on.start()

    def finish(ins, outs, sems):
        own, sends, _, passes, from_sibling = plan(ins, outs, sems)
        for cp in from_sibling:
            cp.wait_recv()
        for cp in sends + passes:
            cp.wait_send()
        for cp in own:
            cp.wait()

    return _Exchange(
        shards, [_sds((N_CHIPS,) + s.shape, s.dtype) for s in shards],
        [pltpu.SemaphoreType.DMA((6 * n,)), pltpu.SemaphoreType.DMA((6 * n,)), pltpu.SemaphoreType.DMA((2 * n,))],
        start, finish, relay)


def _call_carrying(ex, body, name, grid, in_specs, out_specs, out_shape, scratch_shapes, args):
    n_in, n_out, n_scr = len(in_specs), len(out_specs), len(scratch_shapes)
    k = 0 if ex is None else len(ex.arrays)

    def wrapped(*refs):
        ins, xin = refs[:n_in], refs[n_in:n_in + k]
        outs, xout = refs[n_in + k:n_in + k + n_out], refs[n_in + k + n_out:n_in + 2 * k + n_out]
        scr, sems = refs[n_in + 2 * k + n_out:n_in + 2 * k + n_out + n_scr], refs[n_in + 2 * k + n_out + n_scr:]
        ids = [pl.program_id(d) for d in range(len(grid))]
        if ex is not None:
            @pl.when(functools.reduce(lambda a, b: a & b, [i == 0 for i in ids]))
            def _():
                ex.start(xin, xout, sems)

        if ex is not None and ex.relay is not None:
            relay_at = [grid[0] - 1] + [0] * (len(grid) - 1) if len(grid) > 1 else [grid[0] - 1]

            @pl.when(functools.reduce(lambda a, b: a & b, [i == r for i, r in zip(ids, relay_at)]))
            def _():
                ex.relay(xin, xout, sems)

        body(*ins, *outs, *scr)
        if ex is not None:
            @pl.when(functools.reduce(lambda a, b: a & b, [i == g - 1 for i, g in zip(ids, grid)]))
            def _():
                ex.finish(xin, xout, sems)

    res = pl.pallas_call(
        wrapped, name=name, grid=grid,
        in_specs=list(in_specs) + [ANY] * k, out_specs=list(out_specs) + [ANY] * k,
        out_shape=list(out_shape) + ([] if ex is None else list(ex.out_shapes)),
        scratch_shapes=list(scratch_shapes) + ([] if ex is None else list(ex.sems)),
        compiler_params=_arb(len(grid)),
    )(*args, *([] if ex is None else ex.arrays))
    return res[:n_out], res[n_out:]


def _run_exchange(ex, name):
    n = len(ex.arrays)

    def body(*refs):
        ins, outs, sems = refs[:n], refs[n:2 * n], refs[2 * n:]
        ex.start(ins, outs, sems)
        if ex.relay is not None:
            ex.relay(ins, outs, sems)
        ex.finish(ins, outs, sems)

    return pl.pallas_call(
        body, name=name, out_shape=ex.out_shapes, in_specs=[ANY] * n, out_specs=[ANY] * n,
        scratch_shapes=ex.sems,
    )(*ex.arrays)


def _pair_exchange(grads):
    n = len(grads)

    def plan(ins, outs, sems):
        send_sems, recv_sems = sems
        x, y, c = _place()
        return [pltpu.make_async_remote_copy(
            src_ref=ins[i].at[:, 1 - c], dst_ref=outs[i],
            send_sem=send_sems.at[i], recv_sem=recv_sems.at[i],
            device_id=(x, y, 1 - c), device_id_type=MESH) for i in range(n)]

    def start(ins, outs, sems):
        for cp in plan(ins, outs, sems):
            cp.start()

    def finish(ins, outs, sems):
        for cp in plan(ins, outs, sems):
            cp.wait()

    views = [g.reshape(N_CHIPS, 2, g.shape[1] // 2, g.shape[2]) for g in grads]
    return _Exchange(
        views, [_sds((N_CHIPS, v.shape[2], v.shape[3]), v.dtype) for v in views],
        [pltpu.SemaphoreType.DMA((n,)), pltpu.SemaphoreType.DMA((n,))], start, finish)


def _chip_exchange(parts):
    n = len(parts)

    def plan(ins, outs, sems):
        send_sems, recv_sems = sems
        x, y, c = _place()
        me = 2 * x + y
        pairs = [(i, j, px, py) for i in range(n) for j, (px, py) in enumerate(_other_chips(x, y))]

        def copy(i, j, src, slot, px, py):
            return pltpu.make_async_remote_copy(
                src_ref=src, dst_ref=outs[i].at[slot],
                send_sem=send_sems.at[3 * i + j], recv_sem=recv_sems.at[3 * i + j],
                device_id=(px, py, c), device_id_type=MESH)

        sends = [copy(i, j, ins[i].at[2 * px + py], me, px, py) for i, j, px, py in pairs]
        lands = [copy(i, j, outs[i].at[2 * px + py], 2 * px + py, px, py) for i, j, px, py in pairs]
        return sends, lands

    def start(ins, outs, sems):
        sends, _ = plan(ins, outs, sems)
        for cp in sends:
            cp.start()

    def finish(ins, outs, sems):
        sends, lands = plan(ins, outs, sems)
        for cp in lands:
            cp.wait_recv()
        for cp in sends:
            cp.wait_send()

    return _Exchange(
        parts, [_sds(p.shape, p.dtype) for p in parts],
        [pltpu.SemaphoreType.DMA((3 * n,)), pltpu.SemaphoreType.DMA((3 * n,))],
        start, finish)


def _sibling_exchange(both):
    n = len(both)

    def body(*refs):
        outs = refs[n:2 * n]
        send_sems, recv_sems = refs[2 * n:]
        x, y, c = _place()
        sent = []
        for i in range(n):
            cp = pltpu.make_async_remote_copy(
                src_ref=outs[i].at[c], dst_ref=outs[i].at[c],
                send_sem=send_sems.at[i], recv_sem=recv_sems.at[i],
                device_id=(x, y, 1 - c), device_id_type=MESH)
            cp.start()
            sent.append(cp)
        for cp in sent:
            cp.wait()

    return pl.pallas_call(
        body, name="grad_sibling_exchange",
        out_shape=[_sds(b.shape, f32) for b in both],
        in_specs=[ANY] * n, out_specs=[ANY] * n,
        input_output_aliases={i: i for i in range(n)},
        scratch_shapes=[pltpu.SemaphoreType.DMA((n,)), pltpu.SemaphoreType.DMA((n,))],
    )(*both)


def _row_tile(rows, cols):
    t = rows
    while t * cols * 4 > (2 << 20) and t % 16 == 0:
        t //= 2
    return t


def _pair_sum(grad, recv, core, name):
    _, r, c = grad.shape
    hr = r // 2
    tr = _row_tile(hr, c)
    view = grad.reshape(N_CHIPS, 2, hr, c)

    def body(core_ref, mine_ref, recv_ref, out_ref):
        out_ref[...] = (mine_ref[...] + recv_ref[...].astype(f32)).astype(bf16)

    return pl.pallas_call(
        body, name=name,
        grid_spec=pltpu.PrefetchScalarGridSpec(
            num_scalar_prefetch=1, grid=(N_CHIPS, hr // tr),
            in_specs=[pl.BlockSpec((None, None, tr, c), lambda k, t, core_ref: (k, core_ref[0], t, 0)),
                      pl.BlockSpec((None, tr, c), lambda k, t, core_ref: (k, t, 0))],
            out_specs=pl.BlockSpec((None, tr, c), lambda k, t, core_ref: (k, t, 0))),
        out_shape=_sds((N_CHIPS, hr, c), bf16),
        compiler_params=_arb(2),
    )(core, view, recv)


def _chip_sum(mine, landed, place, name):
    _, hr, c = mine.shape
    tr = _row_tile(hr, c)

    def body(place_ref, a_ref, b_ref, c_ref, d_ref, out_ref):
        out_ref[...] = ((a_ref[...].astype(f32) + b_ref[...].astype(f32)) + c_ref[...].astype(f32)) + d_ref[...].astype(f32)

    def slot(k):
        return pl.BlockSpec((None, tr, c), lambda t, place_ref: ((place_ref[1] + k) % N_CHIPS, t, 0))

    return pl.pallas_call(
        body, name=name,
        grid_spec=pltpu.PrefetchScalarGridSpec(
            num_scalar_prefetch=1, grid=(hr // tr,),
            in_specs=[slot(0), slot(1), slot(2), slot(3)],
            out_specs=pl.BlockSpec((None, tr, c), lambda t, place_ref: (place_ref[0], t, 0))),
        out_shape=_sds((2, hr, c), f32), compiler_params=_arb(1),
    )(place, mine, landed, landed, landed)


def _adamw(g, w, m, v, name):
    r, c = g.shape
    tr = _row_tile(r, c)

    def body(g_ref, w_ref, m_ref, v_ref, d_ref, nm_ref, nv_ref):
        gv = g_ref[...]
        mv = ADAM_B1 * m_ref[...] + (1.0 - ADAM_B1) * gv
        vv = ADAM_B2 * v_ref[...] + (1.0 - ADAM_B2) * jnp.square(gv)
        m_hat = mv / (1.0 - ADAM_B1 ** ADAM_STEP)
        v_hat = vv / (1.0 - ADAM_B2 ** ADAM_STEP)
        d_ref[...] = -ADAM_LR * (m_hat / (jnp.sqrt(v_hat) + ADAM_EPS) + ADAM_WD * w_ref[...])
        nm_ref[...] = mv
        nv_ref[...] = vv

    spec = pl.BlockSpec((tr, c), lambda t: (t, 0))
    return pl.pallas_call(
        body, name=name, grid=(r // tr,), in_specs=[spec] * 4, out_specs=[spec] * 3,
        out_shape=[_sds((r, c), f32)] * 3, compiler_params=_arb(1),
    )(g, w, m, v)


def _allreduce_small(packed):
    shape = packed.shape

    def body(x_ref, out_ref, buf, send_sems, recv_sems):
        x, y, c = _place()
        me = 4 * x + 2 * y + c
        buf[me] = x_ref[...]
        sent = []
        for r in range(1, 8):
            dx, dy, dc = (r >> 2) & 1, (r >> 1) & 1, r & 1
            peer = ((1 - x) if dx else x, (1 - y) if dy else y, (1 - c) if dc else c)
            cp = pltpu.make_async_remote_copy(
                src_ref=x_ref, dst_ref=buf.at[me],
                send_sem=send_sems.at[r], recv_sem=recv_sems.at[r],
                device_id=peer, device_id_type=MESH)
            cp.start()
            sent.append((cp, peer))
        for r, (cp, peer) in enumerate(sent, start=1):
            src = 4 * peer[0] + 2 * peer[1] + peer[2]
            pltpu.make_async_remote_copy(
                src_ref=x_ref, dst_ref=buf.at[src],
                send_sem=send_sems.at[r], recv_sem=recv_sems.at[r],
                device_id=peer, device_id_type=MESH).wait_recv()
        for cp, _ in sent:
            cp.wait_send()
        total = buf[0]
        for k in range(1, 8):
            total = total + buf[k]
        out_ref[...] = total

    vmem = pl.BlockSpec(memory_space=pltpu.VMEM)
    return pl.pallas_call(
        body, name="allreduce_small", out_shape=_sds(shape, f32),
        in_specs=[vmem], out_specs=vmem,
        scratch_shapes=[pltpu.VMEM((8,) + shape, f32), pltpu.SemaphoreType.DMA((8,)),
                        pltpu.SemaphoreType.DMA((8,))],
    )(packed)


def _in_proj(x, g_mix, w_in, tm):
    s = x.shape[0]
    ncol = IN_COLS // N_CHIPS
    t = ATTN_TILE
    nb = tm // t
    koff = 3 * W_CONV + W_ATTN

    def body(x_ref, g_ref, w_ref, proj_ref, a_ref, kp_ref, vp_ref, kt_ref, vt_ref, perm, kv):
        a, _ = _rms_fwd(x_ref[...], g_ref[...])
        ab = a.astype(bf16)
        a_ref[...] = ab
        for k in range(N_CHIPS):
            proj_ref[:, k * ncol:(k + 1) * ncol] = _dot(ab, w_ref[k])
        for j in range(2 * W_ATTN // LANES):
            kv[j] = proj_ref[:, koff + j * LANES:koff + (j + 1) * LANES]
        for b in range(nb):
            for r in range(KEY_RUN):
                for j in range(2 * W_ATTN // LANES):
                    perm[SUBLANES * r:SUBLANES * (r + 1), j * LANES:(j + 1) * LANES] = kv[
                        j, pl.ds(b * t + r, SUBLANES, stride=KEY_RUN), :]
            kp_ref[b] = perm[:, :W_ATTN].astype(bf16)
            vp_ref[b] = perm[:, W_ATTN:].astype(bf16)
            kt_ref[b] = perm[:, :W_ATTN].T.astype(bf16)
            vt_ref[b] = perm[:, W_ATTN:].T.astype(bf16)

    keys = pl.BlockSpec((nb, t, W_ATTN), lambda i: (i, 0, 0))
    keys_t = pl.BlockSpec((nb, W_ATTN, t), lambda i: (i, 0, 0))
    return pl.pallas_call(
        body, name="in_proj", grid=(s // tm,),
        in_specs=[pl.BlockSpec((tm, D_MODEL), lambda i: (i, 0)),
                  pl.BlockSpec((1, D_MODEL), lambda i: (0, 0)),
                  pl.BlockSpec((N_CHIPS, D_MODEL, ncol), lambda i: (0, 0, 0))],
        out_specs=[pl.BlockSpec((tm, IN_COLS), lambda i: (i, 0)),
                   pl.BlockSpec((tm, D_MODEL), lambda i: (i, 0)), keys, keys, keys_t, keys_t],
        out_shape=[_sds((s, IN_COLS), f32), _sds((s, D_MODEL), bf16),
                   _sds((s // t, t, W_ATTN), bf16), _sds((s // t, t, W_ATTN), bf16),
                   _sds((s // t, W_ATTN, t), bf16), _sds((s // t, W_ATTN, t), bf16)],
        scratch_shapes=[pltpu.VMEM((t, 2 * W_ATTN), f32), pltpu.VMEM((2 * W_ATTN // LANES, tm, LANES), f32)],
        compiler_params=_arb(1),
    )(x, g_mix, w_in)


def _shifted(prev8, cur, shift):
    ext = jnp.concatenate([prev8, cur], axis=0)
    return pltpu.roll(ext, shift, axis=0)[8:]


def _conv_fwd(proj, conv_w, g_conv):
    s = proj.shape[0]
    nblk = W_CONV // LANES
    rc = min(CONV_CHUNK, s)

    def body(cb_ref, cc_ref, cu_ref, w_ref, g_ref, out_ref):
        ones = _group_ones(LANES)
        w0, w1, w2 = w_ref[0:1, :], w_ref[1:2, :], w_ref[2:3, :]
        g = g_ref[...]

        def chunk(i, carry):
            r0 = pl.multiple_of(i * rc, rc)
            rows = pl.ds(r0, rc)
            prev = pl.ds(pl.multiple_of(jnp.maximum(r0 - 8, 0), 8), 8)
            v = cc_ref[rows, :] * cu_ref[rows, :]
            vp = jnp.where(i > 0, cc_ref[prev, :] * cu_ref[prev, :], 0.0)
            y = w2 * v + w1 * _shifted(vp, v, 1) + w0 * _shifted(vp, v, 2)
            out_ref[rows, :] = _head_rms_fwd(cb_ref[rows, :] * y, g, ones).astype(bf16)
            return carry

        lax.fori_loop(0, s // rc, chunk, 0)

    def col(off):
        return pl.BlockSpec((s, LANES), lambda j: (0, off + j))

    return pl.pallas_call(
        body, name="conv_fwd", grid=(nblk,),
        in_specs=[col(0), col(nblk), col(2 * nblk),
                  pl.BlockSpec((None, CONV_W_ROWS, LANES), lambda j: (j, 0, 0)),
                  pl.BlockSpec((1, LANES), lambda j: (0, j))],
        out_specs=pl.BlockSpec((s, LANES), lambda j: (0, j)),
        out_shape=_sds((s, W_CONV), bf16), compiler_params=_arb(1),
    )(proj, proj, proj, conv_w, g_conv)


LOG2_E = 1.4426950408889634


def _log2_keep(z2):
    nz2 = -z2
    return jnp.minimum(nz2, 0.0) - jnp.log2(1.0 + jnp.exp2(jnp.minimum(z2, nz2)))


def _head_pair_masks(rows):
    lane = lax.broadcasted_iota(jnp.int32, (rows, LANES), 1)
    return lane < HEAD_DIM


SUBLANES = 8
KEY_RUN = ATTN_TILE // SUBLANES


def _causal_tiles():
    r = lax.broadcasted_iota(jnp.int32, (ATTN_TILE, ATTN_TILE), 0)
    key = (r % SUBLANES) * KEY_RUN + r // SUBLANES
    below = key < lax.broadcasted_iota(jnp.int32, (ATTN_TILE, ATTN_TILE), 1)
    return below.astype(f32), jnp.where(below, 0.0, -1e30).astype(f32)


def _sublane_scan(x, reverse):
    row = lax.broadcasted_iota(jnp.int32, x.shape, 0)
    inc = x
    for sh in (1, 2, 4):
        if reverse:
            inc = inc + jnp.where(row < SUBLANES - sh, pltpu.roll(inc, SUBLANES - sh, axis=0), 0.0)
        else:
            inc = inc + jnp.where(row >= sh, pltpu.roll(inc, sh, axis=0), 0.0)
    return inc - x


def _attn_fwd(proj, kp, vt, ex=None):
    s = proj.shape[0]
    t = ATTN_TILE
    nblk = s // t
    npair = W_ATTN // LANES
    qoff = 3 * W_CONV // LANES
    keep01, keepneg = _causal_tiles()

    def body(q_ref, k_ref, vt_ref, m01_ref, neg_ref, o_ref, tot_ref, w_s, a_s, acc):
        qb = pl.program_id(1)
        first = _head_pair_masks(t)
        q = q_ref[...] * (HEAD_DIM ** -0.5)
        qh = (jnp.where(first, q, 0.0).astype(bf16), jnp.where(first, 0.0, q).astype(bf16))
        acc[...] = jnp.zeros_like(acc)
        a_s[1] = jnp.zeros((t, t), bf16)

        def scores(kb, h):
            w_s[h] = _dot(k_ref[kb], qh[h], NT)

        def weigh(kb, h):
            acc[h] += _dot(vt_ref[kb], a_s[h])

        def weights(h, diagonal, later):
            run = jnp.zeros((SUBLANES, t), f32)
            for a in reversed(range(KEY_RUN)):
                rows = slice(SUBLANES * a, SUBLANES * (a + 1))
                z2 = w_s[h, rows, :] * LOG2_E
                lk = _log2_keep(z2)
                if diagonal:
                    lk = lk * m01_ref[rows, :]
                run = run + lk
                w_s[h, rows, :] = z2 + run
            off = _sublane_scan(run, reverse=True) + later
            off2 = jnp.concatenate([off, off], axis=0)
            for a in range(t // (2 * SUBLANES)):
                rows = slice(2 * SUBLANES * a, 2 * SUBLANES * (a + 1))
                w = w_s[h, rows, :] + off2
                if diagonal:
                    w = w + neg_ref[rows, :]
                a_s[h, rows, :] = jnp.exp2(w).astype(bf16)
            return later + jnp.sum(run, axis=0, keepdims=True)

        def block(kb, before, after, diagonal, later):
            scores(kb, 1)
            weigh(before, 1)
            l0 = weights(0, diagonal, later[0])
            scores(after, 0)
            weigh(kb, 0)
            l1 = weights(1, diagonal, later[1])
            return l0, l1

        zero = jnp.zeros((1, t), f32)
        scores(qb, 0)
        later = block(qb, qb, jnp.maximum(qb - 1, 0), True, (zero, zero))

        def earlier(i, c):
            kb = qb - 1 - i
            return block(kb, kb + 1, jnp.maximum(kb - 1, 0), False, c)

        later = lax.fori_loop(0, qb, earlier, later)
        weigh(0, 1)
        top = lax.broadcasted_iota(jnp.int32, (LANES, t), 0) < HEAD_DIM
        o_ref[...] = jnp.where(top, acc[0], acc[1]).T
        tot_ref[...] = jnp.concatenate([later[0], later[1], jnp.zeros((SUBLANES - 2, t), f32)], axis=0)

    return _call_carrying(
        ex, body, "attn_fwd", (npair, nblk),
        in_specs=[pl.BlockSpec((t, LANES), lambda p, i: (i, qoff + p)),
                  pl.BlockSpec((nblk, t, LANES), lambda p, i: (0, 0, p)),
                  pl.BlockSpec((nblk, LANES, t), lambda p, i: (0, p, 0)),
                  pl.BlockSpec((t, t), lambda p, i: (0, 0)),
                  pl.BlockSpec((t, t), lambda p, i: (0, 0))],
        out_specs=[pl.BlockSpec((t, LANES), lambda p, i: (i, p)),
                   pl.BlockSpec((None, SUBLANES, t), lambda p, i: (p, 0, i))],
        out_shape=[_sds((s, W_ATTN), f32), _sds((npair, SUBLANES, s), f32)],
        scratch_shapes=[pltpu.VMEM((2, t, t), f32), pltpu.VMEM((2, t, t), bf16), pltpu.VMEM((2, LANES, t), f32)],
        args=(proj, kp, vt, keep01, keepneg))


def _out_proj(o, conv_n, x, g_attn, w_out, tm):
    s = x.shape[0]

    def body(o_ref, c_ref, x_ref, g_ref, w_ref, h_ref, cat_ref):
        ones = _group_ones(LANES)
        cat_ref[:, :W_CONV] = c_ref[...]
        for j in range(W_ATTN // LANES):
            cols = slice(j * LANES, (j + 1) * LANES)
            cat_ref[:, W_CONV + j * LANES:W_CONV + (j + 1) * LANES] = _head_rms_fwd(
                o_ref[:, cols], g_ref[:, cols], ones).astype(bf16)
        h_ref[...] = x_ref[...] + _dot(cat_ref[...], w_ref[...])

    return pl.pallas_call(
        body, name="out_proj", grid=(s // tm,),
        in_specs=[pl.BlockSpec((tm, W_ATTN), lambda i: (i, 0)),
                  pl.BlockSpec((tm, W_CONV), lambda i: (i, 0)),
                  pl.BlockSpec((tm, D_MODEL), lambda i: (i, 0)),
                  pl.BlockSpec((1, W_ATTN), lambda i: (0, 0)),
                  pl.BlockSpec((D_MODEL, D_MODEL), lambda i: (0, 0))],
        out_specs=[pl.BlockSpec((tm, D_MODEL), lambda i: (i, 0)),
                   pl.BlockSpec((tm, D_MODEL), lambda i: (i, 0))],
        out_shape=[_sds((s, D_MODEL), f32), _sds((s, D_MODEL), bf16)],
        compiler_params=_arb(1),
    )(o, conv_n, x, g_attn, w_out)


def _mlp_fwd(h1, g_mlp, w_up, w_down, tm):
    s = h1.shape[0]
    fc = D_FF // N_CHIPS

    def body(h_ref, g_ref, wu_ref, wd_ref, h2_ref, u_ref, m_ref):
        j = pl.program_id(1)

        @pl.when(j == 0)
        def _():
            m, _ = _rms_fwd(h_ref[...], g_ref[...])
            m_ref[...] = m.astype(bf16)
            h2_ref[...] = h_ref[...]

        u = _dot(m_ref[...], wu_ref[...])
        u_ref[...] = u.astype(bf16)
        h2_ref[...] += _dot(jnp.square(jnp.maximum(u, 0.0)).astype(bf16), wd_ref[...])

    return pl.pallas_call(
        body, name="mlp_fwd", grid=(s // tm, N_CHIPS),
        in_specs=[pl.BlockSpec((tm, D_MODEL), lambda i, j: (i, 0)),
                  pl.BlockSpec((1, D_MODEL), lambda i, j: (0, 0)),
                  pl.BlockSpec((None, D_MODEL, fc), lambda i, j: (j, 0, 0)),
                  pl.BlockSpec((None, fc, D_MODEL), lambda i, j: (j, 0, 0))],
        out_specs=[pl.BlockSpec((tm, D_MODEL), lambda i, j: (i, 0)),
                   pl.BlockSpec((tm, fc), lambda i, j: (i, j)),
                   pl.BlockSpec((tm, D_MODEL), lambda i, j: (i, 0))],
        out_shape=[_sds((s, D_MODEL), f32), _sds((s, D_FF), bf16), _sds((s, D_MODEL), bf16)],
        compiler_params=_arb(2),
    )(h1, g_mlp, w_up, w_down)


def _tail(h2, p, target, g_ple, g_final, w_gate, w_proj, tm):
    s = h2.shape[0]
    pc = D_MODEL // N_CHIPS

    def body(h_ref, p_ref, t_ref, gp_ref, gf_ref, wg_ref, wp_ref,
             dh_ref, dhb_ref, n3_ref, dgl_ref, dpp_ref, pb_ref, ggp_ref, ggf_ref, loss_ref, pp_ref):
        i = pl.program_id(0)
        h2v = h_ref[...]
        n3, _ = _rms_fwd(h2v, gp_ref[...])
        n3b = n3.astype(bf16)
        n3_ref[...] = n3b
        gate = jax.nn.sigmoid(_dot(n3b, wg_ref[...]))
        pb = p_ref[...].astype(bf16)
        pb_ref[...] = pb
        for k in range(N_CHIPS):
            pp_ref[:, k * pc:(k + 1) * pc] = _dot(pb, wp_ref[k])
        pp = pp_ref[...]
        h3 = h2v + gate * pp
        yv, _ = _rms_fwd(h3, gf_ref[...])
        err = yv - t_ref[...]
        loss = 0.5 * jnp.sum(err * err) * (1.0 / D_MODEL)
        dh3, ggf = _rms_bwd(err * (1.0 / D_MODEL), h3, gf_ref[...])
        dpp_ref[...] = (dh3 * gate).astype(bf16)
        dgl = (dh3 * pp * gate * (1.0 - gate)).astype(bf16)
        dgl_ref[...] = dgl
        dn3 = _dot(dgl, wg_ref[...], NT)
        dh2n, ggp = _rms_bwd(dn3, h2v, gp_ref[...])
        dh2 = dh3 + dh2n
        dh_ref[...] = dh2
        dhb_ref[...] = dh2.astype(bf16)

        @pl.when(i == 0)
        def _():
            ggp_ref[...] = jnp.zeros_like(ggp_ref)
            ggf_ref[...] = jnp.zeros_like(ggf_ref)
            loss_ref[...] = jnp.zeros_like(loss_ref)

        ggp_ref[...] += ggp
        ggf_ref[...] += ggf
        loss_ref[...] += jnp.full(loss_ref.shape, loss, f32)

    tok = lambda w: pl.BlockSpec((tm, w), lambda i: (i, 0))
    vec = lambda w: pl.BlockSpec((1, w), lambda i: (0, 0))
    return pl.pallas_call(
        body, name="tail", grid=(s // tm,),
        in_specs=[tok(D_MODEL), tok(PLE_DIM), tok(D_MODEL), vec(D_MODEL), vec(D_MODEL),
                  pl.BlockSpec((D_MODEL, D_MODEL), lambda i: (0, 0)),
                  pl.BlockSpec((N_CHIPS, PLE_DIM, pc), lambda i: (0, 0, 0))],
        out_specs=[tok(D_MODEL), tok(D_MODEL), tok(D_MODEL), tok(D_MODEL), tok(D_MODEL), tok(PLE_DIM),
                   vec(D_MODEL), vec(D_MODEL), vec(LANES)],
        out_shape=[_sds((s, D_MODEL), f32), _sds((s, D_MODEL), bf16), _sds((s, D_MODEL), bf16),
                   _sds((s, D_MODEL), bf16), _sds((s, D_MODEL), bf16), _sds((s, PLE_DIM), bf16),
                   _sds((1, D_MODEL), f32), _sds((1, D_MODEL), f32), _sds((1, LANES), f32)],
        scratch_shapes=[pltpu.VMEM((tm, D_MODEL), f32)],
        compiler_params=_arb(1),
    )(h2, p, target, g_ple, g_final, w_gate, w_proj)


def _mlp_bwd(dh2, dh2b, h1, u, g_mlp, w_up, w_down, tm):
    s = h1.shape[0]
    fc = D_FF // N_CHIPS

    def body(dh_ref, dhb_ref, h_ref, u_ref, g_ref, wu_ref, wd_ref, dh1_ref, dh1b_ref, du_ref, gg_ref, dm):
        i, j = pl.program_id(0), pl.program_id(1)

        @pl.when(j == 0)
        def _():
            dm[...] = jnp.zeros_like(dm)

        dr = _dot(dhb_ref[...], wd_ref[...], NT)
        du = (dr * (2.0 * jnp.maximum(u_ref[...].astype(f32), 0.0))).astype(bf16)
        du_ref[...] = du
        dm[...] += _dot(du, wu_ref[...], NT)

        @pl.when((i == 0) & (j == 0))
        def _():
            gg_ref[...] = jnp.zeros_like(gg_ref)

        @pl.when(j == N_CHIPS - 1)
        def _():
            dh1n, gg = _rms_bwd(dm[...], h_ref[...], g_ref[...])
            dh1 = dh_ref[...] + dh1n
            dh1_ref[...] = dh1
            dh1b_ref[...] = dh1.astype(bf16)
            gg_ref[...] += gg

    tok = pl.BlockSpec((tm, D_MODEL), lambda i, j: (i, 0))
    ffb = pl.BlockSpec((tm, fc), lambda i, j: (i, j))
    vec = pl.BlockSpec((1, D_MODEL), lambda i, j: (0, 0))
    return pl.pallas_call(
        body, name="mlp_bwd", grid=(s // tm, N_CHIPS),
        in_specs=[tok, tok, tok, ffb, vec,
                  pl.BlockSpec((None, D_MODEL, fc), lambda i, j: (j, 0, 0)),
                  pl.BlockSpec((None, fc, D_MODEL), lambda i, j: (j, 0, 0))],
        out_specs=[tok, tok, ffb, vec],
        out_shape=[_sds((s, D_MODEL), f32), _sds((s, D_MODEL), bf16), _sds((s, D_FF), bf16),
                   _sds((1, D_MODEL), f32)],
        scratch_shapes=[pltpu.VMEM((tm, D_MODEL), f32)],
        compiler_params=_arb(2),
    )(dh2, dh2b, h1, u, g_mlp, w_up, w_down)


def _out_proj_bwd(dh1b, o, g_attn, w_out, tm, ex=None):
    s = o.shape[0]

    def body(dh_ref, o_ref, g_ref, w_ref, dc_ref, do_ref, gg_ref, dcat):
        i = pl.program_id(0)
        ones = _group_ones(LANES)
        dcat[...] = _dot(dh_ref[...], w_ref[...], NT)
        dc_ref[...] = dcat[:, :W_CONV]

        @pl.when(i == 0)
        def _():
            gg_ref[...] = jnp.zeros_like(gg_ref)

        for j in range(W_ATTN // LANES):
            cols = slice(j * LANES, (j + 1) * LANES)
            d, gg = _head_rms_bwd(dcat[:, W_CONV + j * LANES:W_CONV + (j + 1) * LANES],
                                  o_ref[:, cols], g_ref[:, cols], ones)
            do_ref[:, cols] = d
            gg_ref[:, cols] += gg

    return _call_carrying(
        ex, body, "out_proj_bwd", (s // tm,),
        in_specs=[pl.BlockSpec((tm, D_MODEL), lambda i: (i, 0)),
                  pl.BlockSpec((tm, W_ATTN), lambda i: (i, 0)),
                  pl.BlockSpec((1, W_ATTN), lambda i: (0, 0)),
                  pl.BlockSpec((D_MODEL, D_MODEL), lambda i: (0, 0))],
        out_specs=[pl.BlockSpec((tm, W_CONV), lambda i: (i, 0)),
                   pl.BlockSpec((tm, W_ATTN), lambda i: (i, 0)),
                   pl.BlockSpec((1, W_ATTN), lambda i: (0, 0))],
        out_shape=[_sds((s, W_CONV), f32), _sds((s, W_ATTN), f32), _sds((1, W_ATTN), f32)],
        scratch_shapes=[pltpu.VMEM((tm, D_MODEL), f32)],
        args=(dh1b, o, g_attn, w_out))


def _attn_bwd(proj, kp, vp, kt, do, tot, ex=None):
    s = proj.shape[0]
    t = ATTN_TILE
    nblk = s // t
    npair = W_ATTN // LANES
    qoff = 3 * W_CONV // LANES
    keep01, keepneg = _causal_tiles()

    def body(q_ref, k_ref, v_ref, kt_ref, do_ref, tot_ref, m01_ref, neg_ref, dq_ref, dk_ref, dv_ref,
             w_s, da_s, b_s, g_s, a_s, dz_s, dq_acc, dk_acc, dv_acc):
        qb = pl.program_id(1)
        first = _head_pair_masks(t)
        q = q_ref[...] * (HEAD_DIM ** -0.5)
        qh = (jnp.where(first, q, 0.0).astype(bf16), jnp.where(first, 0.0, q).astype(bf16))
        dov = do_ref[...]
        doh = (jnp.where(first, dov, 0.0).astype(bf16), jnp.where(first, 0.0, dov).astype(bf16))
        total = (tot_ref[0:1, :], tot_ref[1:2, :])

        @pl.when(qb == 0)
        def _():
            dk_acc[...] = jnp.zeros_like(dk_acc)
            dv_acc[...] = jnp.zeros_like(dv_acc)

        dq_acc[...] = jnp.zeros_like(dq_acc)
        a_s[1] = jnp.zeros((t, t), bf16)
        dz_s[1] = jnp.zeros((t, t), bf16)

        def scores(kb, h):
            w_s[h] = _dot(k_ref[kb], qh[h], NT)
            da_s[h] = _dot(v_ref[kb], doh[h], NT)

        def spread(kb, h):
            dq_acc[h] += _dot(kt_ref[kb], dz_s[h])
            dk_acc[kb] += _dot(dz_s[h], qh[h])
            dv_acc[kb] += _dot(a_s[h], doh[h])

        def grads(h, diagonal, lk_before, g_before):
            run = jnp.zeros((SUBLANES, t), f32)
            for a in range(KEY_RUN):
                rows = slice(SUBLANES * a, SUBLANES * (a + 1))
                z2 = w_s[h, rows, :] * LOG2_E
                lk = _log2_keep(z2)
                if diagonal:
                    lk = lk * m01_ref[rows, :]
                log_beta = jnp.minimum(z2 + lk, 0.0)
                run = run + lk
                b_s[h, rows, :] = jnp.exp2(log_beta)
                w_s[h, rows, :] = log_beta - run
            off = total[h] - lk_before - _sublane_scan(run, reverse=False)
            lk_sum = jnp.sum(run, axis=0, keepdims=True)
            run = jnp.zeros((SUBLANES, t), f32)
            for a in range(KEY_RUN // 2):
                parts = []
                for r in (slice(2 * SUBLANES * a, 2 * SUBLANES * a + SUBLANES),
                          slice(2 * SUBLANES * a + SUBLANES, 2 * SUBLANES * (a + 1))):
                    w = w_s[h, r, :] + off
                    if diagonal:
                        w = w + neg_ref[r, :]
                    av = jnp.exp2(w)
                    g = av * da_s[h, r, :]
                    run = run + g
                    da_s[h, r, :] = g
                    g_s[h, r, :] = run
                    parts.append(av)
                a_s[h, 2 * SUBLANES * a:2 * SUBLANES * (a + 1), :] = jnp.concatenate(parts, axis=0).astype(bf16)
            goff = g_before + _sublane_scan(run, reverse=False)
            goff2 = jnp.concatenate([goff, goff], axis=0)
            for a in range(KEY_RUN // 2):
                rows = slice(2 * SUBLANES * a, 2 * SUBLANES * (a + 1))
                dz = da_s[h, rows, :] - b_s[h, rows, :] * (g_s[h, rows, :] + goff2)
                if diagonal:
                    dz = dz * m01_ref[rows, :]
                dz_s[h, rows, :] = dz.astype(bf16)
            return lk_before + lk_sum, g_before + jnp.sum(run, axis=0, keepdims=True)

        def block(kb, before, after, diagonal, carry):
            scores(kb, 1)
            spread(before, 1)
            c0 = grads(0, diagonal, carry[0], carry[1])
            if after is not None:
                scores(after, 0)
            spread(kb, 0)
            c1 = grads(1, diagonal, carry[2], carry[3])
            return c0 + c1

        zero = jnp.zeros((1, t), f32)
        scores(0, 0)
        def two_blocks(i, c):
            kb = 2 * i
            c = block(kb, jnp.maximum(kb - 1, 0), kb + 1, False, c)
            return block(kb + 1, kb, kb + 2, False, c)

        carry = lax.fori_loop(0, qb // 2, two_blocks, (zero, zero, zero, zero))
        carry = lax.fori_loop(qb - qb % 2, qb, lambda kb, c: block(kb, jnp.maximum(kb - 1, 0), kb + 1, False, c), carry)
        block(qb, jnp.maximum(qb - 1, 0), None, True, carry)
        spread(qb, 1)
        top = lax.broadcasted_iota(jnp.int32, (LANES, t), 0) < HEAD_DIM
        dq_ref[...] = (jnp.where(top, dq_acc[0], dq_acc[1]).T * (HEAD_DIM ** -0.5)).astype(bf16)

        @pl.when(qb == nblk - 1)
        def _():
            for kb in range(nblk):
                for b in range(SUBLANES):
                    keys_b = slice(KEY_RUN * b, KEY_RUN * (b + 1))
                    dk_ref[kb, keys_b, :] = dk_acc[kb, pl.ds(b, KEY_RUN, stride=SUBLANES), :].astype(bf16)
                    dv_ref[kb, keys_b, :] = dv_acc[kb, pl.ds(b, KEY_RUN, stride=SUBLANES), :].astype(bf16)

    keys = pl.BlockSpec((nblk, t, LANES), lambda p, i: (0, 0, p))
    tile = pl.BlockSpec((t, t), lambda p, i: (0, 0))
    return _call_carrying(
        ex, body, "attn_bwd", (npair, nblk),
        in_specs=[pl.BlockSpec((t, LANES), lambda p, i: (i, qoff + p)),
                  keys, keys,
                  pl.BlockSpec((nblk, LANES, t), lambda p, i: (0, p, 0)),
                  pl.BlockSpec((t, LANES), lambda p, i: (i, p)),
                  pl.BlockSpec((None, SUBLANES, t), lambda p, i: (p, 0, i)),
                  tile, tile],
        out_specs=[pl.BlockSpec((t, LANES), lambda p, i: (i, p)), keys, keys],
        out_shape=[_sds((s, W_ATTN), bf16), _sds((nblk, t, W_ATTN), bf16), _sds((nblk, t, W_ATTN), bf16)],
        scratch_shapes=[pltpu.VMEM((2, t, t), f32), pltpu.VMEM((2, t, t), f32), pltpu.VMEM((2, t, t), f32),
                        pltpu.VMEM((2, t, t), f32), pltpu.VMEM((2, t, t), bf16), pltpu.VMEM((2, t, t), bf16),
                        pltpu.VMEM((2, LANES, t), f32), pltpu.VMEM((nblk, t, LANES), f32),
                        pltpu.VMEM((nblk, t, LANES), f32)],
        args=(proj, kp, vp, kt, do, tot, keep01, keepneg))


def _conv_bwd(proj, dcn, conv_w, g_conv):
    s = proj.shape[0]
    nblk = W_CONV // LANES
    rc = min(CONV_CHUNK, s)
    nchunk = s // rc

    def body(cb_ref, cc_ref, cu_ref, d_ref, w_ref, g_ref, d3_ref, gw_ref, gg_ref, dy_buf):
        ones = _group_ones(LANES)
        w0, w1, w2 = w_ref[0:1, :], w_ref[1:2, :], w_ref[2:3, :]
        g = g_ref[...]

        def first_pass(i, carry):
            gw0, gw1, gw2, gg = carry
            r0 = pl.multiple_of(i * rc, rc)
            rows = pl.ds(r0, rc)
            prev = pl.ds(pl.multiple_of(jnp.maximum(r0 - 8, 0), 8), 8)
            v = cc_ref[rows, :] * cu_ref[rows, :]
            vp = jnp.where(i > 0, cc_ref[prev, :] * cu_ref[prev, :], 0.0)
            v1, v2 = _shifted(vp, v, 1), _shifted(vp, v, 2)
            y = w2 * v + w1 * v1 + w0 * v2
            cb = cb_ref[rows, :]
            dcy, ggi = _head_rms_bwd(d_ref[rows, :], cb * y, g, ones)
            d3_ref[0, rows, :] = (dcy * y).astype(bf16)
            dy = dcy * cb
            dy_buf[rows, :] = dy
            return (gw0 + jnp.sum(dy * v2, axis=0, keepdims=True), gw1 + jnp.sum(dy * v1, axis=0, keepdims=True),
                    gw2 + jnp.sum(dy * v, axis=0, keepdims=True), gg + ggi)

        zero = jnp.zeros((1, LANES), f32)
        gw0, gw1, gw2, gg = lax.fori_loop(0, nchunk, first_pass, (zero, zero, zero, zero))
        gw_ref[...] = jnp.zeros_like(gw_ref)
        gw_ref[0:1, :] = gw0
        gw_ref[1:2, :] = gw1
        gw_ref[2:3, :] = gw2
        gg_ref[...] = gg

        def second_pass(i, carry):
            r0 = pl.multiple_of(i * rc, rc)
            rows = pl.ds(r0, rc)
            nxt = pl.ds(pl.multiple_of(jnp.minimum(r0 + rc, s - 8), 8), 8)
            dy = dy_buf[rows, :]
            dyn = jnp.where(i < nchunk - 1, dy_buf[nxt, :], 0.0)
            ext = jnp.concatenate([dy, dyn], axis=0)
            up1 = pltpu.roll(ext, rc + 8 - 1, axis=0)[:rc]
            up2 = pltpu.roll(ext, rc + 8 - 2, axis=0)[:rc]
            dv = w2 * dy + w1 * up1 + w0 * up2
            d3_ref[1, rows, :] = (dv * cu_ref[rows, :]).astype(bf16)
            d3_ref[2, rows, :] = (dv * cc_ref[rows, :]).astype(bf16)
            return carry

        lax.fori_loop(0, nchunk, second_pass, 0)

    def col(off):
        return pl.BlockSpec((s, LANES), lambda j: (0, off + j))

    return pl.pallas_call(
        body, name="conv_bwd", grid=(nblk,),
        in_specs=[col(0), col(nblk), col(2 * nblk), col(0),
                  pl.BlockSpec((None, CONV_W_ROWS, LANES), lambda j: (j, 0, 0)),
                  pl.BlockSpec((1, LANES), lambda j: (0, j))],
        out_specs=[pl.BlockSpec((3, s, LANES), lambda j: (0, 0, j)),
                   pl.BlockSpec((None, CONV_W_ROWS, LANES), lambda j: (j, 0, 0)),
                   pl.BlockSpec((1, LANES), lambda j: (0, j))],
        out_shape=[_sds((3, s, W_CONV), bf16), _sds((nblk, CONV_W_ROWS, LANES), f32), _sds((1, W_CONV), f32)],
        scratch_shapes=[pltpu.VMEM((s, LANES), f32)],
        compiler_params=_arb(1),
    )(proj, proj, proj, dcn, conv_w, g_conv)


PIECE = W_CONV


def _piece_spans():
    ncol = IN_COLS // N_CHIPS
    spans = []
    for p in range(IN_COLS // PIECE):
        for k in range(N_CHIPS):
            lo, hi = max(p * PIECE, k * ncol), min((p + 1) * PIECE, (k + 1) * ncol)
            if lo < hi:
                spans.append((p, lo - p * PIECE, hi - p * PIECE, k, lo - k * ncol, hi - k * ncol))
    return spans


def _in_proj_bwd(dconv, dq, dk, dv, dh1, x, g_mix, w_in, tm, ex=None):
    s = x.shape[0]
    ncol = IN_COLS // N_CHIPS

    def body(dc_ref, dq_ref, dk_ref, dv_ref, dh_ref, x_ref, g_ref, w_ref, dx_ref, gg_ref):
        i = pl.program_id(0)
        pieces = [dc_ref.at[0], dc_ref.at[1], dc_ref.at[2], dq_ref, dk_ref, dv_ref]
        da = jnp.zeros((tm, D_MODEL), f32)
        for p, plo, phi, k, wlo, whi in _piece_spans():
            da += _dot(pieces[p][:, plo:phi], w_ref[k, :, wlo:whi], NT)
        dxn, gg = _rms_bwd(da, x_ref[...], g_ref[...])
        dx_ref[...] = dh_ref[...] + dxn

        @pl.when(i == 0)
        def _():
            gg_ref[...] = jnp.zeros_like(gg_ref)

        gg_ref[...] += gg

    return _call_carrying(
        ex, body, "in_proj_bwd", (s // tm,),
        in_specs=[pl.BlockSpec((3, tm, PIECE), lambda i: (0, i, 0)),
                  pl.BlockSpec((tm, PIECE), lambda i: (i, 0)),
                  pl.BlockSpec((tm, PIECE), lambda i: (i, 0)),
                  pl.BlockSpec((tm, PIECE), lambda i: (i, 0)),
                  pl.BlockSpec((tm, D_MODEL), lambda i: (i, 0)),
                  pl.BlockSpec((tm, D_MODEL), lambda i: (i, 0)),
                  pl.BlockSpec((1, D_MODEL), lambda i: (0, 0)),
                  pl.BlockSpec((N_CHIPS, D_MODEL, ncol), lambda i: (0, 0, 0))],
        out_specs=[pl.BlockSpec((tm, D_MODEL), lambda i: (i, 0)),
                   pl.BlockSpec((1, D_MODEL), lambda i: (0, 0))],
        out_shape=[_sds((s, D_MODEL), f32), _sds((1, D_MODEL), f32)],
        scratch_shapes=[],
        args=(dconv, dq, dk, dv, dh1, x, g_mix, w_in))


def _grad_w_in(a, dconv, dq, dk, dv, ts):
    s = a.shape[0]
    ncol = IN_COLS // N_CHIPS

    def body(a_ref, dc_ref, dq_ref, dk_ref, dv_ref, o_ref):
        @pl.when(pl.program_id(0) == 0)
        def _():
            o_ref[...] = jnp.zeros_like(o_ref)

        pieces = [dc_ref.at[0], dc_ref.at[1], dc_ref.at[2], dq_ref, dk_ref, dv_ref]
        av = a_ref[...]
        for p, plo, phi, k, wlo, whi in _piece_spans():
            o_ref[k, :, wlo:whi] += _dot(av, pieces[p][:, plo:phi], TN)

    tok = pl.BlockSpec((ts, PIECE), lambda i: (i, 0))
    return pl.pallas_call(
        body, name="grad_w_in", grid=(s // ts,),
        in_specs=[pl.BlockSpec((ts, D_MODEL), lambda i: (i, 0)),
                  pl.BlockSpec((3, ts, PIECE), lambda i: (0, i, 0)), tok, tok, tok],
        out_specs=pl.BlockSpec((N_CHIPS, D_MODEL, ncol), lambda i: (0, 0, 0)),
        out_shape=_sds((N_CHIPS, D_MODEL, ncol), f32),
        compiler_params=_arb(1),
    )(a, dconv, dq, dk, dv)


def _weight_grad(a, b, bm, bn, ts, name, relu_sq=False):
    s, m = a.shape
    n = b.shape[1]
    nn = n // bn
    nk = s // ts

    def body(a_ref, b_ref, o_ref, ob_ref):
        @pl.when(pl.program_id(2) == 0)
        def _():
            o_ref[...] = jnp.zeros_like(o_ref)

        av = a_ref[...]
        if relu_sq:
            av = jnp.square(jnp.maximum(av.astype(f32), 0.0)).astype(bf16)
        o_ref[...] += _dot(av, b_ref[...], TN)

        @pl.when(pl.program_id(2) == nk - 1)
        def _():
            ob_ref[...] = o_ref[...].astype(bf16)

    tile = pl.BlockSpec((None, bm, bn), lambda i, j, k: (i * nn + j, 0, 0))
    return pl.pallas_call(
        body, name=name, grid=(m // bm, nn, nk),
        in_specs=[pl.BlockSpec((ts, bm), lambda i, j, k: (k, i)),
                  pl.BlockSpec((ts, bn), lambda i, j, k: (k, j))],
        out_specs=[tile, tile],
        out_shape=[_sds(((m // bm) * nn, bm, bn), f32), _sds(((m // bm) * nn, bm, bn), bf16)],
        compiler_params=_arb(3),
    )(a, b)


def kernel(x, p, g_mix, w_in, conv_w, g_conv_out, g_attn_out, w_out, g_mlp, w_up, w_down, g_ple, w_ple_gate, w_ple_proj, g_final, loss_target, m_g_mix, m_w_in, m_conv_w, m_g_conv_out, m_g_attn_out, m_w_out, m_g_mlp, m_w_up, m_w_down, m_g_ple, m_w_ple_gate, m_w_ple_proj, m_g_final, v_g_mix, v_w_in, v_conv_w, v_g_conv_out, v_g_attn_out, v_w_out, v_g_mlp, v_w_up, v_w_down, v_g_ple, v_w_ple_gate, v_w_ple_proj, v_g_final):
    s = x.shape[1]
    tm = min(TOKEN_TILE, s)
    tg = min(GRAD_TOKEN_TILE, s)
    xs = x.reshape(s, D_MODEL)
    ps = p.reshape(s, PLE_DIM)
    target = loss_target.reshape(s, D_MODEL)
    core = lax.axis_index("c").astype(jnp.int32).reshape(1)
    chip = 2 * lax.axis_index("x") + lax.axis_index("y")

    big = {"w_in": w_in[0], "w_out": w_out[0], "w_up": w_up[0], "w_down": w_down[0],
           "w_ple_gate": w_ple_gate[0], "w_ple_proj": w_ple_proj[0]}
    names = list(big)
    conv_shard = jnp.pad(conv_w[0], ((0, CONV_W_ROWS - conv_w.shape[1]), (0, 0)))
    later_names = names[1:]
    w_in_f, conv_f = _run_exchange(_gather_exchange([big["w_in"].astype(bf16), conv_shard]), "gather_w_in")

    proj, a_b, kp, vp, kt, vt = _in_proj(xs, g_mix, w_in_f, tm)
    conv_n = _conv_fwd(proj, conv_f, g_conv_out)
    (o, tot), gathered = _attn_fwd(proj, kp, vt, _gather_exchange([big[k].astype(bf16) for k in later_names]))
    w_out_f, w_up_f, w_down_f, w_gate_f, w_proj_f = gathered
    w_out_f = w_out_f.reshape(D_MODEL, D_MODEL)
    w_gate_f = w_gate_f.reshape(D_MODEL, D_MODEL)
    h1, cat_b = _out_proj(o, conv_n, xs, g_attn_out, w_out_f, tm)
    h2, u_b, m_b = _mlp_fwd(h1, g_mlp, w_up_f, w_down_f, min(MLP_TOKEN_TILE, s))

    dh2, dh2_b, n3_b, dgl_b, dpp_b, p_b, gg_ple, gg_final, loss_row = _tail(
        h2, ps, target, g_ple, g_final.reshape(1, D_MODEL), w_gate_f, w_proj_f, tm)
    dh1, dh1_b, du_b, gg_mlp = _mlp_bwd(dh2, dh2_b, h1, u_b, g_mlp, w_up_f, w_down_f, tm)
    both_kinds = {
        "w_out": _weight_grad(cat_b, dh1_b, D_MODEL, D_MODEL, tg, "grad_w_out"),
        "w_up": _weight_grad(m_b, du_b, D_MODEL, D_FF // N_CHIPS, tg, "grad_w_up"),
        "w_down": _weight_grad(u_b, dh2_b, D_FF // N_CHIPS, D_MODEL, tg, "grad_w_down", relu_sq=True),
        "w_ple_gate": _weight_grad(n3_b, dgl_b, D_MODEL, D_MODEL, tg, "grad_w_ple_gate"),
        "w_ple_proj": _weight_grad(p_b, dpp_b, PLE_DIM, D_MODEL // N_CHIPS, tg, "grad_w_ple_proj"),
    }
    by_chip = lambda k, g: g.reshape((N_CHIPS, big[k].shape[0], big[k].shape[1]))
    part = {k: by_chip(k, g32) for k, (g32, _) in both_kinds.items()}
    part_b = {k: by_chip(k, g16) for k, (_, g16) in both_kinds.items()}
    (dcn, do, gg_attn), from_sibling = _out_proj_bwd(
        dh1_b, o, g_attn_out, w_out_f, tm, _pair_exchange([part_b[k] for k in later_names]))
    pair = [_pair_sum(part[k], r, core, "pair_sum_" + k) for k, r in zip(later_names, from_sibling)]
    dconv, g_conv_w, gg_conv = _conv_bwd(proj, dcn, conv_f, g_conv_out)
    (dq, dk, dv), from_chips = _attn_bwd(proj, kp, vp, kt, do, tot, _chip_exchange(pair))
    dk, dv = dk.reshape(s, W_ATTN), dv.reshape(s, W_ATTN)

    part["w_in"] = _grad_w_in(a_b, dconv, dq, dk, dv, min(MLP_TOKEN_TILE, s))
    in_sibling = _run_exchange(_pair_exchange([part["w_in"]]), "grad_pair_exchange_w_in")
    in_pair = _pair_sum(part["w_in"], in_sibling[0], core, "pair_sum_w_in")
    (grad_x, gg_mix), in_chips = _in_proj_bwd(
        dconv, dq, dk, dv, dh1, xs, g_mix, w_in_f, tm, _chip_exchange([in_pair]))

    place = jnp.stack([lax.axis_index("c"), chip]).astype(jnp.int32)
    half = [_chip_sum(mine, landed, place, "chip_sum_" + k)
            for k, mine, landed in zip(names, [in_pair] + pair, list(in_chips) + list(from_chips))]
    both = _sibling_exchange(half)
    grad = {k: b.reshape(big[k].shape) for k, b in zip(names, both)}

    gcw = g_conv_w[:, :3, :].transpose(1, 0, 2).reshape(3, W_CONV)
    row = lambda *parts: jnp.concatenate(parts, axis=1)
    packed = jnp.concatenate([
        gg_mix, gg_mlp, gg_ple, gg_final, row(gg_conv, gg_attn), row(gcw[0:1], gcw[1:2]),
        row(gcw[2:3], loss_row, jnp.zeros((1, W_CONV - LANES), f32)), jnp.zeros((1, D_MODEL), f32)], axis=0)
    summed = _allreduce_small(packed)
    loss = summed[6, W_CONV]
    gcw_full = jnp.stack([summed[5, :W_CONV], summed[5, W_CONV:], summed[6, :W_CONV]])
    grad["conv_w"] = lax.dynamic_slice(gcw_full, (0, chip * LANES), (3, LANES))
    vec_names = ["g_mix", "g_mlp", "g_ple", "g_final"]
    vec_w = {"g_mix": g_mix, "g_mlp": g_mlp, "g_ple": g_ple, "g_final": g_final.reshape(1, D_MODEL)}
    vec_m = {"g_mix": m_g_mix, "g_mlp": m_g_mlp, "g_ple": m_g_ple, "g_final": m_g_final.reshape(1, D_MODEL)}
    vec_v = {"g_mix": v_g_mix, "g_mlp": v_g_mlp, "g_ple": v_g_ple, "g_final": v_g_final.reshape(1, D_MODEL)}

    def pack_vec(d, conv, attn):
        return jnp.concatenate([d[k] for k in vec_names] + [row(conv, attn)], axis=0)

    vec_g = summed[0:5]
    vec_d, vec_nm, vec_nv = _adamw(vec_g, pack_vec(vec_w, g_conv_out, g_attn_out),
                                   pack_vec(vec_m, m_g_conv_out, m_g_attn_out),
                                   pack_vec(vec_v, v_g_conv_out, v_g_attn_out), "adamw_vectors")

    given_w = dict(big, conv_w=conv_w[0])
    given_m = {"w_in": m_w_in[0], "w_out": m_w_out[0], "w_up": m_w_up[0], "w_down": m_w_down[0],
               "w_ple_gate": m_w_ple_gate[0], "w_ple_proj": m_w_ple_proj[0], "conv_w": m_conv_w[0]}
    given_v = {"w_in": v_w_in[0], "w_out": v_w_out[0], "w_up": v_w_up[0], "w_down": v_w_down[0],
               "w_ple_gate": v_w_ple_gate[0], "w_ple_proj": v_w_ple_proj[0], "conv_w": v_conv_w[0]}
    delta, new_m, new_v = {}, {}, {}
    for k in names + ["conv_w"]:
        delta[k], new_m[k], new_v[k] = _adamw(grad[k], given_w[k], given_m[k], given_v[k], "adamw_" + k)

    def unpack(vals, kind):
        out = {k: vals[i:i + 1] for i, k in enumerate(vec_names)}
        out["g_final"] = out["g_final"].reshape(D_MODEL)
        out["g_conv_out"] = vals[4:5, :W_CONV]
        out["g_attn_out"] = vals[4:5, W_CONV:]
        out.update({k: v[None] for k, v in kind.items()})
        return out

    order = ["g_mix", "w_in", "conv_w", "g_conv_out", "g_attn_out", "w_out", "g_mlp", "w_up", "w_down",
             "g_ple", "w_ple_gate", "w_ple_proj", "g_final"]
    groups = [unpack(vec_g, grad), unpack(vec_d, delta), unpack(vec_nm, new_m), unpack(vec_nv, new_v)]
    return (loss, grad_x[None]) + tuple(g[k] for g in groups for k in order)
```

```python
import functools

import jax
import jax.numpy as jnp
from jax import lax
from jax.experimental import pallas as pl
from jax.experimental.pallas import tpu as pltpu

f32 = jnp.float32
bf16 = jnp.bfloat16

D_MODEL = 1024
HEAD_DIM = 64
W_CONV = 512
W_ATTN = 512
D_FF = 4096
PLE_DIM = 256
IN_COLS = 3 * W_CONV + 3 * W_ATTN
N_CHIPS = 4
EPS = 1e-6
ADAM_LR = 0.001
ADAM_B1 = 0.9
ADAM_B2 = 0.999
ADAM_EPS = 1e-08
ADAM_WD = 0.01
ADAM_STEP = 10

LANES = 128
TOKEN_TILE = 512
MLP_TOKEN_TILE = 1024
GRAD_TOKEN_TILE = 2048
ATTN_TILE = 256
CONV_CHUNK = 512
CONV_W_ROWS = 16

MESH = pl.DeviceIdType.MESH
ANY = pl.BlockSpec(memory_space=pl.ANY)
NT = (((1,), (1,)), ((), ()))
TN = (((0,), (0,)), ((), ()))


def _arb(n):
    return pltpu.CompilerParams(dimension_semantics=("arbitrary",) * n)


def _sds(shape, dtype):
    return jax.ShapeDtypeStruct(shape, dtype)


def _dot(a, b, dims=None):
    if dims is None:
        return jnp.dot(a, b, preferred_element_type=f32)
    return lax.dot_general(a, b, dims, preferred_element_type=f32)


def _split_dot(x, ones):
    hi = x.astype(bf16)
    lo = (x - hi.astype(f32)).astype(bf16)
    return _dot(hi, ones) + _dot(lo, ones)


def _rms_fwd(h, g):
    rstd = lax.rsqrt(jnp.mean(h * h, axis=-1, keepdims=True) + EPS)
    return h * rstd * g, rstd


def _rms_bwd(dy, h, g):
    rstd = lax.rsqrt(jnp.mean(h * h, axis=-1, keepdims=True) + EPS)
    hn = h * rstd
    dyg = dy * g
    dh = rstd * (dyg - hn * jnp.mean(dyg * hn, axis=-1, keepdims=True))
    return dh, jnp.sum(dy * hn, axis=0, keepdims=True)


def _group_ones(n):
    r = lax.broadcasted_iota(jnp.int32, (n, n), 0) // HEAD_DIM
    c = lax.broadcasted_iota(jnp.int32, (n, n), 1) // HEAD_DIM
    return (r == c).astype(bf16)


def _head_rms_fwd(y, g, ones):
    rstd = lax.rsqrt(_split_dot(y * y, ones) * (1.0 / HEAD_DIM) + EPS)
    return y * rstd * g


def _head_rms_bwd(dy, y, g, ones):
    rstd = lax.rsqrt(_split_dot(y * y, ones) * (1.0 / HEAD_DIM) + EPS)
    yn = y * rstd
    dyg = dy * g
    dyy = rstd * (dyg - yn * (_split_dot(dyg * yn, ones) * (1.0 / HEAD_DIM)))
    return dyy, jnp.sum(dy * yn, axis=0, keepdims=True)


def _place():
    return lax.axis_index("x"), lax.axis_index("y"), lax.axis_index("c")


def _other_chips(x, y):
    return [(1 - x, y), (x, 1 - y), (1 - x, 1 - y)]


class _Exchange:
    def __init__(self, arrays, out_shapes, sems, start, finish, relay=None):
        self.arrays, self.out_shapes, self.sems, self.start, self.finish = arrays, out_shapes, sems, start, finish
        self.relay = relay


def _gather_exchange(shards):
    n = len(shards)
    halves = [s.shape[0] // 2 for s in shards]

    def plan(ins, outs, sems):
        send_sems, recv_sems, own_sems = sems
        x, y, c = _place()
        me = 2 * x + y
        chips = _other_chips(x, y)

        def half(ref, i, which):
            return ref.at[pl.ds(which * halves[i], halves[i]), :]

        def over_ici(i, j, src, slot, to):
            return pltpu.make_async_remote_copy(
                src_ref=src, dst_ref=half(outs[i].at[slot], i, c),
                send_sem=send_sems.at[3 * i + j], recv_sem=recv_sems.at[3 * i + j],
                device_id=to, device_id_type=MESH)

        def to_sibling(i, j, slot, which):
            blk = half(outs[i].at[slot], i, which)
            return pltpu.make_async_remote_copy(
                src_ref=blk, dst_ref=blk,
                send_sem=send_sems.at[3 * n + 3 * i + j], recv_sem=recv_sems.at[3 * n + 3 * i + j],
                device_id=(x, y, 1 - c), device_id_type=MESH)

        own = [pltpu.make_async_remote_copy(
            src_ref=ins[i], dst_ref=outs[i].at[me], send_sem=own_sems.at[i], recv_sem=own_sems.at[n + i],
            device_id=(x, y, 1 - c), device_id_type=MESH) for i in range(n)]
        pairs = [(i, j, px, py) for i in range(n) for j, (px, py) in enumerate(chips)]
        sends = [over_ici(i, j, half(ins[i], i, c), me, (px, py, c)) for i, j, px, py in pairs]
        lands = [over_ici(i, j, half(outs[i].at[2 * px + py], i, c), 2 * px + py, (px, py, c)) for i, j, px, py in pairs]
        passes = [to_sibling(i, j, 2 * px + py, c) for i, j, px, py in pairs]
        from_sibling = [to_sibling(i, j, 2 * px + py, 1 - c) for i, j, px, py in pairs]
        return own, sends, lands, passes, from_sibling

    def start(ins, outs, sems):
        own, sends, _, _, _ = plan(ins, outs, sems)
        for cp in own + sends:
            cp.start()

    def relay(ins, outs, sems):
        _, _, lands, passes, _ = plan(ins, outs, sems)
        for land, on in zip(lands, passes):
            land.wait_recv()
            on.start()

    def finish(ins, outs, sems):
        own, sends, _, passes, from_sibling = plan(ins, outs, sems)
        for cp in from_sibling:
            cp.wait_recv()
        for cp in sends + passes:
            cp.wait_send()
        for cp in own:
            cp.wait()

    return _Exchange(
        shards, [_sds((N_CHIPS,) + s.shape, s.dtype) for s in shards],
        [pltpu.SemaphoreType.DMA((6 * n,)), pltpu.SemaphoreType.DMA((6 * n,)), pltpu.SemaphoreType.DMA((2 * n,))],
        start, finish, relay)


def _call_carrying(ex, body, name, grid, in_specs, out_specs, out_shape, scratch_shapes, args):
    n_in, n_out, n_scr = len(in_specs), len(out_specs), len(scratch_shapes)
    k = 0 if ex is None else len(ex.arrays)

    def wrapped(*refs):
        ins, xin = refs[:n_in], refs[n_in:n_in + k]
        outs, xout = refs[n_in + k:n_in + k + n_out], refs[n_in + k + n_out:n_in + 2 * k + n_out]
        scr, sems = refs[n_in + 2 * k + n_out:n_in + 2 * k + n_out + n_scr], refs[n_in + 2 * k + n_out + n_scr:]
        ids = [pl.program_id(d) for d in range(len(grid))]
        if ex is not None:
            @pl.when(functools.reduce(lambda a, b: a & b, [i == 0 for i in ids]))
            def _():
                ex.start(xin, xout, sems)

        if ex is not None and ex.relay is not None:
            relay_at = [grid[0] - 1] + [0] * (len(grid) - 1) if len(grid) > 1 else [grid[0] - 1]

            @pl.when(functools.reduce(lambda a, b: a & b, [i == r for i, r in zip(ids, relay_at)]))
            def _():
                ex.relay(xin, xout, sems)

        body(*ins, *outs, *scr)
        if ex is not None:
            @pl.when(functools.reduce(lambda a, b: a & b, [i == g - 1 for i, g in zip(ids, grid)]))
            def _():
                ex.finish(xin, xout, sems)

    res = pl.pallas_call(
        wrapped, name=name, grid=grid,
        in_specs=list(in_specs) + [ANY] * k, out_specs=list(out_specs) + [ANY] * k,
        out_shape=list(out_shape) + ([] if ex is None else list(ex.out_shapes)),
        scratch_shapes=list(scratch_shapes) + ([] if ex is None else list(ex.sems)),
        compiler_params=_arb(len(grid)),
    )(*args, *([] if ex is None else ex.arrays))
    return res[:n_out], res[n_out:]


def _run_exchange(ex, name):
    n = len(ex.arrays)

    def body(*refs):
        ins, outs, sems = refs[:n], refs[n:2 * n], refs[2 * n:]
        ex.start(ins, outs, sems)
        if ex.relay is not None:
            ex.relay(ins, outs, sems)
        ex.finish(ins, outs, sems)

    return pl.pallas_call(
        body, name=name, out_shape=ex.out_shapes, in_specs=[ANY] * n, out_specs=[ANY] * n,
        scratch_shapes=ex.sems,
    )(*ex.arrays)


def _pair_exchange(grads):
    n = len(grads)

    def plan(ins, outs, sems):
        send_sems, recv_sems = sems
        x, y, c = _place()
        return [pltpu.make_async_remote_copy(
            src_ref=ins[i].at[:, 1 - c], dst_ref=outs[i],
            send_sem=send_sems.at[i], recv_sem=recv_sems.at[i],
            device_id=(x, y, 1 - c), device_id_type=MESH) for i in range(n)]

    def start(ins, outs, sems):
        for cp in plan(ins, outs, sems):
            cp.start()

    def finish(ins, outs, sems):
        for cp in plan(ins, outs, sems):
            cp.wait()

    views = [g.reshape(N_CHIPS, 2, g.shape[1] // 2, g.shape[2]) for g in grads]
    return _Exchange(
        views, [_sds((N_CHIPS, v.shape[2], v.shape[3]), v.dtype) for v in views],
        [pltpu.SemaphoreType.DMA((n,)), pltpu.SemaphoreType.DMA((n,))], start, finish)


def _chip_exchange(parts):
    n = len(parts)

    def plan(ins, outs, sems):
        send_sems, recv_sems = sems
        x, y, c = _place()
        me = 2 * x + y
        pairs = [(i, j, px, py) for i in range(n) for j, (px, py) in enumerate(_other_chips(x, y))]

        def copy(i, j, src, slot, px, py):
            return pltpu.make_async_remote_copy(
                src_ref=src, dst_ref=outs[i].at[slot],
                send_sem=send_sems.at[3 * i + j], recv_sem=recv_sems.at[3 * i + j],
                device_id=(px, py, c), device_id_type=MESH)

        sends = [copy(i, j, ins[i].at[2 * px + py], me, px, py) for i, j, px, py in pairs]
        lands = [copy(i, j, outs[i].at[2 * px + py], 2 * px + py, px, py) for i, j, px, py in pairs]
        return sends, lands

    def start(ins, outs, sems):
        sends, _ = plan(ins, outs, sems)
        for cp in sends:
            cp.start()

    def finish(ins, outs, sems):
        sends, lands = plan(ins, outs, sems)
        for cp in lands:
            cp.wait_recv()
        for cp in sends:
            cp.wait_send()

    return _Exchange(
        parts, [_sds(p.shape, p.dtype) for p in parts],
        [pltpu.SemaphoreType.DMA((3 * n,)), pltpu.SemaphoreType.DMA((3 * n,))],
        start, finish)


def _sibling_exchange(both):
    n = len(both)

    def body(*refs):
        outs = refs[n:2 * n]
        send_sems, recv_sems = refs[2 * n:]
        x, y, c = _place()
        sent = []
        for i in range(n):
            cp = pltpu.make_async_remote_copy(
                src_ref=outs[i].at[c], dst_ref=outs[i].at[c],
                send_sem=send_sems.at[i], recv_sem=recv_sems.at[i],
                device_id=(x, y, 1 - c), device_id_type=MESH)
            cp.start()
            sent.append(cp)
        for cp in sent:
            cp.wait()

    return pl.pallas_call(
        body, name="grad_sibling_exchange",
        out_shape=[_sds(b.shape, f32) for b in both],
        in_specs=[ANY] * n, out_specs=[ANY] * n,
        input_output_aliases={i: i for i in range(n)},
        scratch_shapes=[pltpu.SemaphoreType.DMA((n,)), pltpu.SemaphoreType.DMA((n,))],
    )(*both)


def _row_tile(rows, cols):
    t = rows
    while t * cols * 4 > (2 << 20) and t % 16 == 0:
        t //= 2
    return t


def _pair_sum(grad, recv, core, name):
    _, r, c = grad.shape
    hr = r // 2
    tr = _row_tile(hr, c)
    view = grad.reshape(N_CHIPS, 2, hr, c)

    def body(core_ref, mine_ref, recv_ref, out_ref):
        out_ref[...] = (mine_ref[...] + recv_ref[...].astype(f32)).astype(bf16)

    return pl.pallas_call(
        body, name=name,
        grid_spec=pltpu.PrefetchScalarGridSpec(
            num_scalar_prefetch=1, grid=(N_CHIPS, hr // tr),
            in_specs=[pl.BlockSpec((None, None, tr, c), lambda k, t, core_ref: (k, core_ref[0], t, 0)),
                      pl.BlockSpec((None, tr, c), lambda k, t, core_ref: (k, t, 0))],
            out_specs=pl.BlockSpec((None, tr, c), lambda k, t, core_ref: (k, t, 0))),
        out_shape=_sds((N_CHIPS, hr, c), bf16),
        compiler_params=_arb(2),
    )(core, view, recv)


def _chip_sum(mine, landed, place, name):
    _, hr, c = mine.shape
    tr = _row_tile(hr, c)

    def body(place_ref, a_ref, b_ref, c_ref, d_ref, out_ref):
        out_ref[...] = ((a_ref[...].astype(f32) + b_ref[...].astype(f32)) + c_ref[...].astype(f32)) + d_ref[...].astype(f32)

    def slot(k):
        return pl.BlockSpec((None, tr, c), lambda t, place_ref: ((place_ref[1] + k) % N_CHIPS, t, 0))

    return pl.pallas_call(
        body, name=name,
        grid_spec=pltpu.PrefetchScalarGridSpec(
            num_scalar_prefetch=1, grid=(hr // tr,),
            in_specs=[slot(0), slot(1), slot(2), slot(3)],
            out_specs=pl.BlockSpec((None, tr, c), lambda t, place_ref: (place_ref[0], t, 0))),
        out_shape=_sds((2, hr, c), f32), compiler_params=_arb(1),
    )(place, mine, landed, landed, landed)


def _adamw(g, w, m, v, name):
    r, c = g.shape
    tr = _row_tile(r, c)

    def body(g_ref, w_ref, m_ref, v_ref, d_ref, nm_ref, nv_ref, go_ref):
        gv = g_ref[...]
        go_ref[...] = gv
        mv = ADAM_B1 * m_ref[...] + (1.0 - ADAM_B1) * gv
        vv = ADAM_B2 * v_ref[...] + (1.0 - ADAM_B2) * jnp.square(gv)
        m_hat = mv / (1.0 - ADAM_B1 ** ADAM_STEP)
        v_hat = vv / (1.0 - ADAM_B2 ** ADAM_STEP)
        d_ref[...] = -ADAM_LR * (m_hat / (jnp.sqrt(v_hat) + ADAM_EPS) + ADAM_WD * w_ref[...])
        nm_ref[...] = mv
        nv_ref[...] = vv

    spec = pl.BlockSpec((tr, c), lambda t: (t, 0))
    return pl.pallas_call(
        body, name=name, grid=(r // tr,), in_specs=[spec] * 4, out_specs=[spec] * 4,
        out_shape=[_sds((r, c), f32)] * 4, compiler_params=_arb(1),
    )(g, w, m, v)


def _allreduce_small(packed):
    shape = packed.shape

    def body(x_ref, out_ref, buf, send_sems, recv_sems):
        x, y, c = _place()
        me = 4 * x + 2 * y + c
        buf[me] = x_ref[...]
        sent = []
        for r in range(1, 8):
            dx, dy, dc = (r >> 2) & 1, (r >> 1) & 1, r & 1
            peer = ((1 - x) if dx else x, (1 - y) if dy else y, (1 - c) if dc else c)
            cp = pltpu.make_async_remote_copy(
                src_ref=x_ref, dst_ref=buf.at[me],
                send_sem=send_sems.at[r], recv_sem=recv_sems.at[r],
                device_id=peer, device_id_type=MESH)
            cp.start()
            sent.append((cp, peer))
        for r, (cp, peer) in enumerate(sent, start=1):
            src = 4 * peer[0] + 2 * peer[1] + peer[2]
            pltpu.make_async_remote_copy(
                src_ref=x_ref, dst_ref=buf.at[src],
                send_sem=send_sems.at[r], recv_sem=recv_sems.at[r],
                device_id=peer, device_id_type=MESH).wait_recv()
        for cp, _ in sent:
            cp.wait_send()
        total = buf[0]
        for k in range(1, 8):
            total = total + buf[k]
        out_ref[...] = total

    vmem = pl.BlockSpec(memory_space=pltpu.VMEM)
    return pl.pallas_call(
        body, name="allreduce_small", out_shape=_sds(shape, f32),
        in_specs=[vmem], out_specs=vmem,
        scratch_shapes=[pltpu.VMEM((8,) + shape, f32), pltpu.SemaphoreType.DMA((8,)),
                        pltpu.SemaphoreType.DMA((8,))],
    )(packed)


def _in_proj(x, g_mix, w_in, tm):
    s = x.shape[0]
    ncol = IN_COLS // N_CHIPS
    t = ATTN_TILE
    nb = tm // t
    koff = 3 * W_CONV + W_ATTN

    def body(x_ref, g_ref, w_ref, proj_ref, a_ref, kp_ref, vp_ref, kt_ref, vt_ref, perm, kv):
        a, _ = _rms_fwd(x_ref[...], g_ref[...])
        ab = a.astype(bf16)
        a_ref[...] = ab
        for k in range(N_CHIPS):
            proj_ref[:, k * ncol:(k + 1) * ncol] = _dot(ab, w_ref[k])
        for j in range(2 * W_ATTN // LANES):
            kv[j] = proj_ref[:, koff + j * LANES:koff + (j + 1) * LANES]
        for b in range(nb):
            for r in range(KEY_RUN):
                for j in range(2 * W_ATTN // LANES):
                    perm[SUBLANES * r:SUBLANES * (r + 1), j * LANES:(j + 1) * LANES] = kv[
                        j, pl.ds(b * t + r, SUBLANES, stride=KEY_RUN), :]
            kp_ref[b] = perm[:, :W_ATTN].astype(bf16)
            vp_ref[b] = perm[:, W_ATTN:].astype(bf16)
            kt_ref[b] = perm[:, :W_ATTN].T.astype(bf16)
            vt_ref[b] = perm[:, W_ATTN:].T.astype(bf16)

    keys = pl.BlockSpec((nb, t, W_ATTN), lambda i: (i, 0, 0))
    keys_t = pl.BlockSpec((nb, W_ATTN, t), lambda i: (i, 0, 0))
    return pl.pallas_call(
        body, name="in_proj", grid=(s // tm,),
        in_specs=[pl.BlockSpec((tm, D_MODEL), lambda i: (i, 0)),
                  pl.BlockSpec((1, D_MODEL), lambda i: (0, 0)),
                  pl.BlockSpec((N_CHIPS, D_MODEL, ncol), lambda i: (0, 0, 0))],
        out_specs=[pl.BlockSpec((tm, IN_COLS), lambda i: (i, 0)),
                   pl.BlockSpec((tm, D_MODEL), lambda i: (i, 0)), keys, keys, keys_t, keys_t],
        out_shape=[_sds((s, IN_COLS), f32), _sds((s, D_MODEL), bf16),
                   _sds((s // t, t, W_ATTN), bf16), _sds((s // t, t, W_ATTN), bf16),
                   _sds((s // t, W_ATTN, t), bf16), _sds((s // t, W_ATTN, t), bf16)],
        scratch_shapes=[pltpu.VMEM((t, 2 * W_ATTN), f32), pltpu.VMEM((2 * W_ATTN // LANES, tm, LANES), f32)],
        compiler_params=_arb(1),
    )(x, g_mix, w_in)


def _shifted(prev8, cur, shift):
    ext = jnp.concatenate([prev8, cur], axis=0)
    return pltpu.roll(ext, shift, axis=0)[8:]


def _conv_fwd(proj, conv_w, g_conv):
    s = proj.shape[0]
    nblk = W_CONV // LANES
    rc = min(CONV_CHUNK, s)

    def body(cb_ref, cc_ref, cu_ref, w_ref, g_ref, out_ref):
        ones = _group_ones(LANES)
        w0, w1, w2 = w_ref[0:1, :], w_ref[1:2, :], w_ref[2:3, :]
        g = g_ref[...]

        def chunk(i, carry):
            r0 = pl.multiple_of(i * rc, rc)
            rows = pl.ds(r0, rc)
            prev = pl.ds(pl.multiple_of(jnp.maximum(r0 - 8, 0), 8), 8)
            v = cc_ref[rows, :] * cu_ref[rows, :]
            vp = jnp.where(i > 0, cc_ref[prev, :] * cu_ref[prev, :], 0.0)
            y = w2 * v + w1 * _shifted(vp, v, 1) + w0 * _shifted(vp, v, 2)
            out_ref[rows, :] = _head_rms_fwd(cb_ref[rows, :] * y, g, ones).astype(bf16)
            return carry

        lax.fori_loop(0, s // rc, chunk, 0)

    def col(off):
        return pl.BlockSpec((s, LANES), lambda j: (0, off + j))

    return pl.pallas_call(
        body, name="conv_fwd", grid=(nblk,),
        in_specs=[col(0), col(nblk), col(2 * nblk),
                  pl.BlockSpec((None, CONV_W_ROWS, LANES), lambda j: (j, 0, 0)),
                  pl.BlockSpec((1, LANES), lambda j: (0, j))],
        out_specs=pl.BlockSpec((s, LANES), lambda j: (0, j)),
        out_shape=_sds((s, W_CONV), bf16), compiler_params=_arb(1),
    )(proj, proj, proj, conv_w, g_conv)


LOG2_E = 1.4426950408889634


def _log2_keep(z2):
    nz2 = -z2
    return jnp.minimum(nz2, 0.0) - jnp.log2(1.0 + jnp.exp2(jnp.minimum(z2, nz2)))


def _head_pair_masks(rows):
    lane = lax.broadcasted_iota(jnp.int32, (rows, LANES), 1)
    return lane < HEAD_DIM


SUBLANES = 8
KEY_RUN = ATTN_TILE // SUBLANES


def _causal_tiles():
    r = lax.broadcasted_iota(jnp.int32, (ATTN_TILE, ATTN_TILE), 0)
    key = (r % SUBLANES) * KEY_RUN + r // SUBLANES
    below = key < lax.broadcasted_iota(jnp.int32, (ATTN_TILE, ATTN_TILE), 1)
    return below.astype(f32), jnp.where(below, 0.0, -1e30).astype(f32)


def _sublane_scan(x, reverse):
    row = lax.broadcasted_iota(jnp.int32, x.shape, 0)
    inc = x
    for sh in (1, 2, 4):
        if reverse:
            inc = inc + jnp.where(row < SUBLANES - sh, pltpu.roll(inc, SUBLANES - sh, axis=0), 0.0)
        else:
            inc = inc + jnp.where(row >= sh, pltpu.roll(inc, sh, axis=0), 0.0)
    return inc - x


def _attn_fwd(proj, kp, vt, ex=None):
    s = proj.shape[0]
    t = ATTN_TILE
    nblk = s // t
    npair = W_ATTN // LANES
    qoff = 3 * W_CONV // LANES
    keep01, keepneg = _causal_tiles()

    def body(q_ref, k_ref, vt_ref, m01_ref, neg_ref, o_ref, tot_ref, w_s, a_s, acc):
        qb = pl.program_id(1)
        first = _head_pair_masks(t)
        q = q_ref[...] * (HEAD_DIM ** -0.5)
        qh = (jnp.where(first, q, 0.0).astype(bf16), jnp.where(first, 0.0, q).astype(bf16))
        acc[...] = jnp.zeros_like(acc)
        a_s[1] = jnp.zeros((t, t), bf16)

        def scores(kb, h):
            w_s[h] = _dot(k_ref[kb], qh[h], NT)

        def weigh(kb, h):
            acc[h] += _dot(vt_ref[kb], a_s[h])

        def weights(h, diagonal, later):
            run = jnp.zeros((SUBLANES, t), f32)
            for a in reversed(range(KEY_RUN)):
                rows = slice(SUBLANES * a, SUBLANES * (a + 1))
                z2 = w_s[h, rows, :] * LOG2_E
                lk = _log2_keep(z2)
                if diagonal:
                    lk = lk * m01_ref[rows, :]
                run = run + lk
                w_s[h, rows, :] = z2 + run
            off = _sublane_scan(run, reverse=True) + later
            off2 = jnp.concatenate([off, off], axis=0)
            for a in range(t // (2 * SUBLANES)):
                rows = slice(2 * SUBLANES * a, 2 * SUBLANES * (a + 1))
                w = w_s[h, rows, :] + off2
                if diagonal:
                    w = w + neg_ref[rows, :]
                a_s[h, rows, :] = jnp.exp2(w).astype(bf16)
            return later + jnp.sum(run, axis=0, keepdims=True)

        def block(kb, before, after, diagonal, later):
            scores(kb, 1)
            weigh(before, 1)
            l0 = weights(0, diagonal, later[0])
            scores(after, 0)
            weigh(kb, 0)
            l1 = weights(1, diagonal, later[1])
            return l0, l1

        zero = jnp.zeros((1, t), f32)
        scores(qb, 0)
        later = block(qb, qb, jnp.maximum(qb - 1, 0), True, (zero, zero))

        def earlier(i, c):
            kb = qb - 1 - i
            return block(kb, kb + 1, jnp.maximum(kb - 1, 0), False, c)

        later = lax.fori_loop(0, qb, earlier, later)
        weigh(0, 1)
        top = lax.broadcasted_iota(jnp.int32, (LANES, t), 0) < HEAD_DIM
        o_ref[...] = jnp.where(top, acc[0], acc[1]).T
        tot_ref[...] = jnp.concatenate([later[0], later[1], jnp.zeros((SUBLANES - 2, t), f32)], axis=0)

    return _call_carrying(
        ex, body, "attn_fwd", (npair, nblk),
        in_specs=[pl.BlockSpec((t, LANES), lambda p, i: (i, qoff + p)),
                  pl.BlockSpec((nblk, t, LANES), lambda p, i: (0, 0, p)),
                  pl.BlockSpec((nblk, LANES, t), lambda p, i: (0, p, 0)),
                  pl.BlockSpec((t, t), lambda p, i: (0, 0)),
                  pl.BlockSpec((t, t), lambda p, i: (0, 0))],
        out_specs=[pl.BlockSpec((t, LANES), lambda p, i: (i, p)),
                   pl.BlockSpec((None, SUBLANES, t), lambda p, i: (p, 0, i))],
        out_shape=[_sds((s, W_ATTN), f32), _sds((npair, SUBLANES, s), f32)],
        scratch_shapes=[pltpu.VMEM((2, t, t), f32), pltpu.VMEM((2, t, t), bf16), pltpu.VMEM((2, LANES, t), f32)],
        args=(proj, kp, vt, keep01, keepneg))


def _out_proj(o, conv_n, x, g_attn, w_out, tm):
    s = x.shape[0]

    def body(o_ref, c_ref, x_ref, g_ref, w_ref, h_ref, cat_ref):
        ones = _group_ones(LANES)
        cat_ref[:, :W_CONV] = c_ref[...]
        for j in range(W_ATTN // LANES):
            cols = slice(j * LANES, (j + 1) * LANES)
            cat_ref[:, W_CONV + j * LANES:W_CONV + (j + 1) * LANES] = _head_rms_fwd(
                o_ref[:, cols], g_ref[:, cols], ones).astype(bf16)
        h_ref[...] = x_ref[...] + _dot(cat_ref[...], w_ref[...])

    return pl.pallas_call(
        body, name="out_proj", grid=(s // tm,),
        in_specs=[pl.BlockSpec((tm, W_ATTN), lambda i: (i, 0)),
                  pl.BlockSpec((tm, W_CONV), lambda i: (i, 0)),
                  pl.BlockSpec((tm, D_MODEL), lambda i: (i, 0)),
                  pl.BlockSpec((1, W_ATTN), lambda i: (0, 0)),
                  pl.BlockSpec((D_MODEL, D_MODEL), lambda i: (0, 0))],
        out_specs=[pl.BlockSpec((tm, D_MODEL), lambda i: (i, 0)),
                   pl.BlockSpec((tm, D_MODEL), lambda i: (i, 0))],
        out_shape=[_sds((s, D_MODEL), f32), _sds((s, D_MODEL), bf16)],
        compiler_params=_arb(1),
    )(o, conv_n, x, g_attn, w_out)


def _mlp_fwd(h1, g_mlp, w_up, w_down, tm):
    s = h1.shape[0]
    fc = D_FF // N_CHIPS

    def body(h_ref, g_ref, wu_ref, wd_ref, h2_ref, u_ref, m_ref):
        j = pl.program_id(1)

        @pl.when(j == 0)
        def _():
            m, _ = _rms_fwd(h_ref[...], g_ref[...])
            m_ref[...] = m.astype(bf16)
            h2_ref[...] = h_ref[...]

        u = _dot(m_ref[...], wu_ref[...])
        u_ref[...] = u.astype(bf16)
        h2_ref[...] += _dot(jnp.square(jnp.maximum(u, 0.0)).astype(bf16), wd_ref[...])

    return pl.pallas_call(
        body, name="mlp_fwd", grid=(s // tm, N_CHIPS),
        in_specs=[pl.BlockSpec((tm, D_MODEL), lambda i, j: (i, 0)),
                  pl.BlockSpec((1, D_MODEL), lambda i, j: (0, 0)),
                  pl.BlockSpec((None, D_MODEL, fc), lambda i, j: (j, 0, 0)),
                  pl.BlockSpec((None, fc, D_MODEL), lambda i, j: (j, 0, 0))],
        out_specs=[pl.BlockSpec((tm, D_MODEL), lambda i, j: (i, 0)),
                   pl.BlockSpec((tm, fc), lambda i, j: (i, j)),
                   pl.BlockSpec((tm, D_MODEL), lambda i, j: (i, 0))],
        out_shape=[_sds((s, D_MODEL), f32), _sds((s, D_FF), bf16), _sds((s, D_MODEL), bf16)],
        compiler_params=_arb(2),
    )(h1, g_mlp, w_up, w_down)


def _tail(h2, p, target, g_ple, g_final, w_gate, w_proj, tm):
    s = h2.shape[0]
    pc = D_MODEL // N_CHIPS

    def body(h_ref, p_ref, t_ref, gp_ref, gf_ref, wg_ref, wp_ref,
             dh_ref, dhb_ref, n3_ref, dgl_ref, dpp_ref, pb_ref, ggp_ref, ggf_ref, loss_ref, pp_ref):
        i = pl.program_id(0)
        h2v = h_ref[...]
        n3, _ = _rms_fwd(h2v, gp_ref[...])
        n3b = n3.astype(bf16)
        n3_ref[...] = n3b
        gate = jax.nn.sigmoid(_dot(n3b, wg_ref[...]))
        pb = p_ref[...].astype(bf16)
        pb_ref[...] = pb
        for k in range(N_CHIPS):
            pp_ref[:, k * pc:(k + 1) * pc] = _dot(pb, wp_ref[k])
        pp = pp_ref[...]
        h3 = h2v + gate * pp
        yv, _ = _rms_fwd(h3, gf_ref[...])
        err = yv - t_ref[...]
        loss = 0.5 * jnp.sum(err * err) * (1.0 / D_MODEL)
        dh3, ggf = _rms_bwd(err * (1.0 / D_MODEL), h3, gf_ref[...])
        dpp_ref[...] = (dh3 * gate).astype(bf16)
        dgl = (dh3 * pp * gate * (1.0 - gate)).astype(bf16)
        dgl_ref[...] = dgl
        dn3 = _dot(dgl, wg_ref[...], NT)
        dh2n, ggp = _rms_bwd(dn3, h2v, gp_ref[...])
        dh2 = dh3 + dh2n
        dh_ref[...] = dh2
        dhb_ref[...] = dh2.astype(bf16)

        @pl.when(i == 0)
        def _():
            ggp_ref[...] = jnp.zeros_like(ggp_ref)
            ggf_ref[...] = jnp.zeros_like(ggf_ref)
            loss_ref[...] = jnp.zeros_like(loss_ref)

        ggp_ref[...] += ggp
        ggf_ref[...] += ggf
        loss_ref[...] += jnp.full(loss_ref.shape, loss, f32)

    tok = lambda w: pl.BlockSpec((tm, w), lambda i: (i, 0))
    vec = lambda w: pl.BlockSpec((1, w), lambda i: (0, 0))
    return pl.pallas_call(
        body, name="tail", grid=(s // tm,),
        in_specs=[tok(D_MODEL), tok(PLE_DIM), tok(D_MODEL), vec(D_MODEL), vec(D_MODEL),
                  pl.BlockSpec((D_MODEL, D_MODEL), lambda i: (0, 0)),
                  pl.BlockSpec((N_CHIPS, PLE_DIM, pc), lambda i: (0, 0, 0))],
        out_specs=[tok(D_MODEL), tok(D_MODEL), tok(D_MODEL), tok(D_MODEL), tok(D_MODEL), tok(PLE_DIM),
                   vec(D_MODEL), vec(D_MODEL), vec(LANES)],
        out_shape=[_sds((s, D_MODEL), f32), _sds((s, D_MODEL), bf16), _sds((s, D_MODEL), bf16),
                   _sds((s, D_MODEL), bf16), _sds((s, D_MODEL), bf16), _sds((s, PLE_DIM), bf16),
                   _sds((1, D_MODEL), f32), _sds((1, D_MODEL), f32), _sds((1, LANES), f32)],
        scratch_shapes=[pltpu.VMEM((tm, D_MODEL), f32)],
        compiler_params=_arb(1),
    )(h2, p, target, g_ple, g_final, w_gate, w_proj)


def _mlp_bwd(dh2, dh2b, h1, u, g_mlp, w_up, w_down, tm):
    s = h1.shape[0]
    fc = D_FF // N_CHIPS

    def body(dh_ref, dhb_ref, h_ref, u_ref, g_ref, wu_ref, wd_ref, dh1_ref, dh1b_ref, du_ref, gg_ref, dm):
        i, j = pl.program_id(0), pl.program_id(1)

        @pl.when(j == 0)
        def _():
            dm[...] = jnp.zeros_like(dm)

        dr = _dot(dhb_ref[...], wd_ref[...], NT)
        du = (dr * (2.0 * jnp.maximum(u_ref[...].astype(f32), 0.0))).astype(bf16)
        du_ref[...] = du
        dm[...] += _dot(du, wu_ref[...], NT)

        @pl.when((i == 0) & (j == 0))
        def _():
            gg_ref[...] = jnp.zeros_like(gg_ref)

        @pl.when(j == N_CHIPS - 1)
        def _():
            dh1n, gg = _rms_bwd(dm[...], h_ref[...], g_ref[...])
            dh1 = dh_ref[...] + dh1n
            dh1_ref[...] = dh1
            dh1b_ref[...] = dh1.astype(bf16)
            gg_ref[...] += gg

    tok = pl.BlockSpec((tm, D_MODEL), lambda i, j: (i, 0))
    ffb = pl.BlockSpec((tm, fc), lambda i, j: (i, j))
    vec = pl.BlockSpec((1, D_MODEL), lambda i, j: (0, 0))
    return pl.pallas_call(
        body, name="mlp_bwd", grid=(s // tm, N_CHIPS),
        in_specs=[tok, tok, tok, ffb, vec,
                  pl.BlockSpec((None, D_MODEL, fc), lambda i, j: (j, 0, 0)),
                  pl.BlockSpec((None, fc, D_MODEL), lambda i, j: (j, 0, 0))],
        out_specs=[tok, tok, ffb, vec],
        out_shape=[_sds((s, D_MODEL), f32), _sds((s, D_MODEL), bf16), _sds((s, D_FF), bf16),
                   _sds((1, D_MODEL), f32)],
        scratch_shapes=[pltpu.VMEM((tm, D_MODEL), f32)],
        compiler_params=_arb(2),
    )(dh2, dh2b, h1, u, g_mlp, w_up, w_down)


def _out_proj_bwd(dh1b, o, g_attn, w_out, tm, ex=None):
    s = o.shape[0]

    def body(dh_ref, o_ref, g_ref, w_ref, dc_ref, do_ref, gg_ref, dcat):
        i = pl.program_id(0)
        ones = _group_ones(LANES)
        dcat[...] = _dot(dh_ref[...], w_ref[...], NT)
        dc_ref[...] = dcat[:, :W_CONV]

        @pl.when(i == 0)
        def _():
            gg_ref[...] = jnp.zeros_like(gg_ref)

        for j in range(W_ATTN // LANES):
            cols = slice(j * LANES, (j + 1) * LANES)
            d, gg = _head_rms_bwd(dcat[:, W_CONV + j * LANES:W_CONV + (j + 1) * LANES],
                                  o_ref[:, cols], g_ref[:, cols], ones)
            do_ref[:, cols] = d
            gg_ref[:, cols] += gg

    return _call_carrying(
        ex, body, "out_proj_bwd", (s // tm,),
        in_specs=[pl.BlockSpec((tm, D_MODEL), lambda i: (i, 0)),
                  pl.BlockSpec((tm, W_ATTN), lambda i: (i, 0)),
                  pl.BlockSpec((1, W_ATTN), lambda i: (0, 0)),
                  pl.BlockSpec((D_MODEL, D_MODEL), lambda i: (0, 0))],
        out_specs=[pl.BlockSpec((tm, W_CONV), lambda i: (i, 0)),
                   pl.BlockSpec((tm, W_ATTN), lambda i: (i, 0)),
                   pl.BlockSpec((1, W_ATTN), lambda i: (0, 0))],
        out_shape=[_sds((s, W_CONV), f32), _sds((s, W_ATTN), f32), _sds((1, W_ATTN), f32)],
        scratch_shapes=[pltpu.VMEM((tm, D_MODEL), f32)],
        args=(dh1b, o, g_attn, w_out))


def _attn_bwd(proj, kp, vp, kt, do, tot, ex=None):
    s = proj.shape[0]
    t = ATTN_TILE
    nblk = s // t
    npair = W_ATTN // LANES
    qoff = 3 * W_CONV // LANES
    keep01, keepneg = _causal_tiles()

    def body(q_ref, k_ref, v_ref, kt_ref, do_ref, tot_ref, m01_ref, neg_ref, dq_ref, dk_ref, dv_ref,
             w_s, da_s, b_s, g_s, a_s, dz_s, dq_acc, dk_acc, dv_acc):
        qb = pl.program_id(1)
        first = _head_pair_masks(t)
        q = q_ref[...] * (HEAD_DIM ** -0.5)
        qh = (jnp.where(first, q, 0.0).astype(bf16), jnp.where(first, 0.0, q).astype(bf16))
        dov = do_ref[...]
        doh = (jnp.where(first, dov, 0.0).astype(bf16), jnp.where(first, 0.0, dov).astype(bf16))
        total = (tot_ref[0:1, :], tot_ref[1:2, :])

        @pl.when(qb == 0)
        def _():
            dk_acc[...] = jnp.zeros_like(dk_acc)
            dv_acc[...] = jnp.zeros_like(dv_acc)

        dq_acc[...] = jnp.zeros_like(dq_acc)
        a_s[1] = jnp.zeros((t, t), bf16)
        dz_s[1] = jnp.zeros((t, t), bf16)

        def scores(kb, h):
            w_s[h] = _dot(k_ref[kb], qh[h], NT)
            da_s[h] = _dot(v_ref[kb], doh[h], NT)

        def spread(kb, h):
            dq_acc[h] += _dot(kt_ref[kb], dz_s[h])
            dk_acc[kb] += _dot(dz_s[h], qh[h])
            dv_acc[kb] += _dot(a_s[h], doh[h])

        def grads(h, diagonal, lk_before, g_before):
            run = jnp.zeros((SUBLANES, t), f32)
            for a in range(KEY_RUN):
                rows = slice(SUBLANES * a, SUBLANES * (a + 1))
                z2 = w_s[h, rows, :] * LOG2_E
                lk = _log2_keep(z2)
                if diagonal:
                    lk = lk * m01_ref[rows, :]
                log_beta = jnp.minimum(z2 + lk, 0.0)
                run = run + lk
                b_s[h, rows, :] = jnp.exp2(log_beta)
                w_s[h, rows, :] = log_beta - run
            off = total[h] - lk_before - _sublane_scan(run, reverse=False)
            lk_sum = jnp.sum(run, axis=0, keepdims=True)
            run = jnp.zeros((SUBLANES, t), f32)
            for a in range(KEY_RUN // 2):
                parts = []
                for r in (slice(2 * SUBLANES * a, 2 * SUBLANES * a + SUBLANES),
                          slice(2 * SUBLANES * a + SUBLANES, 2 * SUBLANES * (a + 1))):
                    w = w_s[h, r, :] + off
                    if diagonal:
                        w = w + neg_ref[r, :]
                    av = jnp.exp2(w)
                    g = av * da_s[h, r, :]
                    run = run + g
                    da_s[h, r, :] = g
                    g_s[h, r, :] = run
                    parts.append(av)
                a_s[h, 2 * SUBLANES * a:2 * SUBLANES * (a + 1), :] = jnp.concatenate(parts, axis=0).astype(bf16)
            goff = g_before + _sublane_scan(run, reverse=False)
            goff2 = jnp.concatenate([goff, goff], axis=0)
            for a in range(KEY_RUN // 2):
                rows = slice(2 * SUBLANES * a, 2 * SUBLANES * (a + 1))
                dz = da_s[h, rows, :] - b_s[h, rows, :] * (g_s[h, rows, :] + goff2)
                if diagonal:
                    dz = dz * m01_ref[rows, :]
                dz_s[h, rows, :] = dz.astype(bf16)
            return lk_before + lk_sum, g_before + jnp.sum(run, axis=0, keepdims=True)

        def block(kb, before, after, diagonal, carry):
            scores(kb, 1)
            spread(before, 1)
            c0 = grads(0, diagonal, carry[0], carry[1])
            if after is not None:
                scores(after, 0)
            spread(kb, 0)
            c1 = grads(1, diagonal, carry[2], carry[3])
            return c0 + c1

        zero = jnp.zeros((1, t), f32)
        scores(0, 0)
        def two_blocks(i, c):
            kb = 2 * i
            c = block(kb, jnp.maximum(kb - 1, 0), kb + 1, False, c)
            return block(kb + 1, kb, kb + 2, False, c)

        carry = lax.fori_loop(0, qb // 2, two_blocks, (zero, zero, zero, zero))
        carry = lax.fori_loop(qb - qb % 2, qb, lambda kb, c: block(kb, jnp.maximum(kb - 1, 0), kb + 1, False, c), carry)
        block(qb, jnp.maximum(qb - 1, 0), None, True, carry)
        spread(qb, 1)
        top = lax.broadcasted_iota(jnp.int32, (LANES, t), 0) < HEAD_DIM
        dq_ref[...] = (jnp.where(top, dq_acc[0], dq_acc[1]).T * (HEAD_DIM ** -0.5)).astype(bf16)

        @pl.when(qb == nblk - 1)
        def _():
            for kb in range(nblk):
                for b in range(SUBLANES):
                    keys_b = slice(KEY_RUN * b, KEY_RUN * (b + 1))
                    dk_ref[kb, keys_b, :] = dk_acc[kb, pl.ds(b, KEY_RUN, stride=SUBLANES), :].astype(bf16)
                    dv_ref[kb, keys_b, :] = dv_acc[kb, pl.ds(b, KEY_RUN, stride=SUBLANES), :].astype(bf16)

    keys = pl.BlockSpec((nblk, t, LANES), lambda p, i: (0, 0, p))
    tile = pl.BlockSpec((t, t), lambda p, i: (0, 0))
    return _call_carrying(
        ex, body, "attn_bwd", (npair, nblk),
        in_specs=[pl.BlockSpec((t, LANES), lambda p, i: (i, qoff + p)),
                  keys, keys,
                  pl.BlockSpec((nblk, LANES, t), lambda p, i: (0, p, 0)),
                  pl.BlockSpec((t, LANES), lambda p, i: (i, p)),
                  pl.BlockSpec((None, SUBLANES, t), lambda p, i: (p, 0, i)),
                  tile, tile],
        out_specs=[pl.BlockSpec((t, LANES), lambda p, i: (i, p)), keys, keys],
        out_shape=[_sds((s, W_ATTN), bf16), _sds((nblk, t, W_ATTN), bf16), _sds((nblk, t, W_ATTN), bf16)],
        scratch_shapes=[pltpu.VMEM((2, t, t), f32), pltpu.VMEM((2, t, t), f32), pltpu.VMEM((2, t, t), f32),
                        pltpu.VMEM((2, t, t), f32), pltpu.VMEM((2, t, t), bf16), pltpu.VMEM((2, t, t), bf16),
                        pltpu.VMEM((2, LANES, t), f32), pltpu.VMEM((nblk, t, LANES), f32),
                        pltpu.VMEM((nblk, t, LANES), f32)],
        args=(proj, kp, vp, kt, do, tot, keep01, keepneg))


def _conv_bwd(proj, dcn, conv_w, g_conv):
    s = proj.shape[0]
    nblk = W_CONV // LANES
    rc = min(CONV_CHUNK, s)
    nchunk = s // rc

    def body(cb_ref, cc_ref, cu_ref, d_ref, w_ref, g_ref, d3_ref, gw_ref, gg_ref, dy_buf):
        ones = _group_ones(LANES)
        w0, w1, w2 = w_ref[0:1, :], w_ref[1:2, :], w_ref[2:3, :]
        g = g_ref[...]

        def first_pass(i, carry):
            gw0, gw1, gw2, gg = carry
            r0 = pl.multiple_of(i * rc, rc)
            rows = pl.ds(r0, rc)
            prev = pl.ds(pl.multiple_of(jnp.maximum(r0 - 8, 0), 8), 8)
            v = cc_ref[rows, :] * cu_ref[rows, :]
            vp = jnp.where(i > 0, cc_ref[prev, :] * cu_ref[prev, :], 0.0)
            v1, v2 = _shifted(vp, v, 1), _shifted(vp, v, 2)
            y = w2 * v + w1 * v1 + w0 * v2
            cb = cb_ref[rows, :]
            dcy, ggi = _head_rms_bwd(d_ref[rows, :], cb * y, g, ones)
            d3_ref[0, rows, :] = (dcy * y).astype(bf16)
            dy = dcy * cb
            dy_buf[rows, :] = dy
            return (gw0 + jnp.sum(dy * v2, axis=0, keepdims=True), gw1 + jnp.sum(dy * v1, axis=0, keepdims=True),
                    gw2 + jnp.sum(dy * v, axis=0, keepdims=True), gg + ggi)

        zero = jnp.zeros((1, LANES), f32)
        gw0, gw1, gw2, gg = lax.fori_loop(0, nchunk, first_pass, (zero, zero, zero, zero))
        gw_ref[...] = jnp.zeros_like(gw_ref)
        gw_ref[0:1, :] = gw0
        gw_ref[1:2, :] = gw1
        gw_ref[2:3, :] = gw2
        gg_ref[...] = gg

        def second_pass(i, carry):
            r0 = pl.multiple_of(i * rc, rc)
            rows = pl.ds(r0, rc)
            nxt = pl.ds(pl.multiple_of(jnp.minimum(r0 + rc, s - 8), 8), 8)
            dy = dy_buf[rows, :]
            dyn = jnp.where(i < nchunk - 1, dy_buf[nxt, :], 0.0)
            ext = jnp.concatenate([dy, dyn], axis=0)
            up1 = pltpu.roll(ext, rc + 8 - 1, axis=0)[:rc]
            up2 = pltpu.roll(ext, rc + 8 - 2, axis=0)[:rc]
            dv = w2 * dy + w1 * up1 + w0 * up2
            d3_ref[1, rows, :] = (dv * cu_ref[rows, :]).astype(bf16)
            d3_ref[2, rows, :] = (dv * cc_ref[rows, :]).astype(bf16)
            return carry

        lax.fori_loop(0, nchunk, second_pass, 0)

    def col(off):
        return pl.BlockSpec((s, LANES), lambda j: (0, off + j))

    return pl.pallas_call(
        body, name="conv_bwd", grid=(nblk,),
        in_specs=[col(0), col(nblk), col(2 * nblk), col(0),
                  pl.BlockSpec((None, CONV_W_ROWS, LANES), lambda j: (j, 0, 0)),
                  pl.BlockSpec((1, LANES), lambda j: (0, j))],
        out_specs=[pl.BlockSpec((3, s, LANES), lambda j: (0, 0, j)),
                   pl.BlockSpec((None, CONV_W_ROWS, LANES), lambda j: (j, 0, 0)),
                   pl.BlockSpec((1, LANES), lambda j: (0, j))],
        out_shape=[_sds((3, s, W_CONV), bf16), _sds((nblk, CONV_W_ROWS, LANES), f32), _sds((1, W_CONV), f32)],
        scratch_shapes=[pltpu.VMEM((s, LANES), f32)],
        compiler_params=_arb(1),
    )(proj, proj, proj, dcn, conv_w, g_conv)


PIECE = W_CONV


def _piece_spans():
    ncol = IN_COLS // N_CHIPS
    spans = []
    for p in range(IN_COLS // PIECE):
        for k in range(N_CHIPS):
            lo, hi = max(p * PIECE, k * ncol), min((p + 1) * PIECE, (k + 1) * ncol)
            if lo < hi:
                spans.append((p, lo - p * PIECE, hi - p * PIECE, k, lo - k * ncol, hi - k * ncol))
    return spans


def _in_proj_bwd(dconv, dq, dk, dv, dh1, x, g_mix, w_in, tm, ex=None):
    s = x.shape[0]
    ncol = IN_COLS // N_CHIPS

    def body(dc_ref, dq_ref, dk_ref, dv_ref, dh_ref, x_ref, g_ref, w_ref, dx_ref, gg_ref):
        i = pl.program_id(0)
        pieces = [dc_ref.at[0], dc_ref.at[1], dc_ref.at[2], dq_ref, dk_ref, dv_ref]
        da = jnp.zeros((tm, D_MODEL), f32)
        for p, plo, phi, k, wlo, whi in _piece_spans():
            da += _dot(pieces[p][:, plo:phi], w_ref[k, :, wlo:whi], NT)
        dxn, gg = _rms_bwd(da, x_ref[...], g_ref[...])
        dx_ref[...] = dh_ref[...] + dxn

        @pl.when(i == 0)
        def _():
            gg_ref[...] = jnp.zeros_like(gg_ref)

        gg_ref[...] += gg

    return _call_carrying(
        ex, body, "in_proj_bwd", (s // tm,),
        in_specs=[pl.BlockSpec((3, tm, PIECE), lambda i: (0, i, 0)),
                  pl.BlockSpec((tm, PIECE), lambda i: (i, 0)),
                  pl.BlockSpec((tm, PIECE), lambda i: (i, 0)),
                  pl.BlockSpec((tm, PIECE), lambda i: (i, 0)),
                  pl.BlockSpec((tm, D_MODEL), lambda i: (i, 0)),
                  pl.BlockSpec((tm, D_MODEL), lambda i: (i, 0)),
                  pl.BlockSpec((1, D_MODEL), lambda i: (0, 0)),
                  pl.BlockSpec((N_CHIPS, D_MODEL, ncol), lambda i: (0, 0, 0))],
        out_specs=[pl.BlockSpec((tm, D_MODEL), lambda i: (i, 0)),
                   pl.BlockSpec((1, D_MODEL), lambda i: (0, 0))],
        out_shape=[_sds((s, D_MODEL), f32), _sds((1, D_MODEL), f32)],
        scratch_shapes=[],
        args=(dconv, dq, dk, dv, dh1, x, g_mix, w_in))


def _grad_w_in(a, dconv, dq, dk, dv, ts):
    s = a.shape[0]
    ncol = IN_COLS // N_CHIPS

    def body(a_ref, dc_ref, dq_ref, dk_ref, dv_ref, o_ref):
        @pl.when(pl.program_id(0) == 0)
        def _():
            o_ref[...] = jnp.zeros_like(o_ref)

        pieces = [dc_ref.at[0], dc_ref.at[1], dc_ref.at[2], dq_ref, dk_ref, dv_ref]
        av = a_ref[...]
        for p, plo, phi, k, wlo, whi in _piece_spans():
            o_ref[k, :, wlo:whi] += _dot(av, pieces[p][:, plo:phi], TN)

    tok = pl.BlockSpec((ts, PIECE), lambda i: (i, 0))
    return pl.pallas_call(
        body, name="grad_w_in", grid=(s // ts,),
        in_specs=[pl.BlockSpec((ts, D_MODEL), lambda i: (i, 0)),
                  pl.BlockSpec((3, ts, PIECE), lambda i: (0, i, 0)), tok, tok, tok],
        out_specs=pl.BlockSpec((N_CHIPS, D_MODEL, ncol), lambda i: (0, 0, 0)),
        out_shape=_sds((N_CHIPS, D_MODEL, ncol), f32),
        compiler_params=_arb(1),
    )(a, dconv, dq, dk, dv)


def _weight_grad(a, b, bm, bn, ts, name, relu_sq=False):
    s, m = a.shape
    n = b.shape[1]
    nn = n // bn
    nk = s // ts

    def body(a_ref, b_ref, o_ref, ob_ref):
        @pl.when(pl.program_id(2) == 0)
        def _():
            o_ref[...] = jnp.zeros_like(o_ref)

        av = a_ref[...]
        if relu_sq:
            av = jnp.square(jnp.maximum(av.astype(f32), 0.0)).astype(bf16)
        o_ref[...] += _dot(av, b_ref[...], TN)

        @pl.when(pl.program_id(2) == nk - 1)
        def _():
            ob_ref[...] = o_ref[...].astype(bf16)

    tile = pl.BlockSpec((None, bm, bn), lambda i, j, k: (i * nn + j, 0, 0))
    return pl.pallas_call(
        body, name=name, grid=(m // bm, nn, nk),
        in_specs=[pl.BlockSpec((ts, bm), lambda i, j, k: (k, i)),
                  pl.BlockSpec((ts, bn), lambda i, j, k: (k, j))],
        out_specs=[tile, tile],
        out_shape=[_sds(((m // bm) * nn, bm, bn), f32), _sds(((m // bm) * nn, bm, bn), bf16)],
        compiler_params=_arb(3),
    )(a, b)


def kernel(x, p, g_mix, w_in, conv_w, g_conv_out, g_attn_out, w_out, g_mlp, w_up, w_down, g_ple, w_ple_gate, w_ple_proj, g_final, loss_target, m_g_mix, m_w_in, m_conv_w, m_g_conv_out, m_g_attn_out, m_w_out, m_g_mlp, m_w_up, m_w_down, m_g_ple, m_w_ple_gate, m_w_ple_proj, m_g_final, v_g_mix, v_w_in, v_conv_w, v_g_conv_out, v_g_attn_out, v_w_out, v_g_mlp, v_w_up, v_w_down, v_g_ple, v_w_ple_gate, v_w_ple_proj, v_g_final):
    s = x.shape[1]
    tm = min(TOKEN_TILE, s)
    tg = min(GRAD_TOKEN_TILE, s)
    xs = x.reshape(s, D_MODEL)
    ps = p.reshape(s, PLE_DIM)
    target = loss_target.reshape(s, D_MODEL)
    core = lax.axis_index("c").astype(jnp.int32).reshape(1)
    chip = 2 * lax.axis_index("x") + lax.axis_index("y")

    big = {"w_in": w_in[0], "w_out": w_out[0], "w_up": w_up[0], "w_down": w_down[0],
           "w_ple_gate": w_ple_gate[0], "w_ple_proj": w_ple_proj[0]}
    names = list(big)
    conv_shard = jnp.pad(conv_w[0], ((0, CONV_W_ROWS - conv_w.shape[1]), (0, 0)))
    later_names = names[1:]
    w_in_f, conv_f = _run_exchange(_gather_exchange([big["w_in"].astype(bf16), conv_shard]), "gather_w_in")

    proj, a_b, kp, vp, kt, vt = _in_proj(xs, g_mix, w_in_f, tm)
    conv_n = _conv_fwd(proj, conv_f, g_conv_out)
    (o, tot), gathered = _attn_fwd(proj, kp, vt, _gather_exchange([big[k].astype(bf16) for k in later_names]))
    w_out_f, w_up_f, w_down_f, w_gate_f, w_proj_f = gathered
    w_out_f = w_out_f.reshape(D_MODEL, D_MODEL)
    w_gate_f = w_gate_f.reshape(D_MODEL, D_MODEL)
    h1, cat_b = _out_proj(o, conv_n, xs, g_attn_out, w_out_f, tm)
    h2, u_b, m_b = _mlp_fwd(h1, g_mlp, w_up_f, w_down_f, min(MLP_TOKEN_TILE, s))

    dh2, dh2_b, n3_b, dgl_b, dpp_b, p_b, gg_ple, gg_final, loss_row = _tail(
        h2, ps, target, g_ple, g_final.reshape(1, D_MODEL), w_gate_f, w_proj_f, tm)
    dh1, dh1_b, du_b, gg_mlp = _mlp_bwd(dh2, dh2_b, h1, u_b, g_mlp, w_up_f, w_down_f, tm)
    both_kinds = {
        "w_out": _weight_grad(cat_b, dh1_b, D_MODEL, D_MODEL, tg, "grad_w_out"),
        "w_up": _weight_grad(m_b, du_b, D_MODEL, D_FF // N_CHIPS, tg, "grad_w_up"),
        "w_down": _weight_grad(u_b, dh2_b, D_FF // N_CHIPS, D_MODEL, tg, "grad_w_down", relu_sq=True),
        "w_ple_gate": _weight_grad(n3_b, dgl_b, D_MODEL, D_MODEL, tg, "grad_w_ple_gate"),
        "w_ple_proj": _weight_grad(p_b, dpp_b, PLE_DIM, D_MODEL // N_CHIPS, tg, "grad_w_ple_proj"),
    }
    by_chip = lambda k, g: g.reshape((N_CHIPS, big[k].shape[0], big[k].shape[1]))
    part = {k: by_chip(k, g32) for k, (g32, _) in both_kinds.items()}
    part_b = {k: by_chip(k, g16) for k, (_, g16) in both_kinds.items()}
    (dcn, do, gg_attn), from_sibling = _out_proj_bwd(
        dh1_b, o, g_attn_out, w_out_f, tm, _pair_exchange([part_b[k] for k in later_names]))
    pair = [_pair_sum(part[k], r, core, "pair_sum_" + k) for k, r in zip(later_names, from_sibling)]
    dconv, g_conv_w, gg_conv = _conv_bwd(proj, dcn, conv_f, g_conv_out)
    (dq, dk, dv), from_chips = _attn_bwd(proj, kp, vp, kt, do, tot, _chip_exchange(pair))
    dk, dv = dk.reshape(s, W_ATTN), dv.reshape(s, W_ATTN)

    part["w_in"] = _grad_w_in(a_b, dconv, dq, dk, dv, min(MLP_TOKEN_TILE, s))
    in_sibling = _run_exchange(_pair_exchange([part["w_in"]]), "grad_pair_exchange_w_in")
    in_pair = _pair_sum(part["w_in"], in_sibling[0], core, "pair_sum_w_in")
    (grad_x, gg_mix), in_chips = _in_proj_bwd(
        dconv, dq, dk, dv, dh1, xs, g_mix, w_in_f, tm, _chip_exchange([in_pair]))

    place = jnp.stack([lax.axis_index("c"), chip]).astype(jnp.int32)
    half = [_chip_sum(mine, landed, place, "chip_sum_" + k)
            for k, mine, landed in zip(names, [in_pair] + pair, list(in_chips) + list(from_chips))]
    both = _sibling_exchange(half)
    grad = {k: b.reshape(big[k].shape) for k, b in zip(names, both)}

    gcw = g_conv_w[:, :3, :].transpose(1, 0, 2).reshape(3, W_CONV)
    row = lambda *parts: jnp.concatenate(parts, axis=1)
    packed = jnp.concatenate([
        gg_mix, gg_mlp, gg_ple, gg_final, row(gg_conv, gg_attn), row(gcw[0:1], gcw[1:2]),
        row(gcw[2:3], loss_row, jnp.zeros((1, W_CONV - LANES), f32)), jnp.zeros((1, D_MODEL), f32)], axis=0)
    summed = _allreduce_small(packed)
    loss = summed[6, W_CONV]
    gcw_full = jnp.stack([summed[5, :W_CONV], summed[5, W_CONV:], summed[6, :W_CONV]])
    grad["conv_w"] = lax.dynamic_slice(gcw_full, (0, chip * LANES), (3, LANES))
    vec_names = ["g_mix", "g_mlp", "g_ple", "g_final"]
    vec_w = {"g_mix": g_mix, "g_mlp": g_mlp, "g_ple": g_ple, "g_final": g_final.reshape(1, D_MODEL)}
    vec_m = {"g_mix": m_g_mix, "g_mlp": m_g_mlp, "g_ple": m_g_ple, "g_final": m_g_final.reshape(1, D_MODEL)}
    vec_v = {"g_mix": v_g_mix, "g_mlp": v_g_mlp, "g_ple": v_g_ple, "g_final": v_g_final.reshape(1, D_MODEL)}

    def pack_vec(d, conv, attn):
        return jnp.concatenate([d[k] for k in vec_names] + [row(conv, attn)], axis=0)

    vec_g = summed[0:5]
    vec_d, vec_nm, vec_nv, vec_g = _adamw(vec_g, pack_vec(vec_w, g_conv_out, g_attn_out),
                                   pack_vec(vec_m, m_g_conv_out, m_g_attn_out),
                                   pack_vec(vec_v, v_g_conv_out, v_g_attn_out), "adamw_vectors")

    given_w = dict(big, conv_w=conv_w[0])
    given_m = {"w_in": m_w_in[0], "w_out": m_w_out[0], "w_up": m_w_up[0], "w_down": m_w_down[0],
               "w_ple_gate": m_w_ple_gate[0], "w_ple_proj": m_w_ple_proj[0], "conv_w": m_conv_w[0]}
    given_v = {"w_in": v_w_in[0], "w_out": v_w_out[0], "w_up": v_w_up[0], "w_down": v_w_down[0],
               "w_ple_gate": v_w_ple_gate[0], "w_ple_proj": v_w_ple_proj[0], "conv_w": v_conv_w[0]}
    delta, new_m, new_v = {}, {}, {}
    for k in names + ["conv_w"]:
        delta[k], new_m[k], new_v[k], grad[k] = _adamw(grad[k], given_w[k], given_m[k], given_v[k], "adamw_" + k)

    def unpack(vals, kind):
        out = {k: vals[i:i + 1] for i, k in enumerate(vec_names)}
        out["g_final"] = out["g_final"].reshape(D_MODEL)
        out["g_conv_out"] = vals[4:5, :W_CONV]
        out["g_attn_out"] = vals[4:5, W_CONV:]
        out.update({k: v[None] for k, v in kind.items()})
        return out

    order = ["g_mix", "w_in", "conv_w", "g_conv_out", "g_attn_out", "w_out", "g_mlp", "w_up", "w_down",
             "g_ple", "w_ple_gate", "w_ple_proj", "g_final"]
    groups = [unpack(vec_g, grad), unpack(vec_d, delta), unpack(vec_nm, new_m), unpack(vec_nv, new_v)]
    return (loss, grad_x[None]) + tuple(g[k] for g in groups for k in order)
```

```python
import functools

import jax
import jax.numpy as jnp
from jax import lax
from jax.experimental import pallas as pl
from jax.experimental.pallas import tpu as pltpu

f32 = jnp.float32
bf16 = jnp.bfloat16

D_MODEL = 1024
HEAD_DIM = 64
W_CONV = 512
W_ATTN = 512
D_FF = 4096
PLE_DIM = 256
IN_COLS = 3 * W_CONV + 3 * W_ATTN
N_CHIPS = 4
EPS = 1e-6
ADAM_LR = 0.001
ADAM_B1 = 0.9
ADAM_B2 = 0.999
ADAM_EPS = 1e-08
ADAM_WD = 0.01
ADAM_STEP = 10

LANES = 128
TOKEN_TILE = 512
MLP_TOKEN_TILE = 1024
GRAD_TOKEN_TILE = 4096
ATTN_TILE = 256
CONV_CHUNK = 512
CONV_W_ROWS = 16

MESH = pl.DeviceIdType.MESH
ANY = pl.BlockSpec(memory_space=pl.ANY)
NT = (((1,), (1,)), ((), ()))
TN = (((0,), (0,)), ((), ()))


def _arb(n):
    return pltpu.CompilerParams(dimension_semantics=("arbitrary",) * n)


def _sds(shape, dtype):
    return jax.ShapeDtypeStruct(shape, dtype)


def _dot(a, b, dims=None):
    if dims is None:
        return jnp.dot(a, b, preferred_element_type=f32)
    return lax.dot_general(a, b, dims, preferred_element_type=f32)


def _split_dot(x, ones):
    hi = x.astype(bf16)
    lo = (x - hi.astype(f32)).astype(bf16)
    return _dot(hi, ones) + _dot(lo, ones)


def _rms_fwd(h, g):
    rstd = lax.rsqrt(jnp.mean(h * h, axis=-1, keepdims=True) + EPS)
    return h * rstd * g, rstd


def _rms_bwd(dy, h, g):
    rstd = lax.rsqrt(jnp.mean(h * h, axis=-1, keepdims=True) + EPS)
    hn = h * rstd
    dyg = dy * g
    dh = rstd * (dyg - hn * jnp.mean(dyg * hn, axis=-1, keepdims=True))
    return dh, jnp.sum(dy * hn, axis=0, keepdims=True)


def _group_ones(n):
    r = lax.broadcasted_iota(jnp.int32, (n, n), 0) // HEAD_DIM
    c = lax.broadcasted_iota(jnp.int32, (n, n), 1) // HEAD_DIM
    return (r == c).astype(bf16)


def _head_rms_fwd(y, g, ones):
    rstd = lax.rsqrt(_split_dot(y * y, ones) * (1.0 / HEAD_DIM) + EPS)
    return y * rstd * g


def _head_rms_bwd(dy, y, g, ones):
    rstd = lax.rsqrt(_split_dot(y * y, ones) * (1.0 / HEAD_DIM) + EPS)
    yn = y * rstd
    dyg = dy * g
    dyy = rstd * (dyg - yn * (_split_dot(dyg * yn, ones) * (1.0 / HEAD_DIM)))
    return dyy, jnp.sum(dy * yn, axis=0, keepdims=True)


def _place():
    return lax.axis_index("x"), lax.axis_index("y"), lax.axis_index("c")


def _other_chips(x, y):
    return [(1 - x, y), (x, 1 - y), (1 - x, 1 - y)]


class _Exchange:
    def __init__(self, arrays, out_shapes, sems, start, finish, relay=None):
        self.arrays, self.out_shapes, self.sems, self.start, self.finish = arrays, out_shapes, sems, start, finish
        self.relay = relay


def _gather_exchange(shards):
    n = len(shards)
    halves = [s.shape[0] // 2 for s in shards]

    def plan(ins, outs, sems):
        send_sems, recv_sems, own_sems = sems
        x, y, c = _place()
        me = 2 * x + y
        chips = _other_chips(x, y)

        def half(ref, i, which):
            return ref.at[pl.ds(which * halves[i], halves[i]), :]

        def over_ici(i, j, src, slot, to):
            return pltpu.make_async_remote_copy(
                src_ref=src, dst_ref=half(outs[i].at[slot], i, c),
                send_sem=send_sems.at[3 * i + j], recv_sem=recv_sems.at[3 * i + j],
                device_id=to, device_id_type=MESH)

        def to_sibling(i, j, slot, which):
            blk = half(outs[i].at[slot], i, which)
            return pltpu.make_async_remote_copy(
                src_ref=blk, dst_ref=blk,
                send_sem=send_sems.at[3 * n + 3 * i + j], recv_sem=recv_sems.at[3 * n + 3 * i + j],
                device_id=(x, y, 1 - c), device_id_type=MESH)

        own = [pltpu.make_async_remote_copy(
            src_ref=ins[i], dst_ref=outs[i].at[me], send_sem=own_sems.at[i], recv_sem=own_sems.at[n + i],
            device_id=(x, y, 1 - c), device_id_type=MESH) for i in range(n)]
        pairs = [(i, j, px, py) for i in range(n) for j, (px, py) in enumerate(chips)]
        sends = [over_ici(i, j, half(ins[i], i, c), me, (px, py, c)) for i, j, px, py in pairs]
        lands = [over_ici(i, j, half(outs[i].at[2 * px + py], i, c), 2 * px + py, (px, py, c)) for i, j, px, py in pairs]
        passes = [to_sibling(i, j, 2 * px + py, c) for i, j, px, py in pairs]
        from_sibling = [to_sibling(i, j, 2 * px + py, 1 - c) for i, j, px, py in pairs]
        return own, sends, lands, passes, from_sibling

    def start(ins, outs, sems):
        own, sends, _, _, _ = plan(ins, outs, sems)
        for cp in own + sends:
            cp.start()

    def relay(ins, outs, sems):
        _, _, lands, passes, _ = plan(ins, outs, sems)
        for land, on in zip(lands, passes):
            land.wait_recv()
            on.start()

    def finish(ins, outs, sems):
        own, sends, _, passes, from_sibling = plan(ins, outs, sems)
        for cp in from_sibling:
            cp.wait_recv()
        for cp in sends + passes:
            cp.wait_send()
        for cp in own:
            cp.wait()

    return _Exchange(
        shards, [_sds((N_CHIPS,) + s.shape, s.dtype) for s in shards],
        [pltpu.SemaphoreType.DMA((6 * n,)), pltpu.SemaphoreType.DMA((6 * n,)), pltpu.SemaphoreType.DMA((2 * n,))],
        start, finish, relay)


def _call_carrying(ex, body, name, grid, in_specs, out_specs, out_shape, scratch_shapes, args):
    n_in, n_out, n_scr = len(in_specs), len(out_specs), len(scratch_shapes)
    k = 0 if ex is None else len(ex.arrays)

    def wrapped(*refs):
        ins, xin = refs[:n_in], refs[n_in:n_in + k]
        outs, xout = refs[n_in + k:n_in + k + n_out], refs[n_in + k + n_out:n_in + 2 * k + n_out]
        scr, sems = refs[n_in + 2 * k + n_out:n_in + 2 * k + n_out + n_scr], refs[n_in + 2 * k + n_out + n_scr:]
        ids = [pl.program_id(d) for d in range(len(grid))]
        if ex is not None:
            @pl.when(functools.reduce(lambda a, b: a & b, [i == 0 for i in ids]))
            def _():
                ex.start(xin, xout, sems)

        if ex is not None and ex.relay is not None:
            relay_at = [grid[0] - 1] + [0] * (len(grid) - 1) if len(grid) > 1 else [grid[0] - 1]

            @pl.when(functools.reduce(lambda a, b: a & b, [i == r for i, r in zip(ids, relay_at)]))
            def _():
                ex.relay(xin, xout, sems)

        body(*ins, *outs, *scr)
        if ex is not None:
            @pl.when(functools.reduce(lambda a, b: a & b, [i == g - 1 for i, g in zip(ids, grid)]))
            def _():
                ex.finish(xin, xout, sems)

    res = pl.pallas_call(
        wrapped, name=name, grid=grid,
        in_specs=list(in_specs) + [ANY] * k, out_specs=list(out_specs) + [ANY] * k,
        out_shape=list(out_shape) + ([] if ex is None else list(ex.out_shapes)),
        scratch_shapes=list(scratch_shapes) + ([] if ex is None else list(ex.sems)),
        compiler_params=_arb(len(grid)),
    )(*args, *([] if ex is None else ex.arrays))
    return res[:n_out], res[n_out:]


def _run_exchange(ex, name):
    n = len(ex.arrays)

    def body(*refs):
        ins, outs, sems = refs[:n], refs[n:2 * n], refs[2 * n:]
        ex.start(ins, outs, sems)
        if ex.relay is not None:
            ex.relay(ins, outs, sems)
        ex.finish(ins, outs, sems)

    return pl.pallas_call(
        body, name=name, out_shape=ex.out_shapes, in_specs=[ANY] * n, out_specs=[ANY] * n,
        scratch_shapes=ex.sems,
    )(*ex.arrays)


def _pair_exchange(grads):
    n = len(grads)

    def plan(ins, outs, sems):
        send_sems, recv_sems = sems
        x, y, c = _place()
        return [pltpu.make_async_remote_copy(
            src_ref=ins[i].at[:, 1 - c], dst_ref=outs[i],
            send_sem=send_sems.at[i], recv_sem=recv_sems.at[i],
            device_id=(x, y, 1 - c), device_id_type=MESH) for i in range(n)]

    def start(ins, outs, sems):
        for cp in plan(ins, outs, sems):
            cp.start()

    def finish(ins, outs, sems):
        for cp in plan(ins, outs, sems):
            cp.wait()

    views = [g.reshape(N_CHIPS, 2, g.shape[1] // 2, g.shape[2]) for g in grads]
    return _Exchange(
        views, [_sds((N_CHIPS, v.shape[2], v.shape[3]), v.dtype) for v in views],
        [pltpu.SemaphoreType.DMA((n,)), pltpu.SemaphoreType.DMA((n,))], start, finish)


def _chip_exchange(parts):
    n = len(parts)

    def plan(ins, outs, sems):
        send_sems, recv_sems = sems
        x, y, c = _place()
        me = 2 * x + y
        pairs = [(i, j, px, py) for i in range(n) for j, (px, py) in enumerate(_other_chips(x, y))]

        def copy(i, j, src, slot, px, py):
            return pltpu.make_async_remote_copy(
                src_ref=src, dst_ref=outs[i].at[slot],
                send_sem=send_sems.at[3 * i + j], recv_sem=recv_sems.at[3 * i + j],
                device_id=(px, py, c), device_id_type=MESH)

        sends = [copy(i, j, ins[i].at[2 * px + py], me, px, py) for i, j, px, py in pairs]
        lands = [copy(i, j, outs[i].at[2 * px + py], 2 * px + py, px, py) for i, j, px, py in pairs]
        return sends, lands

    def start(ins, outs, sems):
        sends, _ = plan(ins, outs, sems)
        for cp in sends:
            cp.start()

    def finish(ins, outs, sems):
        sends, lands = plan(ins, outs, sems)
        for cp in lands:
            cp.wait_recv()
        for cp in sends:
            cp.wait_send()

    return _Exchange(
        parts, [_sds(p.shape, p.dtype) for p in parts],
        [pltpu.SemaphoreType.DMA((3 * n,)), pltpu.SemaphoreType.DMA((3 * n,))],
        start, finish)


def _sibling_exchange(both):
    n = len(both)

    def body(*refs):
        outs = refs[n:2 * n]
        send_sems, recv_sems = refs[2 * n:]
        x, y, c = _place()
        sent = []
        for i in range(n):
            cp = pltpu.make_async_remote_copy(
                src_ref=outs[i].at[c], dst_ref=outs[i].at[c],
                send_sem=send_sems.at[i], recv_sem=recv_sems.at[i],
                device_id=(x, y, 1 - c), device_id_type=MESH)
            cp.start()
            sent.append(cp)
        for cp in sent:
            cp.wait()

    return pl.pallas_call(
        body, name="grad_sibling_exchange",
        out_shape=[_sds(b.shape, f32) for b in both],
        in_specs=[ANY] * n, out_specs=[ANY] * n,
        input_output_aliases={i: i for i in range(n)},
        scratch_shapes=[pltpu.SemaphoreType.DMA((n,)), pltpu.SemaphoreType.DMA((n,))],
    )(*both)


def _row_tile(rows, cols):
    t = rows
    while t * cols * 4 > (2 << 20) and t % 16 == 0:
        t //= 2
    return t


def _pair_sum(grad, recv, core, name):
    _, r, c = grad.shape
    hr = r // 2
    tr = _row_tile(hr, c)
    view = grad.reshape(N_CHIPS, 2, hr, c)

    def body(core_ref, mine_ref, recv_ref, out_ref):
        out_ref[...] = (mine_ref[...] + recv_ref[...].astype(f32)).astype(bf16)

    return pl.pallas_call(
        body, name=name,
        grid_spec=pltpu.PrefetchScalarGridSpec(
            num_scalar_prefetch=1, grid=(N_CHIPS, hr // tr),
            in_specs=[pl.BlockSpec((None, None, tr, c), lambda k, t, core_ref: (k, core_ref[0], t, 0)),
                      pl.BlockSpec((None, tr, c), lambda k, t, core_ref: (k, t, 0))],
            out_specs=pl.BlockSpec((None, tr, c), lambda k, t, core_ref: (k, t, 0))),
        out_shape=_sds((N_CHIPS, hr, c), bf16),
        compiler_params=_arb(2),
    )(core, view, recv)


def _chip_sum(mine, landed, place, name):
    _, hr, c = mine.shape
    tr = _row_tile(hr, c)

    def body(place_ref, a_ref, b_ref, c_ref, d_ref, out_ref):
        out_ref[...] = ((a_ref[...].astype(f32) + b_ref[...].astype(f32)) + c_ref[...].astype(f32)) + d_ref[...].astype(f32)

    def slot(k):
        return pl.BlockSpec((None, tr, c), lambda t, place_ref: ((place_ref[1] + k) % N_CHIPS, t, 0))

    return pl.pallas_call(
        body, name=name,
        grid_spec=pltpu.PrefetchScalarGridSpec(
            num_scalar_prefetch=1, grid=(hr // tr,),
            in_specs=[slot(0), slot(1), slot(2), slot(3)],
            out_specs=pl.BlockSpec((None, tr, c), lambda t, place_ref: (place_ref[0], t, 0))),
        out_shape=_sds((2, hr, c), f32), compiler_params=_arb(1),
    )(place, mine, landed, landed, landed)


def _adamw(g, w, m, v, name):
    r, c = g.shape
    tr = _row_tile(r, c)

    def body(g_ref, w_ref, m_ref, v_ref, d_ref, nm_ref, nv_ref, go_ref):
        gv = g_ref[...]
        go_ref[...] = gv
        mv = ADAM_B1 * m_ref[...] + (1.0 - ADAM_B1) * gv
        vv = ADAM_B2 * v_ref[...] + (1.0 - ADAM_B2) * jnp.square(gv)
        m_hat = mv / (1.0 - ADAM_B1 ** ADAM_STEP)
        v_hat = vv / (1.0 - ADAM_B2 ** ADAM_STEP)
        d_ref[...] = -ADAM_LR * (m_hat / (jnp.sqrt(v_hat) + ADAM_EPS) + ADAM_WD * w_ref[...])
        nm_ref[...] = mv
        nv_ref[...] = vv

    spec = pl.BlockSpec((tr, c), lambda t: (t, 0))
    return pl.pallas_call(
        body, name=name, grid=(r // tr,), in_specs=[spec] * 4, out_specs=[spec] * 4,
        out_shape=[_sds((r, c), f32)] * 4, compiler_params=_arb(1),
    )(g, w, m, v)


def _allreduce_small(packed):
    shape = packed.shape

    def body(x_ref, out_ref, buf, send_sems, recv_sems):
        x, y, c = _place()
        me = 4 * x + 2 * y + c
        buf[me] = x_ref[...]
        sent = []
        for r in range(1, 8):
            dx, dy, dc = (r >> 2) & 1, (r >> 1) & 1, r & 1
            peer = ((1 - x) if dx else x, (1 - y) if dy else y, (1 - c) if dc else c)
            cp = pltpu.make_async_remote_copy(
                src_ref=x_ref, dst_ref=buf.at[me],
                send_sem=send_sems.at[r], recv_sem=recv_sems.at[r],
                device_id=peer, device_id_type=MESH)
            cp.start()
            sent.append((cp, peer))
        for r, (cp, peer) in enumerate(sent, start=1):
            src = 4 * peer[0] + 2 * peer[1] + peer[2]
            pltpu.make_async_remote_copy(
                src_ref=x_ref, dst_ref=buf.at[src],
                send_sem=send_sems.at[r], recv_sem=recv_sems.at[r],
                device_id=peer, device_id_type=MESH).wait_recv()
        for cp, _ in sent:
            cp.wait_send()
        total = buf[0]
        for k in range(1, 8):
            total = total + buf[k]
        out_ref[...] = total

    vmem = pl.BlockSpec(memory_space=pltpu.VMEM)
    return pl.pallas_call(
        body, name="allreduce_small", out_shape=_sds(shape, f32),
        in_specs=[vmem], out_specs=vmem,
        scratch_shapes=[pltpu.VMEM((8,) + shape, f32), pltpu.SemaphoreType.DMA((8,)),
                        pltpu.SemaphoreType.DMA((8,))],
    )(packed)


def _in_proj(x, g_mix, w_in, tm):
    s = x.shape[0]
    ncol = IN_COLS // N_CHIPS
    t = ATTN_TILE
    nb = tm // t
    koff = 3 * W_CONV + W_ATTN

    def body(x_ref, g_ref, w_ref, proj_ref, a_ref, kp_ref, vp_ref, kt_ref, vt_ref, perm, kv):
        a, _ = _rms_fwd(x_ref[...], g_ref[...])
        ab = a.astype(bf16)
        a_ref[...] = ab
        for k in range(N_CHIPS):
            proj_ref[:, k * ncol:(k + 1) * ncol] = _dot(ab, w_ref[k])
        for j in range(2 * W_ATTN // LANES):
            kv[j] = proj_ref[:, koff + j * LANES:koff + (j + 1) * LANES]
        for b in range(nb):
            for r in range(KEY_RUN):
                for j in range(2 * W_ATTN // LANES):
                    perm[SUBLANES * r:SUBLANES * (r + 1), j * LANES:(j + 1) * LANES] = kv[
                        j, pl.ds(b * t + r, SUBLANES, stride=KEY_RUN), :]
            kp_ref[b] = perm[:, :W_ATTN].astype(bf16)
            vp_ref[b] = perm[:, W_ATTN:].astype(bf16)
            kt_ref[b] = perm[:, :W_ATTN].T.astype(bf16)
            vt_ref[b] = perm[:, W_ATTN:].T.astype(bf16)

    keys = pl.BlockSpec((nb, t, W_ATTN), lambda i: (i, 0, 0))
    keys_t = pl.BlockSpec((nb, W_ATTN, t), lambda i: (i, 0, 0))
    return pl.pallas_call(
        body, name="in_proj", grid=(s // tm,),
        in_specs=[pl.BlockSpec((tm, D_MODEL), lambda i: (i, 0)),
                  pl.BlockSpec((1, D_MODEL), lambda i: (0, 0)),
                  pl.BlockSpec((N_CHIPS, D_MODEL, ncol), lambda i: (0, 0, 0))],
        out_specs=[pl.BlockSpec((tm, IN_COLS), lambda i: (i, 0)),
                   pl.BlockSpec((tm, D_MODEL), lambda i: (i, 0)), keys, keys, keys_t, keys_t],
        out_shape=[_sds((s, IN_COLS), f32), _sds((s, D_MODEL), bf16),
                   _sds((s // t, t, W_ATTN), bf16), _sds((s // t, t, W_ATTN), bf16),
                   _sds((s // t, W_ATTN, t), bf16), _sds((s // t, W_ATTN, t), bf16)],
        scratch_shapes=[pltpu.VMEM((t, 2 * W_ATTN), f32), pltpu.VMEM((2 * W_ATTN // LANES, tm, LANES), f32)],
        compiler_params=_arb(1),
    )(x, g_mix, w_in)


def _shifted(prev8, cur, shift):
    ext = jnp.concatenate([prev8, cur], axis=0)
    return pltpu.roll(ext, shift, axis=0)[8:]


def _conv_fwd(proj, conv_w, g_conv):
    s = proj.shape[0]
    nblk = W_CONV // LANES
    rc = min(CONV_CHUNK, s)

    def body(cb_ref, cc_ref, cu_ref, w_ref, g_ref, out_ref):
        ones = _group_ones(LANES)
        w0, w1, w2 = w_ref[0:1, :], w_ref[1:2, :], w_ref[2:3, :]
        g = g_ref[...]

        def chunk(i, carry):
            r0 = pl.multiple_of(i * rc, rc)
            rows = pl.ds(r0, rc)
            prev = pl.ds(pl.multiple_of(jnp.maximum(r0 - 8, 0), 8), 8)
            v = cc_ref[rows, :] * cu_ref[rows, :]
            vp = jnp.where(i > 0, cc_ref[prev, :] * cu_ref[prev, :], 0.0)
            y = w2 * v + w1 * _shifted(vp, v, 1) + w0 * _shifted(vp, v, 2)
            out_ref[rows, :] = _head_rms_fwd(cb_ref[rows, :] * y, g, ones).astype(bf16)
            return carry

        lax.fori_loop(0, s // rc, chunk, 0)

    def col(off):
        return pl.BlockSpec((s, LANES), lambda j: (0, off + j))

    return pl.pallas_call(
        body, name="conv_fwd", grid=(nblk,),
        in_specs=[col(0), col(nblk), col(2 * nblk),
                  pl.BlockSpec((None, CONV_W_ROWS, LANES), lambda j: (j, 0, 0)),
                  pl.BlockSpec((1, LANES), lambda j: (0, j))],
        out_specs=pl.BlockSpec((s, LANES), lambda j: (0, j)),
        out_shape=_sds((s, W_CONV), bf16), compiler_params=_arb(1),
    )(proj, proj, proj, conv_w, g_conv)


LOG2_E = 1.4426950408889634


def _log2_keep(z2):
    nz2 = -z2
    return jnp.minimum(nz2, 0.0) - jnp.log2(1.0 + jnp.exp2(jnp.minimum(z2, nz2)))


def _head_pair_masks(rows):
    lane = lax.broadcasted_iota(jnp.int32, (rows, LANES), 1)
    return lane < HEAD_DIM


SUBLANES = 8
KEY_RUN = ATTN_TILE // SUBLANES


def _causal_tiles():
    r = lax.broadcasted_iota(jnp.int32, (ATTN_TILE, ATTN_TILE), 0)
    key = (r % SUBLANES) * KEY_RUN + r // SUBLANES
    below = key < lax.broadcasted_iota(jnp.int32, (ATTN_TILE, ATTN_TILE), 1)
    return below.astype(f32), jnp.where(below, 0.0, -1e30).astype(f32)


def _sublane_scan(x, reverse):
    row = lax.broadcasted_iota(jnp.int32, x.shape, 0)
    inc = x
    for sh in (1, 2, 4):
        if reverse:
            inc = inc + jnp.where(row < SUBLANES - sh, pltpu.roll(inc, SUBLANES - sh, axis=0), 0.0)
        else:
            inc = inc + jnp.where(row >= sh, pltpu.roll(inc, sh, axis=0), 0.0)
    return inc - x


def _attn_fwd(proj, kp, vt, ex=None):
    s = proj.shape[0]
    t = ATTN_TILE
    nblk = s // t
    npair = W_ATTN // LANES
    qoff = 3 * W_CONV // LANES
    keep01, keepneg = _causal_tiles()

    def body(q_ref, k_ref, vt_ref, m01_ref, neg_ref, o_ref, tot_ref, w_s, a_s, acc):
        qb = pl.program_id(1)
        first = _head_pair_masks(t)
        q = q_ref[...] * (HEAD_DIM ** -0.5)
        qh = (jnp.where(first, q, 0.0).astype(bf16), jnp.where(first, 0.0, q).astype(bf16))
        acc[...] = jnp.zeros_like(acc)
        a_s[1] = jnp.zeros((t, t), bf16)

        def scores(kb, h):
            w_s[h] = _dot(k_ref[kb], qh[h], NT)

        def weigh(kb, h):
            acc[h] += _dot(vt_ref[kb], a_s[h])

        def weights(h, diagonal, later):
            run = jnp.zeros((SUBLANES, t), f32)
            for a in reversed(range(KEY_RUN)):
                rows = slice(SUBLANES * a, SUBLANES * (a + 1))
                z2 = w_s[h, rows, :] * LOG2_E
                lk = _log2_keep(z2)
                if diagonal:
                    lk = lk * m01_ref[rows, :]
                run = run + lk
                w_s[h, rows, :] = z2 + run
            off = _sublane_scan(run, reverse=True) + later
            off2 = jnp.concatenate([off, off], axis=0)
            for a in range(t // (2 * SUBLANES)):
                rows = slice(2 * SUBLANES * a, 2 * SUBLANES * (a + 1))
                w = w_s[h, rows, :] + off2
                if diagonal:
                    w = w + neg_ref[rows, :]
                a_s[h, rows, :] = jnp.exp2(w).astype(bf16)
            return later + jnp.sum(run, axis=0, keepdims=True)

        def block(kb, before, after, diagonal, later):
            scores(kb, 1)
            weigh(before, 1)
            l0 = weights(0, diagonal, later[0])
            scores(after, 0)
            weigh(kb, 0)
            l1 = weights(1, diagonal, later[1])
            return l0, l1

        zero = jnp.zeros((1, t), f32)
        scores(qb, 0)
        later = block(qb, qb, jnp.maximum(qb - 1, 0), True, (zero, zero))

        def earlier(i, c):
            kb = qb - 1 - i
            return block(kb, kb + 1, jnp.maximum(kb - 1, 0), False, c)

        later = lax.fori_loop(0, qb, earlier, later)
        weigh(0, 1)
        top = lax.broadcasted_iota(jnp.int32, (LANES, t), 0) < HEAD_DIM
        o_ref[...] = jnp.where(top, acc[0], acc[1]).T
        tot_ref[...] = jnp.concatenate([later[0], later[1], jnp.zeros((SUBLANES - 2, t), f32)], axis=0)

    return _call_carrying(
        ex, body, "attn_fwd", (npair, nblk),
        in_specs=[pl.BlockSpec((t, LANES), lambda p, i: (i, qoff + p)),
                  pl.BlockSpec((nblk, t, LANES), lambda p, i: (0, 0, p)),
                  pl.BlockSpec((nblk, LANES, t), lambda p, i: (0, p, 0)),
                  pl.BlockSpec((t, t), lambda p, i: (0, 0)),
                  pl.BlockSpec((t, t), lambda p, i: (0, 0))],
        out_specs=[pl.BlockSpec((t, LANES), lambda p, i: (i, p)),
                   pl.BlockSpec((None, SUBLANES, t), lambda p, i: (p, 0, i))],
        out_shape=[_sds((s, W_ATTN), f32), _sds((npair, SUBLANES, s), f32)],
        scratch_shapes=[pltpu.VMEM((2, t, t), f32), pltpu.VMEM((2, t, t), bf16), pltpu.VMEM((2, LANES, t), f32)],
        args=(proj, kp, vt, keep01, keepneg))


def _out_proj(o, conv_n, x, g_attn, w_out, tm):
    s = x.shape[0]

    def body(o_ref, c_ref, x_ref, g_ref, w_ref, h_ref, cat_ref):
        ones = _group_ones(LANES)
        cat_ref[:, :W_CONV] = c_ref[...]
        for j in range(W_ATTN // LANES):
            cols = slice(j * LANES, (j + 1) * LANES)
            cat_ref[:, W_CONV + j * LANES:W_CONV + (j + 1) * LANES] = _head_rms_fwd(
                o_ref[:, cols], g_ref[:, cols], ones).astype(bf16)
        h_ref[...] = x_ref[...] + _dot(cat_ref[...], w_ref[...])

    return pl.pallas_call(
        body, name="out_proj", grid=(s // tm,),
        in_specs=[pl.BlockSpec((tm, W_ATTN), lambda i: (i, 0)),
                  pl.BlockSpec((tm, W_CONV), lambda i: (i, 0)),
                  pl.BlockSpec((tm, D_MODEL), lambda i: (i, 0)),
                  pl.BlockSpec((1, W_ATTN), lambda i: (0, 0)),
                  pl.BlockSpec((D_MODEL, D_MODEL), lambda i: (0, 0))],
        out_specs=[pl.BlockSpec((tm, D_MODEL), lambda i: (i, 0)),
                   pl.BlockSpec((tm, D_MODEL), lambda i: (i, 0))],
        out_shape=[_sds((s, D_MODEL), f32), _sds((s, D_MODEL), bf16)],
        compiler_params=_arb(1),
    )(o, conv_n, x, g_attn, w_out)


def _mlp_fwd(h1, g_mlp, w_up, w_down, tm):
    s = h1.shape[0]
    fc = D_FF // N_CHIPS

    def body(h_ref, g_ref, wu_ref, wd_ref, h2_ref, u_ref, m_ref):
        j = pl.program_id(1)

        @pl.when(j == 0)
        def _():
            m, _ = _rms_fwd(h_ref[...], g_ref[...])
            m_ref[...] = m.astype(bf16)
            h2_ref[...] = h_ref[...]

        u = _dot(m_ref[...], wu_ref[...])
        u_ref[...] = u.astype(bf16)
        h2_ref[...] += _dot(jnp.square(jnp.maximum(u, 0.0)).astype(bf16), wd_ref[...])

    return pl.pallas_call(
        body, name="mlp_fwd", grid=(s // tm, N_CHIPS),
        in_specs=[pl.BlockSpec((tm, D_MODEL), lambda i, j: (i, 0)),
                  pl.BlockSpec((1, D_MODEL), lambda i, j: (0, 0)),
                  pl.BlockSpec((None, D_MODEL, fc), lambda i, j: (j, 0, 0)),
                  pl.BlockSpec((None, fc, D_MODEL), lambda i, j: (j, 0, 0))],
        out_specs=[pl.BlockSpec((tm, D_MODEL), lambda i, j: (i, 0)),
                   pl.BlockSpec((tm, fc), lambda i, j: (i, j)),
                   pl.BlockSpec((tm, D_MODEL), lambda i, j: (i, 0))],
        out_shape=[_sds((s, D_MODEL), f32), _sds((s, D_FF), bf16), _sds((s, D_MODEL), bf16)],
        compiler_params=_arb(2),
    )(h1, g_mlp, w_up, w_down)


def _tail(h2, p, target, g_ple, g_final, w_gate, w_proj, tm):
    s = h2.shape[0]
    pc = D_MODEL // N_CHIPS

    def body(h_ref, p_ref, t_ref, gp_ref, gf_ref, wg_ref, wp_ref,
             dh_ref, dhb_ref, n3_ref, dgl_ref, dpp_ref, pb_ref, ggp_ref, ggf_ref, loss_ref, pp_ref):
        i = pl.program_id(0)
        h2v = h_ref[...]
        n3, _ = _rms_fwd(h2v, gp_ref[...])
        n3b = n3.astype(bf16)
        n3_ref[...] = n3b
        gate = jax.nn.sigmoid(_dot(n3b, wg_ref[...]))
        pb = p_ref[...].astype(bf16)
        pb_ref[...] = pb
        for k in range(N_CHIPS):
            pp_ref[:, k * pc:(k + 1) * pc] = _dot(pb, wp_ref[k])
        pp = pp_ref[...]
        h3 = h2v + gate * pp
        yv, _ = _rms_fwd(h3, gf_ref[...])
        err = yv - t_ref[...]
        loss = 0.5 * jnp.sum(err * err) * (1.0 / D_MODEL)
        dh3, ggf = _rms_bwd(err * (1.0 / D_MODEL), h3, gf_ref[...])
        dpp_ref[...] = (dh3 * gate).astype(bf16)
        dgl = (dh3 * pp * gate * (1.0 - gate)).astype(bf16)
        dgl_ref[...] = dgl
        dn3 = _dot(dgl, wg_ref[...], NT)
        dh2n, ggp = _rms_bwd(dn3, h2v, gp_ref[...])
        dh2 = dh3 + dh2n
        dh_ref[...] = dh2
        dhb_ref[...] = dh2.astype(bf16)

        @pl.when(i == 0)
        def _():
            ggp_ref[...] = jnp.zeros_like(ggp_ref)
            ggf_ref[...] = jnp.zeros_like(ggf_ref)
            loss_ref[...] = jnp.zeros_like(loss_ref)

        ggp_ref[...] += ggp
        ggf_ref[...] += ggf
        loss_ref[...] += jnp.full(loss_ref.shape, loss, f32)

    tok = lambda w: pl.BlockSpec((tm, w), lambda i: (i, 0))
    vec = lambda w: pl.BlockSpec((1, w), lambda i: (0, 0))
    return pl.pallas_call(
        body, name="tail", grid=(s // tm,),
        in_specs=[tok(D_MODEL), tok(PLE_DIM), tok(D_MODEL), vec(D_MODEL), vec(D_MODEL),
                  pl.BlockSpec((D_MODEL, D_MODEL), lambda i: (0, 0)),
                  pl.BlockSpec((N_CHIPS, PLE_DIM, pc), lambda i: (0, 0, 0))],
        out_specs=[tok(D_MODEL), tok(D_MODEL), tok(D_MODEL), tok(D_MODEL), tok(D_MODEL), tok(PLE_DIM),
                   vec(D_MODEL), vec(D_MODEL), vec(LANES)],
        out_shape=[_sds((s, D_MODEL), f32), _sds((s, D_MODEL), bf16), _sds((s, D_MODEL), bf16),
                   _sds((s, D_MODEL), bf16), _sds((s, D_MODEL), bf16), _sds((s, PLE_DIM), bf16),
                   _sds((1, D_MODEL), f32), _sds((1, D_MODEL), f32), _sds((1, LANES), f32)],
        scratch_shapes=[pltpu.VMEM((tm, D_MODEL), f32)],
        compiler_params=_arb(1),
    )(h2, p, target, g_ple, g_final, w_gate, w_proj)


def _mlp_bwd(dh2, dh2b, h1, u, g_mlp, w_up, w_down, tm):
    s = h1.shape[0]
    fc = D_FF // N_CHIPS

    def body(dh_ref, dhb_ref, h_ref, u_ref, g_ref, wu_ref, wd_ref, dh1_ref, dh1b_ref, du_ref, gg_ref, dm):
        i, j = pl.program_id(0), pl.program_id(1)

        @pl.when(j == 0)
        def _():
            dm[...] = jnp.zeros_like(dm)

        dr = _dot(dhb_ref[...], wd_ref[...], NT)
        du = (dr * (2.0 * jnp.maximum(u_ref[...].astype(f32), 0.0))).astype(bf16)
        du_ref[...] = du
        dm[...] += _dot(du, wu_ref[...], NT)

        @pl.when((i == 0) & (j == 0))
        def _():
            gg_ref[...] = jnp.zeros_like(gg_ref)

        @pl.when(j == N_CHIPS - 1)
        def _():
            dh1n, gg = _rms_bwd(dm[...], h_ref[...], g_ref[...])
            dh1 = dh_ref[...] + dh1n
            dh1_ref[...] = dh1
            dh1b_ref[...] = dh1.astype(bf16)
            gg_ref[...] += gg

    tok = pl.BlockSpec((tm, D_MODEL), lambda i, j: (i, 0))
    ffb = pl.BlockSpec((tm, fc), lambda i, j: (i, j))
    vec = pl.BlockSpec((1, D_MODEL), lambda i, j: (0, 0))
    return pl.pallas_call(
        body, name="mlp_bwd", grid=(s // tm, N_CHIPS),
        in_specs=[tok, tok, tok, ffb, vec,
                  pl.BlockSpec((None, D_MODEL, fc), lambda i, j: (j, 0, 0)),
                  pl.BlockSpec((None, fc, D_MODEL), lambda i, j: (j, 0, 0))],
        out_specs=[tok, tok, ffb, vec],
        out_shape=[_sds((s, D_MODEL), f32), _sds((s, D_MODEL), bf16), _sds((s, D_FF), bf16),
                   _sds((1, D_MODEL), f32)],
        scratch_shapes=[pltpu.VMEM((tm, D_MODEL), f32)],
        compiler_params=_arb(2),
    )(dh2, dh2b, h1, u, g_mlp, w_up, w_down)


def _out_proj_bwd(dh1b, o, g_attn, w_out, tm, ex=None):
    s = o.shape[0]

    def body(dh_ref, o_ref, g_ref, w_ref, dc_ref, do_ref, gg_ref, dcat):
        i = pl.program_id(0)
        ones = _group_ones(LANES)
        dcat[...] = _dot(dh_ref[...], w_ref[...], NT)
        dc_ref[...] = dcat[:, :W_CONV]

        @pl.when(i == 0)
        def _():
            gg_ref[...] = jnp.zeros_like(gg_ref)

        for j in range(W_ATTN // LANES):
            cols = slice(j * LANES, (j + 1) * LANES)
            d, gg = _head_rms_bwd(dcat[:, W_CONV + j * LANES:W_CONV + (j + 1) * LANES],
                                  o_ref[:, cols], g_ref[:, cols], ones)
            do_ref[:, cols] = d
            gg_ref[:, cols] += gg

    return _call_carrying(
        ex, body, "out_proj_bwd", (s // tm,),
        in_specs=[pl.BlockSpec((tm, D_MODEL), lambda i: (i, 0)),
                  pl.BlockSpec((tm, W_ATTN), lambda i: (i, 0)),
                  pl.BlockSpec((1, W_ATTN), lambda i: (0, 0)),
                  pl.BlockSpec((D_MODEL, D_MODEL), lambda i: (0, 0))],
        out_specs=[pl.BlockSpec((tm, W_CONV), lambda i: (i, 0)),
                   pl.BlockSpec((tm, W_ATTN), lambda i: (i, 0)),
                   pl.BlockSpec((1, W_ATTN), lambda i: (0, 0))],
        out_shape=[_sds((s, W_CONV), f32), _sds((s, W_ATTN), f32), _sds((1, W_ATTN), f32)],
        scratch_shapes=[pltpu.VMEM((tm, D_MODEL), f32)],
        args=(dh1b, o, g_attn, w_out))


def _attn_bwd(proj, kp, vp, kt, do, tot, ex=None):
    s = proj.shape[0]
    t = ATTN_TILE
    nblk = s // t
    npair = W_ATTN // LANES
    qoff = 3 * W_CONV // LANES
    keep01, keepneg = _causal_tiles()

    def body(q_ref, k_ref, v_ref, kt_ref, do_ref, tot_ref, m01_ref, neg_ref, dq_ref, dk_ref, dv_ref,
             w_s, da_s, b_s, g_s, a_s, dz_s, dq_acc, dk_acc, dv_acc):
        qb = pl.program_id(1)
        first = _head_pair_masks(t)
        q = q_ref[...] * (HEAD_DIM ** -0.5)
        qh = (jnp.where(first, q, 0.0).astype(bf16), jnp.where(first, 0.0, q).astype(bf16))
        dov = do_ref[...]
        doh = (jnp.where(first, dov, 0.0).astype(bf16), jnp.where(first, 0.0, dov).astype(bf16))
        total = (tot_ref[0:1, :], tot_ref[1:2, :])

        @pl.when(qb == 0)
        def _():
            dk_acc[...] = jnp.zeros_like(dk_acc)
            dv_acc[...] = jnp.zeros_like(dv_acc)

        dq_acc[...] = jnp.zeros_like(dq_acc)
        a_s[1] = jnp.zeros((t, t), bf16)
        dz_s[1] = jnp.zeros((t, t), bf16)

        def scores(kb, h):
            w_s[h] = _dot(k_ref[kb], qh[h], NT)
            da_s[h] = _dot(v_ref[kb], doh[h], NT)

        def spread(kb, h):
            dq_acc[h] += _dot(kt_ref[kb], dz_s[h])
            dk_acc[kb] += _dot(dz_s[h], qh[h])
            dv_acc[kb] += _dot(a_s[h], doh[h])

        def grads(h, diagonal, lk_before, g_before):
            run = jnp.zeros((SUBLANES, t), f32)
            for a in range(KEY_RUN):
                rows = slice(SUBLANES * a, SUBLANES * (a + 1))
                z2 = w_s[h, rows, :] * LOG2_E
                lk = _log2_keep(z2)
                if diagonal:
                    lk = lk * m01_ref[rows, :]
                log_beta = jnp.minimum(z2 + lk, 0.0)
                run = run + lk
                b_s[h, rows, :] = jnp.exp2(log_beta)
                w_s[h, rows, :] = log_beta - run
            off = total[h] - lk_before - _sublane_scan(run, reverse=False)
            lk_sum = jnp.sum(run, axis=0, keepdims=True)
            run = jnp.zeros((SUBLANES, t), f32)
            for a in range(KEY_RUN // 2):
                parts = []
                for r in (slice(2 * SUBLANES * a, 2 * SUBLANES * a + SUBLANES),
                          slice(2 * SUBLANES * a + SUBLANES, 2 * SUBLANES * (a + 1))):
                    w = w_s[h, r, :] + off
                    if diagonal:
                        w = w + neg_ref[r, :]
                    av = jnp.exp2(w)
                    g = av * da_s[h, r, :]
                    run = run + g
                    da_s[h, r, :] = g
                    g_s[h, r, :] = run
                    parts.append(av)
                a_s[h, 2 * SUBLANES * a:2 * SUBLANES * (a + 1), :] = jnp.concatenate(parts, axis=0).astype(bf16)
            goff = g_before + _sublane_scan(run, reverse=False)
            goff2 = jnp.concatenate([goff, goff], axis=0)
            for a in range(KEY_RUN // 2):
                rows = slice(2 * SUBLANES * a, 2 * SUBLANES * (a + 1))
                dz = da_s[h, rows, :] - b_s[h, rows, :] * (g_s[h, rows, :] + goff2)
                if diagonal:
                    dz = dz * m01_ref[rows, :]
                dz_s[h, rows, :] = dz.astype(bf16)
            return lk_before + lk_sum, g_before + jnp.sum(run, axis=0, keepdims=True)

        def block(kb, before, after, diagonal, carry):
            scores(kb, 1)
            spread(before, 1)
            c0 = grads(0, diagonal, carry[0], carry[1])
            if after is not None:
                scores(after, 0)
            spread(kb, 0)
            c1 = grads(1, diagonal, carry[2], carry[3])
            return c0 + c1

        zero = jnp.zeros((1, t), f32)
        scores(0, 0)
        def two_blocks(i, c):
            kb = 2 * i
            c = block(kb, jnp.maximum(kb - 1, 0), kb + 1, False, c)
            return block(kb + 1, kb, kb + 2, False, c)

        carry = lax.fori_loop(0, qb // 2, two_blocks, (zero, zero, zero, zero))
        carry = lax.fori_loop(qb - qb % 2, qb, lambda kb, c: block(kb, jnp.maximum(kb - 1, 0), kb + 1, False, c), carry)
        block(qb, jnp.maximum(qb - 1, 0), None, True, carry)
        spread(qb, 1)
        top = lax.broadcasted_iota(jnp.int32, (LANES, t), 0) < HEAD_DIM
        dq_ref[...] = (jnp.where(top, dq_acc[0], dq_acc[1]).T * (HEAD_DIM ** -0.5)).astype(bf16)

        @pl.when(qb == nblk - 1)
        def _():
            for kb in range(nblk):
                for b in range(SUBLANES):
                    keys_b = slice(KEY_RUN * b, KEY_RUN * (b + 1))
                    dk_ref[kb, keys_b, :] = dk_acc[kb, pl.ds(b, KEY_RUN, stride=SUBLANES), :].astype(bf16)
                    dv_ref[kb, keys_b, :] = dv_acc[kb, pl.ds(b, KEY_RUN, stride=SUBLANES), :].astype(bf16)

    keys = pl.BlockSpec((nblk, t, LANES), lambda p, i: (0, 0, p))
    tile = pl.BlockSpec((t, t), lambda p, i: (0, 0))
    return _call_carrying(
        ex, body, "attn_bwd", (npair, nblk),
        in_specs=[pl.BlockSpec((t, LANES), lambda p, i: (i, qoff + p)),
                  keys, keys,
                  pl.BlockSpec((nblk, LANES, t), lambda p, i: (0, p, 0)),
                  pl.BlockSpec((t, LANES), lambda p, i: (i, p)),
                  pl.BlockSpec((None, SUBLANES, t), lambda p, i: (p, 0, i)),
                  tile, tile],
        out_specs=[pl.BlockSpec((t, LANES), lambda p, i: (i, p)), keys, keys],
        out_shape=[_sds((s, W_ATTN), bf16), _sds((nblk, t, W_ATTN), bf16), _sds((nblk, t, W_ATTN), bf16)],
        scratch_shapes=[pltpu.VMEM((2, t, t), f32), pltpu.VMEM((2, t, t), f32), pltpu.VMEM((2, t, t), f32),
                        pltpu.VMEM((2, t, t), f32), pltpu.VMEM((2, t, t), bf16), pltpu.VMEM((2, t, t), bf16),
                        pltpu.VMEM((2, LANES, t), f32), pltpu.VMEM((nblk, t, LANES), f32),
                        pltpu.VMEM((nblk, t, LANES), f32)],
        args=(proj, kp, vp, kt, do, tot, keep01, keepneg))


def _conv_bwd(proj, dcn, conv_w, g_conv):
    s = proj.shape[0]
    nblk = W_CONV // LANES
    rc = min(CONV_CHUNK, s)
    nchunk = s // rc

    def body(cb_ref, cc_ref, cu_ref, d_ref, w_ref, g_ref, d3_ref, gw_ref, gg_ref, dy_buf):
        ones = _group_ones(LANES)
        w0, w1, w2 = w_ref[0:1, :], w_ref[1:2, :], w_ref[2:3, :]
        g = g_ref[...]

        def first_pass(i, carry):
            gw0, gw1, gw2, gg = carry
            r0 = pl.multiple_of(i * rc, rc)
            rows = pl.ds(r0, rc)
            prev = pl.ds(pl.multiple_of(jnp.maximum(r0 - 8, 0), 8), 8)
            v = cc_ref[rows, :] * cu_ref[rows, :]
            vp = jnp.where(i > 0, cc_ref[prev, :] * cu_ref[prev, :], 0.0)
            v1, v2 = _shifted(vp, v, 1), _shifted(vp, v, 2)
            y = w2 * v + w1 * v1 + w0 * v2
            cb = cb_ref[rows, :]
            dcy, ggi = _head_rms_bwd(d_ref[rows, :], cb * y, g, ones)
            d3_ref[0, rows, :] = (dcy * y).astype(bf16)
            dy = dcy * cb
            dy_buf[rows, :] = dy
            return (gw0 + jnp.sum(dy * v2, axis=0, keepdims=True), gw1 + jnp.sum(dy * v1, axis=0, keepdims=True),
                    gw2 + jnp.sum(dy * v, axis=0, keepdims=True), gg + ggi)

        zero = jnp.zeros((1, LANES), f32)
        gw0, gw1, gw2, gg = lax.fori_loop(0, nchunk, first_pass, (zero, zero, zero, zero))
        gw_ref[...] = jnp.zeros_like(gw_ref)
        gw_ref[0:1, :] = gw0
        gw_ref[1:2, :] = gw1
        gw_ref[2:3, :] = gw2
        gg_ref[...] = gg

        def second_pass(i, carry):
            r0 = pl.multiple_of(i * rc, rc)
            rows = pl.ds(r0, rc)
            nxt = pl.ds(pl.multiple_of(jnp.minimum(r0 + rc, s - 8), 8), 8)
            dy = dy_buf[rows, :]
            dyn = jnp.where(i < nchunk - 1, dy_buf[nxt, :], 0.0)
            ext = jnp.concatenate([dy, dyn], axis=0)
            up1 = pltpu.roll(ext, rc + 8 - 1, axis=0)[:rc]
            up2 = pltpu.roll(ext, rc + 8 - 2, axis=0)[:rc]
            dv = w2 * dy + w1 * up1 + w0 * up2
            d3_ref[1, rows, :] = (dv * cu_ref[rows, :]).astype(bf16)
            d3_ref[2, rows, :] = (dv * cc_ref[rows, :]).astype(bf16)
            return carry

        lax.fori_loop(0, nchunk, second_pass, 0)

    def col(off):
        return pl.BlockSpec((s, LANES), lambda j: (0, off + j))

    return pl.pallas_call(
        body, name="conv_bwd", grid=(nblk,),
        in_specs=[col(0), col(nblk), col(2 * nblk), col(0),
                  pl.BlockSpec((None, CONV_W_ROWS, LANES), lambda j: (j, 0, 0)),
                  pl.BlockSpec((1, LANES), lambda j: (0, j))],
        out_specs=[pl.BlockSpec((3, s, LANES), lambda j: (0, 0, j)),
                   pl.BlockSpec((None, CONV_W_ROWS, LANES), lambda j: (j, 0, 0)),
                   pl.BlockSpec((1, LANES), lambda j: (0, j))],
        out_shape=[_sds((3, s, W_CONV), bf16), _sds((nblk, CONV_W_ROWS, LANES), f32), _sds((1, W_CONV), f32)],
        scratch_shapes=[pltpu.VMEM((s, LANES), f32)],
        compiler_params=_arb(1),
    )(proj, proj, proj, dcn, conv_w, g_conv)


PIECE = W_CONV


def _piece_spans():
    ncol = IN_COLS // N_CHIPS
    spans = []
    for p in range(IN_COLS // PIECE):
        for k in range(N_CHIPS):
            lo, hi = max(p * PIECE, k * ncol), min((p + 1) * PIECE, (k + 1) * ncol)
            if lo < hi:
                spans.append((p, lo - p * PIECE, hi - p * PIECE, k, lo - k * ncol, hi - k * ncol))
    return spans


def _in_proj_bwd(dconv, dq, dk, dv, dh1, x, g_mix, w_in, tm, ex=None):
    s = x.shape[0]
    ncol = IN_COLS // N_CHIPS

    def body(dc_ref, dq_ref, dk_ref, dv_ref, dh_ref, x_ref, g_ref, w_ref, dx_ref, gg_ref):
        i = pl.program_id(0)
        pieces = [dc_ref.at[0], dc_ref.at[1], dc_ref.at[2], dq_ref, dk_ref, dv_ref]
        da = jnp.zeros((tm, D_MODEL), f32)
        for p, plo, phi, k, wlo, whi in _piece_spans():
            da += _dot(pieces[p][:, plo:phi], w_ref[k, :, wlo:whi], NT)
        dxn, gg = _rms_bwd(da, x_ref[...], g_ref[...])
        dx_ref[...] = dh_ref[...] + dxn

        @pl.when(i == 0)
        def _():
            gg_ref[...] = jnp.zeros_like(gg_ref)

        gg_ref[...] += gg

    return _call_carrying(
        ex, body, "in_proj_bwd", (s // tm,),
        in_specs=[pl.BlockSpec((3, tm, PIECE), lambda i: (0, i, 0)),
                  pl.BlockSpec((tm, PIECE), lambda i: (i, 0)),
                  pl.BlockSpec((tm, PIECE), lambda i: (i, 0)),
                  pl.BlockSpec((tm, PIECE), lambda i: (i, 0)),
                  pl.BlockSpec((tm, D_MODEL), lambda i: (i, 0)),
                  pl.BlockSpec((tm, D_MODEL), lambda i: (i, 0)),
                  pl.BlockSpec((1, D_MODEL), lambda i: (0, 0)),
                  pl.BlockSpec((N_CHIPS, D_MODEL, ncol), lambda i: (0, 0, 0))],
        out_specs=[pl.BlockSpec((tm, D_MODEL), lambda i: (i, 0)),
                   pl.BlockSpec((1, D_MODEL), lambda i: (0, 0))],
        out_shape=[_sds((s, D_MODEL), f32), _sds((1, D_MODEL), f32)],
        scratch_shapes=[],
        args=(dconv, dq, dk, dv, dh1, x, g_mix, w_in))


def _grad_w_in(a, dconv, dq, dk, dv, ts):
    s = a.shape[0]
    ncol = IN_COLS // N_CHIPS

    def body(a_ref, dc_ref, dq_ref, dk_ref, dv_ref, o_ref):
        @pl.when(pl.program_id(0) == 0)
        def _():
            o_ref[...] = jnp.zeros_like(o_ref)

        pieces = [dc_ref.at[0], dc_ref.at[1], dc_ref.at[2], dq_ref, dk_ref, dv_ref]
        av = a_ref[...]
        for p, plo, phi, k, wlo, whi in _piece_spans():
            o_ref[k, :, wlo:whi] += _dot(av, pieces[p][:, plo:phi], TN)

    tok = pl.BlockSpec((ts, PIECE), lambda i: (i, 0))
    return pl.pallas_call(
        body, name="grad_w_in", grid=(s // ts,),
        in_specs=[pl.BlockSpec((ts, D_MODEL), lambda i: (i, 0)),
                  pl.BlockSpec((3, ts, PIECE), lambda i: (0, i, 0)), tok, tok, tok],
        out_specs=pl.BlockSpec((N_CHIPS, D_MODEL, ncol), lambda i: (0, 0, 0)),
        out_shape=_sds((N_CHIPS, D_MODEL, ncol), f32),
        compiler_params=_arb(1),
    )(a, dconv, dq, dk, dv)


def _weight_grad(a, b, bm, bn, ts, name, relu_sq=False):
    s, m = a.shape
    n = b.shape[1]
    nn = n // bn
    nk = s // ts

    def body(a_ref, b_ref, o_ref, ob_ref):
        @pl.when(pl.program_id(2) == 0)
        def _():
            o_ref[...] = jnp.zeros_like(o_ref)

        av = a_ref[...]
        if relu_sq:
            av = jnp.square(jnp.maximum(av.astype(f32), 0.0)).astype(bf16)
        o_ref[...] += _dot(av, b_ref[...], TN)

        @pl.when(pl.program_id(2) == nk - 1)
        def _():
            ob_ref[...] = o_ref[...].astype(bf16)

    tile = pl.BlockSpec((None, bm, bn), lambda i, j, k: (i * nn + j, 0, 0))
    return pl.pallas_call(
        body, name=name, grid=(m // bm, nn, nk),
        in_specs=[pl.BlockSpec((ts, bm), lambda i, j, k: (k, i)),
                  pl.BlockSpec((ts, bn), lambda i, j, k: (k, j))],
        out_specs=[tile, tile],
        out_shape=[_sds(((m // bm) * nn, bm, bn), f32), _sds(((m // bm) * nn, bm, bn), bf16)],
        compiler_params=_arb(3),
    )(a, b)


def kernel(x, p, g_mix, w_in, conv_w, g_conv_out, g_attn_out, w_out, g_mlp, w_up, w_down, g_ple, w_ple_gate, w_ple_proj, g_final, loss_target, m_g_mix, m_w_in, m_conv_w, m_g_conv_out, m_g_attn_out, m_w_out, m_g_mlp, m_w_up, m_w_down, m_g_ple, m_w_ple_gate, m_w_ple_proj, m_g_final, v_g_mix, v_w_in, v_conv_w, v_g_conv_out, v_g_attn_out, v_w_out, v_g_mlp, v_w_up, v_w_down, v_g_ple, v_w_ple_gate, v_w_ple_proj, v_g_final):
    s = x.shape[1]
    tm = min(TOKEN_TILE, s)
    tg = min(GRAD_TOKEN_TILE, s)
    xs = x.reshape(s, D_MODEL)
    ps = p.reshape(s, PLE_DIM)
    target = loss_target.reshape(s, D_MODEL)
    core = lax.axis_index("c").astype(jnp.int32).reshape(1)
    chip = 2 * lax.axis_index("x") + lax.axis_index("y")

    big = {"w_in": w_in[0], "w_out": w_out[0], "w_up": w_up[0], "w_down": w_down[0],
           "w_ple_gate": w_ple_gate[0], "w_ple_proj": w_ple_proj[0]}
    names = list(big)
    conv_shard = jnp.pad(conv_w[0], ((0, CONV_W_ROWS - conv_w.shape[1]), (0, 0)))
    later_names = names[1:]
    w_in_f, conv_f = _run_exchange(_gather_exchange([big["w_in"].astype(bf16), conv_shard]), "gather_w_in")

    proj, a_b, kp, vp, kt, vt = _in_proj(xs, g_mix, w_in_f, tm)
    conv_n = _conv_fwd(proj, conv_f, g_conv_out)
    (o, tot), gathered = _attn_fwd(proj, kp, vt, _gather_exchange([big[k].astype(bf16) for k in later_names]))
    w_out_f, w_up_f, w_down_f, w_gate_f, w_proj_f = gathered
    w_out_f = w_out_f.reshape(D_MODEL, D_MODEL)
    w_gate_f = w_gate_f.reshape(D_MODEL, D_MODEL)
    h1, cat_b = _out_proj(o, conv_n, xs, g_attn_out, w_out_f, tm)
    h2, u_b, m_b = _mlp_fwd(h1, g_mlp, w_up_f, w_down_f, min(MLP_TOKEN_TILE, s))

    dh2, dh2_b, n3_b, dgl_b, dpp_b, p_b, gg_ple, gg_final, loss_row = _tail(
        h2, ps, target, g_ple, g_final.reshape(1, D_MODEL), w_gate_f, w_proj_f, tm)
    dh1, dh1_b, du_b, gg_mlp = _mlp_bwd(dh2, dh2_b, h1, u_b, g_mlp, w_up_f, w_down_f, tm)
    both_kinds = {
        "w_out": _weight_grad(cat_b, dh1_b, D_MODEL, D_MODEL, tg, "grad_w_out"),
        "w_up": _weight_grad(m_b, du_b, D_MODEL, D_FF // N_CHIPS, tg, "grad_w_up"),
        "w_down": _weight_grad(u_b, dh2_b, D_FF // N_CHIPS, D_MODEL, tg, "grad_w_down", relu_sq=True),
        "w_ple_gate": _weight_grad(n3_b, dgl_b, D_MODEL, D_MODEL, tg, "grad_w_ple_gate"),
        "w_ple_proj": _weight_grad(p_b, dpp_b, PLE_DIM, D_MODEL // N_CHIPS, tg, "grad_w_ple_proj"),
    }
    by_chip = lambda k, g: g.reshape((N_CHIPS, big[k].shape[0], big[k].shape[1]))
    part = {k: by_chip(k, g32) for k, (g32, _) in both_kinds.items()}
    part_b = {k: by_chip(k, g16) for k, (_, g16) in both_kinds.items()}
    (dcn, do, gg_attn), from_sibling = _out_proj_bwd(
        dh1_b, o, g_attn_out, w_out_f, tm, _pair_exchange([part_b[k] for k in later_names]))
    pair = [_pair_sum(part[k], r, core, "pair_sum_" + k) for k, r in zip(later_names, from_sibling)]
    dconv, g_conv_w, gg_conv = _conv_bwd(proj, dcn, conv_f, g_conv_out)
    (dq, dk, dv), from_chips = _attn_bwd(proj, kp, vp, kt, do, tot, _chip_exchange(pair))
    dk, dv = dk.reshape(s, W_ATTN), dv.reshape(s, W_ATTN)

    part["w_in"] = _grad_w_in(a_b, dconv, dq, dk, dv, min(MLP_TOKEN_TILE, s))
    in_sibling = _run_exchange(_pair_exchange([part["w_in"]]), "grad_pair_exchange_w_in")
    in_pair = _pair_sum(part["w_in"], in_sibling[0], core, "pair_sum_w_in")
    (grad_x, gg_mix), in_chips = _in_proj_bwd(
        dconv, dq, dk, dv, dh1, xs, g_mix, w_in_f, tm, _chip_exchange([in_pair]))

    place = jnp.stack([lax.axis_index("c"), chip]).astype(jnp.int32)
    half = [_chip_sum(mine, landed, place, "chip_sum_" + k)
            for k, mine, landed in zip(names, [in_pair] + pair, list(in_chips) + list(from_chips))]
    both = _sibling_exchange(half)
    grad = {k: b.reshape(big[k].shape) for k, b in zip(names, both)}

    gcw = g_conv_w[:, :3, :].transpose(1, 0, 2).reshape(3, W_CONV)
    row = lambda *parts: jnp.concatenate(parts, axis=1)
    packed = jnp.concatenate([
        gg_mix, gg_mlp, gg_ple, gg_final, row(gg_conv, gg_attn), row(gcw[0:1], gcw[1:2]),
        row(gcw[2:3], loss_row, jnp.zeros((1, W_CONV - LANES), f32)), jnp.zeros((1, D_MODEL), f32)], axis=0)
    summed = _allreduce_small(packed)
    loss = summed[6, W_CONV]
    gcw_full = jnp.stack([summed[5, :W_CONV], summed[5, W_CONV:], summed[6, :W_CONV]])
    grad["conv_w"] = lax.dynamic_slice(gcw_full, (0, chip * LANES), (3, LANES))
    vec_names = ["g_mix", "g_mlp", "g_ple", "g_final"]
    vec_w = {"g_mix": g_mix, "g_mlp": g_mlp, "g_ple": g_ple, "g_final": g_final.reshape(1, D_MODEL)}
    vec_m = {"g_mix": m_g_mix, "g_mlp": m_g_mlp, "g_ple": m_g_ple, "g_final": m_g_final.reshape(1, D_MODEL)}
    vec_v = {"g_mix": v_g_mix, "g_mlp": v_g_mlp, "g_ple": v_g_ple, "g_final": v_g_final.reshape(1, D_MODEL)}

    def pack_vec(d, conv, attn):
        return jnp.concatenate([d[k] for k in vec_names] + [row(conv, attn)], axis=0)

    vec_g = summed[0:5]
    vec_d, vec_nm, vec_nv, vec_g = _adamw(vec_g, pack_vec(vec_w, g_conv_out, g_attn_out),
                                   pack_vec(vec_m, m_g_conv_out, m_g_attn_out),
                                   pack_vec(vec_v, v_g_conv_out, v_g_attn_out), "adamw_vectors")

    given_w = dict(big, conv_w=conv_w[0])
    given_m = {"w_in": m_w_in[0], "w_out": m_w_out[0], "w_up": m_w_up[0], "w_down": m_w_down[0],
               "w_ple_gate": m_w_ple_gate[0], "w_ple_proj": m_w_ple_proj[0], "conv_w": m_conv_w[0]}
    given_v = {"w_in": v_w_in[0], "w_out": v_w_out[0], "w_up": v_w_up[0], "w_down": v_w_down[0],
               "w_ple_gate": v_w_ple_gate[0], "w_ple_proj": v_w_ple_proj[0], "conv_w": v_conv_w[0]}
    delta, new_m, new_v = {}, {}, {}
    for k in names + ["conv_w"]:
        delta[k], new_m[k], new_v[k], grad[k] = _adamw(grad[k], given_w[k], given_m[k], given_v[k], "adamw_" + k)

    def unpack(vals, kind):
        out = {k: vals[i:i + 1] for i, k in enumerate(vec_names)}
        out["g_final"] = out["g_final"].reshape(D_MODEL)
        out["g_conv_out"] = vals[4:5, :W_CONV]
        out["g_attn_out"] = vals[4:5, W_CONV:]
        out.update({k: v[None] for k, v in kind.items()})
        return out

    order = ["g_mix", "w_in", "conv_w", "g_conv_out", "g_attn_out", "w_out", "g_mlp", "w_up", "w_down",
             "g_ple", "w_ple_gate", "w_ple_proj", "g_final"]
    groups = [unpack(vec_g, grad), unpack(vec_d, delta), unpack(vec_nm, new_m), unpack(vec_nv, new_v)]
    return (loss, grad_x[None]) + tuple(g[k] for g in groups for k in order)
```

```python
import functools

import jax
import jax.numpy as jnp
from jax import lax
from jax.experimental import pallas as pl
from jax.experimental.pallas import tpu as pltpu

f32 = jnp.float32
bf16 = jnp.bfloat16

D_MODEL = 1024
HEAD_DIM = 64
W_CONV = 512
W_ATTN = 512
D_FF = 4096
PLE_DIM = 256
IN_COLS = 3 * W_CONV + 3 * W_ATTN
N_CHIPS = 4
EPS = 1e-6
ADAM_LR = 0.001
ADAM_B1 = 0.9
ADAM_B2 = 0.999
ADAM_EPS = 1e-08
ADAM_WD = 0.01
ADAM_STEP = 10

LANES = 128
TOKEN_TILE = 512
MLP_TOKEN_TILE = 1024
GRAD_TOKEN_TILE = 4096
ATTN_TILE = 256
CONV_CHUNK = 512
CONV_W_ROWS = 16

MESH = pl.DeviceIdType.MESH
ANY = pl.BlockSpec(memory_space=pl.ANY)
NT = (((1,), (1,)), ((), ()))
TN = (((0,), (0,)), ((), ()))


def _arb(n):
    return pltpu.CompilerParams(dimension_semantics=("arbitrary",) * n)


def _sds(shape, dtype):
    return jax.ShapeDtypeStruct(shape, dtype)


def _dot(a, b, dims=None):
    if dims is None:
        return jnp.dot(a, b, preferred_element_type=f32)
    return lax.dot_general(a, b, dims, preferred_element_type=f32)


def _split_dot(x, ones):
    hi = x.astype(bf16)
    lo = (x - hi.astype(f32)).astype(bf16)
    return _dot(hi, ones) + _dot(lo, ones)


def _rms_fwd(h, g):
    rstd = lax.rsqrt(jnp.mean(h * h, axis=-1, keepdims=True) + EPS)
    return h * rstd * g, rstd


def _rms_bwd(dy, h, g):
    rstd = lax.rsqrt(jnp.mean(h * h, axis=-1, keepdims=True) + EPS)
    hn = h * rstd
    dyg = dy * g
    dh = rstd * (dyg - hn * jnp.mean(dyg * hn, axis=-1, keepdims=True))
    return dh, jnp.sum(dy * hn, axis=0, keepdims=True)


def _group_ones(n):
    r = lax.broadcasted_iota(jnp.int32, (n, n), 0) // HEAD_DIM
    c = lax.broadcasted_iota(jnp.int32, (n, n), 1) // HEAD_DIM
    return (r == c).astype(bf16)


def _head_rms_fwd(y, g, ones):
    rstd = lax.rsqrt(_split_dot(y * y, ones) * (1.0 / HEAD_DIM) + EPS)
    return y * rstd * g


def _head_rms_bwd(dy, y, g, ones):
    rstd = lax.rsqrt(_split_dot(y * y, ones) * (1.0 / HEAD_DIM) + EPS)
    yn = y * rstd
    dyg = dy * g
    dyy = rstd * (dyg - yn * (_split_dot(dyg * yn, ones) * (1.0 / HEAD_DIM)))
    return dyy, jnp.sum(dy * yn, axis=0, keepdims=True)


def _place():
    return lax.axis_index("x"), lax.axis_index("y"), lax.axis_index("c")


def _other_chips(x, y):
    return [(1 - x, y), (x, 1 - y), (1 - x, 1 - y)]


class _Exchange:
    def __init__(self, arrays, out_shapes, sems, start, finish, relay=None):
        self.arrays, self.out_shapes, self.sems, self.start, self.finish = arrays, out_shapes, sems, start, finish
        self.relay = relay


def _gather_exchange(shards):
    n = len(shards)
    halves = [s.shape[0] // 2 for s in shards]

    def plan(ins, outs, sems):
        send_sems, recv_sems, own_sems = sems
        x, y, c = _place()
        me = 2 * x + y
        chips = _other_chips(x, y)

        def half(ref, i, which):
            return ref.at[pl.ds(which * halves[i], halves[i]), :]

        def over_ici(i, j, src, slot, to):
            return pltpu.make_async_remote_copy(
                src_ref=src, dst_ref=half(outs[i].at[slot], i, c),
                send_sem=send_sems.at[3 * i + j], recv_sem=recv_sems.at[3 * i + j],
                device_id=to, device_id_type=MESH)

        def to_sibling(i, j, slot, which):
            blk = half(outs[i].at[slot], i, which)
            return pltpu.make_async_remote_copy(
                src_ref=blk, dst_ref=blk,
                send_sem=send_sems.at[3 * n + 3 * i + j], recv_sem=recv_sems.at[3 * n + 3 * i + j],
                device_id=(x, y, 1 - c), device_id_type=MESH)

        own = [pltpu.make_async_remote_copy(
            src_ref=ins[i], dst_ref=outs[i].at[me], send_sem=own_sems.at[i], recv_sem=own_sems.at[n + i],
            device_id=(x, y, 1 - c), device_id_type=MESH) for i in range(n)]
        pairs = [(i, j, px, py) for i in range(n) for j, (px, py) in enumerate(chips)]
        sends = [over_ici(i, j, half(ins[i], i, c), me, (px, py, c)) for i, j, px, py in pairs]
        lands = [over_ici(i, j, half(outs[i].at[2 * px + py], i, c), 2 * px + py, (px, py, c)) for i, j, px, py in pairs]
        passes = [to_sibling(i, j, 2 * px + py, c) for i, j, px, py in pairs]
        from_sibling = [to_sibling(i, j, 2 * px + py, 1 - c) for i, j, px, py in pairs]
        return own, sends, lands, passes, from_sibling

    def start(ins, outs, sems):
        own, sends, _, _, _ = plan(ins, outs, sems)
        for cp in own + sends:
            cp.start()

    def relay(ins, outs, sems):
        _, _, lands, passes, _ = plan(ins, outs, sems)
        for land, on in zip(lands, passes):
            land.wait_recv()
            on.start()

    def finish(ins, outs, sems):
        own, sends, _, passes, from_sibling = plan(ins, outs, sems)
        for cp in from_sibling:
            cp.wait_recv()
        for cp in sends + passes:
            cp.wait_send()
        for cp in own:
            cp.wait()

    return _Exchange(
        shards, [_sds((N_CHIPS,) + s.shape, s.dtype) for s in shards],
        [pltpu.SemaphoreType.DMA((6 * n,)), pltpu.SemaphoreType.DMA((6 * n,)), pltpu.SemaphoreType.DMA((2 * n,))],
        start, finish, relay)


def _call_carrying(ex, body, name, grid, in_specs, out_specs, out_shape, scratch_shapes, args):
    n_in, n_out, n_scr = len(in_specs), len(out_specs), len(scratch_shapes)
    k = 0 if ex is None else len(ex.arrays)

    def wrapped(*refs):
        ins, xin = refs[:n_in], refs[n_in:n_in + k]
        outs, xout = refs[n_in + k:n_in + k + n_out], refs[n_in + k + n_out:n_in + 2 * k + n_out]
        scr, sems = refs[n_in + 2 * k + n_out:n_in + 2 * k + n_out + n_scr], refs[n_in + 2 * k + n_out + n_scr:]
        ids = [pl.program_id(d) for d in range(len(grid))]
        if ex is not None:
            @pl.when(functools.reduce(lambda a, b: a & b, [i == 0 for i in ids]))
            def _():
                ex.start(xin, xout, sems)

        if ex is not None and ex.relay is not None:
            relay_at = [grid[0] - 1] + [0] * (len(grid) - 1) if len(grid) > 1 else [grid[0] - 1]

            @pl.when(functools.reduce(lambda a, b: a & b, [i == r for i, r in zip(ids, relay_at)]))
            def _():
                ex.relay(xin, xout, sems)

        body(*ins, *outs, *scr)
        if ex is not None:
            @pl.when(functools.reduce(lambda a, b: a & b, [i == g - 1 for i, g in zip(ids, grid)]))
            def _():
                ex.finish(xin, xout, sems)

    res = pl.pallas_call(
        wrapped, name=name, grid=grid,
        in_specs=list(in_specs) + [ANY] * k, out_specs=list(out_specs) + [ANY] * k,
        out_shape=list(out_shape) + ([] if ex is None else list(ex.out_shapes)),
        scratch_shapes=list(scratch_shapes) + ([] if ex is None else list(ex.sems)),
        compiler_params=_arb(len(grid)),
    )(*args, *([] if ex is None else ex.arrays))
    return res[:n_out], res[n_out:]


def _run_exchange(ex, name):
    n = len(ex.arrays)

    def body(*refs):
        ins, outs, sems = refs[:n], refs[n:2 * n], refs[2 * n:]
        ex.start(ins, outs, sems)
        if ex.relay is not None:
            ex.relay(ins, outs, sems)
        ex.finish(ins, outs, sems)

    return pl.pallas_call(
        body, name=name, out_shape=ex.out_shapes, in_specs=[ANY] * n, out_specs=[ANY] * n,
        scratch_shapes=ex.sems,
    )(*ex.arrays)


def _pair_exchange(grads):
    n = len(grads)

    def plan(ins, outs, sems):
        send_sems, recv_sems = sems
        x, y, c = _place()
        return [pltpu.make_async_remote_copy(
            src_ref=ins[i].at[:, 1 - c], dst_ref=outs[i],
            send_sem=send_sems.at[i], recv_sem=recv_sems.at[i],
            device_id=(x, y, 1 - c), device_id_type=MESH) for i in range(n)]

    def start(ins, outs, sems):
        for cp in plan(ins, outs, sems):
            cp.start()

    def finish(ins, outs, sems):
        for cp in plan(ins, outs, sems):
            cp.wait()

    views = [g.reshape(N_CHIPS, 2, g.shape[1] // 2, g.shape[2]) for g in grads]
    return _Exchange(
        views, [_sds((N_CHIPS, v.shape[2], v.shape[3]), v.dtype) for v in views],
        [pltpu.SemaphoreType.DMA((n,)), pltpu.SemaphoreType.DMA((n,))], start, finish)


def _chip_exchange(parts):
    n = len(parts)

    def plan(ins, outs, sems):
        send_sems, recv_sems = sems
        x, y, c = _place()
        me = 2 * x + y
        pairs = [(i, j, px, py) for i in range(n) for j, (px, py) in enumerate(_other_chips(x, y))]

        def copy(i, j, src, slot, px, py):
            return pltpu.make_async_remote_copy(
                src_ref=src, dst_ref=outs[i].at[slot],
                send_sem=send_sems.at[3 * i + j], recv_sem=recv_sems.at[3 * i + j],
                device_id=(px, py, c), device_id_type=MESH)

        sends = [copy(i, j, ins[i].at[2 * px + py], me, px, py) for i, j, px, py in pairs]
        lands = [copy(i, j, outs[i].at[2 * px + py], 2 * px + py, px, py) for i, j, px, py in pairs]
        return sends, lands

    def start(ins, outs, sems):
        sends, _ = plan(ins, outs, sems)
        for cp in sends:
            cp.start()

    def finish(ins, outs, sems):
        sends, lands = plan(ins, outs, sems)
        for cp in lands:
            cp.wait_recv()
        for cp in sends:
            cp.wait_send()

    return _Exchange(
        parts, [_sds(p.shape, p.dtype) for p in parts],
        [pltpu.SemaphoreType.DMA((3 * n,)), pltpu.SemaphoreType.DMA((3 * n,))],
        start, finish)


def _sibling_exchange(both):
    n = len(both)

    def body(*refs):
        outs = refs[n:2 * n]
        send_sems, recv_sems = refs[2 * n:]
        x, y, c = _place()
        sent = []
        for i in range(n):
            cp = pltpu.make_async_remote_copy(
                src_ref=outs[i].at[c], dst_ref=outs[i].at[c],
                send_sem=send_sems.at[i], recv_sem=recv_sems.at[i],
                device_id=(x, y, 1 - c), device_id_type=MESH)
            cp.start()
            sent.append(cp)
        for cp in sent:
            cp.wait()

    return pl.pallas_call(
        body, name="grad_sibling_exchange",
        out_shape=[_sds(b.shape, f32) for b in both],
        in_specs=[ANY] * n, out_specs=[ANY] * n,
        input_output_aliases={i: i for i in range(n)},
        scratch_shapes=[pltpu.SemaphoreType.DMA((n,)), pltpu.SemaphoreType.DMA((n,))],
    )(*both)


def _row_tile(rows, cols):
    t = rows
    while t * cols * 4 > (2 << 20) and t % 16 == 0:
        t //= 2
    return t


def _pair_sum(grad, recv, core, name):
    _, r, c = grad.shape
    hr = r // 2
    tr = _row_tile(hr, c)
    view = grad.reshape(N_CHIPS, 2, hr, c)

    def body(core_ref, mine_ref, recv_ref, out_ref):
        out_ref[...] = (mine_ref[...] + recv_ref[...].astype(f32)).astype(bf16)

    return pl.pallas_call(
        body, name=name,
        grid_spec=pltpu.PrefetchScalarGridSpec(
            num_scalar_prefetch=1, grid=(N_CHIPS, hr // tr),
            in_specs=[pl.BlockSpec((None, None, tr, c), lambda k, t, core_ref: (k, core_ref[0], t, 0)),
                      pl.BlockSpec((None, tr, c), lambda k, t, core_ref: (k, t, 0))],
            out_specs=pl.BlockSpec((None, tr, c), lambda k, t, core_ref: (k, t, 0))),
        out_shape=_sds((N_CHIPS, hr, c), bf16),
        compiler_params=_arb(2),
    )(core, view, recv)


def _chip_sum(mine, landed, place, name):
    _, hr, c = mine.shape
    tr = _row_tile(hr, c)

    def body(place_ref, a_ref, b_ref, c_ref, d_ref, out_ref):
        out_ref[...] = ((a_ref[...].astype(f32) + b_ref[...].astype(f32)) + c_ref[...].astype(f32)) + d_ref[...].astype(f32)

    def slot(k):
        return pl.BlockSpec((None, tr, c), lambda t, place_ref: ((place_ref[1] + k) % N_CHIPS, t, 0))

    return pl.pallas_call(
        body, name=name,
        grid_spec=pltpu.PrefetchScalarGridSpec(
            num_scalar_prefetch=1, grid=(hr // tr,),
            in_specs=[slot(0), slot(1), slot(2), slot(3)],
            out_specs=pl.BlockSpec((None, tr, c), lambda t, place_ref: (place_ref[0], t, 0))),
        out_shape=_sds((2, hr, c), f32), compiler_params=_arb(1),
    )(place, mine, landed, landed, landed)


def _adamw(g, w, m, v, name):
    r, c = g.shape
    tr = _row_tile(r, c)

    def body(g_ref, w_ref, m_ref, v_ref, d_ref, nm_ref, nv_ref, go_ref):
        gv = g_ref[...]
        go_ref[...] = gv
        mv = ADAM_B1 * m_ref[...] + (1.0 - ADAM_B1) * gv
        vv = ADAM_B2 * v_ref[...] + (1.0 - ADAM_B2) * jnp.square(gv)
        m_hat = mv / (1.0 - ADAM_B1 ** ADAM_STEP)
        v_hat = vv / (1.0 - ADAM_B2 ** ADAM_STEP)
        d_ref[...] = -ADAM_LR * (m_hat / (jnp.sqrt(v_hat) + ADAM_EPS) + ADAM_WD * w_ref[...])
        nm_ref[...] = mv
        nv_ref[...] = vv

    spec = pl.BlockSpec((tr, c), lambda t: (t, 0))
    return pl.pallas_call(
        body, name=name, grid=(r // tr,), in_specs=[spec] * 4, out_specs=[spec] * 4,
        out_shape=[_sds((r, c), f32)] * 4, compiler_params=_arb(1),
    )(g, w, m, v)


def _allreduce_small(packed):
    shape = packed.shape

    def body(x_ref, out_ref, buf, send_sems, recv_sems):
        x, y, c = _place()
        me = 4 * x + 2 * y + c
        buf[me] = x_ref[...]
        sent = []
        for r in range(1, 8):
            dx, dy, dc = (r >> 2) & 1, (r >> 1) & 1, r & 1
            peer = ((1 - x) if dx else x, (1 - y) if dy else y, (1 - c) if dc else c)
            cp = pltpu.make_async_remote_copy(
                src_ref=x_ref, dst_ref=buf.at[me],
                send_sem=send_sems.at[r], recv_sem=recv_sems.at[r],
                device_id=peer, device_id_type=MESH)
            cp.start()
            sent.append((cp, peer))
        for r, (cp, peer) in enumerate(sent, start=1):
            src = 4 * peer[0] + 2 * peer[1] + peer[2]
            pltpu.make_async_remote_copy(
                src_ref=x_ref, dst_ref=buf.at[src],
                send_sem=send_sems.at[r], recv_sem=recv_sems.at[r],
                device_id=peer, device_id_type=MESH).wait_recv()
        for cp, _ in sent:
            cp.wait_send()
        total = buf[0]
        for k in range(1, 8):
            total = total + buf[k]
        out_ref[...] = total

    vmem = pl.BlockSpec(memory_space=pltpu.VMEM)
    return pl.pallas_call(
        body, name="allreduce_small", out_shape=_sds(shape, f32),
        in_specs=[vmem], out_specs=vmem,
        scratch_shapes=[pltpu.VMEM((8,) + shape, f32), pltpu.SemaphoreType.DMA((8,)),
                        pltpu.SemaphoreType.DMA((8,))],
    )(packed)


def _in_proj(x, g_mix, w_in, tm):
    s = x.shape[0]
    ncol = IN_COLS // N_CHIPS
    t = ATTN_TILE
    nb = tm // t
    koff = 3 * W_CONV + W_ATTN

    def body(x_ref, g_ref, w_ref, proj_ref, a_ref, kp_ref, vp_ref, kt_ref, vt_ref, perm, kv):
        a, _ = _rms_fwd(x_ref[...], g_ref[...])
        ab = a.astype(bf16)
        a_ref[...] = ab
        for k in range(N_CHIPS):
            proj_ref[:, k * ncol:(k + 1) * ncol] = _dot(ab, w_ref[k])
        for j in range(2 * W_ATTN // LANES):
            kv[j] = proj_ref[:, koff + j * LANES:koff + (j + 1) * LANES]
        for b in range(nb):
            for r in range(KEY_RUN):
                for j in range(2 * W_ATTN // LANES):
                    perm[SUBLANES * r:SUBLANES * (r + 1), j * LANES:(j + 1) * LANES] = kv[
                        j, pl.ds(b * t + r, SUBLANES, stride=KEY_RUN), :]
            kp_ref[b] = perm[:, :W_ATTN].astype(bf16)
            vp_ref[b] = perm[:, W_ATTN:].astype(bf16)
            kt_ref[b] = perm[:, :W_ATTN].T.astype(bf16)
            vt_ref[b] = perm[:, W_ATTN:].T.astype(bf16)

    keys = pl.BlockSpec((nb, t, W_ATTN), lambda i: (i, 0, 0))
    keys_t = pl.BlockSpec((nb, W_ATTN, t), lambda i: (i, 0, 0))
    return pl.pallas_call(
        body, name="in_proj", grid=(s // tm,),
        in_specs=[pl.BlockSpec((tm, D_MODEL), lambda i: (i, 0)),
                  pl.BlockSpec((1, D_MODEL), lambda i: (0, 0)),
                  pl.BlockSpec((N_CHIPS, D_MODEL, ncol), lambda i: (0, 0, 0))],
        out_specs=[pl.BlockSpec((tm, IN_COLS), lambda i: (i, 0)),
                   pl.BlockSpec((tm, D_MODEL), lambda i: (i, 0)), keys, keys, keys_t, keys_t],
        out_shape=[_sds((s, IN_COLS), f32), _sds((s, D_MODEL), bf16),
                   _sds((s // t, t, W_ATTN), bf16), _sds((s // t, t, W_ATTN), bf16),
                   _sds((s // t, W_ATTN, t), bf16), _sds((s // t, W_ATTN, t), bf16)],
        scratch_shapes=[pltpu.VMEM((t, 2 * W_ATTN), f32), pltpu.VMEM((2 * W_ATTN // LANES, tm, LANES), f32)],
        compiler_params=_arb(1),
    )(x, g_mix, w_in)


def _shifted(prev8, cur, shift):
    ext = jnp.concatenate([prev8, cur], axis=0)
    return pltpu.roll(ext, shift, axis=0)[8:]


def _conv_fwd(proj, conv_w, g_conv):
    s = proj.shape[0]
    nblk = W_CONV // LANES
    rc = min(CONV_CHUNK, s)

    def body(cb_ref, cc_ref, cu_ref, w_ref, g_ref, out_ref):
        ones = _group_ones(LANES)
        w0, w1, w2 = w_ref[0:1, :], w_ref[1:2, :], w_ref[2:3, :]
        g = g_ref[...]

        def chunk(i, carry):
            r0 = pl.multiple_of(i * rc, rc)
            rows = pl.ds(r0, rc)
            prev = pl.ds(pl.multiple_of(jnp.maximum(r0 - 8, 0), 8), 8)
            v = cc_ref[rows, :] * cu_ref[rows, :]
            vp = jnp.where(i > 0, cc_ref[prev, :] * cu_ref[prev, :], 0.0)
            y = w2 * v + w1 * _shifted(vp, v, 1) + w0 * _shifted(vp, v, 2)
            out_ref[rows, :] = _head_rms_fwd(cb_ref[rows, :] * y, g, ones).astype(bf16)
            return carry

        lax.fori_loop(0, s // rc, chunk, 0)

    def col(off):
        return pl.BlockSpec((s, LANES), lambda j: (0, off + j))

    return pl.pallas_call(
        body, name="conv_fwd", grid=(nblk,),
        in_specs=[col(0), col(nblk), col(2 * nblk),
                  pl.BlockSpec((None, CONV_W_ROWS, LANES), lambda j: (j, 0, 0)),
                  pl.BlockSpec((1, LANES), lambda j: (0, j))],
        out_specs=pl.BlockSpec((s, LANES), lambda j: (0, j)),
        out_shape=_sds((s, W_CONV), bf16), compiler_params=_arb(1),
    )(proj, proj, proj, conv_w, g_conv)


LOG2_E = 1.4426950408889634


def _log2_keep(z2):
    nz2 = -z2
    return jnp.minimum(nz2, 0.0) - jnp.log2(1.0 + jnp.exp2(jnp.minimum(z2, nz2)))


def _head_pair_masks(rows):
    lane = lax.broadcasted_iota(jnp.int32, (rows, LANES), 1)
    return lane < HEAD_DIM


SUBLANES = 8
KEY_RUN = ATTN_TILE // SUBLANES


def _causal_tiles():
    r = lax.broadcasted_iota(jnp.int32, (ATTN_TILE, ATTN_TILE), 0)
    key = (r % SUBLANES) * KEY_RUN + r // SUBLANES
    below = key < lax.broadcasted_iota(jnp.int32, (ATTN_TILE, ATTN_TILE), 1)
    return below.astype(f32), jnp.where(below, 0.0, -1e30).astype(f32)


def _sublane_scan(x, reverse):
    row = lax.broadcasted_iota(jnp.int32, x.shape, 0)
    inc = x
    for sh in (1, 2, 4):
        if reverse:
            inc = inc + jnp.where(row < SUBLANES - sh, pltpu.roll(inc, SUBLANES - sh, axis=0), 0.0)
        else:
            inc = inc + jnp.where(row >= sh, pltpu.roll(inc, sh, axis=0), 0.0)
    return inc - x


def _attn_fwd(proj, kp, vt, ex=None):
    s = proj.shape[0]
    t = ATTN_TILE
    nblk = s // t
    npair = W_ATTN // LANES
    qoff = 3 * W_CONV // LANES
    keep01, keepneg = _causal_tiles()

    def body(q_ref, k_ref, vt_ref, m01_ref, neg_ref, o_ref, tot_ref, w_s, a_s, acc):
        qb = pl.program_id(1)
        first = _head_pair_masks(t)
        q = q_ref[...] * (HEAD_DIM ** -0.5)
        qh = (jnp.where(first, q, 0.0).astype(bf16), jnp.where(first, 0.0, q).astype(bf16))
        acc[...] = jnp.zeros_like(acc)
        a_s[1] = jnp.zeros((t, t), bf16)

        def scores(kb, h):
            w_s[h] = _dot(k_ref[kb], qh[h], NT)

        def weigh(kb, h):
            acc[h] += _dot(vt_ref[kb], a_s[h])

        def weights(h, diagonal, later):
            run = jnp.zeros((SUBLANES, t), f32)
            for a in reversed(range(KEY_RUN)):
                rows = slice(SUBLANES * a, SUBLANES * (a + 1))
                z2 = w_s[h, rows, :] * LOG2_E
                lk = _log2_keep(z2)
                if diagonal:
                    lk = lk * m01_ref[rows, :]
                run = run + lk
                w_s[h, rows, :] = z2 + run
            off = _sublane_scan(run, reverse=True) + later
            off2 = jnp.concatenate([off, off], axis=0)
            for a in range(t // (2 * SUBLANES)):
                rows = slice(2 * SUBLANES * a, 2 * SUBLANES * (a + 1))
                w = w_s[h, rows, :] + off2
                if diagonal:
                    w = w + neg_ref[rows, :]
                a_s[h, rows, :] = jnp.exp2(w).astype(bf16)
            return later + jnp.sum(run, axis=0, keepdims=True)

        def block(kb, before, after, diagonal, later):
            scores(kb, 1)
            weigh(before, 1)
            l0 = weights(0, diagonal, later[0])
            scores(after, 0)
            weigh(kb, 0)
            l1 = weights(1, diagonal, later[1])
            return l0, l1

        zero = jnp.zeros((1, t), f32)
        scores(qb, 0)
        later = block(qb, qb, jnp.maximum(qb - 1, 0), True, (zero, zero))

        def earlier(i, c):
            kb = qb - 1 - i
            return block(kb, kb + 1, jnp.maximum(kb - 1, 0), False, c)

        later = lax.fori_loop(0, qb, earlier, later)
        weigh(0, 1)
        top = lax.broadcasted_iota(jnp.int32, (LANES, t), 0) < HEAD_DIM
        o_ref[...] = jnp.where(top, acc[0], acc[1]).T
        tot_ref[...] = jnp.concatenate([later[0], later[1], jnp.zeros((SUBLANES - 2, t), f32)], axis=0)

    return _call_carrying(
        ex, body, "attn_fwd", (npair, nblk),
        in_specs=[pl.BlockSpec((t, LANES), lambda p, i: (i, qoff + p)),
                  pl.BlockSpec((nblk, t, LANES), lambda p, i: (0, 0, p)),
                  pl.BlockSpec((nblk, LANES, t), lambda p, i: (0, p, 0)),
                  pl.BlockSpec((t, t), lambda p, i: (0, 0)),
                  pl.BlockSpec((t, t), lambda p, i: (0, 0))],
        out_specs=[pl.BlockSpec((t, LANES), lambda p, i: (i, p)),
                   pl.BlockSpec((None, SUBLANES, t), lambda p, i: (p, 0, i))],
        out_shape=[_sds((s, W_ATTN), f32), _sds((npair, SUBLANES, s), f32)],
        scratch_shapes=[pltpu.VMEM((2, t, t), f32), pltpu.VMEM((2, t, t), bf16), pltpu.VMEM((2, LANES, t), f32)],
        args=(proj, kp, vt, keep01, keepneg))


def _out_proj(o, conv_n, x, g_attn, w_out, tm):
    s = x.shape[0]

    def body(o_ref, c_ref, x_ref, g_ref, w_ref, h_ref, cat_ref):
        ones = _group_ones(LANES)
        cat_ref[:, :W_CONV] = c_ref[...]
        for j in range(W_ATTN // LANES):
            cols = slice(j * LANES, (j + 1) * LANES)
            cat_ref[:, W_CONV + j * LANES:W_CONV + (j + 1) * LANES] = _head_rms_fwd(
                o_ref[:, cols], g_ref[:, cols], ones).astype(bf16)
        h_ref[...] = x_ref[...] + _dot(cat_ref[...], w_ref[...])

    return pl.pallas_call(
        body, name="out_proj", grid=(s // tm,),
        in_specs=[pl.BlockSpec((tm, W_ATTN), lambda i: (i, 0)),
                  pl.BlockSpec((tm, W_CONV), lambda i: (i, 0)),
                  pl.BlockSpec((tm, D_MODEL), lambda i: (i, 0)),
                  pl.BlockSpec((1, W_ATTN), lambda i: (0, 0)),
                  pl.BlockSpec((D_MODEL, D_MODEL), lambda i: (0, 0))],
        out_specs=[pl.BlockSpec((tm, D_MODEL), lambda i: (i, 0)),
                   pl.BlockSpec((tm, D_MODEL), lambda i: (i, 0))],
        out_shape=[_sds((s, D_MODEL), f32), _sds((s, D_MODEL), bf16)],
        compiler_params=_arb(1),
    )(o, conv_n, x, g_attn, w_out)


def _mlp_fwd(h1, g_mlp, w_up, w_down, tm):
    s = h1.shape[0]
    fc = D_FF // N_CHIPS

    def body(h_ref, g_ref, wu_ref, wd_ref, h2_ref, u_ref, m_ref):
        j = pl.program_id(1)

        @pl.when(j == 0)
        def _():
            m, _ = _rms_fwd(h_ref[...], g_ref[...])
            m_ref[...] = m.astype(bf16)
            h2_ref[...] = h_ref[...]

        u = _dot(m_ref[...], wu_ref[...])
        u_ref[...] = u.astype(bf16)
        h2_ref[...] += _dot(jnp.square(jnp.maximum(u, 0.0)).astype(bf16), wd_ref[...])

    return pl.pallas_call(
        body, name="mlp_fwd", grid=(s // tm, N_CHIPS),
        in_specs=[pl.BlockSpec((tm, D_MODEL), lambda i, j: (i, 0)),
                  pl.BlockSpec((1, D_MODEL), lambda i, j: (0, 0)),
                  pl.BlockSpec((None, D_MODEL, fc), lambda i, j: (j, 0, 0)),
                  pl.BlockSpec((None, fc, D_MODEL), lambda i, j: (j, 0, 0))],
        out_specs=[pl.BlockSpec((tm, D_MODEL), lambda i, j: (i, 0)),
                   pl.BlockSpec((tm, fc), lambda i, j: (i, j)),
                   pl.BlockSpec((tm, D_MODEL), lambda i, j: (i, 0))],
        out_shape=[_sds((s, D_MODEL), f32), _sds((s, D_FF), bf16), _sds((s, D_MODEL), bf16)],
        compiler_params=_arb(2),
    )(h1, g_mlp, w_up, w_down)


def _tail(h2, p, target, g_ple, g_final, w_gate, w_proj, tm):
    s = h2.shape[0]
    pc = D_MODEL // N_CHIPS

    def body(h_ref, p_ref, t_ref, gp_ref, gf_ref, wg_ref, wp_ref,
             dh_ref, dhb_ref, n3_ref, dgl_ref, dpp_ref, pb_ref, ggp_ref, ggf_ref, loss_ref, pp_ref):
        i = pl.program_id(0)

        @pl.when(i == 0)
        def _():
            ggp_ref[...] = jnp.zeros_like(ggp_ref)
            ggf_ref[...] = jnp.zeros_like(ggf_ref)
            loss_ref[...] = jnp.zeros_like(loss_ref)

        for half in range(2):
            rows = slice(half * (tm // 2), (half + 1) * (tm // 2))
            h2v = h_ref[rows, :]
            n3, _ = _rms_fwd(h2v, gp_ref[...])
            n3b = n3.astype(bf16)
            n3_ref[rows, :] = n3b
            gate = jax.nn.sigmoid(_dot(n3b, wg_ref[...]))
            pb = p_ref[rows, :].astype(bf16)
            pb_ref[rows, :] = pb
            for k in range(N_CHIPS):
                pp_ref[rows, k * pc:(k + 1) * pc] = _dot(pb, wp_ref[k])
            pp = pp_ref[rows, :]
            h3 = h2v + gate * pp
            yv, _ = _rms_fwd(h3, gf_ref[...])
            err = yv - t_ref[rows, :]
            loss = 0.5 * jnp.sum(err * err) * (1.0 / D_MODEL)
            dh3, ggf = _rms_bwd(err * (1.0 / D_MODEL), h3, gf_ref[...])
            dpp_ref[rows, :] = (dh3 * gate).astype(bf16)
            dgl = (dh3 * pp * gate * (1.0 - gate)).astype(bf16)
            dgl_ref[rows, :] = dgl
            dn3 = _dot(dgl, wg_ref[...], NT)
            dh2n, ggp = _rms_bwd(dn3, h2v, gp_ref[...])
            dh2 = dh3 + dh2n
            dh_ref[rows, :] = dh2
            dhb_ref[rows, :] = dh2.astype(bf16)
            ggp_ref[...] += ggp
            ggf_ref[...] += ggf
            loss_ref[...] += jnp.full(loss_ref.shape, loss, f32)

    tok = lambda w: pl.BlockSpec((tm, w), lambda i: (i, 0))
    vec = lambda w: pl.BlockSpec((1, w), lambda i: (0, 0))
    return pl.pallas_call(
        body, name="tail", grid=(s // tm,),
        in_specs=[tok(D_MODEL), tok(PLE_DIM), tok(D_MODEL), vec(D_MODEL), vec(D_MODEL),
                  pl.BlockSpec((D_MODEL, D_MODEL), lambda i: (0, 0)),
                  pl.BlockSpec((N_CHIPS, PLE_DIM, pc), lambda i: (0, 0, 0))],
        out_specs=[tok(D_MODEL), tok(D_MODEL), tok(D_MODEL), tok(D_MODEL), tok(D_MODEL), tok(PLE_DIM),
                   vec(D_MODEL), vec(D_MODEL), vec(LANES)],
        out_shape=[_sds((s, D_MODEL), f32), _sds((s, D_MODEL), bf16), _sds((s, D_MODEL), bf16),
                   _sds((s, D_MODEL), bf16), _sds((s, D_MODEL), bf16), _sds((s, PLE_DIM), bf16),
                   _sds((1, D_MODEL), f32), _sds((1, D_MODEL), f32), _sds((1, LANES), f32)],
        scratch_shapes=[pltpu.VMEM((tm, D_MODEL), f32)],
        compiler_params=_arb(1),
    )(h2, p, target, g_ple, g_final, w_gate, w_proj)


def _mlp_bwd(dh2, dh2b, h1, u, g_mlp, w_up, w_down, tm):
    s = h1.shape[0]
    fc = D_FF // N_CHIPS

    def body(dh_ref, dhb_ref, h_ref, u_ref, g_ref, wu_ref, wd_ref, dh1_ref, dh1b_ref, du_ref, gg_ref, dm):
        i, j = pl.program_id(0), pl.program_id(1)

        @pl.when(j == 0)
        def _():
            dm[...] = jnp.zeros_like(dm)

        dr = _dot(dhb_ref[...], wd_ref[...], NT)
        du = (dr * (2.0 * jnp.maximum(u_ref[...].astype(f32), 0.0))).astype(bf16)
        du_ref[...] = du
        dm[...] += _dot(du, wu_ref[...], NT)

        @pl.when((i == 0) & (j == 0))
        def _():
            gg_ref[...] = jnp.zeros_like(gg_ref)

        @pl.when(j == N_CHIPS - 1)
        def _():
            dh1n, gg = _rms_bwd(dm[...], h_ref[...], g_ref[...])
            dh1 = dh_ref[...] + dh1n
            dh1_ref[...] = dh1
            dh1b_ref[...] = dh1.astype(bf16)
            gg_ref[...] += gg

    tok = pl.BlockSpec((tm, D_MODEL), lambda i, j: (i, 0))
    ffb = pl.BlockSpec((tm, fc), lambda i, j: (i, j))
    vec = pl.BlockSpec((1, D_MODEL), lambda i, j: (0, 0))
    return pl.pallas_call(
        body, name="mlp_bwd", grid=(s // tm, N_CHIPS),
        in_specs=[tok, tok, tok, ffb, vec,
                  pl.BlockSpec((None, D_MODEL, fc), lambda i, j: (j, 0, 0)),
                  pl.BlockSpec((None, fc, D_MODEL), lambda i, j: (j, 0, 0))],
        out_specs=[tok, tok, ffb, vec],
        out_shape=[_sds((s, D_MODEL), f32), _sds((s, D_MODEL), bf16), _sds((s, D_FF), bf16),
                   _sds((1, D_MODEL), f32)],
        scratch_shapes=[pltpu.VMEM((tm, D_MODEL), f32)],
        compiler_params=_arb(2),
    )(dh2, dh2b, h1, u, g_mlp, w_up, w_down)


def _out_proj_bwd(dh1b, o, g_attn, w_out, tm, ex=None):
    s = o.shape[0]

    def body(dh_ref, o_ref, g_ref, w_ref, dc_ref, do_ref, gg_ref, dcat):
        i = pl.program_id(0)
        ones = _group_ones(LANES)
        dcat[...] = _dot(dh_ref[...], w_ref[...], NT)
        dc_ref[...] = dcat[:, :W_CONV]

        @pl.when(i == 0)
        def _():
            gg_ref[...] = jnp.zeros_like(gg_ref)

        for j in range(W_ATTN // LANES):
            cols = slice(j * LANES, (j + 1) * LANES)
            d, gg = _head_rms_bwd(dcat[:, W_CONV + j * LANES:W_CONV + (j + 1) * LANES],
                                  o_ref[:, cols], g_ref[:, cols], ones)
            do_ref[:, cols] = d
            gg_ref[:, cols] += gg

    return _call_carrying(
        ex, body, "out_proj_bwd", (s // tm,),
        in_specs=[pl.BlockSpec((tm, D_MODEL), lambda i: (i, 0)),
                  pl.BlockSpec((tm, W_ATTN), lambda i: (i, 0)),
                  pl.BlockSpec((1, W_ATTN), lambda i: (0, 0)),
                  pl.BlockSpec((D_MODEL, D_MODEL), lambda i: (0, 0))],
        out_specs=[pl.BlockSpec((tm, W_CONV), lambda i: (i, 0)),
                   pl.BlockSpec((tm, W_ATTN), lambda i: (i, 0)),
                   pl.BlockSpec((1, W_ATTN), lambda i: (0, 0))],
        out_shape=[_sds((s, W_CONV), f32), _sds((s, W_ATTN), f32), _sds((1, W_ATTN), f32)],
        scratch_shapes=[pltpu.VMEM((tm, D_MODEL), f32)],
        args=(dh1b, o, g_attn, w_out))


def _attn_bwd(proj, kp, vp, kt, do, tot, ex=None):
    s = proj.shape[0]
    t = ATTN_TILE
    nblk = s // t
    npair = W_ATTN // LANES
    qoff = 3 * W_CONV // LANES
    keep01, keepneg = _causal_tiles()

    def body(q_ref, k_ref, v_ref, kt_ref, do_ref, tot_ref, m01_ref, neg_ref, dq_ref, dk_ref, dv_ref,
             w_s, da_s, b_s, g_s, a_s, dz_s, dq_acc, dk_acc, dv_acc):
        qb = pl.program_id(1)
        first = _head_pair_masks(t)
        q = q_ref[...] * (HEAD_DIM ** -0.5)
        qh = (jnp.where(first, q, 0.0).astype(bf16), jnp.where(first, 0.0, q).astype(bf16))
        dov = do_ref[...]
        doh = (jnp.where(first, dov, 0.0).astype(bf16), jnp.where(first, 0.0, dov).astype(bf16))
        total = (tot_ref[0:1, :], tot_ref[1:2, :])

        @pl.when(qb == 0)
        def _():
            dk_acc[...] = jnp.zeros_like(dk_acc)
            dv_acc[...] = jnp.zeros_like(dv_acc)

        dq_acc[...] = jnp.zeros_like(dq_acc)
        a_s[1] = jnp.zeros((t, t), bf16)
        dz_s[1] = jnp.zeros((t, t), bf16)

        def scores(kb, h):
            w_s[h] = _dot(k_ref[kb], qh[h], NT)
            da_s[h] = _dot(v_ref[kb], doh[h], NT)

        def spread(kb, h):
            dq_acc[h] += _dot(kt_ref[kb], dz_s[h])
            dk_acc[kb] += _dot(dz_s[h], qh[h])
            dv_acc[kb] += _dot(a_s[h], doh[h])

        def grads(h, diagonal, lk_before, g_before):
            run = jnp.zeros((SUBLANES, t), f32)
            for a in range(KEY_RUN):
                rows = slice(SUBLANES * a, SUBLANES * (a + 1))
                z2 = w_s[h, rows, :] * LOG2_E
                lk = _log2_keep(z2)
                if diagonal:
                    lk = lk * m01_ref[rows, :]
                log_beta = jnp.minimum(z2 + lk, 0.0)
                run = run + lk
                b_s[h, rows, :] = jnp.exp2(log_beta)
                w_s[h, rows, :] = log_beta - run
            off = total[h] - lk_before - _sublane_scan(run, reverse=False)
            lk_sum = jnp.sum(run, axis=0, keepdims=True)
            run = jnp.zeros((SUBLANES, t), f32)
            for a in range(KEY_RUN // 2):
                parts = []
                for r in (slice(2 * SUBLANES * a, 2 * SUBLANES * a + SUBLANES),
                          slice(2 * SUBLANES * a + SUBLANES, 2 * SUBLANES * (a + 1))):
                    w = w_s[h, r, :] + off
                    if diagonal:
                        w = w + neg_ref[r, :]
                    av = jnp.exp2(w)
                    g = av * da_s[h, r, :]
                    run = run + g
                    da_s[h, r, :] = g
                    g_s[h, r, :] = run
                    parts.append(av)
                a_s[h, 2 * SUBLANES * a:2 * SUBLANES * (a + 1), :] = jnp.concatenate(parts, axis=0).astype(bf16)
            goff = g_before + _sublane_scan(run, reverse=False)
            goff2 = jnp.concatenate([goff, goff], axis=0)
            for a in range(KEY_RUN // 2):
                rows = slice(2 * SUBLANES * a, 2 * SUBLANES * (a + 1))
                dz = da_s[h, rows, :] - b_s[h, rows, :] * (g_s[h, rows, :] + goff2)
                if diagonal:
                    dz = dz * m01_ref[rows, :]
                dz_s[h, rows, :] = dz.astype(bf16)
            return lk_before + lk_sum, g_before + jnp.sum(run, axis=0, keepdims=True)

        def block(kb, before, after, diagonal, carry):
            scores(kb, 1)
            spread(before, 1)
            c0 = grads(0, diagonal, carry[0], carry[1])
            if after is not None:
                scores(after, 0)
            spread(kb, 0)
            c1 = grads(1, diagonal, carry[2], carry[3])
            return c0 + c1

        zero = jnp.zeros((1, t), f32)
        scores(0, 0)
        def two_blocks(i, c):
            kb = 2 * i
            c = block(kb, jnp.maximum(kb - 1, 0), kb + 1, False, c)
            return block(kb + 1, kb, kb + 2, False, c)

        carry = lax.fori_loop(0, qb // 2, two_blocks, (zero, zero, zero, zero))
        carry = lax.fori_loop(qb - qb % 2, qb, lambda kb, c: block(kb, jnp.maximum(kb - 1, 0), kb + 1, False, c), carry)
        block(qb, jnp.maximum(qb - 1, 0), None, True, carry)
        spread(qb, 1)
        top = lax.broadcasted_iota(jnp.int32, (LANES, t), 0) < HEAD_DIM
        dq_ref[...] = (jnp.where(top, dq_acc[0], dq_acc[1]).T * (HEAD_DIM ** -0.5)).astype(bf16)

        @pl.when(qb == nblk - 1)
        def _():
            for kb in range(nblk):
                for b in range(SUBLANES):
                    keys_b = slice(KEY_RUN * b, KEY_RUN * (b + 1))
                    dk_ref[kb, keys_b, :] = dk_acc[kb, pl.ds(b, KEY_RUN, stride=SUBLANES), :].astype(bf16)
                    dv_ref[kb, keys_b, :] = dv_acc[kb, pl.ds(b, KEY_RUN, stride=SUBLANES), :].astype(bf16)

    keys = pl.BlockSpec((nblk, t, LANES), lambda p, i: (0, 0, p))
    tile = pl.BlockSpec((t, t), lambda p, i: (0, 0))
    return _call_carrying(
        ex, body, "attn_bwd", (npair, nblk),
        in_specs=[pl.BlockSpec((t, LANES), lambda p, i: (i, qoff + p)),
                  keys, keys,
                  pl.BlockSpec((nblk, LANES, t), lambda p, i: (0, p, 0)),
                  pl.BlockSpec((t, LANES), lambda p, i: (i, p)),
                  pl.BlockSpec((None, SUBLANES, t), lambda p, i: (p, 0, i)),
                  tile, tile],
        out_specs=[pl.BlockSpec((t, LANES), lambda p, i: (i, p)), keys, keys],
        out_shape=[_sds((s, W_ATTN), bf16), _sds((nblk, t, W_ATTN), bf16), _sds((nblk, t, W_ATTN), bf16)],
        scratch_shapes=[pltpu.VMEM((2, t, t), f32), pltpu.VMEM((2, t, t), f32), pltpu.VMEM((2, t, t), f32),
                        pltpu.VMEM((2, t, t), f32), pltpu.VMEM((2, t, t), bf16), pltpu.VMEM((2, t, t), bf16),
                        pltpu.VMEM((2, LANES, t), f32), pltpu.VMEM((nblk, t, LANES), f32),
                        pltpu.VMEM((nblk, t, LANES), f32)],
        args=(proj, kp, vp, kt, do, tot, keep01, keepneg))


def _conv_bwd(proj, dcn, conv_w, g_conv):
    s = proj.shape[0]
    nblk = W_CONV // LANES
    rc = min(CONV_CHUNK, s)
    nchunk = s // rc

    def body(cb_ref, cc_ref, cu_ref, d_ref, w_ref, g_ref, d3_ref, gw_ref, gg_ref, dy_buf):
        ones = _group_ones(LANES)
        w0, w1, w2 = w_ref[0:1, :], w_ref[1:2, :], w_ref[2:3, :]
        g = g_ref[...]

        def first_pass(i, carry):
            gw0, gw1, gw2, gg = carry
            r0 = pl.multiple_of(i * rc, rc)
            rows = pl.ds(r0, rc)
            prev = pl.ds(pl.multiple_of(jnp.maximum(r0 - 8, 0), 8), 8)
            v = cc_ref[rows, :] * cu_ref[rows, :]
            vp = jnp.where(i > 0, cc_ref[prev, :] * cu_ref[prev, :], 0.0)
            v1, v2 = _shifted(vp, v, 1), _shifted(vp, v, 2)
            y = w2 * v + w1 * v1 + w0 * v2
            cb = cb_ref[rows, :]
            dcy, ggi = _head_rms_bwd(d_ref[rows, :], cb * y, g, ones)
            d3_ref[0, rows, :] = (dcy * y).astype(bf16)
            dy = dcy * cb
            dy_buf[rows, :] = dy
            return (gw0 + jnp.sum(dy * v2, axis=0, keepdims=True), gw1 + jnp.sum(dy * v1, axis=0, keepdims=True),
                    gw2 + jnp.sum(dy * v, axis=0, keepdims=True), gg + ggi)

        zero = jnp.zeros((1, LANES), f32)
        gw0, gw1, gw2, gg = lax.fori_loop(0, nchunk, first_pass, (zero, zero, zero, zero))
        gw_ref[...] = jnp.zeros_like(gw_ref)
        gw_ref[0:1, :] = gw0
        gw_ref[1:2, :] = gw1
        gw_ref[2:3, :] = gw2
        gg_ref[...] = gg

        def second_pass(i, carry):
            r0 = pl.multiple_of(i * rc, rc)
            rows = pl.ds(r0, rc)
            nxt = pl.ds(pl.multiple_of(jnp.minimum(r0 + rc, s - 8), 8), 8)
            dy = dy_buf[rows, :]
            dyn = jnp.where(i < nchunk - 1, dy_buf[nxt, :], 0.0)
            ext = jnp.concatenate([dy, dyn], axis=0)
            up1 = pltpu.roll(ext, rc + 8 - 1, axis=0)[:rc]
            up2 = pltpu.roll(ext, rc + 8 - 2, axis=0)[:rc]
            dv = w2 * dy + w1 * up1 + w0 * up2
            d3_ref[1, rows, :] = (dv * cu_ref[rows, :]).astype(bf16)
            d3_ref[2, rows, :] = (dv * cc_ref[rows, :]).astype(bf16)
            return carry

        lax.fori_loop(0, nchunk, second_pass, 0)

    def col(off):
        return pl.BlockSpec((s, LANES), lambda j: (0, off + j))

    return pl.pallas_call(
        body, name="conv_bwd", grid=(nblk,),
        in_specs=[col(0), col(nblk), col(2 * nblk), col(0),
                  pl.BlockSpec((None, CONV_W_ROWS, LANES), lambda j: (j, 0, 0)),
                  pl.BlockSpec((1, LANES), lambda j: (0, j))],
        out_specs=[pl.BlockSpec((3, s, LANES), lambda j: (0, 0, j)),
                   pl.BlockSpec((None, CONV_W_ROWS, LANES), lambda j: (j, 0, 0)),
                   pl.BlockSpec((1, LANES), lambda j: (0, j))],
        out_shape=[_sds((3, s, W_CONV), bf16), _sds((nblk, CONV_W_ROWS, LANES), f32), _sds((1, W_CONV), f32)],
        scratch_shapes=[pltpu.VMEM((s, LANES), f32)],
        compiler_params=_arb(1),
    )(proj, proj, proj, dcn, conv_w, g_conv)


PIECE = W_CONV


def _piece_spans():
    ncol = IN_COLS // N_CHIPS
    spans = []
    for p in range(IN_COLS // PIECE):
        for k in range(N_CHIPS):
            lo, hi = max(p * PIECE, k * ncol), min((p + 1) * PIECE, (k + 1) * ncol)
            if lo < hi:
                spans.append((p, lo - p * PIECE, hi - p * PIECE, k, lo - k * ncol, hi - k * ncol))
    return spans


def _in_proj_bwd(dconv, dq, dk, dv, dh1, x, g_mix, w_in, tm, ex=None):
    s = x.shape[0]
    ncol = IN_COLS // N_CHIPS

    def body(dc_ref, dq_ref, dk_ref, dv_ref, dh_ref, x_ref, g_ref, w_ref, dx_ref, gg_ref):
        i = pl.program_id(0)
        pieces = [dc_ref.at[0], dc_ref.at[1], dc_ref.at[2], dq_ref, dk_ref, dv_ref]
        da = jnp.zeros((tm, D_MODEL), f32)
        for p, plo, phi, k, wlo, whi in _piece_spans():
            da += _dot(pieces[p][:, plo:phi], w_ref[k, :, wlo:whi], NT)
        dxn, gg = _rms_bwd(da, x_ref[...], g_ref[...])
        dx_ref[...] = dh_ref[...] + dxn

        @pl.when(i == 0)
        def _():
            gg_ref[...] = jnp.zeros_like(gg_ref)

        gg_ref[...] += gg

    return _call_carrying(
        ex, body, "in_proj_bwd", (s // tm,),
        in_specs=[pl.BlockSpec((3, tm, PIECE), lambda i: (0, i, 0)),
                  pl.BlockSpec((tm, PIECE), lambda i: (i, 0)),
                  pl.BlockSpec((tm, PIECE), lambda i: (i, 0)),
                  pl.BlockSpec((tm, PIECE), lambda i: (i, 0)),
                  pl.BlockSpec((tm, D_MODEL), lambda i: (i, 0)),
                  pl.BlockSpec((tm, D_MODEL), lambda i: (i, 0)),
                  pl.BlockSpec((1, D_MODEL), lambda i: (0, 0)),
                  pl.BlockSpec((N_CHIPS, D_MODEL, ncol), lambda i: (0, 0, 0))],
        out_specs=[pl.BlockSpec((tm, D_MODEL), lambda i: (i, 0)),
                   pl.BlockSpec((1, D_MODEL), lambda i: (0, 0))],
        out_shape=[_sds((s, D_MODEL), f32), _sds((1, D_MODEL), f32)],
        scratch_shapes=[],
        args=(dconv, dq, dk, dv, dh1, x, g_mix, w_in))


def _grad_w_in(a, dconv, dq, dk, dv, ts):
    s = a.shape[0]
    ncol = IN_COLS // N_CHIPS

    def body(a_ref, dc_ref, dq_ref, dk_ref, dv_ref, o_ref):
        @pl.when(pl.program_id(0) == 0)
        def _():
            o_ref[...] = jnp.zeros_like(o_ref)

        pieces = [dc_ref.at[0], dc_ref.at[1], dc_ref.at[2], dq_ref, dk_ref, dv_ref]
        av = a_ref[...]
        for p, plo, phi, k, wlo, whi in _piece_spans():
            o_ref[k, :, wlo:whi] += _dot(av, pieces[p][:, plo:phi], TN)

    tok = pl.BlockSpec((ts, PIECE), lambda i: (i, 0))
    return pl.pallas_call(
        body, name="grad_w_in", grid=(s // ts,),
        in_specs=[pl.BlockSpec((ts, D_MODEL), lambda i: (i, 0)),
                  pl.BlockSpec((3, ts, PIECE), lambda i: (0, i, 0)), tok, tok, tok],
        out_specs=pl.BlockSpec((N_CHIPS, D_MODEL, ncol), lambda i: (0, 0, 0)),
        out_shape=_sds((N_CHIPS, D_MODEL, ncol), f32),
        compiler_params=_arb(1),
    )(a, dconv, dq, dk, dv)


def _weight_grad(a, b, bm, bn, ts, name, relu_sq=False):
    s, m = a.shape
    n = b.shape[1]
    nn = n // bn
    nk = s // ts

    def body(a_ref, b_ref, o_ref, ob_ref):
        @pl.when(pl.program_id(2) == 0)
        def _():
            o_ref[...] = jnp.zeros_like(o_ref)

        av = a_ref[...]
        if relu_sq:
            av = jnp.square(jnp.maximum(av.astype(f32), 0.0)).astype(bf16)
        o_ref[...] += _dot(av, b_ref[...], TN)

        @pl.when(pl.program_id(2) == nk - 1)
        def _():
            ob_ref[...] = o_ref[...].astype(bf16)

    tile = pl.BlockSpec((None, bm, bn), lambda i, j, k: (i * nn + j, 0, 0))
    return pl.pallas_call(
        body, name=name, grid=(m // bm, nn, nk),
        in_specs=[pl.BlockSpec((ts, bm), lambda i, j, k: (k, i)),
                  pl.BlockSpec((ts, bn), lambda i, j, k: (k, j))],
        out_specs=[tile, tile],
        out_shape=[_sds(((m // bm) * nn, bm, bn), f32), _sds(((m // bm) * nn, bm, bn), bf16)],
        compiler_params=_arb(3),
    )(a, b)


def kernel(x, p, g_mix, w_in, conv_w, g_conv_out, g_attn_out, w_out, g_mlp, w_up, w_down, g_ple, w_ple_gate, w_ple_proj, g_final, loss_target, m_g_mix, m_w_in, m_conv_w, m_g_conv_out, m_g_attn_out, m_w_out, m_g_mlp, m_w_up, m_w_down, m_g_ple, m_w_ple_gate, m_w_ple_proj, m_g_final, v_g_mix, v_w_in, v_conv_w, v_g_conv_out, v_g_attn_out, v_w_out, v_g_mlp, v_w_up, v_w_down, v_g_ple, v_w_ple_gate, v_w_ple_proj, v_g_final):
    s = x.shape[1]
    tm = min(TOKEN_TILE, s)
    tg = min(GRAD_TOKEN_TILE, s)
    xs = x.reshape(s, D_MODEL)
    ps = p.reshape(s, PLE_DIM)
    target = loss_target.reshape(s, D_MODEL)
    core = lax.axis_index("c").astype(jnp.int32).reshape(1)
    chip = 2 * lax.axis_index("x") + lax.axis_index("y")

    big = {"w_in": w_in[0], "w_out": w_out[0], "w_up": w_up[0], "w_down": w_down[0],
           "w_ple_gate": w_ple_gate[0], "w_ple_proj": w_ple_proj[0]}
    names = list(big)
    conv_shard = jnp.pad(conv_w[0], ((0, CONV_W_ROWS - conv_w.shape[1]), (0, 0)))
    later_names = names[1:]
    w_in_f, conv_f = _run_exchange(_gather_exchange([big["w_in"].astype(bf16), conv_shard]), "gather_w_in")

    proj, a_b, kp, vp, kt, vt = _in_proj(xs, g_mix, w_in_f, tm)
    conv_n = _conv_fwd(proj, conv_f, g_conv_out)
    (o, tot), gathered = _attn_fwd(proj, kp, vt, _gather_exchange([big[k].astype(bf16) for k in later_names]))
    w_out_f, w_up_f, w_down_f, w_gate_f, w_proj_f = gathered
    w_out_f = w_out_f.reshape(D_MODEL, D_MODEL)
    w_gate_f = w_gate_f.reshape(D_MODEL, D_MODEL)
    h1, cat_b = _out_proj(o, conv_n, xs, g_attn_out, w_out_f, tm)
    h2, u_b, m_b = _mlp_fwd(h1, g_mlp, w_up_f, w_down_f, min(MLP_TOKEN_TILE, s))

    dh2, dh2_b, n3_b, dgl_b, dpp_b, p_b, gg_ple, gg_final, loss_row = _tail(
        h2, ps, target, g_ple, g_final.reshape(1, D_MODEL), w_gate_f, w_proj_f, tm)
    dh1, dh1_b, du_b, gg_mlp = _mlp_bwd(dh2, dh2_b, h1, u_b, g_mlp, w_up_f, w_down_f, tm)
    both_kinds = {
        "w_out": _weight_grad(cat_b, dh1_b, D_MODEL, D_MODEL, tg, "grad_w_out"),
        "w_up": _weight_grad(m_b, du_b, D_MODEL, D_FF // N_CHIPS, tg, "grad_w_up"),
        "w_down": _weight_grad(u_b, dh2_b, D_FF // N_CHIPS, D_MODEL, tg, "grad_w_down", relu_sq=True),
        "w_ple_gate": _weight_grad(n3_b, dgl_b, D_MODEL, D_MODEL, tg, "grad_w_ple_gate"),
        "w_ple_proj": _weight_grad(p_b, dpp_b, PLE_DIM, D_MODEL // N_CHIPS, tg, "grad_w_ple_proj"),
    }
    by_chip = lambda k, g: g.reshape((N_CHIPS, big[k].shape[0], big[k].shape[1]))
    part = {k: by_chip(k, g32) for k, (g32, _) in both_kinds.items()}
    part_b = {k: by_chip(k, g16) for k, (_, g16) in both_kinds.items()}
    (dcn, do, gg_attn), from_sibling = _out_proj_bwd(
        dh1_b, o, g_attn_out, w_out_f, tm, _pair_exchange([part_b[k] for k in later_names]))
    pair = [_pair_sum(part[k], r, core, "pair_sum_" + k) for k, r in zip(later_names, from_sibling)]
    dconv, g_conv_w, gg_conv = _conv_bwd(proj, dcn, conv_f, g_conv_out)
    (dq, dk, dv), from_chips = _attn_bwd(proj, kp, vp, kt, do, tot, _chip_exchange(pair))
    dk, dv = dk.reshape(s, W_ATTN), dv.reshape(s, W_ATTN)

    part["w_in"] = _grad_w_in(a_b, dconv, dq, dk, dv, min(MLP_TOKEN_TILE, s))
    in_sibling = _run_exchange(_pair_exchange([part["w_in"]]), "grad_pair_exchange_w_in")
    in_pair = _pair_sum(part["w_in"], in_sibling[0], core, "pair_sum_w_in")
    (grad_x, gg_mix), in_chips = _in_proj_bwd(
        dconv, dq, dk, dv, dh1, xs, g_mix, w_in_f, tm, _chip_exchange([in_pair]))

    place = jnp.stack([lax.axis_index("c"), chip]).astype(jnp.int32)
    half = [_chip_sum(mine, landed, place, "chip_sum_" + k)
            for k, mine, landed in zip(names, [in_pair] + pair, list(in_chips) + list(from_chips))]
    both = _sibling_exchange(half)
    grad = {k: b.reshape(big[k].shape) for k, b in zip(names, both)}

    gcw = g_conv_w[:, :3, :].transpose(1, 0, 2).reshape(3, W_CONV)
    row = lambda *parts: jnp.concatenate(parts, axis=1)
    packed = jnp.concatenate([
        gg_mix, gg_mlp, gg_ple, gg_final, row(gg_conv, gg_attn), row(gcw[0:1], gcw[1:2]),
        row(gcw[2:3], loss_row, jnp.zeros((1, W_CONV - LANES), f32)), jnp.zeros((1, D_MODEL), f32)], axis=0)
    summed = _allreduce_small(packed)
    loss = summed[6, W_CONV]
    gcw_full = jnp.stack([summed[5, :W_CONV], summed[5, W_CONV:], summed[6, :W_CONV]])
    grad["conv_w"] = lax.dynamic_slice(gcw_full, (0, chip * LANES), (3, LANES))
    vec_names = ["g_mix", "g_mlp", "g_ple", "g_final"]
    vec_w = {"g_mix": g_mix, "g_mlp": g_mlp, "g_ple": g_ple, "g_final": g_final.reshape(1, D_MODEL)}
    vec_m = {"g_mix": m_g_mix, "g_mlp": m_g_mlp, "g_ple": m_g_ple, "g_final": m_g_final.reshape(1, D_MODEL)}
    vec_v = {"g_mix": v_g_mix, "g_mlp": v_g_mlp, "g_ple": v_g_ple, "g_final": v_g_final.reshape(1, D_MODEL)}

    def pack_vec(d, conv, attn):
        return jnp.concatenate([d[k] for k in vec_names] + [row(conv, attn)], axis=0)

    vec_g = summed[0:5]
    vec_d, vec_nm, vec_nv, vec_g = _adamw(vec_g, pack_vec(vec_w, g_conv_out, g_attn_out),
                                   pack_vec(vec_m, m_g_conv_out, m_g_attn_out),
                                   pack_vec(vec_v, v_g_conv_out, v_g_attn_out), "adamw_vectors")

    given_w = dict(big, conv_w=conv_w[0])
    given_m = {"w_in": m_w_in[0], "w_out": m_w_out[0], "w_up": m_w_up[0], "w_down": m_w_down[0],
               "w_ple_gate": m_w_ple_gate[0], "w_ple_proj": m_w_ple_proj[0], "conv_w": m_conv_w[0]}
    given_v = {"w_in": v_w_in[0], "w_out": v_w_out[0], "w_up": v_w_up[0], "w_down": v_w_down[0],
               "w_ple_gate": v_w_ple_gate[0], "w_ple_proj": v_w_ple_proj[0], "conv_w": v_conv_w[0]}
    delta, new_m, new_v = {}, {}, {}
    for k in names + ["conv_w"]:
        delta[k], new_m[k], new_v[k], grad[k] = _adamw(grad[k], given_w[k], given_m[k], given_v[k], "adamw_" + k)

    def unpack(vals, kind):
        out = {k: vals[i:i + 1] for i, k in enumerate(vec_names)}
        out["g_final"] = out["g_final"].reshape(D_MODEL)
        out["g_conv_out"] = vals[4:5, :W_CONV]
        out["g_attn_out"] = vals[4:5, W_CONV:]
        out.update({k: v[None] for k, v in kind.items()})
        return out

    order = ["g_mix", "w_in", "conv_w", "g_conv_out", "g_attn_out", "w_out", "g_mlp", "w_up", "w_down",
             "g_ple", "w_ple_gate", "w_ple_proj", "g_final"]
    groups = [unpack(vec_g, grad), unpack(vec_d, delta), unpack(vec_nm, new_m), unpack(vec_nv, new_v)]
    return (loss, grad_x[None]) + tuple(g[k] for g in groups for k in order)
```
